```python
import math
import jax
import jax.numpy as jnp
from jax import lax
import numpy as np


D_MODEL = 1024
BATCH = 8
SEQ = 4096
DEPTH = 2

CTX_LEN = 256
GRID_W = 64
EPS = 1e-6
CONV_CH = 512
CONV_GROUPS = 8
CONV_WIDTH = 31
FOURIER_CH = 512
FOURIER_GROUPS = 4
MIX_IN = 2 * CONV_CH + FOURIER_CH
MIX_OUT = CONV_CH + FOURIER_CH
DA_HEADS = 8
DA_QK_DIM = 64
DA_V_DIM = 2 * DA_QK_DIM
QK_W = DA_HEADS * 2 * DA_QK_DIM
V_W = DA_HEADS * DA_V_DIM
ROPE_BASE = 10000.0
Q_BLOCK = 128
N_EXPERTS = 64
TOP_K = 6
N_GROUPS = 8
TOPK_GROUPS = 4
EXPERT_FF = 256
SHARED_FF = 256
ROUTED_SCALE = 2.5
MOE_BLOCK = 128
N_EVEN = (DEPTH + 1) // 2
N_ODD = DEPTH // 2

kernel_name = 'hybrid_conv_fourier_diffattn_moe_dit'


def rms_norm(x, g):
    xf = x.astype(jnp.float32)
    y = xf * lax.rsqrt(jnp.mean(xf * xf, axis=-1, keepdims=True) + EPS)
    return (y * g.astype(jnp.float32)).astype(x.dtype)


def swiglu(x, w_gate, w_up, w_down):
    return (jax.nn.silu(x @ w_gate) * (x @ w_up)) @ w_down


def conv_fourier_mix(h, w_in, conv_w, conv_b, ln_g, ln_b, w_out):
    bn, seq_len, _ = h.shape
    u = h @ w_in
    a_val, a_gate, f = jnp.split(u, [CONV_CH, 2 * CONV_CH], axis=-1)
    a = a_val * jax.nn.sigmoid(a_gate)
    a = lax.conv_general_dilated(a, conv_w.astype(a.dtype), window_strides=(1,),
                                 padding=[(CONV_WIDTH // 2, CONV_WIDTH // 2)],
                                 dimension_numbers=('NWC', 'WIO', 'NWC'),
                                 feature_group_count=CONV_CH) + conv_b
    ag = a.reshape(bn, seq_len, CONV_GROUPS, -1).astype(jnp.float32)
    mu = jnp.mean(ag, axis=-1, keepdims=True)
    var = jnp.mean(jnp.square(ag - mu), axis=-1, keepdims=True)
    an = ((ag - mu) * lax.rsqrt(var + EPS)).reshape(bn, seq_len, CONV_CH)
    a = jax.nn.silu(an * ln_g.astype(jnp.float32) + ln_b.astype(jnp.float32)).astype(h.dtype)
    fg = f.reshape(bn, seq_len, FOURIER_GROUPS, -1).astype(jnp.float32)
    fr = jnp.fft.fft2(fg, axes=(1, 3), norm='ortho').real.reshape(bn, seq_len, FOURIER_CH).astype(h.dtype)
    return jnp.concatenate([a, fr], axis=-1) @ w_out


def axial_rope_tables(n_tokens):
    rows = n_tokens // GRID_W
    row = jnp.repeat(jnp.arange(rows), GRID_W).astype(jnp.float32)
    col = jnp.tile(jnp.arange(GRID_W), rows).astype(jnp.float32)
    n_freq = DA_QK_DIM // 4
    inv = jnp.power(ROPE_BASE, -jnp.arange(n_freq, dtype=jnp.float32) / n_freq)
    ang = jnp.concatenate([row[:, None] * inv, col[:, None] * inv], axis=-1)
    ang = jnp.concatenate([ang, ang], axis=-1)
    return jnp.cos(ang), jnp.sin(ang)


def apply_rope(x, cos, sin):
    half = DA_QK_DIM // 2
    xf = x.astype(jnp.float32)
    rot = jnp.concatenate([-xf[..., half:], xf[..., :half]], axis=-1)
    c = cos[:, None, None, :]
    s = sin[:, None, None, :]
    return (xf * c + rot * s).astype(x.dtype)


def da_queries(h, w_qkv):
    return (h @ w_qkv[:, :QK_W]).reshape(h.shape[0], h.shape[1], DA_HEADS, 2, DA_QK_DIM)


def da_keys_values(h, w_qkv):
    k, v = jnp.split(h @ w_qkv[:, QK_W:], [QK_W], axis=-1)
    return (k.reshape(h.shape[0], h.shape[1], DA_HEADS, 2, DA_QK_DIM),
            v.reshape(h.shape[0], h.shape[1], DA_HEADS, DA_V_DIM))


def diff_attention_core(q, k, v, lam):
    bn, lq = q.shape[0], q.shape[1]
    nb = lq // Q_BLOCK
    qb = jnp.moveaxis(q.reshape(bn, nb, Q_BLOCK, DA_HEADS, 2, DA_QK_DIM), 1, 0)
    scale = DA_QK_DIM ** -0.5

    def block(qi):
        s = jnp.einsum('bqhcd,bkhcd->bchqk', qi, k, preferred_element_type=jnp.float32) * scale
        p = jax.nn.softmax(s, axis=-1)
        pd = p[:, 0] - lam * p[:, 1]
        return jnp.einsum('bhqk,bkhe->bqhe', pd.astype(v.dtype), v,
                          preferred_element_type=jnp.float32).astype(v.dtype)

    o = lax.map(block, qb)
    return jnp.moveaxis(o, 0, 1).reshape(bn, lq, DA_HEADS, DA_V_DIM)


def diff_attn_out(o, subln_g, lam_init, w_out):
    o = rms_norm(o, subln_g) * (1.0 - lam_init)
    return o.reshape(o.shape[0], o.shape[1], V_W) @ w_out


def moe_ffn(h, router_w, router_b, w_gate, w_up, w_down, sh_gate, sh_up, sh_down):
    n_tok, dm = h.shape
    scores = jax.nn.sigmoid(jnp.matmul(h, router_w, preferred_element_type=jnp.float32))
    biased = scores + router_b.astype(jnp.float32)
    grouped = biased.reshape(n_tok, N_GROUPS, N_EXPERTS // N_GROUPS)
    group_score = jnp.sum(lax.top_k(grouped, 2)[0], axis=-1)
    top_groups = lax.top_k(group_score, TOPK_GROUPS)[1]
    group_mask = jnp.sum(jax.nn.one_hot(top_groups, N_GROUPS, dtype=jnp.float32), axis=1) > 0
    expert_mask = jnp.repeat(group_mask, N_EXPERTS // N_GROUPS, axis=1)
    top_e = lax.top_k(jnp.where(expert_mask, biased, -jnp.inf), TOP_K)[1]
    gates = jnp.take_along_axis(scores, top_e, axis=1)
    gates = gates / jnp.sum(gates, axis=-1, keepdims=True) * ROUTED_SCALE
    n_assign = n_tok * TOP_K
    flat_e = top_e.reshape(-1)
    flat_tok = jnp.repeat(jnp.arange(n_tok, dtype=jnp.int32), TOP_K)
    flat_g = gates.reshape(-1)
    order = jnp.argsort(flat_e)
    sorted_e = flat_e[order]
    counts = jnp.bincount(flat_e, length=N_EXPERTS)
    starts = jnp.cumsum(counts) - counts
    padded = (counts + MOE_BLOCK - 1) // MOE_BLOCK * MOE_BLOCK
    pad_ends = jnp.cumsum(padded)
    pad_starts = pad_ends - padded
    dest = pad_starts[sorted_e] + jnp.arange(n_assign) - starts[sorted_e]
    n_blocks = -(-(n_assign + N_EXPERTS * (MOE_BLOCK - 1)) // MOE_BLOCK)
    n_rows = n_blocks * MOE_BLOCK
    row_tok = jnp.zeros((n_rows,), jnp.int32).at[dest].set(flat_tok[order])
    row_gate = jnp.zeros((n_rows,), jnp.float32).at[dest].set(flat_g[order])
    block_e = jnp.minimum(jnp.searchsorted(pad_ends, jnp.arange(n_blocks) * MOE_BLOCK, side='right'),
                          N_EXPERTS - 1)
    xb = h[row_tok].reshape(n_blocks, MOE_BLOCK, dm)
    yb = lax.map(lambda a: swiglu(a[0], w_gate[a[1]], w_up[a[1]], w_down[a[1]]), (xb, block_e))
    routed = jnp.zeros((n_tok, dm), jnp.float32).at[row_tok].add(
        yb.reshape(n_rows, dm).astype(jnp.float32) * row_gate[:, None])
    shared = swiglu(h, sh_gate, sh_up, sh_down).astype(jnp.float32)
    return (routed + shared).astype(h.dtype)


def setup_inputs(seed: int = 0) -> dict:
    key = jax.random.key(seed)
    ks = jax.random.split(key, 32)
    d = D_MODEL

    def nrm(k, shape, s):
        return jax.random.normal(k, shape, jnp.float32) * s

    return {
        'x': nrm(ks[0], (BATCH, SEQ, d), 1.0),
        'c': nrm(ks[1], (BATCH, d), 1.0),
        'ctx': nrm(ks[2], (BATCH, CTX_LEN, d), 1.0),
        'c_ctx': nrm(ks[3], (d,), 1.0),
        'ada_w': nrm(ks[4], (DEPTH, d, 6 * d), 0.5 * d ** -0.5),
        'ada_b': nrm(ks[5], (DEPTH, 6 * d), 0.02),
        'norm_g': 1.0 + nrm(ks[6], (DEPTH, 4, d), 0.02),
        'cf_w_in': nrm(ks[7], (N_EVEN, d, MIX_IN), d ** -0.5),
        'cf_conv_w': nrm(ks[8], (N_EVEN, CONV_WIDTH, 1, CONV_CH), CONV_WIDTH ** -0.5),
        'cf_conv_b': nrm(ks[9], (N_EVEN, CONV_CH), 0.02),
        'cf_ln_g': 1.0 + nrm(ks[10], (N_EVEN, CONV_CH), 0.02),
        'cf_ln_b': nrm(ks[11], (N_EVEN, CONV_CH), 0.02),
        'cf_w_out': nrm(ks[12], (N_EVEN, MIX_OUT, d), MIX_OUT ** -0.5),
        'da_w_qkv': nrm(ks[13], (N_ODD, d, 2 * QK_W + V_W), d ** -0.5),
        'da_lambda': nrm(ks[14], (N_ODD, 4, DA_QK_DIM), 0.1),
        'da_subln_g': 1.0 + nrm(ks[15], (N_ODD, DA_V_DIM), 0.02),
        'da_w_out': nrm(ks[16], (N_ODD, V_W, d), V_W ** -0.5),
        'moe_router_w': nrm(ks[17], (DEPTH, d, N_EXPERTS), d ** -0.5),
        'moe_router_b': nrm(ks[18], (DEPTH, N_EXPERTS), 0.01),
        'moe_w_gate': nrm(ks[19], (DEPTH, N_EXPERTS, d, EXPERT_FF), d ** -0.5),
        'moe_w_up': nrm(ks[20], (DEPTH, N_EXPERTS, d, EXPERT_FF), d ** -0.5),
        'moe_w_down': nrm(ks[21], (DEPTH, N_EXPERTS, EXPERT_FF, d), EXPERT_FF ** -0.5),
        'moe_sh_gate': nrm(ks[22], (DEPTH, d, SHARED_FF), d ** -0.5),
        'moe_sh_up': nrm(ks[23], (DEPTH, d, SHARED_FF), d ** -0.5),
        'moe_sh_down': nrm(ks[24], (DEPTH, SHARED_FF, d), SHARED_FF ** -0.5),
    }


def reference(x, c, ctx, c_ctx, ada_w, ada_b, norm_g, cf_w_in, cf_conv_w, cf_conv_b, cf_ln_g, cf_ln_b,
              cf_w_out, da_w_qkv, da_lambda, da_subln_g, da_w_out, moe_router_w, moe_router_b,
              moe_w_gate, moe_w_up, moe_w_down, moe_sh_gate, moe_sh_up, moe_sh_down):
    bn, n_lat, dm = x.shape
    n_ctx = ctx.shape[1]
    cos, sin = axial_rope_tables(n_lat)
    c_lat = jax.nn.silu(c)[:, None, :]
    c_con = jax.nn.silu(c_ctx)[None, None, :]
    for i in range(DEPTH):
        last = i == DEPTH - 1
        odd = i % 2 == 1
        j = i // 2
        m_lat = jnp.split(c_lat @ ada_w[i] + ada_b[i], 6, axis=-1)
        m_ctx = jnp.split(c_con @ ada_w[i] + ada_b[i], 6, axis=-1)
        hx = rms_norm(x, norm_g[i, 0]) * (1.0 + m_lat[1]) + m_lat[0]
        if odd:
            hc = rms_norm(ctx, norm_g[i, 0]) * (1.0 + m_ctx[1]) + m_ctx[0]
            lam_init = 0.8 - 0.6 * math.exp(-0.3 * i)
            lp = da_lambda[j].astype(jnp.float32)
            lam = jnp.exp(jnp.sum(lp[0] * lp[1])) - jnp.exp(jnp.sum(lp[2] * lp[3])) + lam_init
            qx = apply_rope(da_queries(hx, da_w_qkv[j]), cos, sin)
            kx, vx = da_keys_values(hx, da_w_qkv[j])
            kx = apply_rope(kx, cos, sin)
            kc, vc = da_keys_values(hc, da_w_qkv[j])
            k_all = jnp.concatenate([kc, kx], axis=1)
            v_all = jnp.concatenate([vc, vx], axis=1)
            yx = diff_attn_out(diff_attention_core(qx, k_all, v_all, lam), da_subln_g[j], lam_init, da_w_out[j])
            if not last:
                qc = da_queries(hc, da_w_qkv[j])
                yc = diff_attn_out(diff_attention_core(qc, kc, vc, lam), da_subln_g[j], lam_init, da_w_out[j])
        else:
            yx = conv_fourier_mix(hx, cf_w_in[j], cf_conv_w[j], cf_conv_b[j], cf_ln_g[j], cf_ln_b[j], cf_w_out[j])
            if not last:
                hc = rms_norm(ctx, norm_g[i, 0]) * (1.0 + m_ctx[1]) + m_ctx[0]
                yc = conv_fourier_mix(hc, cf_w_in[j], cf_conv_w[j], cf_conv_b[j], cf_ln_g[j], cf_ln_b[j],
                                      cf_w_out[j])
        x = x + m_lat[2] * rms_norm(yx, norm_g[i, 1])
        if not last:
            ctx = ctx + m_ctx[2] * rms_norm(yc, norm_g[i, 1])
        tok = (rms_norm(x, norm_g[i, 2]) * (1.0 + m_lat[4]) + m_lat[3]).reshape(-1, dm)
        if not last:
            tok_c = (rms_norm(ctx, norm_g[i, 2]) * (1.0 + m_ctx[4]) + m_ctx[3]).reshape(-1, dm)
            tok = jnp.concatenate([tok, tok_c], axis=0)
        y = moe_ffn(tok, moe_router_w[i], moe_router_b[i], moe_w_gate[i], moe_w_up[i], moe_w_down[i],
                    moe_sh_gate[i], moe_sh_up[i], moe_sh_down[i])
        x = x + m_lat[5] * rms_norm(y[:bn * n_lat].reshape(bn, n_lat, dm), norm_g[i, 3])
        if not last:
            ctx = ctx + m_ctx[5] * rms_norm(y[bn * n_lat:].reshape(bn, n_ctx, dm), norm_g[i, 3])
    return x
```

```python
import functools
import math

import numpy as np
import jax
import jax.numpy as jnp
from jax import lax
from jax.experimental import pallas as pl
from jax.experimental.pallas import tpu as pltpu

F32 = jnp.float32
BF16 = jnp.bfloat16

EPS = 1e-6
GRID_W = 64
CONV_GROUPS = 8
FOURIER_GROUPS = 4
DA_HEADS = 8
DA_QK_DIM = 64
ROPE_BASE = 10000.0
N_GROUPS = 8
TOPK_GROUPS = 4
TOP_K = 6
ROUTED_SCALE = 2.5

LANES = 128
SUBLANES = 8
N_SEG = 16
HALO = 16
EXPERT_ROWS = 256
MIB = 1024 * 1024


def _cparams(n_axes, vmem_mib):
    return pltpu.CompilerParams(dimension_semantics=("arbitrary",) * n_axes,
                                vmem_limit_bytes=vmem_mib * MIB)


def _sigmoid(v):
    return 1.0 / (1.0 + jnp.exp(-v))


def _rms(v, g):
    return v * lax.rsqrt(jnp.mean(v * v, axis=-1, keepdims=True) + EPS) * g


def _normmod(v, g, shift, scale):
    return _rms(v, g) * (1.0 + scale) + shift


def _split_bf16(v):
    hi = v.astype(BF16)
    lo = (v - hi.astype(F32)).astype(BF16)
    return hi, lo


def _dot(a, b):
    return jnp.dot(a, b, preferred_element_type=F32)


def _dot_nt(a, b):
    return lax.dot_general(a, b, (((1,), (1,)), ((), ())), preferred_element_type=F32)


def _mods_kernel(c_ref, w_ref, b_ref, o_ref):
    cv = c_ref[...]
    o_ref[...] = _dot(cv * _sigmoid(cv), w_ref[...]) + b_ref[...]


def _mods_call(cond, ada_w, ada_b):
    depth, d, n = ada_w.shape
    tn = n // 4
    return pl.pallas_call(
        _mods_kernel,
        grid=(depth, n // tn),
        in_specs=[pl.BlockSpec((N_SEG, d), lambda l, j: (0, 0)),
                  pl.BlockSpec((None, d, tn), lambda l, j: (l, 0, j)),
                  pl.BlockSpec((None, 1, tn), lambda l, j: (l, 0, j))],
        out_specs=pl.BlockSpec((None, N_SEG, tn), lambda l, j: (l, 0, j)),
        out_shape=jax.ShapeDtypeStruct((depth, N_SEG, n), F32),
        compiler_params=_cparams(2, 40),
        name="adaln_mods",
    )(cond, ada_w, ada_b.reshape(depth, 1, n))


def _inproj_kernel(x_ref, g_ref, sh_ref, sc_ref, w_ref, cs_ref, a_ref, ab_ref, *, cc, gw):
    h = _normmod(x_ref[...], g_ref[...], sh_ref[...], sc_ref[...]).astype(BF16)
    u = _dot(h, w_ref[...])
    a_ref[...] = u[:, :cc] * _sigmoid(u[:, cc:2 * cc])
    f = u[:, 2 * cc:].astype(BF16)
    n_g = f.shape[1] // gw
    parts = [_dot(f[:, g * gw:(g + 1) * gw], cs_ref[...]) for g in range(n_g)]
    cos_part = [p[:, :gw] for p in parts]
    sin_part = [p[:, gw:] for p in parts]
    ab_ref[...] = jnp.concatenate(cos_part + sin_part, axis=-1).astype(BF16)


def _seg_map(tm, seq, n_batch):
    return lambda i: (jnp.minimum(i * tm // seq, n_batch), 0, 0)


def _inproj_call(xs, g, shift, scale, w_in, cs, *, cc, gw, tm, seq, n_batch):
    rows, d = xs.shape
    n = w_in.shape[1]
    fc = n - 2 * cc
    seg = _seg_map(tm, seq, n_batch)
    return pl.pallas_call(
        functools.partial(_inproj_kernel, cc=cc, gw=gw),
        grid=(rows // tm,),
        in_specs=[pl.BlockSpec((tm, d), lambda i: (i, 0)),
                  pl.BlockSpec((1, d), lambda i: (0, 0)),
                  pl.BlockSpec((None, 1, d), seg),
                  pl.BlockSpec((None, 1, d), seg),
                  pl.BlockSpec((d, n), lambda i: (0, 0)),
                  pl.BlockSpec((gw, 2 * gw), lambda i: (0, 0))],
        out_specs=[pl.BlockSpec((tm, cc), lambda i: (i, 0)),
                   pl.BlockSpec((tm, 2 * fc), lambda i: (i, 0))],
        out_shape=[jax.ShapeDtypeStruct((rows, cc), F32),
                   jax.ShapeDtypeStruct((rows, 2 * fc), BF16)],
        compiler_params=_cparams(1, 40),
        name="inproj_glu_chdft",
    )(xs, g, shift, scale, w_in, cs)


def _conv_kernel(prev_ref, main_ref, next_ref, w_ref, cb_ref, lg_ref, lb_ref, gm_ref, o_ref,
                 buf, cv, *, tl, width, lat_tiles, lat_tps, ctx_tps, chunk):
    i = pl.program_id(0)
    is_lat = i < lat_tiles
    tps = jnp.where(is_lat, lat_tps, ctx_tps)
    j = jnp.where(is_lat, i, i - lat_tiles) % tps
    zero = jnp.zeros((HALO, buf.shape[1]), F32)
    buf[0:HALO, :] = jnp.where(j > 0, prev_ref[...], zero)
    buf[HALO:HALO + tl, :] = main_ref[...]
    buf[HALO + tl:HALO + tl + HALO, :] = jnp.where(j < tps - 1, next_ref[...], zero)
    base = HALO - width // 2
    for r0 in range(0, tl, chunk):
        acc = buf[base + r0:base + r0 + chunk, :] * w_ref[0:1, :]
        for t in range(1, width):
            acc = acc + buf[base + r0 + t:base + r0 + t + chunk, :] * w_ref[t:t + 1, :]
        cv[r0:r0 + chunk, :] = acc
    a = cv[...] + cb_ref[...]
    gm = gm_ref[...]
    a_hi, a_lo = _split_bf16(a)
    mu = _dot(a_hi, gm) + _dot(a_lo, gm)
    dl = a - mu
    q_hi, q_lo = _split_bf16(dl * dl)
    var = _dot(q_hi, gm) + _dot(q_lo, gm)
    y = dl * lax.rsqrt(var + EPS) * lg_ref[...] + lb_ref[...]
    o_ref[...] = (y * _sigmoid(y)).astype(BF16)


def _conv_call(a_glu, conv_w, conv_b, ln_g, ln_b, gmean, *, tl, seq, ctx_len, n_batch):
    rows, ch = a_glu.shape
    width = conv_w.shape[0]
    assert width // 2 <= HALO and tl % HALO == 0
    hb = tl // HALO
    last_halo = rows // HALO - 1
    return pl.pallas_call(
        functools.partial(_conv_kernel, tl=tl, width=width, lat_tiles=n_batch * seq // tl,
                          lat_tps=seq // tl, ctx_tps=ctx_len // tl, chunk=32),
        grid=(rows // tl,),
        in_specs=[pl.BlockSpec((HALO, ch), lambda i: (jnp.maximum(i * hb - 1, 0), 0)),
                  pl.BlockSpec((tl, ch), lambda i: (i, 0)),
                  pl.BlockSpec((HALO, ch), lambda i: (jnp.minimum((i + 1) * hb, last_halo), 0)),
                  pl.BlockSpec((width, ch), lambda i: (0, 0)),
                  pl.BlockSpec((1, ch), lambda i: (0, 0)),
                  pl.BlockSpec((1, ch), lambda i: (0, 0)),
                  pl.BlockSpec((1, ch), lambda i: (0, 0)),
                  pl.BlockSpec((ch, ch), lambda i: (0, 0))],
        out_specs=pl.BlockSpec((tl, ch), lambda i: (i, 0)),
        out_shape=jax.ShapeDtypeStruct((rows, ch), BF16),
        scratch_shapes=[pltpu.VMEM((tl + 2 * HALO, ch), F32), pltpu.VMEM((tl, ch), F32)],
        compiler_params=_cparams(1, 40),
        name="dwconv_groupln_swish",
    )(a_glu, a_glu, a_glu, conv_w, conv_b, ln_g, ln_b, gmean)


def _seqdft_kernel(c_ref, s_ref, a_ref, b_ref, o_ref):
    o_ref[...] = (_dot(c_ref[...], a_ref[...]) - _dot(s_ref[...], b_ref[...])).astype(BF16)


def _seqdft_call(ab, cmat, smat, *, length, row0, n_seq, fc, tm):
    seq0 = row0 // length
    return pl.pallas_call(
        _seqdft_kernel,
        grid=(n_seq, length // tm),
        in_specs=[pl.BlockSpec((tm, length), lambda b, i: (i, 0)),
                  pl.BlockSpec((tm, length), lambda b, i: (i, 0)),
                  pl.BlockSpec((length, fc), lambda b, i: (seq0 + b, 0)),
                  pl.BlockSpec((length, fc), lambda b, i: (seq0 + b, 1))],
        out_specs=pl.BlockSpec((tm, fc), lambda b, i: (b * (length // tm) + i, 0)),
        out_shape=jax.ShapeDtypeStruct((n_seq * length, fc), BF16),
        compiler_params=_cparams(2, 48),
        name="seq_dft",
    )(cmat, smat, ab, ab)


def _dft_tables(length, scale):
    if length <= 512:
        kn = np.outer(np.arange(length), np.arange(length)) % length
        ang = 2.0 * np.pi * kn / length
        return (jnp.asarray(np.cos(ang) * scale, BF16), jnp.asarray(np.sin(ang) * scale, BF16))
    r = 64
    assert length % r == 0
    k = np.arange(length)[:, None]
    alpha = 2.0 * np.pi * ((k * np.arange(length // r)[None, :] * r) % length) / length
    beta = 2.0 * np.pi * ((k * np.arange(r)[None, :]) % length) / length
    ca, sa = jnp.asarray(np.cos(alpha), F32)[:, :, None], jnp.asarray(np.sin(alpha), F32)[:, :, None]
    cb, sb = jnp.asarray(np.cos(beta) * scale, F32)[:, None, :], jnp.asarray(np.sin(beta) * scale, F32)[:, None, :]
    cmat = (ca * cb - sa * sb).reshape(length, length).astype(BF16)
    smat = (sa * cb + ca * sb).reshape(length, length).astype(BF16)
    return cmat, smat


def _outproj_kernel(*refs, n_in):
    ins, ws = refs[:n_in], refs[n_in:2 * n_in]
    x_ref, g1_ref, gate_ref, g2_ref, sh_ref, sc_ref, xo_ref, tok_ref = refs[2 * n_in:]
    y = _dot(ins[0][...], ws[0][...])
    for a_ref, w_ref in zip(ins[1:], ws[1:]):
        y = y + _dot(a_ref[...], w_ref[...])
    x1 = x_ref[...] + gate_ref[...] * _rms(y, g1_ref[...])
    xo_ref[...] = x1
    tok_ref[...] = _normmod(x1, g2_ref[...], sh_ref[...], sc_ref[...]).astype(BF16)


def _outproj_call(ins, ws, xs, g1, gate, g2, shift, scale, *, rows, tm, seq, n_batch):
    d = xs.shape[1]
    seg = _seg_map(tm, seq, n_batch)
    n_in = len(ins)
    row_spec = lambda a: pl.BlockSpec((tm, a.shape[1]), lambda i: (i, 0))
    full_spec = lambda a: pl.BlockSpec(a.shape, lambda i: (0, 0))
    return pl.pallas_call(
        functools.partial(_outproj_kernel, n_in=n_in),
        grid=(rows // tm,),
        in_specs=[row_spec(a) for a in ins] + [full_spec(w) for w in ws] + [
            row_spec(xs), full_spec(g1), pl.BlockSpec((None, 1, d), seg), full_spec(g2),
            pl.BlockSpec((None, 1, d), seg), pl.BlockSpec((None, 1, d), seg)],
        out_specs=[pl.BlockSpec((tm, d), lambda i: (i, 0)), pl.BlockSpec((tm, d), lambda i: (i, 0))],
        out_shape=[jax.ShapeDtypeStruct((rows, d), F32), jax.ShapeDtypeStruct((rows, d), BF16)],
        compiler_params=_cparams(1, 40),
        name="outproj_residual",
    )(*ins, *ws, xs, g1, gate, g2, shift, scale)


def _pick_max(cur, idx):
    mx = jnp.max(cur, axis=0, keepdims=True)
    first = jnp.min(jnp.where(cur == mx, idx, float(cur.shape[0])), axis=0, keepdims=True)
    return first, idx == first


def _router_kernel(x_ref, g_ref, sh_ref, sc_ref, wh_ref, wl_ref, rb_ref,
                   e_ref, gt_ref, pos_ref, cnt_ref, carry, *, n_exp):
    i = pl.program_id(0)

    @pl.when(i == 0)
    def _():
        carry[...] = jnp.zeros_like(carry)

    tok = _normmod(x_ref[...], g_ref[...], sh_ref[...], sc_ref[...])
    t_hi, t_lo = _split_bf16(tok)
    wh, wl = wh_ref[...], wl_ref[...]
    logits = _dot_nt(wh, t_hi) + _dot_nt(wh, t_lo) + _dot_nt(wl, t_hi)
    tm = logits.shape[1]
    scores = _sigmoid(logits)
    biased = scores + rb_ref[...]
    gsz = n_exp // N_GROUPS
    neg = -jnp.inf

    b3 = biased.reshape(N_GROUPS, gsz, tm)
    im = lax.broadcasted_iota(jnp.int32, b3.shape, 1).astype(F32)
    m1 = jnp.max(b3, axis=1, keepdims=True)
    i1 = jnp.min(jnp.where(b3 == m1, im, float(gsz)), axis=1, keepdims=True)
    m2 = jnp.max(jnp.where(im == i1, neg, b3), axis=1, keepdims=True)
    gscore = (m1 + m2).reshape(N_GROUPS, tm)

    ig = lax.broadcasted_iota(jnp.int32, gscore.shape, 0).astype(F32)
    gsel = jnp.zeros_like(gscore)
    cur = gscore
    for _ in range(TOPK_GROUPS):
        _, hit = _pick_max(cur, ig)
        gsel = jnp.where(hit, 1.0, gsel)
        cur = jnp.where(hit, neg, cur)
    gsel3 = jnp.broadcast_to(gsel.reshape(N_GROUPS, 1, tm), b3.shape)
    cur = jnp.where(gsel3 > 0.0, b3, neg).reshape(n_exp, tm)

    ie = lax.broadcasted_iota(jnp.int32, (n_exp, tm), 0).astype(F32)
    sel = jnp.zeros((n_exp, tm), F32)
    picks, raw = [], []
    for _ in range(TOP_K):
        first, hit = _pick_max(cur, ie)
        picks.append(first)
        raw.append(jnp.sum(jnp.where(hit, scores, 0.0), axis=0, keepdims=True))
        sel = jnp.where(hit, 1.0, sel)
        cur = jnp.where(hit, neg, cur)
    total = raw[0]
    for r in raw[1:]:
        total = total + r

    ri = lax.broadcasted_iota(jnp.int32, (tm, tm), 0)
    ci = lax.broadcasted_iota(jnp.int32, (tm, tm), 1)
    upper = jnp.where(ri < ci, 1.0, 0.0).astype(BF16)
    rank = _dot(sel.astype(BF16), upper) + carry[...]
    ranks = [jnp.sum(jnp.where(ie == p, rank, 0.0), axis=0, keepdims=True) for p in picks]
    carry[...] = carry[...] + jnp.sum(sel, axis=1, keepdims=True)

    pad = jnp.zeros((SUBLANES - TOP_K, tm), F32)
    e_ref[...] = jnp.concatenate(picks + [pad], axis=0).astype(jnp.int32)
    gt_ref[...] = jnp.concatenate([r / total * ROUTED_SCALE for r in raw] + [pad], axis=0)
    pos_ref[...] = jnp.concatenate(ranks + [pad], axis=0).astype(jnp.int32)
    cnt_ref[...] = jnp.broadcast_to(carry[...], cnt_ref.shape)


def _router_call(xs, g, shift, scale, w_hi_t, w_lo_t, rbias, *, rows, tm, seq, n_batch):
    d = xs.shape[1]
    n_exp = w_hi_t.shape[0]
    seg = _seg_map(tm, seq, n_batch)
    tok_rows = lambda dt: jax.ShapeDtypeStruct((SUBLANES, rows), dt)
    lane_spec = pl.BlockSpec((SUBLANES, tm), lambda i: (0, i))
    return pl.pallas_call(
        functools.partial(_router_kernel, n_exp=n_exp),
        grid=(rows // tm,),
        in_specs=[pl.BlockSpec((tm, d), lambda i: (i, 0)),
                  pl.BlockSpec((1, d), lambda i: (0, 0)),
                  pl.BlockSpec((None, 1, d), seg),
                  pl.BlockSpec((None, 1, d), seg),
                  pl.BlockSpec((n_exp, d), lambda i: (0, 0)),
                  pl.BlockSpec((n_exp, d), lambda i: (0, 0)),
                  pl.BlockSpec((n_exp, 1), lambda i: (0, 0))],
        out_specs=[lane_spec, lane_spec, lane_spec,
                   pl.BlockSpec((n_exp, LANES), lambda i: (0, 0))],
        out_shape=[tok_rows(jnp.int32), tok_rows(F32), tok_rows(jnp.int32),
                   jax.ShapeDtypeStruct((n_exp, LANES), F32)],
        scratch_shapes=[pltpu.VMEM((n_exp, 1), F32)],
        compiler_params=_cparams(1, 40),
        name="moe_router",
    )(xs, g, shift, scale, w_hi_t, w_lo_t, rbias)


def _expert_kernel(be_ref, nu_ref, xs_ref, wg_ref, wu_ref, wd_ref, ys_ref, wgu, wdn, *, ff):
    b = pl.program_id(0)
    changed = jnp.logical_or(b == 0, be_ref[b] != be_ref[jnp.maximum(b - 1, 0)])

    @pl.when(changed)
    def _():
        wgu[:, :ff] = wg_ref[...].astype(BF16)
        wgu[:, ff:] = wu_ref[...].astype(BF16)
        wdn[...] = wd_ref[...].astype(BF16)

    @pl.when(b < nu_ref[0])
    def _():
        gu = _dot(xs_ref[...], wgu[...])
        gate = gu[:, :ff]
        hmid = (gate * _sigmoid(gate) * gu[:, ff:]).astype(BF16)
        ys_ref[...] = _dot(hmid, wdn[...]).astype(BF16)


def _expert_call(block_e, n_used, xs, w_gate, w_up, w_down):
    n_rows, d = xs.shape
    ff = w_gate.shape[2]
    te = EXPERT_ROWS
    row_map = lambda b, be, nu: (jnp.minimum(b, nu[0] - 1), 0)
    grid_spec = pltpu.PrefetchScalarGridSpec(
        num_scalar_prefetch=2,
        grid=(n_rows // te,),
        in_specs=[pl.BlockSpec((te, d), row_map),
                  pl.BlockSpec((None, d, ff), lambda b, be, nu: (be[b], 0, 0)),
                  pl.BlockSpec((None, d, ff), lambda b, be, nu: (be[b], 0, 0)),
                  pl.BlockSpec((None, ff, d), lambda b, be, nu: (be[b], 0, 0))],
        out_specs=pl.BlockSpec((te, d), row_map),
        scratch_shapes=[pltpu.VMEM((d, 2 * ff), BF16), pltpu.VMEM((ff, d), BF16)])
    return pl.pallas_call(
        functools.partial(_expert_kernel, ff=ff),
        grid_spec=grid_spec,
        out_shape=jax.ShapeDtypeStruct((n_rows, d), BF16),
        compiler_params=_cparams(1, 40),
        name="moe_experts",
    )(block_e, n_used, xs, w_gate, w_up, w_down)


def _moe_out_kernel(tok_ref, routed_ref, x_ref, sgu_ref, sd_ref, g_ref, gate_ref, xo_ref, *, ff):
    gu = _dot(tok_ref[...], sgu_ref[...])
    gate = gu[:, :ff]
    hmid = (gate * _sigmoid(gate) * gu[:, ff:]).astype(BF16)
    y = routed_ref[...] + _dot(hmid, sd_ref[...])
    xo_ref[...] = x_ref[...] + gate_ref[...] * _rms(y, g_ref[...])


def _moe_out_call(tok, routed, xs, sgu, sd, g, gate, *, rows, tm, seq, n_batch):
    d = xs.shape[1]
    ff = sd.shape[0]
    seg = _seg_map(tm, seq, n_batch)
    row = lambda: pl.BlockSpec((tm, d), lambda i: (i, 0))
    return pl.pallas_call(
        functools.partial(_moe_out_kernel, ff=ff),
        grid=(rows // tm,),
        in_specs=[row(), row(), row(),
                  pl.BlockSpec((d, 2 * ff), lambda i: (0, 0)),
                  pl.BlockSpec((ff, d), lambda i: (0, 0)),
                  pl.BlockSpec((1, d), lambda i: (0, 0)),
                  pl.BlockSpec((None, 1, d), seg)],
        out_specs=row(),
        out_shape=jax.ShapeDtypeStruct((rows, d), F32),
        compiler_params=_cparams(1, 40),
        name="moe_shared_combine",
    )(tok, routed, xs, sgu, sd, g, gate)


def _moe_layer(xs, tok, g_in, shift, scale, g_out, gate, router_w, router_b, w_gate, w_up, w_down,
               sh_gate, sh_up, sh_down, *, rows, tm, seq, n_batch):
    n_exp = router_w.shape[1]
    te = EXPERT_ROWS
    w_t = router_w.T
    w_hi = w_t.astype(BF16)
    w_lo = (w_t - w_hi.astype(F32)).astype(BF16)
    e_tk, g_tk, pos_tk, counts = _router_call(xs, g_in, shift, scale, w_hi, w_lo,
                                              router_b.reshape(n_exp, 1),
                                              rows=rows, tm=tm, seq=seq, n_batch=n_batch)
    e_tk, g_tk, pos_tk = e_tk[:TOP_K], g_tk[:TOP_K], pos_tk[:TOP_K]
    cnt = counts[:, 0].astype(jnp.int32)
    padded = (cnt + te - 1) // te * te
    pad_ends = jnp.cumsum(padded)
    pad_starts = pad_ends - padded
    dest = pad_starts[e_tk] + pos_tk
    n_blocks = -(-(rows * TOP_K + n_exp * (te - 1)) // te)
    n_used = pad_ends[-1] // te
    blk = jnp.arange(n_blocks, dtype=jnp.int32)
    block_e = jnp.minimum(jnp.searchsorted(pad_ends, blk * te, side='right'), n_exp - 1).astype(jnp.int32)
    block_e = jnp.where(blk < n_used, block_e, block_e[n_used - 1])
    tok_id = jnp.broadcast_to(jnp.arange(rows, dtype=jnp.int32)[None, :], dest.shape)
    row_tok = jnp.zeros((n_blocks * te,), jnp.int32).at[dest.reshape(-1)].set(tok_id.reshape(-1))
    xg = tok[row_tok]
    ys = _expert_call(block_e, n_used.reshape(1).astype(jnp.int32), xg, w_gate, w_up, w_down)
    routed = jnp.sum(ys[dest].astype(F32) * g_tk[:, :, None], axis=0)
    sgu = jnp.concatenate([sh_gate, sh_up], axis=1).astype(BF16)
    return _moe_out_call(tok, routed, xs, sgu, sh_down.astype(BF16), g_out, gate,
                         rows=rows, tm=tm, seq=seq, n_batch=n_batch)


def _rope(t, cos, sin):
    half = DA_QK_DIM // 2
    up = pltpu.roll(t, shift=LANES - half, axis=1)
    dn = pltpu.roll(t, shift=half, axis=1)
    lane = lax.broadcasted_iota(jnp.int32, t.shape, 1) % DA_QK_DIM
    return t * cos + jnp.where(lane < half, -up, dn) * sin


def _qkv_kernel(x_ref, g_ref, sh_ref, sc_ref, w_ref, cos_ref, sin_ref, q_ref, k_ref, v_ref, *, qkw):
    h = _normmod(x_ref[...], g_ref[...], sh_ref[...], sc_ref[...]).astype(BF16)
    qkv = _dot(h, w_ref[...])
    cos, sin = cos_ref[...], sin_ref[...]
    q_scale = DA_QK_DIM ** -0.5
    for hb in range(qkw // LANES):
        lo, hi = hb * LANES, (hb + 1) * LANES
        q_ref[:, lo:hi] = (_rope(qkv[:, lo:hi], cos, sin) * q_scale).astype(BF16)
        k_ref[:, lo:hi] = _rope(qkv[:, qkw + lo:qkw + hi], cos, sin).astype(BF16)
    v_ref[...] = qkv[:, 2 * qkw:].astype(BF16)


def _qkv_call(xs, g, shift, scale, w_qkv, cos_t, sin_t, *, qkw, tm, seq, ctx_len, n_batch):
    rows, d = xs.shape
    n = w_qkv.shape[1]
    vw = n - 2 * qkw
    lat_tiles, lat_tps, ctx_tps = n_batch * seq // tm, seq // tm, ctx_len // tm
    seg = _seg_map(tm, seq, n_batch)

    def kv_map(i):
        c = i - lat_tiles
        is_lat = i < lat_tiles
        return (jnp.where(is_lat, i // lat_tps, c // ctx_tps),
                jnp.where(is_lat, ctx_tps + i % lat_tps, c % ctx_tps), 0)

    rope_map = lambda i: (jnp.where(i < lat_tiles, i % lat_tps, lat_tps), 0)
    return pl.pallas_call(
        functools.partial(_qkv_kernel, qkw=qkw),
        grid=(rows // tm,),
        in_specs=[pl.BlockSpec((tm, d), lambda i: (i, 0)),
                  pl.BlockSpec((1, d), lambda i: (0, 0)),
                  pl.BlockSpec((None, 1, d), seg),
                  pl.BlockSpec((None, 1, d), seg),
                  pl.BlockSpec((d, n), lambda i: (0, 0)),
                  pl.BlockSpec((tm, LANES), rope_map),
                  pl.BlockSpec((tm, LANES), rope_map)],
        out_specs=[pl.BlockSpec((tm, qkw), lambda i: (i, 0)),
                   pl.BlockSpec((None, tm, qkw), kv_map),
                   pl.BlockSpec((None, tm, vw), kv_map)],
        out_shape=[jax.ShapeDtypeStruct((rows, qkw), BF16),
                   jax.ShapeDtypeStruct((n_batch, ctx_len + seq, qkw), BF16),
                   jax.ShapeDtypeStruct((n_batch, ctx_len + seq, vw), BF16)],
        compiler_params=_cparams(1, 48),
        name="qkv_rope",
    )(xs, g, shift, scale, w_qkv, cos_t, sin_t)


def _rope_tables(seq, tm):
    pos = np.arange(seq)
    n_freq = DA_QK_DIM // 4
    inv = np.power(ROPE_BASE, -np.arange(n_freq, dtype=np.float32) / n_freq).astype(np.float32)
    row = (pos // GRID_W).astype(np.float32)[:, None] * inv
    col = (pos % GRID_W).astype(np.float32)[:, None] * inv
    ang = np.concatenate([row, col], axis=-1).astype(np.float32)
    ang = np.tile(ang, (1, LANES // ang.shape[1]))
    cos = np.concatenate([np.cos(ang), np.ones((tm, LANES))], axis=0)
    sin = np.concatenate([np.sin(ang), np.zeros((tm, LANES))], axis=0)
    return jnp.asarray(cos, F32), jnp.asarray(sin, F32)


def _attn_kernel(lp_ref, q_ref, k_ref, v_ref, sg_ref, o_ref, vext, *, tq, lam_init):
    @pl.when(pl.program_id(2) == 0)
    def _():
        vext[:, :LANES] = v_ref[...]
        vext[:, LANES:] = jnp.ones((vext.shape[0], LANES), BF16)

    lp = lp_ref[...]
    lam = (jnp.exp(jnp.sum(lp[0:1] * lp[1:2], axis=1, keepdims=True))
           - jnp.exp(jnp.sum(lp[2:3] * lp[3:4], axis=1, keepdims=True)) + lam_init)
    q = q_ref[...]
    lane = lax.broadcasted_iota(jnp.int32, q.shape, 1)
    zero = jnp.zeros_like(q)
    qq = jnp.concatenate([jnp.where(lane < DA_QK_DIM, q, zero),
                          jnp.where(lane >= DA_QK_DIM, q, zero)], axis=0)
    s = _dot_nt(qq, k_ref[...])
    p = jnp.exp(s - jnp.max(s, axis=-1, keepdims=True)).astype(BF16)
    oe = _dot(p, vext[...])
    on = oe[:, :LANES] / oe[:, LANES:LANES + 1]
    o = on[:tq] - lam * on[tq:]
    o_ref[...] = (_rms(o, sg_ref[...]) * (1.0 - lam_init)).astype(BF16)


def _attn_call(lam_p, q, k_all, v_all, subln_g, *, tq, seq, n_batch, lam_init):
    n_heads = q.shape[1] // LANES
    lk = k_all.shape[1]
    qt = seq // tq
    return pl.pallas_call(
        functools.partial(_attn_kernel, tq=tq, lam_init=lam_init),
        grid=(n_batch, n_heads, qt),
        in_specs=[pl.BlockSpec(lam_p.shape, lambda b, h, i: (0, 0)),
                  pl.BlockSpec((tq, LANES), lambda b, h, i: (b * qt + i, h)),
                  pl.BlockSpec((None, lk, LANES), lambda b, h, i: (b, 0, h)),
                  pl.BlockSpec((None, lk, LANES), lambda b, h, i: (b, 0, h)),
                  pl.BlockSpec((1, LANES), lambda b, h, i: (0, 0))],
        out_specs=pl.BlockSpec((tq, LANES), lambda b, h, i: (b * qt + i, h)),
        out_shape=jax.ShapeDtypeStruct((n_batch * seq, n_heads * LANES), BF16),
        scratch_shapes=[pltpu.VMEM((lk, 2 * LANES), BF16)],
        compiler_params=_cparams(3, 48),
        name="diff_attention",
    )(lam_p, q, k_all, v_all, subln_g)


def kernel(x, c, ctx, c_ctx, ada_w, ada_b, norm_g, cf_w_in, cf_conv_w, cf_conv_b, cf_ln_g, cf_ln_b,
           cf_w_out, da_w_qkv, da_lambda, da_subln_g, da_w_out, moe_router_w, moe_router_b,
           moe_w_gate, moe_w_up, moe_w_down, moe_sh_gate, moe_sh_up, moe_sh_down):
    n_batch, seq, d = x.shape
    ctx_len = ctx.shape[1]
    depth = ada_w.shape[0]
    assert depth == 2 and n_batch + 1 <= N_SEG
    assert DA_QK_DIM * 2 == LANES and da_subln_g.shape[1] == LANES
    rows_lat, rows_ctx = n_batch * seq, n_batch * ctx_len
    tm = 512
    tq = 256
    assert seq % tm == 0 and rows_ctx % tm == 0 and seq % tq == 0 and ctx_len % tq == 0

    xs = jnp.concatenate([x.reshape(rows_lat, d), ctx.reshape(rows_ctx, d)], axis=0)
    cond = jnp.concatenate([c, c_ctx[None, :], jnp.zeros((N_SEG - n_batch - 1, d), F32)], axis=0)
    mods = _mods_call(cond, ada_w, ada_b)
    mod = lambda layer, k: mods[layer, :, k * d:(k + 1) * d][:, None, :]
    gain = lambda layer, k: norm_g[layer, k][None, :]
    common = dict(seq=seq, n_batch=n_batch)

    cc = cf_conv_w.shape[-1]
    fc = cf_w_in.shape[2] - 2 * cc
    gw = fc // FOURIER_GROUPS
    ch_ang = 2.0 * np.pi * (np.outer(np.arange(gw), np.arange(gw)) % gw) / gw
    cs = jnp.asarray(np.concatenate([np.cos(ch_ang), np.sin(ch_ang)], axis=1) / math.sqrt(gw), BF16)
    a_glu, ab = _inproj_call(xs, gain(0, 0), mod(0, 0), mod(0, 1), cf_w_in[0].astype(BF16), cs,
                             cc=cc, gw=gw, tm=tm, **common)
    gsz = cc // CONV_GROUPS
    gid = np.arange(cc) // gsz
    gmean = jnp.asarray((gid[:, None] == gid[None, :]) / gsz, BF16)
    a_act = _conv_call(a_glu, cf_conv_w[0][:, 0, :], cf_conv_b[0][None, :], cf_ln_g[0][None, :],
                       cf_ln_b[0][None, :], gmean, tl=tq, seq=seq, ctx_len=ctx_len, n_batch=n_batch)
    c_lat, s_lat = _dft_tables(seq, 1.0 / math.sqrt(seq))
    c_ctx_m, s_ctx_m = _dft_tables(ctx_len, 1.0 / math.sqrt(ctx_len))
    fr = jnp.concatenate([
        _seqdft_call(ab, c_lat, s_lat, length=seq, row0=0, n_seq=n_batch, fc=fc, tm=min(seq, 256)),
        _seqdft_call(ab, c_ctx_m, s_ctx_m, length=ctx_len, row0=rows_lat, n_seq=n_batch, fc=fc,
                     tm=min(ctx_len, 256))], axis=0)
    w_out = cf_w_out[0].astype(BF16)
    xs, tok = _outproj_call([a_act, fr], [w_out[:cc], w_out[cc:]], xs, gain(0, 1), mod(0, 2),
                            gain(0, 2), mod(0, 3), mod(0, 4), rows=rows_lat + rows_ctx, tm=tm, **common)
    xs = _moe_layer(xs, tok, gain(0, 2), mod(0, 3), mod(0, 4), gain(0, 3), mod(0, 5),
                    moe_router_w[0], moe_router_b[0], moe_w_gate[0], moe_w_up[0], moe_w_down[0],
                    moe_sh_gate[0], moe_sh_up[0], moe_sh_down[0], rows=rows_lat + rows_ctx, tm=tm, **common)

    qkw = DA_HEADS * 2 * DA_QK_DIM
    lam_init = 0.8 - 0.6 * math.exp(-0.3 * 1)
    cos_t, sin_t = _rope_tables(seq, tq)
    q, k_all, v_all = _qkv_call(xs, gain(1, 0), mod(1, 0), mod(1, 1), da_w_qkv[0].astype(BF16),
                                cos_t, sin_t, qkw=qkw, tm=tq, ctx_len=ctx_len, **common)
    o = _attn_call(da_lambda[0], q, k_all, v_all, da_subln_g[0][None, :], tq=tq, seq=seq,
                   n_batch=n_batch, lam_init=lam_init)
    xs, tok = _outproj_call([o], [da_w_out[0].astype(BF16)], xs, gain(1, 1), mod(1, 2),
                            gain(1, 2), mod(1, 3), mod(1, 4), rows=rows_lat, tm=tm, **common)
    xs = _moe_layer(xs, tok, gain(1, 2), mod(1, 3), mod(1, 4), gain(1, 3), mod(1, 5),
                    moe_router_w[1], moe_router_b[1], moe_w_gate[1], moe_w_up[1], moe_w_down[1],
                    moe_sh_gate[1], moe_sh_up[1], moe_sh_down[1], rows=rows_lat, tm=tm, **common)
    return xs.reshape(n_batch, seq, d)
```

```python
import functools
import math

import numpy as np
import jax
import jax.numpy as jnp
from jax import lax
from jax.experimental import pallas as pl
from jax.experimental.pallas import tpu as pltpu
from jax.experimental.pallas import tpu_sc as plsc

F32 = jnp.float32
BF16 = jnp.bfloat16

EPS = 1e-6
GRID_W = 64
CONV_GROUPS = 8
FOURIER_GROUPS = 4
DA_HEADS = 8
DA_QK_DIM = 64
ROPE_BASE = 10000.0
N_GROUPS = 8
TOPK_GROUPS = 4
TOP_K = 6
ROUTED_SCALE = 2.5

LANES = 128
SUBLANES = 8
N_SEG = 16
HALO = 16
EXPERT_ROWS = 256
MIB = 1024 * 1024


def _cparams(n_axes, vmem_mib):
    return pltpu.CompilerParams(dimension_semantics=("arbitrary",) * n_axes,
                                vmem_limit_bytes=vmem_mib * MIB)


def _sigmoid(v):
    return 1.0 / (1.0 + jnp.exp(-v))


def _rms(v, g):
    return v * lax.rsqrt(jnp.mean(v * v, axis=-1, keepdims=True) + EPS) * g


def _normmod(v, g, shift, scale):
    return _rms(v, g) * (1.0 + scale) + shift


def _split_bf16(v):
    hi = v.astype(BF16)
    lo = (v - hi.astype(F32)).astype(BF16)
    return hi, lo


def _bf16_high_bits(v):
    b = lax.bitcast_convert_type(v, jnp.uint32)
    r = b + jnp.uint32(0x7FFF) + ((b >> 16) & jnp.uint32(1))
    return r & jnp.uint32(0xFFFF0000)


def _pack_halves(v):
    half = v.shape[1] // 2
    word = _bf16_high_bits(v[:, half:]) | (_bf16_high_bits(v[:, :half]) >> 16)
    return lax.bitcast_convert_type(word, jnp.int32)


def _unpack_halves(w):
    u = lax.bitcast_convert_type(w, jnp.uint32)
    lo = lax.bitcast_convert_type(u << 16, F32)
    hi = lax.bitcast_convert_type(u & jnp.uint32(0xFFFF0000), F32)
    return lo, hi


def _dot(a, b):
    return jnp.dot(a, b, preferred_element_type=F32)


def _dot_nt(a, b):
    return lax.dot_general(a, b, (((1,), (1,)), ((), ())), preferred_element_type=F32)


def _mods_kernel(c_ref, w_ref, b_ref, o_ref):
    cv = c_ref[...]
    o_ref[...] = _dot(cv * _sigmoid(cv), w_ref[...]) + b_ref[...]


def _mods_call(cond, ada_w, ada_b):
    depth, d, n = ada_w.shape
    tn = n // 4
    return pl.pallas_call(
        _mods_kernel,
        grid=(depth, n // tn),
        in_specs=[pl.BlockSpec((N_SEG, d), lambda l, j: (0, 0)),
                  pl.BlockSpec((None, d, tn), lambda l, j: (l, 0, j)),
                  pl.BlockSpec((None, 1, tn), lambda l, j: (l, 0, j))],
        out_specs=pl.BlockSpec((None, N_SEG, tn), lambda l, j: (l, 0, j)),
        out_shape=jax.ShapeDtypeStruct((depth, N_SEG, n), F32),
        compiler_params=_cparams(2, 40),
        name="adaln_mods",
    )(cond, ada_w, ada_b.reshape(depth, 1, n))


def _inproj_kernel(x_ref, g_ref, sh_ref, sc_ref, w_ref, cs_ref, a_ref, ab_ref, *, cc, gw):
    h = _normmod(x_ref[...], g_ref[...], sh_ref[...], sc_ref[...]).astype(BF16)
    u = _dot(h, w_ref[...])
    a_ref[...] = u[:, :cc] * _sigmoid(u[:, cc:2 * cc])
    f = u[:, 2 * cc:].astype(BF16)
    n_g = f.shape[1] // gw
    parts = [_dot(f[:, g * gw:(g + 1) * gw], cs_ref[...]) for g in range(n_g)]
    cos_part = [p[:, :gw] for p in parts]
    sin_part = [p[:, gw:] for p in parts]
    ab_ref[...] = jnp.concatenate(cos_part + sin_part, axis=-1).astype(BF16)


def _seg_map(tm, seq, n_batch):
    return lambda i: (jnp.minimum(i * tm // seq, n_batch), 0, 0)


def _inproj_call(xs, g, shift, scale, w_in, cs, *, cc, gw, tm, seq, n_batch):
    rows, d = xs.shape
    n = w_in.shape[1]
    fc = n - 2 * cc
    seg = _seg_map(tm, seq, n_batch)
    return pl.pallas_call(
        functools.partial(_inproj_kernel, cc=cc, gw=gw),
        grid=(rows // tm,),
        in_specs=[pl.BlockSpec((tm, d), lambda i: (i, 0)),
                  pl.BlockSpec((1, d), lambda i: (0, 0)),
                  pl.BlockSpec((None, 1, d), seg),
                  pl.BlockSpec((None, 1, d), seg),
                  pl.BlockSpec((d, n), lambda i: (0, 0)),
                  pl.BlockSpec((gw, 2 * gw), lambda i: (0, 0))],
        out_specs=[pl.BlockSpec((tm, cc), lambda i: (i, 0)),
                   pl.BlockSpec((tm, 2 * fc), lambda i: (i, 0))],
        out_shape=[jax.ShapeDtypeStruct((rows, cc), F32),
                   jax.ShapeDtypeStruct((rows, 2 * fc), BF16)],
        compiler_params=_cparams(1, 40),
        name="inproj_glu_chdft",
    )(xs, g, shift, scale, w_in, cs)


def _conv_kernel(prev_ref, main_ref, next_ref, w_ref, cb_ref, lg_ref, lb_ref, gm_ref, o_ref,
                 buf, cv, *, tl, width, lat_tiles, lat_tps, ctx_tps, chunk):
    i = pl.program_id(0)
    is_lat = i < lat_tiles
    tps = jnp.where(is_lat, lat_tps, ctx_tps)
    j = jnp.where(is_lat, i, i - lat_tiles) % tps
    zero = jnp.zeros((HALO, buf.shape[1]), F32)
    buf[0:HALO, :] = jnp.where(j > 0, prev_ref[...], zero)
    buf[HALO:HALO + tl, :] = main_ref[...]
    buf[HALO + tl:HALO + tl + HALO, :] = jnp.where(j < tps - 1, next_ref[...], zero)
    base = HALO - width // 2
    for r0 in range(0, tl, chunk):
        acc = buf[base + r0:base + r0 + chunk, :] * w_ref[0:1, :]
        for t in range(1, width):
            acc = acc + buf[base + r0 + t:base + r0 + t + chunk, :] * w_ref[t:t + 1, :]
        cv[r0:r0 + chunk, :] = acc
    a = cv[...] + cb_ref[...]
    gm = gm_ref[...]
    a_hi, a_lo = _split_bf16(a)
    mu = _dot(a_hi, gm) + _dot(a_lo, gm)
    dl = a - mu
    q_hi, q_lo = _split_bf16(dl * dl)
    var = _dot(q_hi, gm) + _dot(q_lo, gm)
    y = dl * lax.rsqrt(var + EPS) * lg_ref[...] + lb_ref[...]
    o_ref[...] = (y * _sigmoid(y)).astype(BF16)


def _conv_call(a_glu, conv_w, conv_b, ln_g, ln_b, gmean, *, tl, seq, ctx_len, n_batch):
    rows, ch = a_glu.shape
    width = conv_w.shape[0]
    assert width // 2 <= HALO and tl % HALO == 0
    hb = tl // HALO
    last_halo = rows // HALO - 1
    return pl.pallas_call(
        functools.partial(_conv_kernel, tl=tl, width=width, lat_tiles=n_batch * seq // tl,
                          lat_tps=seq // tl, ctx_tps=ctx_len // tl, chunk=32),
        grid=(rows // tl,),
        in_specs=[pl.BlockSpec((HALO, ch), lambda i: (jnp.maximum(i * hb - 1, 0), 0)),
                  pl.BlockSpec((tl, ch), lambda i: (i, 0)),
                  pl.BlockSpec((HALO, ch), lambda i: (jnp.minimum((i + 1) * hb, last_halo), 0)),
                  pl.BlockSpec((width, ch), lambda i: (0, 0)),
                  pl.BlockSpec((1, ch), lambda i: (0, 0)),
                  pl.BlockSpec((1, ch), lambda i: (0, 0)),
                  pl.BlockSpec((1, ch), lambda i: (0, 0)),
                  pl.BlockSpec((ch, ch), lambda i: (0, 0))],
        out_specs=pl.BlockSpec((tl, ch), lambda i: (i, 0)),
        out_shape=jax.ShapeDtypeStruct((rows, ch), BF16),
        scratch_shapes=[pltpu.VMEM((tl + 2 * HALO, ch), F32), pltpu.VMEM((tl, ch), F32)],
        compiler_params=_cparams(1, 40),
        name="dwconv_groupln_swish",
    )(a_glu, a_glu, a_glu, conv_w, conv_b, ln_g, ln_b, gmean)


def _seqdft_kernel(c_ref, s_ref, a_ref, b_ref, o_ref):
    o_ref[...] = (_dot(c_ref[...], a_ref[...]) - _dot(s_ref[...], b_ref[...])).astype(BF16)


def _seqdft_call(ab, cmat, smat, *, length, row0, n_seq, fc, tm):
    seq0 = row0 // length
    return pl.pallas_call(
        _seqdft_kernel,
        grid=(n_seq, length // tm),
        in_specs=[pl.BlockSpec((tm, length), lambda b, i: (i, 0)),
                  pl.BlockSpec((tm, length), lambda b, i: (i, 0)),
                  pl.BlockSpec((length, fc), lambda b, i: (seq0 + b, 0)),
                  pl.BlockSpec((length, fc), lambda b, i: (seq0 + b, 1))],
        out_specs=pl.BlockSpec((tm, fc), lambda b, i: (b * (length // tm) + i, 0)),
        out_shape=jax.ShapeDtypeStruct((n_seq * length, fc), BF16),
        compiler_params=_cparams(2, 48),
        name="seq_dft",
    )(cmat, smat, ab, ab)


def _dft_tables(length, scale):
    if length <= 512:
        kn = np.outer(np.arange(length), np.arange(length)) % length
        ang = 2.0 * np.pi * kn / length
        return (jnp.asarray(np.cos(ang) * scale, BF16), jnp.asarray(np.sin(ang) * scale, BF16))
    r = 64
    assert length % r == 0
    k = np.arange(length)[:, None]
    alpha = 2.0 * np.pi * ((k * np.arange(length // r)[None, :] * r) % length) / length
    beta = 2.0 * np.pi * ((k * np.arange(r)[None, :]) % length) / length
    ca, sa = jnp.asarray(np.cos(alpha), F32)[:, :, None], jnp.asarray(np.sin(alpha), F32)[:, :, None]
    cb, sb = jnp.asarray(np.cos(beta) * scale, F32)[:, None, :], jnp.asarray(np.sin(beta) * scale, F32)[:, None, :]
    cmat = (ca * cb - sa * sb).reshape(length, length).astype(BF16)
    smat = (sa * cb + ca * sb).reshape(length, length).astype(BF16)
    return cmat, smat


def _outproj_kernel(*refs, n_in):
    ins, ws = refs[:n_in], refs[n_in:2 * n_in]
    x_ref, g1_ref, gate_ref, g2_ref, sh_ref, sc_ref, xo_ref, tok_ref, tokp_ref = refs[2 * n_in:]
    y = _dot(ins[0][...], ws[0][...])
    for a_ref, w_ref in zip(ins[1:], ws[1:]):
        y = y + _dot(a_ref[...], w_ref[...])
    x1 = x_ref[...] + gate_ref[...] * _rms(y, g1_ref[...])
    xo_ref[...] = x1
    tok = _normmod(x1, g2_ref[...], sh_ref[...], sc_ref[...])
    tok_ref[...] = tok.astype(BF16)
    tokp_ref[...] = _pack_halves(tok)


def _outproj_call(ins, ws, xs, g1, gate, g2, shift, scale, *, rows, tm, seq, n_batch):
    d = xs.shape[1]
    seg = _seg_map(tm, seq, n_batch)
    n_in = len(ins)
    row_spec = lambda a: pl.BlockSpec((tm, a.shape[1]), lambda i: (i, 0))
    full_spec = lambda a: pl.BlockSpec(a.shape, lambda i: (0, 0))
    return pl.pallas_call(
        functools.partial(_outproj_kernel, n_in=n_in),
        grid=(rows // tm,),
        in_specs=[row_spec(a) for a in ins] + [full_spec(w) for w in ws] + [
            row_spec(xs), full_spec(g1), pl.BlockSpec((None, 1, d), seg), full_spec(g2),
            pl.BlockSpec((None, 1, d), seg), pl.BlockSpec((None, 1, d), seg)],
        out_specs=[pl.BlockSpec((tm, d), lambda i: (i, 0)), pl.BlockSpec((tm, d), lambda i: (i, 0)),
                   pl.BlockSpec((tm, d // 2), lambda i: (i, 0))],
        out_shape=[jax.ShapeDtypeStruct((rows, d), F32), jax.ShapeDtypeStruct((rows, d), BF16),
                   jax.ShapeDtypeStruct((rows, d // 2), jnp.int32)],
        compiler_params=_cparams(1, 40),
        name="outproj_residual",
    )(*ins, *ws, xs, g1, gate, g2, shift, scale)


def _pick_max(cur, idx):
    mx = jnp.max(cur, axis=0, keepdims=True)
    first = jnp.min(jnp.where(cur == mx, idx, float(cur.shape[0])), axis=0, keepdims=True)
    return first, idx == first


def _router_kernel(x_ref, g_ref, sh_ref, sc_ref, wh_ref, wl_ref, rb_ref,
                   e_ref, gt_ref, pos_ref, cnt_ref, carry, *, n_exp):
    i = pl.program_id(0)

    @pl.when(i == 0)
    def _():
        carry[...] = jnp.zeros_like(carry)

    tok = _normmod(x_ref[...], g_ref[...], sh_ref[...], sc_ref[...])
    t_hi, t_lo = _split_bf16(tok)
    wh, wl = wh_ref[...], wl_ref[...]
    logits = _dot_nt(wh, t_hi) + _dot_nt(wh, t_lo) + _dot_nt(wl, t_hi)
    tm = logits.shape[1]
    scores = _sigmoid(logits)
    biased = scores + rb_ref[...]
    gsz = n_exp // N_GROUPS
    neg = -jnp.inf

    b3 = biased.reshape(N_GROUPS, gsz, tm)
    im = lax.broadcasted_iota(jnp.int32, b3.shape, 1).astype(F32)
    m1 = jnp.max(b3, axis=1, keepdims=True)
    i1 = jnp.min(jnp.where(b3 == m1, im, float(gsz)), axis=1, keepdims=True)
    m2 = jnp.max(jnp.where(im == i1, neg, b3), axis=1, keepdims=True)
    gscore = (m1 + m2).reshape(N_GROUPS, tm)

    ig = lax.broadcasted_iota(jnp.int32, gscore.shape, 0).astype(F32)
    gsel = jnp.zeros_like(gscore)
    cur = gscore
    for _ in range(TOPK_GROUPS):
        _, hit = _pick_max(cur, ig)
        gsel = jnp.where(hit, 1.0, gsel)
        cur = jnp.where(hit, neg, cur)
    gsel3 = jnp.broadcast_to(gsel.reshape(N_GROUPS, 1, tm), b3.shape)
    cur = jnp.where(gsel3 > 0.0, b3, neg).reshape(n_exp, tm)

    ie = lax.broadcasted_iota(jnp.int32, (n_exp, tm), 0).astype(F32)
    sel = jnp.zeros((n_exp, tm), F32)
    picks, raw = [], []
    for _ in range(TOP_K):
        first, hit = _pick_max(cur, ie)
        picks.append(first)
        raw.append(jnp.sum(jnp.where(hit, scores, 0.0), axis=0, keepdims=True))
        sel = jnp.where(hit, 1.0, sel)
        cur = jnp.where(hit, neg, cur)
    total = raw[0]
    for r in raw[1:]:
        total = total + r

    ri = lax.broadcasted_iota(jnp.int32, (tm, tm), 0)
    ci = lax.broadcasted_iota(jnp.int32, (tm, tm), 1)
    upper = jnp.where(ri < ci, 1.0, 0.0).astype(BF16)
    rank = _dot(sel.astype(BF16), upper) + carry[...]
    ranks = [jnp.sum(jnp.where(ie == p, rank, 0.0), axis=0, keepdims=True) for p in picks]
    carry[...] = carry[...] + jnp.sum(sel, axis=1, keepdims=True)

    pad = jnp.zeros((SUBLANES - TOP_K, tm), F32)
    e_ref[...] = jnp.concatenate(picks + [pad], axis=0).astype(jnp.int32)
    gt_ref[...] = jnp.concatenate([r / total * ROUTED_SCALE for r in raw] + [pad], axis=0)
    pos_ref[...] = jnp.concatenate(ranks + [pad], axis=0).astype(jnp.int32)
    cnt_ref[...] = jnp.broadcast_to(carry[...], cnt_ref.shape)


def _router_call(xs, g, shift, scale, w_hi_t, w_lo_t, rbias, *, rows, tm, seq, n_batch):
    d = xs.shape[1]
    n_exp = w_hi_t.shape[0]
    seg = _seg_map(tm, seq, n_batch)
    tok_rows = lambda dt: jax.ShapeDtypeStruct((SUBLANES, rows), dt)
    lane_spec = pl.BlockSpec((SUBLANES, tm), lambda i: (0, i))
    return pl.pallas_call(
        functools.partial(_router_kernel, n_exp=n_exp),
        grid=(rows // tm,),
        in_specs=[pl.BlockSpec((tm, d), lambda i: (i, 0)),
                  pl.BlockSpec((1, d), lambda i: (0, 0)),
                  pl.BlockSpec((None, 1, d), seg),
                  pl.BlockSpec((None, 1, d), seg),
                  pl.BlockSpec((n_exp, d), lambda i: (0, 0)),
                  pl.BlockSpec((n_exp, d), lambda i: (0, 0)),
                  pl.BlockSpec((n_exp, 1), lambda i: (0, 0))],
        out_specs=[lane_spec, lane_spec, lane_spec,
                   pl.BlockSpec((n_exp, LANES), lambda i: (0, 0))],
        out_shape=[tok_rows(jnp.int32), tok_rows(F32), tok_rows(jnp.int32),
                   jax.ShapeDtypeStruct((n_exp, LANES), F32)],
        scratch_shapes=[pltpu.VMEM((n_exp, 1), F32)],
        compiler_params=_cparams(1, 40),
        name="moe_router",
    )(xs, g, shift, scale, w_hi_t, w_lo_t, rbias)


def _expert_kernel(be_ref, nv_ref, nu_ref, xs_ref, wg_ref, wu_ref, wd_ref, ys_ref, wgu, wdn, *, ff):
    b = pl.program_id(0)
    changed = jnp.logical_or(b == 0, be_ref[b] != be_ref[jnp.maximum(b - 1, 0)])

    @pl.when(changed)
    def _():
        wgu[:, :ff] = wg_ref[...].astype(BF16)
        wgu[:, ff:] = wu_ref[...].astype(BF16)
        wdn[...] = wd_ref[...].astype(BF16)

    @pl.when(b < nu_ref[0])
    def _():
        xw = xs_ref[...]
        row = lax.broadcasted_iota(jnp.int32, xw.shape, 0)
        lo, hi = _unpack_halves(jnp.where(row < nv_ref[b], xw, 0))
        half = lo.shape[1]
        gu = _dot(lo.astype(BF16), wgu[:half, :]) + _dot(hi.astype(BF16), wgu[half:, :])
        gate = gu[:, :ff]
        hmid = (gate * _sigmoid(gate) * gu[:, ff:]).astype(BF16)
        ys_ref[...] = _pack_halves(_dot(hmid, wdn[...]))


def _expert_call(block_e, n_valid, n_used, xs, w_gate, w_up, w_down):
    n_rows, dh = xs.shape
    d = 2 * dh
    ff = w_gate.shape[2]
    te = EXPERT_ROWS
    row_map = lambda b, be, nv, nu: (jnp.minimum(b, nu[0] - 1), 0)
    w_map = lambda b, be, nv, nu: (be[b], 0, 0)
    grid_spec = pltpu.PrefetchScalarGridSpec(
        num_scalar_prefetch=3,
        grid=(n_rows // te,),
        in_specs=[pl.BlockSpec((te, dh), row_map),
                  pl.BlockSpec((None, d, ff), w_map),
                  pl.BlockSpec((None, d, ff), w_map),
                  pl.BlockSpec((None, ff, d), w_map)],
        out_specs=pl.BlockSpec((te, dh), row_map),
        scratch_shapes=[pltpu.VMEM((d, 2 * ff), BF16), pltpu.VMEM((ff, d), BF16)])
    return pl.pallas_call(
        functools.partial(_expert_kernel, ff=ff),
        grid_spec=grid_spec,
        out_shape=jax.ShapeDtypeStruct((n_rows, dh), jnp.int32),
        compiler_params=_cparams(1, 40),
        name="moe_experts",
    )(block_e, n_valid, n_used, xs, w_gate, w_up, w_down)


SC_WINDOW = 128
SC_SPLIT = 2


def _sc_mesh():
    return plsc.VectorSubcoreMesh(core_axis_name="core", subcore_axis_name="subcore")


def _sc_scatter_rows(rows, dests, n_out):
    n, width = rows.shape
    n_k = len(dests)
    assert n % SC_WINDOW == 0

    @functools.partial(pl.kernel, out_type=jax.ShapeDtypeStruct((n_out, width), rows.dtype),
                       mesh=_sc_mesh(), name="sc_dispatch_rows")
    def scatter_kernel(x_hbm, *refs):
        idx_hbm, o_hbm = refs[:n_k], refs[n_k]

        def body(x_vmem, *idx_vmem):
            for iv in idx_vmem:
                pltpu.sync_copy(x_vmem, o_hbm.at[iv.at[0]])

        pltpu.emit_pipeline(
            body,
            grid=(n // SC_WINDOW,),
            in_specs=[pl.BlockSpec((SC_WINDOW, width), lambda i: (i, 0))]
            + [pl.BlockSpec((1, SC_WINDOW), lambda i: (0, i))] * n_k,
            out_specs=[],
            core_axis_name=("core", "subcore"),
            dimension_semantics=(pltpu.PARALLEL,),
        )(x_hbm, *idx_hbm)

    return scatter_kernel(rows, *dests)


def _sc_gather_rows(table, idx):
    n = idx.shape[1]
    width = table.shape[1]
    assert n % SC_WINDOW == 0

    @functools.partial(pl.kernel, out_type=jax.ShapeDtypeStruct((n, width), table.dtype),
                       mesh=_sc_mesh(), name="sc_collect_rows")
    def gather_kernel(t_hbm, i_hbm, o_hbm):
        def body(i_vmem, o_vmem):
            pltpu.sync_copy(t_hbm.at[i_vmem.at[0]], o_vmem)

        pltpu.emit_pipeline(
            body,
            grid=(n // SC_WINDOW,),
            in_specs=[pl.BlockSpec((1, SC_WINDOW), lambda i: (0, i))],
            out_specs=[pl.BlockSpec((SC_WINDOW, width), lambda i: (i, 0))],
            core_axis_name=("core", "subcore"),
            dimension_semantics=(pltpu.PARALLEL,),
        )(i_hbm, o_hbm)

    return gather_kernel(table, idx)


def _moe_out_kernel(tok_ref, yg_ref, gk_ref, x_ref, sgu_ref, sd_ref, g_ref, gate_ref, xo_ref, *, ff):
    gu = _dot(tok_ref[...], sgu_ref[...])
    gate = gu[:, :ff]
    hmid = (gate * _sigmoid(gate) * gu[:, ff:]).astype(BF16)
    shared = _dot(hmid, sd_ref[...])
    gk = gk_ref[...]
    lo_acc, hi_acc = None, None
    for k in range(TOP_K):
        lo, hi = _unpack_halves(yg_ref[k])
        w = gk[:, k:k + 1]
        lo_acc = lo * w if lo_acc is None else lo_acc + lo * w
        hi_acc = hi * w if hi_acc is None else hi_acc + hi * w
    y = jnp.concatenate([lo_acc, hi_acc], axis=-1) + shared
    xo_ref[...] = x_ref[...] + gate_ref[...] * _rms(y, g_ref[...])


def _moe_out_call(tok, yg, gk, xs, sgu, sd, g, gate, *, rows, tm, seq, n_batch):
    d = xs.shape[1]
    ff = sd.shape[0]
    seg = _seg_map(tm, seq, n_batch)
    row = lambda: pl.BlockSpec((tm, d), lambda i: (i, 0))
    return pl.pallas_call(
        functools.partial(_moe_out_kernel, ff=ff),
        grid=(rows // tm,),
        in_specs=[row(),
                  pl.BlockSpec((TOP_K, tm, d // 2), lambda i: (0, i, 0)),
                  pl.BlockSpec((tm, SUBLANES), lambda i: (i, 0)),
                  row(),
                  pl.BlockSpec((d, 2 * ff), lambda i: (0, 0)),
                  pl.BlockSpec((ff, d), lambda i: (0, 0)),
                  pl.BlockSpec((1, d), lambda i: (0, 0)),
                  pl.BlockSpec((None, 1, d), seg)],
        out_specs=row(),
        out_shape=jax.ShapeDtypeStruct((rows, d), F32),
        compiler_params=_cparams(1, 48),
        name="moe_shared_combine",
    )(tok, yg, gk, xs, sgu, sd, g, gate)


def _moe_layer(xs, tok, tokp, g_in, shift, scale, g_out, gate, router_w, router_b, w_gate, w_up, w_down,
               sh_gate, sh_up, sh_down, *, rows, tm, seq, n_batch):
    n_exp = router_w.shape[1]
    te = EXPERT_ROWS
    w_t = router_w.T
    w_hi = w_t.astype(BF16)
    w_lo = (w_t - w_hi.astype(F32)).astype(BF16)
    e_tk, g_tk, pos_tk, counts = _router_call(xs, g_in, shift, scale, w_hi, w_lo,
                                              router_b.reshape(n_exp, 1),
                                              rows=rows, tm=tm, seq=seq, n_batch=n_batch)
    e_tk, pos_tk = e_tk[:TOP_K], pos_tk[:TOP_K]
    cnt = counts[:, 0].astype(jnp.int32)
    padded = (cnt + te - 1) // te * te
    pad_ends = jnp.cumsum(padded)
    pad_starts = pad_ends - padded
    expert_ids = jnp.arange(n_exp, dtype=jnp.int32)
    onehot = e_tk[:, :, None] == expert_ids[None, None, :]
    dest = jnp.sum(jnp.where(onehot, pad_starts[None, None, :], 0), axis=-1) + pos_tk
    n_blocks = -(-(rows * TOP_K + n_exp * (te - 1)) // te)
    n_used = pad_ends[-1] // te
    blk = jnp.arange(n_blocks, dtype=jnp.int32)
    last = jnp.minimum(blk, n_used - 1) * te
    owner = (last[:, None] >= pad_ends[None, :]).astype(jnp.int32)
    block_e = jnp.minimum(jnp.sum(owner, axis=1), n_exp - 1)
    own_hot = block_e[:, None] == expert_ids[None, :]
    n_valid = jnp.sum(jnp.where(own_hot, (pad_starts + cnt)[None, :], 0), axis=1) - last
    n_valid = jnp.clip(n_valid, 0, te)
    dh = tokp.shape[1]
    piece = (dest[:, :, None] * SC_SPLIT + jnp.arange(SC_SPLIT, dtype=jnp.int32)).reshape(TOP_K, 1, -1)
    xg = _sc_scatter_rows(tokp.reshape(rows * SC_SPLIT, dh // SC_SPLIT), [piece[k] for k in range(TOP_K)],
                          n_blocks * te * SC_SPLIT).reshape(n_blocks * te, dh)
    ys = _expert_call(block_e, n_valid, n_used.reshape(1).astype(jnp.int32), xg, w_gate, w_up, w_down)
    yg = _sc_gather_rows(ys.reshape(n_blocks * te * SC_SPLIT, dh // SC_SPLIT),
                         piece.reshape(1, -1)).reshape(TOP_K, rows, dh)
    sgu = jnp.concatenate([sh_gate, sh_up], axis=1).astype(BF16)
    return _moe_out_call(tok, yg, g_tk.T, xs, sgu, sh_down.astype(BF16), g_out, gate,
                         rows=rows, tm=tm, seq=seq, n_batch=n_batch)


def _rope(t, cos, sin):
    half = DA_QK_DIM // 2
    up = pltpu.roll(t, shift=LANES - half, axis=1)
    dn = pltpu.roll(t, shift=half, axis=1)
    lane = lax.broadcasted_iota(jnp.int32, t.shape, 1) % DA_QK_DIM
    return t * cos + jnp.where(lane < half, -up, dn) * sin


def _qkv_kernel(x_ref, g_ref, sh_ref, sc_ref, w_ref, cos_ref, sin_ref, q_ref, k_ref, v_ref, *, qkw):
    h = _normmod(x_ref[...], g_ref[...], sh_ref[...], sc_ref[...]).astype(BF16)
    qkv = _dot(h, w_ref[...])
    cos, sin = cos_ref[...], sin_ref[...]
    q_scale = DA_QK_DIM ** -0.5
    for hb in range(qkw // LANES):
        lo, hi = hb * LANES, (hb + 1) * LANES
        q_ref[:, lo:hi] = (_rope(qkv[:, lo:hi], cos, sin) * q_scale).astype(BF16)
        k_ref[:, lo:hi] = _rope(qkv[:, qkw + lo:qkw + hi], cos, sin).astype(BF16)
    v_ref[...] = qkv[:, 2 * qkw:].astype(BF16)


def _qkv_call(xs, g, shift, scale, w_qkv, cos_t, sin_t, *, qkw, tm, seq, ctx_len, n_batch):
    rows, d = xs.shape
    n = w_qkv.shape[1]
    vw = n - 2 * qkw
    lat_tiles, lat_tps, ctx_tps = n_batch * seq // tm, seq // tm, ctx_len // tm
    seg = _seg_map(tm, seq, n_batch)

    def kv_map(i):
        c = i - lat_tiles
        is_lat = i < lat_tiles
        return (jnp.where(is_lat, i // lat_tps, c // ctx_tps),
                jnp.where(is_lat, ctx_tps + i % lat_tps, c % ctx_tps), 0)

    rope_map = lambda i: (jnp.where(i < lat_tiles, i % lat_tps, lat_tps), 0)
    return pl.pallas_call(
        functools.partial(_qkv_kernel, qkw=qkw),
        grid=(rows // tm,),
        in_specs=[pl.BlockSpec((tm, d), lambda i: (i, 0)),
                  pl.BlockSpec((1, d), lambda i: (0, 0)),
                  pl.BlockSpec((None, 1, d), seg),
                  pl.BlockSpec((None, 1, d), seg),
                  pl.BlockSpec((d, n), lambda i: (0, 0)),
                  pl.BlockSpec((tm, LANES), rope_map),
                  pl.BlockSpec((tm, LANES), rope_map)],
        out_specs=[pl.BlockSpec((tm, qkw), lambda i: (i, 0)),
                   pl.BlockSpec((None, tm, qkw), kv_map),
                   pl.BlockSpec((None, tm, vw), kv_map)],
        out_shape=[jax.ShapeDtypeStruct((rows, qkw), BF16),
                   jax.ShapeDtypeStruct((n_batch, ctx_len + seq, qkw), BF16),
                   jax.ShapeDtypeStruct((n_batch, ctx_len + seq, vw), BF16)],
        compiler_params=_cparams(1, 48),
        name="qkv_rope",
    )(xs, g, shift, scale, w_qkv, cos_t, sin_t)


def _rope_tables(seq, tm):
    pos = np.arange(seq)
    n_freq = DA_QK_DIM // 4
    inv = np.power(ROPE_BASE, -np.arange(n_freq, dtype=np.float32) / n_freq).astype(np.float32)
    row = (pos // GRID_W).astype(np.float32)[:, None] * inv
    col = (pos % GRID_W).astype(np.float32)[:, None] * inv
    ang = np.concatenate([row, col], axis=-1).astype(np.float32)
    ang = np.tile(ang, (1, LANES // ang.shape[1]))
    cos = np.concatenate([np.cos(ang), np.ones((tm, LANES))], axis=0)
    sin = np.concatenate([np.sin(ang), np.zeros((tm, LANES))], axis=0)
    return jnp.asarray(cos, F32), jnp.asarray(sin, F32)


def _attn_kernel(lp_ref, q_ref, k_ref, v_ref, sg_ref, o_ref, vext, *, tq, lam_init):
    @pl.when(pl.program_id(2) == 0)
    def _():
        vext[:, :LANES] = v_ref[...]
        vext[:, LANES:] = jnp.ones((vext.shape[0], LANES), BF16)

    lp = lp_ref[...]
    lam = (jnp.exp(jnp.sum(lp[0:1] * lp[1:2], axis=1, keepdims=True))
           - jnp.exp(jnp.sum(lp[2:3] * lp[3:4], axis=1, keepdims=True)) + lam_init)
    q = q_ref[...]
    lane = lax.broadcasted_iota(jnp.int32, q.shape, 1)
    zero = jnp.zeros_like(q)
    qq = jnp.concatenate([jnp.where(lane < DA_QK_DIM, q, zero),
                          jnp.where(lane >= DA_QK_DIM, q, zero)], axis=0)
    s = _dot_nt(qq, k_ref[...])
    p = jnp.exp(s - jnp.max(s, axis=-1, keepdims=True)).astype(BF16)
    oe = _dot(p, vext[...])
    on = oe[:, :LANES] / oe[:, LANES:LANES + 1]
    o = on[:tq] - lam * on[tq:]
    o_ref[...] = (_rms(o, sg_ref[...]) * (1.0 - lam_init)).astype(BF16)


def _attn_call(lam_p, q, k_all, v_all, subln_g, *, tq, seq, n_batch, lam_init):
    n_heads = q.shape[1] // LANES
    lk = k_all.shape[1]
    qt = seq // tq
    return pl.pallas_call(
        functools.partial(_attn_kernel, tq=tq, lam_init=lam_init),
        grid=(n_batch, n_heads, qt),
        in_specs=[pl.BlockSpec(lam_p.shape, lambda b, h, i: (0, 0)),
                  pl.BlockSpec((tq, LANES), lambda b, h, i: (b * qt + i, h)),
                  pl.BlockSpec((None, lk, LANES), lambda b, h, i: (b, 0, h)),
                  pl.BlockSpec((None, lk, LANES), lambda b, h, i: (b, 0, h)),
                  pl.BlockSpec((1, LANES), lambda b, h, i: (0, 0))],
        out_specs=pl.BlockSpec((tq, LANES), lambda b, h, i: (b * qt + i, h)),
        out_shape=jax.ShapeDtypeStruct((n_batch * seq, n_heads * LANES), BF16),
        scratch_shapes=[pltpu.VMEM((lk, 2 * LANES), BF16)],
        compiler_params=_cparams(3, 48),
        name="diff_attention",
    )(lam_p, q, k_all, v_all, subln_g)


def kernel(x, c, ctx, c_ctx, ada_w, ada_b, norm_g, cf_w_in, cf_conv_w, cf_conv_b, cf_ln_g, cf_ln_b,
           cf_w_out, da_w_qkv, da_lambda, da_subln_g, da_w_out, moe_router_w, moe_router_b,
           moe_w_gate, moe_w_up, moe_w_down, moe_sh_gate, moe_sh_up, moe_sh_down):
    n_batch, seq, d = x.shape
    ctx_len = ctx.shape[1]
    depth = ada_w.shape[0]
    assert depth == 2 and n_batch + 1 <= N_SEG
    assert DA_QK_DIM * 2 == LANES and da_subln_g.shape[1] == LANES
    rows_lat, rows_ctx = n_batch * seq, n_batch * ctx_len
    tm = 512
    tq = 256
    assert seq % tm == 0 and rows_ctx % tm == 0 and seq % tq == 0 and ctx_len % tq == 0

    xs = jnp.concatenate([x.reshape(rows_lat, d), ctx.reshape(rows_ctx, d)], axis=0)
    cond = jnp.concatenate([c, c_ctx[None, :], jnp.zeros((N_SEG - n_batch - 1, d), F32)], axis=0)
    mods = _mods_call(cond, ada_w, ada_b)
    mod = lambda layer, k: mods[layer, :, k * d:(k + 1) * d][:, None, :]
    gain = lambda layer, k: norm_g[layer, k][None, :]
    common = dict(seq=seq, n_batch=n_batch)

    cc = cf_conv_w.shape[-1]
    fc = cf_w_in.shape[2] - 2 * cc
    gw = fc // FOURIER_GROUPS
    ch_ang = 2.0 * np.pi * (np.outer(np.arange(gw), np.arange(gw)) % gw) / gw
    cs = jnp.asarray(np.concatenate([np.cos(ch_ang), np.sin(ch_ang)], axis=1) / math.sqrt(gw), BF16)
    a_glu, ab = _inproj_call(xs, gain(0, 0), mod(0, 0), mod(0, 1), cf_w_in[0].astype(BF16), cs,
                             cc=cc, gw=gw, tm=tm, **common)
    gsz = cc // CONV_GROUPS
    gid = np.arange(cc) // gsz
    gmean = jnp.asarray((gid[:, None] == gid[None, :]) / gsz, BF16)
    a_act = _conv_call(a_glu, cf_conv_w[0][:, 0, :], cf_conv_b[0][None, :], cf_ln_g[0][None, :],
                       cf_ln_b[0][None, :], gmean, tl=tq, seq=seq, ctx_len=ctx_len, n_batch=n_batch)
    c_lat, s_lat = _dft_tables(seq, 1.0 / math.sqrt(seq))
    c_ctx_m, s_ctx_m = _dft_tables(ctx_len, 1.0 / math.sqrt(ctx_len))
    fr = jnp.concatenate([
        _seqdft_call(ab, c_lat, s_lat, length=seq, row0=0, n_seq=n_batch, fc=fc, tm=min(seq, 256)),
        _seqdft_call(ab, c_ctx_m, s_ctx_m, length=ctx_len, row0=rows_lat, n_seq=n_batch, fc=fc,
                     tm=min(ctx_len, 256))], axis=0)
    w_out = cf_w_out[0].astype(BF16)
    xs, tok, tokp = _outproj_call([a_act, fr], [w_out[:cc], w_out[cc:]], xs, gain(0, 1), mod(0, 2),
                            gain(0, 2), mod(0, 3), mod(0, 4), rows=rows_lat + rows_ctx, tm=tm, **common)
    xs = _moe_layer(xs, tok, tokp, gain(0, 2), mod(0, 3), mod(0, 4), gain(0, 3), mod(0, 5),
                    moe_router_w[0], moe_router_b[0], moe_w_gate[0], moe_w_up[0], moe_w_down[0],
                    moe_sh_gate[0], moe_sh_up[0], moe_sh_down[0], rows=rows_lat + rows_ctx, tm=tm, **common)

    qkw = DA_HEADS * 2 * DA_QK_DIM
    lam_init = 0.8 - 0.6 * math.exp(-0.3 * 1)
    cos_t, sin_t = _rope_tables(seq, tq)
    q, k_all, v_all = _qkv_call(xs, gain(1, 0), mod(1, 0), mod(1, 1), da_w_qkv[0].astype(BF16),
                                cos_t, sin_t, qkw=qkw, tm=tq, ctx_len=ctx_len, **common)
    o = _attn_call(da_lambda[0], q, k_all, v_all, da_subln_g[0][None, :], tq=tq, seq=seq,
                   n_batch=n_batch, lam_init=lam_init)
    xs, tok, tokp = _outproj_call([o], [da_w_out[0].astype(BF16)], xs, gain(1, 1), mod(1, 2),
                            gain(1, 2), mod(1, 3), mod(1, 4), rows=rows_lat, tm=tm, **common)
    xs = _moe_layer(xs, tok, tokp, gain(1, 2), mod(1, 3), mod(1, 4), gain(1, 3), mod(1, 5),
                    moe_router_w[1], moe_router_b[1], moe_w_gate[1], moe_w_up[1], moe_w_down[1],
                    moe_sh_gate[1], moe_sh_up[1], moe_sh_down[1], rows=rows_lat, tm=tm, **common)
    return xs.reshape(n_batch, seq, d)
```

```python
import functools
import math

import numpy as np
import jax
import jax.numpy as jnp
from jax import lax
from jax.experimental import pallas as pl
from jax.experimental.pallas import tpu as pltpu
from jax.experimental.pallas import tpu_sc as plsc

F32 = jnp.float32
BF16 = jnp.bfloat16

EPS = 1e-6
GRID_W = 64
CONV_GROUPS = 8
FOURIER_GROUPS = 4
DA_HEADS = 8
DA_QK_DIM = 64
ROPE_BASE = 10000.0
N_GROUPS = 8
TOPK_GROUPS = 4
TOP_K = 6
ROUTED_SCALE = 2.5

LANES = 128
SUBLANES = 8
N_SEG = 16
HALO = 16
EXPERT_ROWS = 256
MIB = 1024 * 1024


def _cparams(n_axes, vmem_mib):
    return pltpu.CompilerParams(dimension_semantics=("arbitrary",) * n_axes,
                                vmem_limit_bytes=vmem_mib * MIB)


def _sigmoid(v):
    return 1.0 / (1.0 + jnp.exp(-v))


def _rms(v, g):
    return v * lax.rsqrt(jnp.mean(v * v, axis=-1, keepdims=True) + EPS) * g


def _normmod(v, g, shift, scale):
    return _rms(v, g) * (1.0 + scale) + shift


def _split_bf16(v):
    hi = v.astype(BF16)
    lo = (v - hi.astype(F32)).astype(BF16)
    return hi, lo


def _bf16_high_bits(v):
    b = lax.bitcast_convert_type(v, jnp.uint32)
    r = b + jnp.uint32(0x7FFF) + ((b >> 16) & jnp.uint32(1))
    return r & jnp.uint32(0xFFFF0000)


def _pack_halves(v):
    half = v.shape[1] // 2
    word = _bf16_high_bits(v[:, half:]) | (_bf16_high_bits(v[:, :half]) >> 16)
    return lax.bitcast_convert_type(word, jnp.int32)


def _unpack_halves(w):
    u = lax.bitcast_convert_type(w, jnp.uint32)
    lo = lax.bitcast_convert_type(u << 16, F32)
    hi = lax.bitcast_convert_type(u & jnp.uint32(0xFFFF0000), F32)
    return lo, hi


SC_SPLIT = 2


def _store_pieces(ref, words):
    q = words.shape[1] // SC_SPLIT
    for j in range(SC_SPLIT):
        ref[j] = words[:, j * q:(j + 1) * q]


def _load_pieces(ref):
    return jnp.concatenate([ref[j] for j in range(SC_SPLIT)], axis=-1)


def _dot(a, b):
    return jnp.dot(a, b, preferred_element_type=F32)


def _dot_nt(a, b):
    return lax.dot_general(a, b, (((1,), (1,)), ((), ())), preferred_element_type=F32)


def _mods_kernel(c_ref, w_ref, b_ref, o_ref):
    cv = c_ref[...]
    o_ref[...] = _dot(cv * _sigmoid(cv), w_ref[...]) + b_ref[...]


def _mods_call(cond, ada_w, ada_b):
    depth, d, n = ada_w.shape
    tn = n // 4
    return pl.pallas_call(
        _mods_kernel,
        grid=(depth, n // tn),
        in_specs=[pl.BlockSpec((N_SEG, d), lambda l, j: (0, 0)),
                  pl.BlockSpec((None, d, tn), lambda l, j: (l, 0, j)),
                  pl.BlockSpec((None, 1, tn), lambda l, j: (l, 0, j))],
        out_specs=pl.BlockSpec((None, N_SEG, tn), lambda l, j: (l, 0, j)),
        out_shape=jax.ShapeDtypeStruct((depth, N_SEG, n), F32),
        compiler_params=_cparams(2, 40),
        name="adaln_mods",
    )(cond, ada_w, ada_b.reshape(depth, 1, n))


def _inproj_kernel(x_ref, g_ref, sh_ref, sc_ref, w_ref, cs_ref, a_ref, ab_ref, *, cc, gw):
    h = _normmod(x_ref[...], g_ref[...], sh_ref[...], sc_ref[...]).astype(BF16)
    u = _dot(h, w_ref[...])
    a_ref[...] = u[:, :cc] * _sigmoid(u[:, cc:2 * cc])
    f = u[:, 2 * cc:].astype(BF16)
    n_g = f.shape[1] // gw
    parts = [_dot(f[:, g * gw:(g + 1) * gw], cs_ref[...]) for g in range(n_g)]
    cos_part = [p[:, :gw] for p in parts]
    sin_part = [p[:, gw:] for p in parts]
    ab_ref[...] = jnp.concatenate(cos_part + sin_part, axis=-1).astype(BF16)


def _seg_map(tm, seq, n_batch):
    return lambda i: (jnp.minimum(i * tm // seq, n_batch), 0, 0)


def _inproj_call(xs, g, shift, scale, w_in, cs, *, cc, gw, tm, seq, n_batch):
    rows, d = xs.shape
    n = w_in.shape[1]
    fc = n - 2 * cc
    seg = _seg_map(tm, seq, n_batch)
    return pl.pallas_call(
        functools.partial(_inproj_kernel, cc=cc, gw=gw),
        grid=(rows // tm,),
        in_specs=[pl.BlockSpec((tm, d), lambda i: (i, 0)),
                  pl.BlockSpec((1, d), lambda i: (0, 0)),
                  pl.BlockSpec((None, 1, d), seg),
                  pl.BlockSpec((None, 1, d), seg),
                  pl.BlockSpec((d, n), lambda i: (0, 0)),
                  pl.BlockSpec((gw, 2 * gw), lambda i: (0, 0))],
        out_specs=[pl.BlockSpec((tm, cc), lambda i: (i, 0)),
                   pl.BlockSpec((tm, 2 * fc), lambda i: (i, 0))],
        out_shape=[jax.ShapeDtypeStruct((rows, cc), F32),
                   jax.ShapeDtypeStruct((rows, 2 * fc), BF16)],
        compiler_params=_cparams(1, 40),
        name="inproj_glu_chdft",
    )(xs, g, shift, scale, w_in, cs)


def _conv_kernel(prev_ref, main_ref, next_ref, w_ref, cb_ref, lg_ref, lb_ref, gm_ref, o_ref,
                 buf, cv, *, tl, width, lat_tiles, lat_tps, ctx_tps, chunk):
    i = pl.program_id(0)
    is_lat = i < lat_tiles
    tps = jnp.where(is_lat, lat_tps, ctx_tps)
    j = jnp.where(is_lat, i, i - lat_tiles) % tps
    zero = jnp.zeros((HALO, buf.shape[1]), F32)
    buf[0:HALO, :] = jnp.where(j > 0, prev_ref[...], zero)
    buf[HALO:HALO + tl, :] = main_ref[...]
    buf[HALO + tl:HALO + tl + HALO, :] = jnp.where(j < tps - 1, next_ref[...], zero)
    base = HALO - width // 2
    for r0 in range(0, tl, chunk):
        acc = buf[base + r0:base + r0 + chunk, :] * w_ref[0:1, :]
        for t in range(1, width):
            acc = acc + buf[base + r0 + t:base + r0 + t + chunk, :] * w_ref[t:t + 1, :]
        cv[r0:r0 + chunk, :] = acc
    a = cv[...] + cb_ref[...]
    gm = gm_ref[...]
    a_hi, a_lo = _split_bf16(a)
    mu = _dot(a_hi, gm) + _dot(a_lo, gm)
    dl = a - mu
    q_hi, q_lo = _split_bf16(dl * dl)
    var = _dot(q_hi, gm) + _dot(q_lo, gm)
    y = dl * lax.rsqrt(var + EPS) * lg_ref[...] + lb_ref[...]
    o_ref[...] = (y * _sigmoid(y)).astype(BF16)


def _conv_call(a_glu, conv_w, conv_b, ln_g, ln_b, gmean, *, tl, seq, ctx_len, n_batch):
    rows, ch = a_glu.shape
    width = conv_w.shape[0]
    assert width // 2 <= HALO and tl % HALO == 0
    hb = tl // HALO
    last_halo = rows // HALO - 1
    return pl.pallas_call(
        functools.partial(_conv_kernel, tl=tl, width=width, lat_tiles=n_batch * seq // tl,
                          lat_tps=seq // tl, ctx_tps=ctx_len // tl, chunk=32),
        grid=(rows // tl,),
        in_specs=[pl.BlockSpec((HALO, ch), lambda i: (jnp.maximum(i * hb - 1, 0), 0)),
                  pl.BlockSpec((tl, ch), lambda i: (i, 0)),
                  pl.BlockSpec((HALO, ch), lambda i: (jnp.minimum((i + 1) * hb, last_halo), 0)),
                  pl.BlockSpec((width, ch), lambda i: (0, 0)),
                  pl.BlockSpec((1, ch), lambda i: (0, 0)),
                  pl.BlockSpec((1, ch), lambda i: (0, 0)),
                  pl.BlockSpec((1, ch), lambda i: (0, 0)),
                  pl.BlockSpec((ch, ch), lambda i: (0, 0))],
        out_specs=pl.BlockSpec((tl, ch), lambda i: (i, 0)),
        out_shape=jax.ShapeDtypeStruct((rows, ch), BF16),
        scratch_shapes=[pltpu.VMEM((tl + 2 * HALO, ch), F32), pltpu.VMEM((tl, ch), F32)],
        compiler_params=_cparams(1, 40),
        name="dwconv_groupln_swish",
    )(a_glu, a_glu, a_glu, conv_w, conv_b, ln_g, ln_b, gmean)


def _seqdft_kernel(c_ref, s_ref, a_ref, b_ref, o_ref):
    o_ref[...] = (_dot(c_ref[...], a_ref[...]) - _dot(s_ref[...], b_ref[...])).astype(BF16)


def _seqdft_call(ab, cmat, smat, *, length, row0, n_seq, fc, tm):
    seq0 = row0 // length
    return pl.pallas_call(
        _seqdft_kernel,
        grid=(n_seq, length // tm),
        in_specs=[pl.BlockSpec((tm, length), lambda b, i: (i, 0)),
                  pl.BlockSpec((tm, length), lambda b, i: (i, 0)),
                  pl.BlockSpec((length, fc), lambda b, i: (seq0 + b, 0)),
                  pl.BlockSpec((length, fc), lambda b, i: (seq0 + b, 1))],
        out_specs=pl.BlockSpec((tm, fc), lambda b, i: (b * (length // tm) + i, 0)),
        out_shape=jax.ShapeDtypeStruct((n_seq * length, fc), BF16),
        compiler_params=_cparams(2, 48),
        name="seq_dft",
    )(cmat, smat, ab, ab)


def _dft_tables(length, scale):
    if length <= 512:
        kn = np.outer(np.arange(length), np.arange(length)) % length
        ang = 2.0 * np.pi * kn / length
        return (jnp.asarray(np.cos(ang) * scale, BF16), jnp.asarray(np.sin(ang) * scale, BF16))
    r = 64
    assert length % r == 0
    k = np.arange(length)[:, None]
    alpha = 2.0 * np.pi * ((k * np.arange(length // r)[None, :] * r) % length) / length
    beta = 2.0 * np.pi * ((k * np.arange(r)[None, :]) % length) / length
    ca, sa = jnp.asarray(np.cos(alpha), F32)[:, :, None], jnp.asarray(np.sin(alpha), F32)[:, :, None]
    cb, sb = jnp.asarray(np.cos(beta) * scale, F32)[:, None, :], jnp.asarray(np.sin(beta) * scale, F32)[:, None, :]
    cmat = (ca * cb - sa * sb).reshape(length, length).astype(BF16)
    smat = (sa * cb + ca * sb).reshape(length, length).astype(BF16)
    return cmat, smat


def _outproj_kernel(*refs, n_in):
    ins, ws = refs[:n_in], refs[n_in:2 * n_in]
    x_ref, g1_ref, gate_ref, g2_ref, sh_ref, sc_ref, xo_ref, tok_ref, tokp_ref = refs[2 * n_in:]
    y = _dot(ins[0][...], ws[0][...])
    for a_ref, w_ref in zip(ins[1:], ws[1:]):
        y = y + _dot(a_ref[...], w_ref[...])
    x1 = x_ref[...] + gate_ref[...] * _rms(y, g1_ref[...])
    xo_ref[...] = x1
    tok = _normmod(x1, g2_ref[...], sh_ref[...], sc_ref[...])
    tok_ref[...] = tok.astype(BF16)
    _store_pieces(tokp_ref, _pack_halves(tok))


def _outproj_call(ins, ws, xs, g1, gate, g2, shift, scale, *, rows, tm, seq, n_batch):
    d = xs.shape[1]
    seg = _seg_map(tm, seq, n_batch)
    n_in = len(ins)
    row_spec = lambda a: pl.BlockSpec((tm, a.shape[1]), lambda i: (i, 0))
    full_spec = lambda a: pl.BlockSpec(a.shape, lambda i: (0, 0))
    return pl.pallas_call(
        functools.partial(_outproj_kernel, n_in=n_in),
        grid=(rows // tm,),
        in_specs=[row_spec(a) for a in ins] + [full_spec(w) for w in ws] + [
            row_spec(xs), full_spec(g1), pl.BlockSpec((None, 1, d), seg), full_spec(g2),
            pl.BlockSpec((None, 1, d), seg), pl.BlockSpec((None, 1, d), seg)],
        out_specs=[pl.BlockSpec((tm, d), lambda i: (i, 0)), pl.BlockSpec((tm, d), lambda i: (i, 0)),
                   pl.BlockSpec((SC_SPLIT, tm, d // 2 // SC_SPLIT), lambda i: (0, i, 0))],
        out_shape=[jax.ShapeDtypeStruct((rows, d), F32), jax.ShapeDtypeStruct((rows, d), BF16),
                   jax.ShapeDtypeStruct((SC_SPLIT, rows, d // 2 // SC_SPLIT), jnp.int32)],
        compiler_params=_cparams(1, 40),
        name="outproj_residual",
    )(*ins, *ws, xs, g1, gate, g2, shift, scale)


def _pick_max(cur, idx):
    mx = jnp.max(cur, axis=0, keepdims=True)
    first = jnp.min(jnp.where(cur == mx, idx, float(cur.shape[0])), axis=0, keepdims=True)
    return first, idx == first


def _router_kernel(x_ref, g_ref, sh_ref, sc_ref, wh_ref, wl_ref, rb_ref,
                   e_ref, gt_ref, pos_ref, cnt_ref, carry, *, n_exp):
    i = pl.program_id(0)

    @pl.when(i == 0)
    def _():
        carry[...] = jnp.zeros_like(carry)

    tok = _normmod(x_ref[...], g_ref[...], sh_ref[...], sc_ref[...])
    t_hi, t_lo = _split_bf16(tok)
    wh, wl = wh_ref[...], wl_ref[...]
    logits = _dot_nt(wh, t_hi) + _dot_nt(wh, t_lo) + _dot_nt(wl, t_hi)
    tm = logits.shape[1]
    scores = _sigmoid(logits)
    biased = scores + rb_ref[...]
    gsz = n_exp // N_GROUPS
    neg = -jnp.inf

    b3 = biased.reshape(N_GROUPS, gsz, tm)
    im = lax.broadcasted_iota(jnp.int32, b3.shape, 1).astype(F32)
    m1 = jnp.max(b3, axis=1, keepdims=True)
    i1 = jnp.min(jnp.where(b3 == m1, im, float(gsz)), axis=1, keepdims=True)
    m2 = jnp.max(jnp.where(im == i1, neg, b3), axis=1, keepdims=True)
    gscore = (m1 + m2).reshape(N_GROUPS, tm)

    ig = lax.broadcasted_iota(jnp.int32, gscore.shape, 0).astype(F32)
    gsel = jnp.zeros_like(gscore)
    cur = gscore
    for _ in range(TOPK_GROUPS):
        _, hit = _pick_max(cur, ig)
        gsel = jnp.where(hit, 1.0, gsel)
        cur = jnp.where(hit, neg, cur)
    gsel3 = jnp.broadcast_to(gsel.reshape(N_GROUPS, 1, tm), b3.shape)
    cur = jnp.where(gsel3 > 0.0, b3, neg).reshape(n_exp, tm)

    ie = lax.broadcasted_iota(jnp.int32, (n_exp, tm), 0).astype(F32)
    sel = jnp.zeros((n_exp, tm), F32)
    picks, raw = [], []
    for _ in range(TOP_K):
        first, hit = _pick_max(cur, ie)
        picks.append(first)
        raw.append(jnp.sum(jnp.where(hit, scores, 0.0), axis=0, keepdims=True))
        sel = jnp.where(hit, 1.0, sel)
        cur = jnp.where(hit, neg, cur)
    total = raw[0]
    for r in raw[1:]:
        total = total + r

    ri = lax.broadcasted_iota(jnp.int32, (tm, tm), 0)
    ci = lax.broadcasted_iota(jnp.int32, (tm, tm), 1)
    upper = jnp.where(ri < ci, 1.0, 0.0).astype(BF16)
    rank = _dot(sel.astype(BF16), upper) + carry[...]
    ranks = [jnp.sum(jnp.where(ie == p, rank, 0.0), axis=0, keepdims=True) for p in picks]
    carry[...] = carry[...] + jnp.sum(sel, axis=1, keepdims=True)

    pad = jnp.zeros((SUBLANES - TOP_K, tm), F32)
    e_ref[...] = jnp.concatenate(picks + [pad], axis=0).astype(jnp.int32)
    gt_ref[...] = jnp.concatenate([r / total * ROUTED_SCALE for r in raw] + [pad], axis=0)
    pos_ref[...] = jnp.concatenate(ranks + [pad], axis=0).astype(jnp.int32)
    cnt_ref[...] = jnp.broadcast_to(carry[...], cnt_ref.shape)


def _router_call(xs, g, shift, scale, w_hi_t, w_lo_t, rbias, *, rows, tm, seq, n_batch):
    d = xs.shape[1]
    n_exp = w_hi_t.shape[0]
    seg = _seg_map(tm, seq, n_batch)
    tok_rows = lambda dt: jax.ShapeDtypeStruct((SUBLANES, rows), dt)
    lane_spec = pl.BlockSpec((SUBLANES, tm), lambda i: (0, i))
    return pl.pallas_call(
        functools.partial(_router_kernel, n_exp=n_exp),
        grid=(rows // tm,),
        in_specs=[pl.BlockSpec((tm, d), lambda i: (i, 0)),
                  pl.BlockSpec((1, d), lambda i: (0, 0)),
                  pl.BlockSpec((None, 1, d), seg),
                  pl.BlockSpec((None, 1, d), seg),
                  pl.BlockSpec((n_exp, d), lambda i: (0, 0)),
                  pl.BlockSpec((n_exp, d), lambda i: (0, 0)),
                  pl.BlockSpec((n_exp, 1), lambda i: (0, 0))],
        out_specs=[lane_spec, lane_spec, lane_spec,
                   pl.BlockSpec((n_exp, LANES), lambda i: (0, 0))],
        out_shape=[tok_rows(jnp.int32), tok_rows(F32), tok_rows(jnp.int32),
                   jax.ShapeDtypeStruct((n_exp, LANES), F32)],
        scratch_shapes=[pltpu.VMEM((n_exp, 1), F32)],
        compiler_params=_cparams(1, 40),
        name="moe_router",
    )(xs, g, shift, scale, w_hi_t, w_lo_t, rbias)


def _expert_kernel(be_ref, nv_ref, nu_ref, xs_ref, wg_ref, wu_ref, wd_ref, ys_ref, wgu, wdn, *, ff):
    b = pl.program_id(0)
    changed = jnp.logical_or(b == 0, be_ref[b] != be_ref[jnp.maximum(b - 1, 0)])

    @pl.when(changed)
    def _():
        wgu[:, :ff] = wg_ref[...].astype(BF16)
        wgu[:, ff:] = wu_ref[...].astype(BF16)
        wdn[...] = wd_ref[...].astype(BF16)

    @pl.when(b < nu_ref[0])
    def _():
        xw = _load_pieces(xs_ref)
        row = lax.broadcasted_iota(jnp.int32, xw.shape, 0)
        lo, hi = _unpack_halves(jnp.where(row < nv_ref[b], xw, 0))
        half = lo.shape[1]
        gu = _dot(lo.astype(BF16), wgu[:half, :]) + _dot(hi.astype(BF16), wgu[half:, :])
        gate = gu[:, :ff]
        hmid = (gate * _sigmoid(gate) * gu[:, ff:]).astype(BF16)
        _store_pieces(ys_ref, _pack_halves(_dot(hmid, wdn[...])))


def _expert_call(block_e, n_valid, n_used, xs, w_gate, w_up, w_down):
    _, n_rows, q = xs.shape
    d = 2 * SC_SPLIT * q
    ff = w_gate.shape[2]
    te = EXPERT_ROWS
    row_map = lambda b, be, nv, nu: (0, jnp.minimum(b, nu[0] - 1), 0)
    w_map = lambda b, be, nv, nu: (be[b], 0, 0)
    grid_spec = pltpu.PrefetchScalarGridSpec(
        num_scalar_prefetch=3,
        grid=(n_rows // te,),
        in_specs=[pl.BlockSpec((SC_SPLIT, te, q), row_map),
                  pl.BlockSpec((None, d, ff), w_map),
                  pl.BlockSpec((None, d, ff), w_map),
                  pl.BlockSpec((None, ff, d), w_map)],
        out_specs=pl.BlockSpec((SC_SPLIT, te, q), row_map),
        scratch_shapes=[pltpu.VMEM((d, 2 * ff), BF16), pltpu.VMEM((ff, d), BF16)])
    return pl.pallas_call(
        functools.partial(_expert_kernel, ff=ff),
        grid_spec=grid_spec,
        out_shape=jax.ShapeDtypeStruct(xs.shape, jnp.int32),
        compiler_params=_cparams(1, 40),
        name="moe_experts",
    )(block_e, n_valid, n_used, xs, w_gate, w_up, w_down)


SC_WINDOW = 128


def _sc_mesh():
    return plsc.VectorSubcoreMesh(core_axis_name="core", subcore_axis_name="subcore")


def _sc_scatter_rows(rows, dests, n_out):
    n, width = rows.shape
    n_k = len(dests)
    assert n % SC_WINDOW == 0

    @functools.partial(pl.kernel, out_type=jax.ShapeDtypeStruct((n_out, width), rows.dtype),
                       mesh=_sc_mesh(), name="sc_dispatch_rows")
    def scatter_kernel(x_hbm, *refs):
        idx_hbm, o_hbm = refs[:n_k], refs[n_k]

        def body(x_vmem, *idx_vmem):
            for iv in idx_vmem:
                pltpu.sync_copy(x_vmem, o_hbm.at[iv.at[0]])

        pltpu.emit_pipeline(
            body,
            grid=(n // SC_WINDOW,),
            in_specs=[pl.BlockSpec((SC_WINDOW, width), lambda i: (i, 0))]
            + [pl.BlockSpec((1, SC_WINDOW), lambda i: (0, i))] * n_k,
            out_specs=[],
            core_axis_name=("core", "subcore"),
            dimension_semantics=(pltpu.PARALLEL,),
        )(x_hbm, *idx_hbm)

    return scatter_kernel(rows, *dests)


def _sc_gather_rows(table, idx):
    n = idx.shape[1]
    width = table.shape[1]
    assert n % SC_WINDOW == 0

    @functools.partial(pl.kernel, out_type=jax.ShapeDtypeStruct((n, width), table.dtype),
                       mesh=_sc_mesh(), name="sc_collect_rows")
    def gather_kernel(t_hbm, i_hbm, o_hbm):
        def body(i_vmem, o_vmem):
            pltpu.sync_copy(t_hbm.at[i_vmem.at[0]], o_vmem)

        pltpu.emit_pipeline(
            body,
            grid=(n // SC_WINDOW,),
            in_specs=[pl.BlockSpec((1, SC_WINDOW), lambda i: (0, i))],
            out_specs=[pl.BlockSpec((SC_WINDOW, width), lambda i: (i, 0))],
            core_axis_name=("core", "subcore"),
            dimension_semantics=(pltpu.PARALLEL,),
        )(i_hbm, o_hbm)

    return gather_kernel(table, idx)


def _moe_out_kernel(tok_ref, yg_ref, gk_ref, x_ref, sgu_ref, sd_ref, g_ref, gate_ref, xo_ref, *, ff):
    gu = _dot(tok_ref[...], sgu_ref[...])
    gate = gu[:, :ff]
    hmid = (gate * _sigmoid(gate) * gu[:, ff:]).astype(BF16)
    shared = _dot(hmid, sd_ref[...])
    gk = gk_ref[...]
    lo_acc, hi_acc = None, None
    for k in range(TOP_K):
        lo, hi = _unpack_halves(jnp.concatenate([yg_ref[j, k] for j in range(SC_SPLIT)], axis=-1))
        w = gk[:, k:k + 1]
        lo_acc = lo * w if lo_acc is None else lo_acc + lo * w
        hi_acc = hi * w if hi_acc is None else hi_acc + hi * w
    y = jnp.concatenate([lo_acc, hi_acc], axis=-1) + shared
    xo_ref[...] = x_ref[...] + gate_ref[...] * _rms(y, g_ref[...])


def _moe_out_call(tok, yg, gk, xs, sgu, sd, g, gate, *, rows, tm, seq, n_batch):
    d = xs.shape[1]
    ff = sd.shape[0]
    seg = _seg_map(tm, seq, n_batch)
    row = lambda: pl.BlockSpec((tm, d), lambda i: (i, 0))
    return pl.pallas_call(
        functools.partial(_moe_out_kernel, ff=ff),
        grid=(rows // tm,),
        in_specs=[row(),
                  pl.BlockSpec((SC_SPLIT, TOP_K, tm, d // 2 // SC_SPLIT), lambda i: (0, 0, i, 0)),
                  pl.BlockSpec((tm, SUBLANES), lambda i: (i, 0)),
                  row(),
                  pl.BlockSpec((d, 2 * ff), lambda i: (0, 0)),
                  pl.BlockSpec((ff, d), lambda i: (0, 0)),
                  pl.BlockSpec((1, d), lambda i: (0, 0)),
                  pl.BlockSpec((None, 1, d), seg)],
        out_specs=row(),
        out_shape=jax.ShapeDtypeStruct((rows, d), F32),
        compiler_params=_cparams(1, 48),
        name="moe_shared_combine",
    )(tok, yg, gk, xs, sgu, sd, g, gate)


def _moe_layer(xs, tok, tokp, g_in, shift, scale, g_out, gate, router_w, router_b, w_gate, w_up, w_down,
               sh_gate, sh_up, sh_down, *, rows, tm, seq, n_batch):
    n_exp = router_w.shape[1]
    te = EXPERT_ROWS
    w_t = router_w.T
    w_hi = w_t.astype(BF16)
    w_lo = (w_t - w_hi.astype(F32)).astype(BF16)
    e_tk, g_tk, pos_tk, counts = _router_call(xs, g_in, shift, scale, w_hi, w_lo,
                                              router_b.reshape(n_exp, 1),
                                              rows=rows, tm=tm, seq=seq, n_batch=n_batch)
    e_tk, pos_tk = e_tk[:TOP_K], pos_tk[:TOP_K]
    cnt = counts[:, 0].astype(jnp.int32)
    padded = (cnt + te - 1) // te * te
    pad_ends = jnp.cumsum(padded)
    pad_starts = pad_ends - padded
    expert_ids = jnp.arange(n_exp, dtype=jnp.int32)
    onehot = e_tk[:, :, None] == expert_ids[None, None, :]
    dest = jnp.sum(jnp.where(onehot, pad_starts[None, None, :], 0), axis=-1) + pos_tk
    n_blocks = -(-(rows * TOP_K + n_exp * (te - 1)) // te)
    n_used = pad_ends[-1] // te
    blk = jnp.arange(n_blocks, dtype=jnp.int32)
    last = jnp.minimum(blk, n_used - 1) * te
    owner = (last[:, None] >= pad_ends[None, :]).astype(jnp.int32)
    block_e = jnp.minimum(jnp.sum(owner, axis=1), n_exp - 1)
    own_hot = block_e[:, None] == expert_ids[None, :]
    n_valid = jnp.sum(jnp.where(own_hot, (pad_starts + cnt)[None, :], 0), axis=1) - last
    n_valid = jnp.clip(n_valid, 0, te)
    q = tokp.shape[2]
    n_rows = n_blocks * te
    piece = dest[None, :, :] + (jnp.arange(SC_SPLIT, dtype=jnp.int32) * n_rows)[:, None, None]
    xg = _sc_scatter_rows(tokp.reshape(SC_SPLIT * rows, q),
                          [piece[:, k, :].reshape(1, SC_SPLIT * rows) for k in range(TOP_K)],
                          SC_SPLIT * n_rows).reshape(SC_SPLIT, n_rows, q)
    ys = _expert_call(block_e, n_valid, n_used.reshape(1).astype(jnp.int32), xg, w_gate, w_up, w_down)
    yg = _sc_gather_rows(ys.reshape(SC_SPLIT * n_rows, q),
                         piece.reshape(1, -1)).reshape(SC_SPLIT, TOP_K, rows, q)
    sgu = jnp.concatenate([sh_gate, sh_up], axis=1).astype(BF16)
    return _moe_out_call(tok, yg, g_tk.T, xs, sgu, sh_down.astype(BF16), g_out, gate,
                         rows=rows, tm=tm, seq=seq, n_batch=n_batch)


def _rope(t, cos, sin):
    half = DA_QK_DIM // 2
    up = pltpu.roll(t, shift=LANES - half, axis=1)
    dn = pltpu.roll(t, shift=half, axis=1)
    lane = lax.broadcasted_iota(jnp.int32, t.shape, 1) % DA_QK_DIM
    return t * cos + jnp.where(lane < half, -up, dn) * sin


def _qkv_kernel(x_ref, g_ref, sh_ref, sc_ref, w_ref, cos_ref, sin_ref, q_ref, k_ref, v_ref, *, qkw):
    h = _normmod(x_ref[...], g_ref[...], sh_ref[...], sc_ref[...]).astype(BF16)
    qkv = _dot(h, w_ref[...])
    cos, sin = cos_ref[...], sin_ref[...]
    q_scale = DA_QK_DIM ** -0.5
    for hb in range(qkw // LANES):
        lo, hi = hb * LANES, (hb + 1) * LANES
        q_ref[:, lo:hi] = (_rope(qkv[:, lo:hi], cos, sin) * q_scale).astype(BF16)
        k_ref[:, lo:hi] = _rope(qkv[:, qkw + lo:qkw + hi], cos, sin).astype(BF16)
    v_ref[...] = qkv[:, 2 * qkw:].astype(BF16)


def _qkv_call(xs, g, shift, scale, w_qkv, cos_t, sin_t, *, qkw, tm, seq, ctx_len, n_batch):
    rows, d = xs.shape
    n = w_qkv.shape[1]
    vw = n - 2 * qkw
    lat_tiles, lat_tps, ctx_tps = n_batch * seq // tm, seq // tm, ctx_len // tm
    seg = _seg_map(tm, seq, n_batch)

    def kv_map(i):
        c = i - lat_tiles
        is_lat = i < lat_tiles
        return (jnp.where(is_lat, i // lat_tps, c // ctx_tps),
                jnp.where(is_lat, ctx_tps + i % lat_tps, c % ctx_tps), 0)

    rope_map = lambda i: (jnp.where(i < lat_tiles, i % lat_tps, lat_tps), 0)
    return pl.pallas_call(
        functools.partial(_qkv_kernel, qkw=qkw),
        grid=(rows // tm,),
        in_specs=[pl.BlockSpec((tm, d), lambda i: (i, 0)),
                  pl.BlockSpec((1, d), lambda i: (0, 0)),
                  pl.BlockSpec((None, 1, d), seg),
                  pl.BlockSpec((None, 1, d), seg),
                  pl.BlockSpec((d, n), lambda i: (0, 0)),
                  pl.BlockSpec((tm, LANES), rope_map),
                  pl.BlockSpec((tm, LANES), rope_map)],
        out_specs=[pl.BlockSpec((tm, qkw), lambda i: (i, 0)),
                   pl.BlockSpec((None, tm, qkw), kv_map),
                   pl.BlockSpec((None, tm, vw), kv_map)],
        out_shape=[jax.ShapeDtypeStruct((rows, qkw), BF16),
                   jax.ShapeDtypeStruct((n_batch, ctx_len + seq, qkw), BF16),
                   jax.ShapeDtypeStruct((n_batch, ctx_len + seq, vw), BF16)],
        compiler_params=_cparams(1, 48),
        name="qkv_rope",
    )(xs, g, shift, scale, w_qkv, cos_t, sin_t)


def _rope_tables(seq, tm):
    pos = np.arange(seq)
    n_freq = DA_QK_DIM // 4
    inv = np.power(ROPE_BASE, -np.arange(n_freq, dtype=np.float32) / n_freq).astype(np.float32)
    row = (pos // GRID_W).astype(np.float32)[:, None] * inv
    col = (pos % GRID_W).astype(np.float32)[:, None] * inv
    ang = np.concatenate([row, col], axis=-1).astype(np.float32)
    ang = np.tile(ang, (1, LANES // ang.shape[1]))
    cos = np.concatenate([np.cos(ang), np.ones((tm, LANES))], axis=0)
    sin = np.concatenate([np.sin(ang), np.zeros((tm, LANES))], axis=0)
    return jnp.asarray(cos, F32), jnp.asarray(sin, F32)


def _attn_kernel(lp_ref, q_ref, k_ref, v_ref, sg_ref, o_ref, vext, *, tq, lam_init):
    @pl.when(pl.program_id(2) == 0)
    def _():
        vext[:, :LANES] = v_ref[...]
        vext[:, LANES:] = jnp.ones((vext.shape[0], LANES), BF16)

    lp = lp_ref[...]
    lam = (jnp.exp(jnp.sum(lp[0:1] * lp[1:2], axis=1, keepdims=True))
           - jnp.exp(jnp.sum(lp[2:3] * lp[3:4], axis=1, keepdims=True)) + lam_init)
    q = q_ref[...]
    lane = lax.broadcasted_iota(jnp.int32, q.shape, 1)
    zero = jnp.zeros_like(q)
    qq = jnp.concatenate([jnp.where(lane < DA_QK_DIM, q, zero),
                          jnp.where(lane >= DA_QK_DIM, q, zero)], axis=0)
    s = _dot_nt(qq, k_ref[...])
    p = jnp.exp(s - jnp.max(s, axis=-1, keepdims=True)).astype(BF16)
    oe = _dot(p, vext[...])
    on = oe[:, :LANES] / oe[:, LANES:LANES + 1]
    o = on[:tq] - lam * on[tq:]
    o_ref[...] = (_rms(o, sg_ref[...]) * (1.0 - lam_init)).astype(BF16)


def _attn_call(lam_p, q, k_all, v_all, subln_g, *, tq, seq, n_batch, lam_init):
    n_heads = q.shape[1] // LANES
    lk = k_all.shape[1]
    qt = seq // tq
    return pl.pallas_call(
        functools.partial(_attn_kernel, tq=tq, lam_init=lam_init),
        grid=(n_batch, n_heads, qt),
        in_specs=[pl.BlockSpec(lam_p.shape, lambda b, h, i: (0, 0)),
                  pl.BlockSpec((tq, LANES), lambda b, h, i: (b * qt + i, h)),
                  pl.BlockSpec((None, lk, LANES), lambda b, h, i: (b, 0, h)),
                  pl.BlockSpec((None, lk, LANES), lambda b, h, i: (b, 0, h)),
                  pl.BlockSpec((1, LANES), lambda b, h, i: (0, 0))],
        out_specs=pl.BlockSpec((tq, LANES), lambda b, h, i: (b * qt + i, h)),
        out_shape=jax.ShapeDtypeStruct((n_batch * seq, n_heads * LANES), BF16),
        scratch_shapes=[pltpu.VMEM((lk, 2 * LANES), BF16)],
        compiler_params=_cparams(3, 48),
        name="diff_attention",
    )(lam_p, q, k_all, v_all, subln_g)


def kernel(x, c, ctx, c_ctx, ada_w, ada_b, norm_g, cf_w_in, cf_conv_w, cf_conv_b, cf_ln_g, cf_ln_b,
           cf_w_out, da_w_qkv, da_lambda, da_subln_g, da_w_out, moe_router_w, moe_router_b,
           moe_w_gate, moe_w_up, moe_w_down, moe_sh_gate, moe_sh_up, moe_sh_down):
    n_batch, seq, d = x.shape
    ctx_len = ctx.shape[1]
    depth = ada_w.shape[0]
    assert depth == 2 and n_batch + 1 <= N_SEG
    assert DA_QK_DIM * 2 == LANES and da_subln_g.shape[1] == LANES
    rows_lat, rows_ctx = n_batch * seq, n_batch * ctx_len
    tm = 512
    tq = 256
    assert seq % tm == 0 and rows_ctx % tm == 0 and seq % tq == 0 and ctx_len % tq == 0

    xs = jnp.concatenate([x.reshape(rows_lat, d), ctx.reshape(rows_ctx, d)], axis=0)
    cond = jnp.concatenate([c, c_ctx[None, :], jnp.zeros((N_SEG - n_batch - 1, d), F32)], axis=0)
    mods = _mods_call(cond, ada_w, ada_b)
    mod = lambda layer, k: mods[layer, :, k * d:(k + 1) * d][:, None, :]
    gain = lambda layer, k: norm_g[layer, k][None, :]
    common = dict(seq=seq, n_batch=n_batch)

    cc = cf_conv_w.shape[-1]
    fc = cf_w_in.shape[2] - 2 * cc
    gw = fc // FOURIER_GROUPS
    ch_ang = 2.0 * np.pi * (np.outer(np.arange(gw), np.arange(gw)) % gw) / gw
    cs = jnp.asarray(np.concatenate([np.cos(ch_ang), np.sin(ch_ang)], axis=1) / math.sqrt(gw), BF16)
    a_glu, ab = _inproj_call(xs, gain(0, 0), mod(0, 0), mod(0, 1), cf_w_in[0].astype(BF16), cs,
                             cc=cc, gw=gw, tm=tm, **common)
    gsz = cc // CONV_GROUPS
    gid = np.arange(cc) // gsz
    gmean = jnp.asarray((gid[:, None] == gid[None, :]) / gsz, BF16)
    a_act = _conv_call(a_glu, cf_conv_w[0][:, 0, :], cf_conv_b[0][None, :], cf_ln_g[0][None, :],
                       cf_ln_b[0][None, :], gmean, tl=tq, seq=seq, ctx_len=ctx_len, n_batch=n_batch)
    c_lat, s_lat = _dft_tables(seq, 1.0 / math.sqrt(seq))
    c_ctx_m, s_ctx_m = _dft_tables(ctx_len, 1.0 / math.sqrt(ctx_len))
    fr = jnp.concatenate([
        _seqdft_call(ab, c_lat, s_lat, length=seq, row0=0, n_seq=n_batch, fc=fc, tm=min(seq, 256)),
        _seqdft_call(ab, c_ctx_m, s_ctx_m, length=ctx_len, row0=rows_lat, n_seq=n_batch, fc=fc,
                     tm=min(ctx_len, 256))], axis=0)
    w_out = cf_w_out[0].astype(BF16)
    xs, tok, tokp = _outproj_call([a_act, fr], [w_out[:cc], w_out[cc:]], xs, gain(0, 1), mod(0, 2),
                            gain(0, 2), mod(0, 3), mod(0, 4), rows=rows_lat + rows_ctx, tm=tm, **common)
    xs = _moe_layer(xs, tok, tokp, gain(0, 2), mod(0, 3), mod(0, 4), gain(0, 3), mod(0, 5),
                    moe_router_w[0], moe_router_b[0], moe_w_gate[0], moe_w_up[0], moe_w_down[0],
                    moe_sh_gate[0], moe_sh_up[0], moe_sh_down[0], rows=rows_lat + rows_ctx, tm=tm, **common)

    qkw = DA_HEADS * 2 * DA_QK_DIM
    lam_init = 0.8 - 0.6 * math.exp(-0.3 * 1)
    cos_t, sin_t = _rope_tables(seq, tq)
    q, k_all, v_all = _qkv_call(xs, gain(1, 0), mod(1, 0), mod(1, 1), da_w_qkv[0].astype(BF16),
                                cos_t, sin_t, qkw=qkw, tm=tq, ctx_len=ctx_len, **common)
    o = _attn_call(da_lambda[0], q, k_all, v_all, da_subln_g[0][None, :], tq=tq, seq=seq,
                   n_batch=n_batch, lam_init=lam_init)
    xs, tok, tokp = _outproj_call([o], [da_w_out[0].astype(BF16)], xs, gain(1, 1), mod(1, 2),
                            gain(1, 2), mod(1, 3), mod(1, 4), rows=rows_lat, tm=tm, **common)
    xs = _moe_layer(xs, tok, tokp, gain(1, 2), mod(1, 3), mod(1, 4), gain(1, 3), mod(1, 5),
                    moe_router_w[1], moe_router_b[1], moe_w_gate[1], moe_w_up[1], moe_w_down[1],
                    moe_sh_gate[1], moe_sh_up[1], moe_sh_down[1], rows=rows_lat, tm=tm, **common)
    return xs.reshape(n_batch, seq, d)
```

```python
import functools
import math

import numpy as np
import jax
import jax.numpy as jnp
from jax import lax
from jax.experimental import pallas as pl
from jax.experimental.pallas import tpu as pltpu
from jax.experimental.pallas import tpu_sc as plsc

F32 = jnp.float32
BF16 = jnp.bfloat16

EPS = 1e-6
GRID_W = 64
CONV_GROUPS = 8
FOURIER_GROUPS = 4
DA_HEADS = 8
DA_QK_DIM = 64
ROPE_BASE = 10000.0
N_GROUPS = 8
TOPK_GROUPS = 4
TOP_K = 6
ROUTED_SCALE = 2.5

LANES = 128
SUBLANES = 8
N_SEG = 16
HALO = 16
EXPERT_ROWS = 512
EXPERT_SUB_ROWS = 128
ATTN_SUB_ROWS = 128
MIB = 1024 * 1024


def _cparams(n_axes, vmem_mib):
    return pltpu.CompilerParams(dimension_semantics=("arbitrary",) * n_axes,
                                vmem_limit_bytes=vmem_mib * MIB)


def _sigmoid(v):
    return 1.0 / (1.0 + jnp.exp(-v))


def _rms(v, g):
    return v * lax.rsqrt(jnp.mean(v * v, axis=-1, keepdims=True) + EPS) * g


def _normmod(v, g, shift, scale):
    return _rms(v, g) * (1.0 + scale) + shift


def _split_bf16(v):
    hi = v.astype(BF16)
    lo = (v - hi.astype(F32)).astype(BF16)
    return hi, lo


def _pack_halves(v):
    half = v.shape[1] // 2
    word = pltpu.pack_elementwise([v[:, :half], v[:, half:]], packed_dtype=BF16)
    return lax.bitcast_convert_type(word, jnp.int32)


def _unpack_halves(w):
    u = lax.bitcast_convert_type(w, jnp.uint32)
    lo = lax.bitcast_convert_type(u << 16, F32)
    hi = lax.bitcast_convert_type(u & jnp.uint32(0xFFFF0000), F32)
    return lo, hi


SC_SPLIT = 2


def _store_pieces(ref, words):
    q = words.shape[1] // SC_SPLIT
    for j in range(SC_SPLIT):
        ref[j] = words[:, j * q:(j + 1) * q]


def _load_pieces(ref):
    return jnp.concatenate([ref[j] for j in range(SC_SPLIT)], axis=-1)


def _dot(a, b):
    return jnp.dot(a, b, preferred_element_type=F32)


def _dot_nt(a, b):
    return lax.dot_general(a, b, (((1,), (1,)), ((), ())), preferred_element_type=F32)


def _mods_kernel(c_ref, w_ref, b_ref, o_ref):
    cv = c_ref[...]
    o_ref[...] = _dot(cv * _sigmoid(cv), w_ref[...]) + b_ref[...]


def _mods_call(cond, ada_w, ada_b):
    depth, d, n = ada_w.shape
    tn = n // 4
    return pl.pallas_call(
        _mods_kernel,
        grid=(depth, n // tn),
        in_specs=[pl.BlockSpec((N_SEG, d), lambda l, j: (0, 0)),
                  pl.BlockSpec((None, d, tn), lambda l, j: (l, 0, j)),
                  pl.BlockSpec((None, 1, tn), lambda l, j: (l, 0, j))],
        out_specs=pl.BlockSpec((None, N_SEG, tn), lambda l, j: (l, 0, j)),
        out_shape=jax.ShapeDtypeStruct((depth, N_SEG, n), F32),
        compiler_params=_cparams(2, 40),
        name="adaln_mods",
    )(cond, ada_w, ada_b.reshape(depth, 1, n))


def _inproj_kernel(x_ref, g_ref, sh_ref, sc_ref, w_ref, cs_ref, a_ref, ab_ref, *, cc, gw):
    h = _normmod(x_ref[...], g_ref[...], sh_ref[...], sc_ref[...]).astype(BF16)
    u = _dot(h, w_ref[...])
    a_ref[...] = u[:, :cc] * _sigmoid(u[:, cc:2 * cc])
    f = u[:, 2 * cc:].astype(BF16)
    n_g = f.shape[1] // gw
    parts = [_dot(f[:, g * gw:(g + 1) * gw], cs_ref[...]) for g in range(n_g)]
    cos_part = [p[:, :gw] for p in parts]
    sin_part = [p[:, gw:] for p in parts]
    ab_ref[...] = jnp.concatenate(cos_part + sin_part, axis=-1).astype(BF16)


def _seg_map(tm, seq, n_batch):
    return lambda i: (jnp.minimum(i * tm // seq, n_batch), 0, 0)


def _inproj_call(xs, g, shift, scale, w_in, cs, *, cc, gw, tm, seq, n_batch):
    rows, d = xs.shape
    n = w_in.shape[1]
    fc = n - 2 * cc
    seg = _seg_map(tm, seq, n_batch)
    return pl.pallas_call(
        functools.partial(_inproj_kernel, cc=cc, gw=gw),
        grid=(rows // tm,),
        in_specs=[pl.BlockSpec((tm, d), lambda i: (i, 0)),
                  pl.BlockSpec((1, d), lambda i: (0, 0)),
                  pl.BlockSpec((None, 1, d), seg),
                  pl.BlockSpec((None, 1, d), seg),
                  pl.BlockSpec((d, n), lambda i: (0, 0)),
                  pl.BlockSpec((gw, 2 * gw), lambda i: (0, 0))],
        out_specs=[pl.BlockSpec((tm, cc), lambda i: (i, 0)),
                   pl.BlockSpec((tm, 2 * fc), lambda i: (i, 0))],
        out_shape=[jax.ShapeDtypeStruct((rows, cc), F32),
                   jax.ShapeDtypeStruct((rows, 2 * fc), BF16)],
        compiler_params=_cparams(1, 40),
        name="inproj_glu_chdft",
    )(xs, g, shift, scale, w_in, cs)


def _conv_kernel(prev_ref, main_ref, next_ref, w_ref, cb_ref, lg_ref, lb_ref, gm_ref, o_ref,
                 buf, cv, *, tl, width, lat_tiles, lat_tps, ctx_tps, chunk):
    i = pl.program_id(0)
    is_lat = i < lat_tiles
    tps = jnp.where(is_lat, lat_tps, ctx_tps)
    j = jnp.where(is_lat, i, i - lat_tiles) % tps
    zero = jnp.zeros((HALO, buf.shape[1]), F32)
    buf[0:HALO, :] = jnp.where(j > 0, prev_ref[...], zero)
    buf[HALO:HALO + tl, :] = main_ref[...]
    buf[HALO + tl:HALO + tl + HALO, :] = jnp.where(j < tps - 1, next_ref[...], zero)
    base = HALO - width // 2
    for r0 in range(0, tl, chunk):
        acc = buf[base + r0:base + r0 + chunk, :] * w_ref[0:1, :]
        for t in range(1, width):
            acc = acc + buf[base + r0 + t:base + r0 + t + chunk, :] * w_ref[t:t + 1, :]
        cv[r0:r0 + chunk, :] = acc
    a = cv[...] + cb_ref[...]
    gm = gm_ref[...]
    a_hi, a_lo = _split_bf16(a)
    mu = _dot(a_hi, gm) + _dot(a_lo, gm)
    dl = a - mu
    q_hi, q_lo = _split_bf16(dl * dl)
    var = _dot(q_hi, gm) + _dot(q_lo, gm)
    y = dl * lax.rsqrt(var + EPS) * lg_ref[...] + lb_ref[...]
    o_ref[...] = (y * _sigmoid(y)).astype(BF16)


def _conv_call(a_glu, conv_w, conv_b, ln_g, ln_b, gmean, *, tl, seq, ctx_len, n_batch):
    rows, ch = a_glu.shape
    width = conv_w.shape[0]
    assert width // 2 <= HALO and tl % HALO == 0
    hb = tl // HALO
    last_halo = rows // HALO - 1
    return pl.pallas_call(
        functools.partial(_conv_kernel, tl=tl, width=width, lat_tiles=n_batch * seq // tl,
                          lat_tps=seq // tl, ctx_tps=ctx_len // tl, chunk=32),
        grid=(rows // tl,),
        in_specs=[pl.BlockSpec((HALO, ch), lambda i: (jnp.maximum(i * hb - 1, 0), 0)),
                  pl.BlockSpec((tl, ch), lambda i: (i, 0)),
                  pl.BlockSpec((HALO, ch), lambda i: (jnp.minimum((i + 1) * hb, last_halo), 0)),
                  pl.BlockSpec((width, ch), lambda i: (0, 0)),
                  pl.BlockSpec((1, ch), lambda i: (0, 0)),
                  pl.BlockSpec((1, ch), lambda i: (0, 0)),
                  pl.BlockSpec((1, ch), lambda i: (0, 0)),
                  pl.BlockSpec((ch, ch), lambda i: (0, 0))],
        out_specs=pl.BlockSpec((tl, ch), lambda i: (i, 0)),
        out_shape=jax.ShapeDtypeStruct((rows, ch), BF16),
        scratch_shapes=[pltpu.VMEM((tl + 2 * HALO, ch), F32), pltpu.VMEM((tl, ch), F32)],
        compiler_params=_cparams(1, 40),
        name="dwconv_groupln_swish",
    )(a_glu, a_glu, a_glu, conv_w, conv_b, ln_g, ln_b, gmean)


def _seqdft_kernel(c_ref, s_ref, a_ref, b_ref, o_ref):
    o_ref[...] = (_dot(c_ref[...], a_ref[...]) - _dot(s_ref[...], b_ref[...])).astype(BF16)


def _seqdft_call(ab, cmat, smat, *, length, row0, n_seq, fc, tm):
    seq0 = row0 // length
    return pl.pallas_call(
        _seqdft_kernel,
        grid=(n_seq, length // tm),
        in_specs=[pl.BlockSpec((tm, length), lambda b, i: (i, 0)),
                  pl.BlockSpec((tm, length), lambda b, i: (i, 0)),
                  pl.BlockSpec((length, fc), lambda b, i: (seq0 + b, 0)),
                  pl.BlockSpec((length, fc), lambda b, i: (seq0 + b, 1))],
        out_specs=pl.BlockSpec((tm, fc), lambda b, i: (b * (length // tm) + i, 0)),
        out_shape=jax.ShapeDtypeStruct((n_seq * length, fc), BF16),
        compiler_params=_cparams(2, 48),
        name="seq_dft",
    )(cmat, smat, ab, ab)


def _dft_tables(length, scale):
    if length <= 512:
        kn = np.outer(np.arange(length), np.arange(length)) % length
        ang = 2.0 * np.pi * kn / length
        return (jnp.asarray(np.cos(ang) * scale, BF16), jnp.asarray(np.sin(ang) * scale, BF16))
    r = 64
    assert length % r == 0
    k = np.arange(length)[:, None]
    alpha = 2.0 * np.pi * ((k * np.arange(length // r)[None, :] * r) % length) / length
    beta = 2.0 * np.pi * ((k * np.arange(r)[None, :]) % length) / length
    ca, sa = jnp.asarray(np.cos(alpha), F32)[:, :, None], jnp.asarray(np.sin(alpha), F32)[:, :, None]
    cb, sb = jnp.asarray(np.cos(beta) * scale, F32)[:, None, :], jnp.asarray(np.sin(beta) * scale, F32)[:, None, :]
    cmat = (ca * cb - sa * sb).reshape(length, length).astype(BF16)
    smat = (sa * cb + ca * sb).reshape(length, length).astype(BF16)
    return cmat, smat


def _outproj_kernel(*refs, n_in):
    ins, ws = refs[:n_in], refs[n_in:2 * n_in]
    x_ref, g1_ref, gate_ref, g2_ref, sh_ref, sc_ref, xo_ref, tok_ref, tokp_ref = refs[2 * n_in:]
    y = _dot(ins[0][...], ws[0][...])
    for a_ref, w_ref in zip(ins[1:], ws[1:]):
        y = y + _dot(a_ref[...], w_ref[...])
    x1 = x_ref[...] + gate_ref[...] * _rms(y, g1_ref[...])
    xo_ref[...] = x1
    tok = _normmod(x1, g2_ref[...], sh_ref[...], sc_ref[...])
    tok_ref[...] = tok.astype(BF16)
    _store_pieces(tokp_ref, _pack_halves(tok))


def _outproj_call(ins, ws, xs, g1, gate, g2, shift, scale, *, rows, tm, seq, n_batch):
    d = xs.shape[1]
    seg = _seg_map(tm, seq, n_batch)
    n_in = len(ins)
    row_spec = lambda a: pl.BlockSpec((tm, a.shape[1]), lambda i: (i, 0))
    full_spec = lambda a: pl.BlockSpec(a.shape, lambda i: (0, 0))
    return pl.pallas_call(
        functools.partial(_outproj_kernel, n_in=n_in),
        grid=(rows // tm,),
        in_specs=[row_spec(a) for a in ins] + [full_spec(w) for w in ws] + [
            row_spec(xs), full_spec(g1), pl.BlockSpec((None, 1, d), seg), full_spec(g2),
            pl.BlockSpec((None, 1, d), seg), pl.BlockSpec((None, 1, d), seg)],
        out_specs=[pl.BlockSpec((tm, d), lambda i: (i, 0)), pl.BlockSpec((tm, d), lambda i: (i, 0)),
                   pl.BlockSpec((SC_SPLIT, tm, d // 2 // SC_SPLIT), lambda i: (0, i, 0))],
        out_shape=[jax.ShapeDtypeStruct((rows, d), F32), jax.ShapeDtypeStruct((rows, d), BF16),
                   jax.ShapeDtypeStruct((SC_SPLIT, rows, d // 2 // SC_SPLIT), jnp.int32)],
        compiler_params=_cparams(1, 40),
        name="outproj_residual",
    )(*ins, *ws, xs, g1, gate, g2, shift, scale)


def _pick_max(cur, idx):
    mx = jnp.max(cur, axis=0, keepdims=True)
    first = jnp.min(jnp.where(cur == mx, idx, float(cur.shape[0])), axis=0, keepdims=True)
    return first, idx == first


def _router_kernel(x_ref, g_ref, sh_ref, sc_ref, wh_ref, wl_ref, rb_ref,
                   e_ref, gt_ref, pos_ref, cnt_ref, carry, *, n_exp):
    i = pl.program_id(0)

    @pl.when(i == 0)
    def _():
        carry[...] = jnp.zeros_like(carry)

    tok = _normmod(x_ref[...], g_ref[...], sh_ref[...], sc_ref[...])
    t_hi, t_lo = _split_bf16(tok)
    wh, wl = wh_ref[...], wl_ref[...]
    logits = _dot_nt(wh, t_hi) + _dot_nt(wh, t_lo) + _dot_nt(wl, t_hi)
    tm = logits.shape[1]
    scores = _sigmoid(logits)
    biased = scores + rb_ref[...]
    gsz = n_exp // N_GROUPS
    neg = -jnp.inf

    b3 = biased.reshape(N_GROUPS, gsz, tm)
    im = lax.broadcasted_iota(jnp.int32, b3.shape, 1).astype(F32)
    m1 = jnp.max(b3, axis=1, keepdims=True)
    i1 = jnp.min(jnp.where(b3 == m1, im, float(gsz)), axis=1, keepdims=True)
    m2 = jnp.max(jnp.where(im == i1, neg, b3), axis=1, keepdims=True)
    gscore = (m1 + m2).reshape(N_GROUPS, tm)

    ig = lax.broadcasted_iota(jnp.int32, gscore.shape, 0).astype(F32)
    gsel = jnp.zeros_like(gscore)
    cur = gscore
    for _ in range(TOPK_GROUPS):
        _, hit = _pick_max(cur, ig)
        gsel = jnp.where(hit, 1.0, gsel)
        cur = jnp.where(hit, neg, cur)
    gsel3 = jnp.broadcast_to(gsel.reshape(N_GROUPS, 1, tm), b3.shape)
    cur = jnp.where(gsel3 > 0.0, b3, neg).reshape(n_exp, tm)

    ie = lax.broadcasted_iota(jnp.int32, (n_exp, tm), 0).astype(F32)
    sel = jnp.zeros((n_exp, tm), F32)
    picks, raw = [], []
    for _ in range(TOP_K):
        first, hit = _pick_max(cur, ie)
        picks.append(first)
        raw.append(jnp.sum(jnp.where(hit, scores, 0.0), axis=0, keepdims=True))
        sel = jnp.where(hit, 1.0, sel)
        cur = jnp.where(hit, neg, cur)
    total = raw[0]
    for r in raw[1:]:
        total = total + r

    ri = lax.broadcasted_iota(jnp.int32, (tm, tm), 0)
    ci = lax.broadcasted_iota(jnp.int32, (tm, tm), 1)
    upper = jnp.where(ri < ci, 1.0, 0.0).astype(BF16)
    rank = _dot(sel.astype(BF16), upper) + carry[...]
    ranks = [jnp.sum(jnp.where(ie == p, rank, 0.0), axis=0, keepdims=True) for p in picks]
    carry[...] = carry[...] + jnp.sum(sel, axis=1, keepdims=True)

    pad = jnp.zeros((SUBLANES - TOP_K, tm), F32)
    e_ref[...] = jnp.concatenate(picks + [pad], axis=0).astype(jnp.int32)
    gt_ref[...] = jnp.concatenate([r / total * ROUTED_SCALE for r in raw] + [pad], axis=0)
    pos_ref[...] = jnp.concatenate(ranks + [pad], axis=0).astype(jnp.int32)
    cnt_ref[...] = jnp.broadcast_to(carry[...], cnt_ref.shape)


def _router_call(xs, g, shift, scale, w_hi_t, w_lo_t, rbias, *, rows, tm, seq, n_batch):
    d = xs.shape[1]
    n_exp = w_hi_t.shape[0]
    seg = _seg_map(tm, seq, n_batch)
    tok_rows = lambda dt: jax.ShapeDtypeStruct((SUBLANES, rows), dt)
    lane_spec = pl.BlockSpec((SUBLANES, tm), lambda i: (0, i))
    return pl.pallas_call(
        functools.partial(_router_kernel, n_exp=n_exp),
        grid=(rows // tm,),
        in_specs=[pl.BlockSpec((tm, d), lambda i: (i, 0)),
                  pl.BlockSpec((1, d), lambda i: (0, 0)),
                  pl.BlockSpec((None, 1, d), seg),
                  pl.BlockSpec((None, 1, d), seg),
                  pl.BlockSpec((n_exp, d), lambda i: (0, 0)),
                  pl.BlockSpec((n_exp, d), lambda i: (0, 0)),
                  pl.BlockSpec((n_exp, 1), lambda i: (0, 0))],
        out_specs=[lane_spec, lane_spec, lane_spec,
                   pl.BlockSpec((n_exp, LANES), lambda i: (0, 0))],
        out_shape=[tok_rows(jnp.int32), tok_rows(F32), tok_rows(jnp.int32),
                   jax.ShapeDtypeStruct((n_exp, LANES), F32)],
        scratch_shapes=[pltpu.VMEM((n_exp, 1), F32)],
        compiler_params=_cparams(1, 40),
        name="moe_router",
    )(xs, g, shift, scale, w_hi_t, w_lo_t, rbias)


def _expert_kernel(be_ref, nv_ref, nu_ref, xs_ref, wg_ref, wu_ref, wd_ref, ys_ref, wgu, wdn, *, ff, n_sub):
    b = pl.program_id(0)
    changed = jnp.logical_or(b == 0, be_ref[b] != be_ref[jnp.maximum(b - 1, 0)])

    @pl.when(changed)
    def _():
        wgu[:, :ff] = wg_ref[...].astype(BF16)
        wgu[:, ff:] = wu_ref[...].astype(BF16)
        wdn[...] = wd_ref[...].astype(BF16)

    @pl.when(b < nu_ref[0])
    def _():
        n_valid = nv_ref[b]
        q = xs_ref.shape[2]
        ts = xs_ref.shape[1] // n_sub
        gus = []
        for a in range(n_sub):
            r0 = a * ts
            xw = jnp.concatenate([xs_ref[j, r0:r0 + ts, :] for j in range(SC_SPLIT)], axis=-1)
            row = lax.broadcasted_iota(jnp.int32, xw.shape, 0) + r0
            lo, hi = _unpack_halves(jnp.where(row < n_valid, xw, 0))
            half = lo.shape[1]
            gus.append(_dot(lo.astype(BF16), wgu[:half, :]) + _dot(hi.astype(BF16), wgu[half:, :]))
        outs = []
        for gu in gus:
            gate = gu[:, :ff]
            hmid = (gate * _sigmoid(gate) * gu[:, ff:]).astype(BF16)
            outs.append(_dot(hmid, wdn[...]))
        for a, y in enumerate(outs):
            words = _pack_halves(y)
            for j in range(SC_SPLIT):
                ys_ref[j, a * ts:(a + 1) * ts, :] = words[:, j * q:(j + 1) * q]


def _expert_call(block_e, n_valid, n_used, xs, w_gate, w_up, w_down):
    _, n_rows, q = xs.shape
    d = 2 * SC_SPLIT * q
    ff = w_gate.shape[2]
    te = EXPERT_ROWS
    row_map = lambda b, be, nv, nu: (0, jnp.minimum(b, nu[0] - 1), 0)
    w_map = lambda b, be, nv, nu: (be[b], 0, 0)
    grid_spec = pltpu.PrefetchScalarGridSpec(
        num_scalar_prefetch=3,
        grid=(n_rows // te,),
        in_specs=[pl.BlockSpec((SC_SPLIT, te, q), row_map),
                  pl.BlockSpec((None, d, ff), w_map),
                  pl.BlockSpec((None, d, ff), w_map),
                  pl.BlockSpec((None, ff, d), w_map)],
        out_specs=pl.BlockSpec((SC_SPLIT, te, q), row_map),
        scratch_shapes=[pltpu.VMEM((d, 2 * ff), BF16), pltpu.VMEM((ff, d), BF16)])
    return pl.pallas_call(
        functools.partial(_expert_kernel, ff=ff, n_sub=EXPERT_ROWS // EXPERT_SUB_ROWS),
        grid_spec=grid_spec,
        out_shape=jax.ShapeDtypeStruct(xs.shape, jnp.int32),
        compiler_params=_cparams(1, 40),
        name="moe_experts",
    )(block_e, n_valid, n_used, xs, w_gate, w_up, w_down)


SC_WINDOW = 128


def _sc_mesh():
    return plsc.VectorSubcoreMesh(core_axis_name="core", subcore_axis_name="subcore")


def _sc_scatter_rows(rows, dests, n_out):
    n, width = rows.shape
    n_k = len(dests)
    assert n % SC_WINDOW == 0

    @functools.partial(pl.kernel, out_type=jax.ShapeDtypeStruct((n_out, width), rows.dtype),
                       mesh=_sc_mesh(), name="sc_dispatch_rows")
    def scatter_kernel(x_hbm, *refs):
        idx_hbm, o_hbm = refs[:n_k], refs[n_k]

        def body(x_vmem, *idx_vmem):
            for iv in idx_vmem:
                pltpu.sync_copy(x_vmem, o_hbm.at[iv.at[0]])

        pltpu.emit_pipeline(
            body,
            grid=(n // SC_WINDOW,),
            in_specs=[pl.BlockSpec((SC_WINDOW, width), lambda i: (i, 0))]
            + [pl.BlockSpec((1, SC_WINDOW), lambda i: (0, i))] * n_k,
            out_specs=[],
            core_axis_name=("core", "subcore"),
            dimension_semantics=(pltpu.PARALLEL,),
        )(x_hbm, *idx_hbm)

    return scatter_kernel(rows, *dests)


def _sc_gather_rows(table, idx):
    n = idx.shape[1]
    width = table.shape[1]
    assert n % SC_WINDOW == 0

    @functools.partial(pl.kernel, out_type=jax.ShapeDtypeStruct((n, width), table.dtype),
                       mesh=_sc_mesh(), name="sc_collect_rows")
    def gather_kernel(t_hbm, i_hbm, o_hbm):
        def body(i_vmem, o_vmem):
            pltpu.sync_copy(t_hbm.at[i_vmem.at[0]], o_vmem)

        pltpu.emit_pipeline(
            body,
            grid=(n // SC_WINDOW,),
            in_specs=[pl.BlockSpec((1, SC_WINDOW), lambda i: (0, i))],
            out_specs=[pl.BlockSpec((SC_WINDOW, width), lambda i: (i, 0))],
            core_axis_name=("core", "subcore"),
            dimension_semantics=(pltpu.PARALLEL,),
        )(i_hbm, o_hbm)

    return gather_kernel(table, idx)


def _moe_out_kernel(tok_ref, yg_ref, gk_ref, x_ref, sgu_ref, sd_ref, g_ref, gate_ref, xo_ref, *, ff):
    gu = _dot(tok_ref[...], sgu_ref[...])
    gate = gu[:, :ff]
    hmid = (gate * _sigmoid(gate) * gu[:, ff:]).astype(BF16)
    shared = _dot(hmid, sd_ref[...])
    gk = gk_ref[...]
    lo_acc, hi_acc = None, None
    for k in range(TOP_K):
        lo, hi = _unpack_halves(jnp.concatenate([yg_ref[j, k] for j in range(SC_SPLIT)], axis=-1))
        w = gk[:, k:k + 1]
        lo_acc = lo * w if lo_acc is None else lo_acc + lo * w
        hi_acc = hi * w if hi_acc is None else hi_acc + hi * w
    y = jnp.concatenate([lo_acc, hi_acc], axis=-1) + shared
    xo_ref[...] = x_ref[...] + gate_ref[...] * _rms(y, g_ref[...])


def _moe_out_call(tok, yg, gk, xs, sgu, sd, g, gate, *, rows, tm, seq, n_batch):
    d = xs.shape[1]
    ff = sd.shape[0]
    seg = _seg_map(tm, seq, n_batch)
    row = lambda: pl.BlockSpec((tm, d), lambda i: (i, 0))
    return pl.pallas_call(
        functools.partial(_moe_out_kernel, ff=ff),
        grid=(rows // tm,),
        in_specs=[row(),
                  pl.BlockSpec((SC_SPLIT, TOP_K, tm, d // 2 // SC_SPLIT), lambda i: (0, 0, i, 0)),
                  pl.BlockSpec((tm, SUBLANES), lambda i: (i, 0)),
                  row(),
                  pl.BlockSpec((d, 2 * ff), lambda i: (0, 0)),
                  pl.BlockSpec((ff, d), lambda i: (0, 0)),
                  pl.BlockSpec((1, d), lambda i: (0, 0)),
                  pl.BlockSpec((None, 1, d), seg)],
        out_specs=row(),
        out_shape=jax.ShapeDtypeStruct((rows, d), F32),
        compiler_params=_cparams(1, 48),
        name="moe_shared_combine",
    )(tok, yg, gk, xs, sgu, sd, g, gate)


def _moe_layer(xs, tok, tokp, g_in, shift, scale, g_out, gate, router_w, router_b, w_gate, w_up, w_down,
               sh_gate, sh_up, sh_down, *, rows, tm, seq, n_batch):
    n_exp = router_w.shape[1]
    te = EXPERT_ROWS
    w_t = router_w.T
    w_hi = w_t.astype(BF16)
    w_lo = (w_t - w_hi.astype(F32)).astype(BF16)
    e_tk, g_tk, pos_tk, counts = _router_call(xs, g_in, shift, scale, w_hi, w_lo,
                                              router_b.reshape(n_exp, 1),
                                              rows=rows, tm=tm, seq=seq, n_batch=n_batch)
    e_tk, pos_tk = e_tk[:TOP_K], pos_tk[:TOP_K]
    cnt = counts[:, 0].astype(jnp.int32)
    padded = (cnt + te - 1) // te * te
    pad_ends = jnp.cumsum(padded)
    pad_starts = pad_ends - padded
    expert_ids = jnp.arange(n_exp, dtype=jnp.int32)
    onehot = e_tk[:, :, None] == expert_ids[None, None, :]
    dest = jnp.sum(jnp.where(onehot, pad_starts[None, None, :], 0), axis=-1) + pos_tk
    n_blocks = -(-(rows * TOP_K + n_exp * (te - 1)) // te)
    n_used = pad_ends[-1] // te
    blk = jnp.arange(n_blocks, dtype=jnp.int32)
    last = jnp.minimum(blk, n_used - 1) * te
    owner = (last[:, None] >= pad_ends[None, :]).astype(jnp.int32)
    block_e = jnp.minimum(jnp.sum(owner, axis=1), n_exp - 1)
    own_hot = block_e[:, None] == expert_ids[None, :]
    n_valid = jnp.sum(jnp.where(own_hot, (pad_starts + cnt)[None, :], 0), axis=1) - last
    n_valid = jnp.clip(n_valid, 0, te)
    q = tokp.shape[2]
    n_rows = n_blocks * te
    piece = dest[None, :, :] + (jnp.arange(SC_SPLIT, dtype=jnp.int32) * n_rows)[:, None, None]
    xg = _sc_scatter_rows(tokp.reshape(SC_SPLIT * rows, q),
                          [piece[:, k, :].reshape(1, SC_SPLIT * rows) for k in range(TOP_K)],
                          SC_SPLIT * n_rows).reshape(SC_SPLIT, n_rows, q)
    ys = _expert_call(block_e, n_valid, n_used.reshape(1).astype(jnp.int32), xg, w_gate, w_up, w_down)
    yg = _sc_gather_rows(ys.reshape(SC_SPLIT * n_rows, q),
                         piece.reshape(1, -1)).reshape(SC_SPLIT, TOP_K, rows, q)
    sgu = jnp.concatenate([sh_gate, sh_up], axis=1).astype(BF16)
    return _moe_out_call(tok, yg, g_tk.T, xs, sgu, sh_down.astype(BF16), g_out, gate,
                         rows=rows, tm=tm, seq=seq, n_batch=n_batch)


def _rope(t, cos, sin):
    half = DA_QK_DIM // 2
    up = pltpu.roll(t, shift=LANES - half, axis=1)
    dn = pltpu.roll(t, shift=half, axis=1)
    lane = lax.broadcasted_iota(jnp.int32, t.shape, 1) % DA_QK_DIM
    return t * cos + jnp.where(lane < half, -up, dn) * sin


def _qkv_kernel(x_ref, g_ref, sh_ref, sc_ref, w_ref, cos_ref, sin_ref, q_ref, k_ref, v_ref, *, qkw):
    h = _normmod(x_ref[...], g_ref[...], sh_ref[...], sc_ref[...]).astype(BF16)
    qkv = _dot(h, w_ref[...])
    cos, sin = cos_ref[...], sin_ref[...]
    q_scale = DA_QK_DIM ** -0.5 * math.log2(math.e)
    for hb in range(qkw // LANES):
        lo, hi = hb * LANES, (hb + 1) * LANES
        q_ref[:, lo:hi] = (_rope(qkv[:, lo:hi], cos, sin) * q_scale).astype(BF16)
        k_ref[:, lo:hi] = _rope(qkv[:, qkw + lo:qkw + hi], cos, sin).astype(BF16)
    v_ref[...] = qkv[:, 2 * qkw:].astype(BF16)


def _qkv_call(xs, g, shift, scale, w_qkv, cos_t, sin_t, *, qkw, tm, seq, ctx_len, n_batch):
    rows, d = xs.shape
    n = w_qkv.shape[1]
    vw = n - 2 * qkw
    lat_tiles, lat_tps, ctx_tps = n_batch * seq // tm, seq // tm, ctx_len // tm
    seg = _seg_map(tm, seq, n_batch)

    def kv_map(i):
        c = i - lat_tiles
        is_lat = i < lat_tiles
        return (jnp.where(is_lat, i // lat_tps, c // ctx_tps),
                jnp.where(is_lat, ctx_tps + i % lat_tps, c % ctx_tps), 0)

    rope_map = lambda i: (jnp.where(i < lat_tiles, i % lat_tps, lat_tps), 0)
    return pl.pallas_call(
        functools.partial(_qkv_kernel, qkw=qkw),
        grid=(rows // tm,),
        in_specs=[pl.BlockSpec((tm, d), lambda i: (i, 0)),
                  pl.BlockSpec((1, d), lambda i: (0, 0)),
                  pl.BlockSpec((None, 1, d), seg),
                  pl.BlockSpec((None, 1, d), seg),
                  pl.BlockSpec((d, n), lambda i: (0, 0)),
                  pl.BlockSpec((tm, LANES), rope_map),
                  pl.BlockSpec((tm, LANES), rope_map)],
        out_specs=[pl.BlockSpec((tm, qkw), lambda i: (i, 0)),
                   pl.BlockSpec((None, tm, qkw), kv_map),
                   pl.BlockSpec((None, tm, vw), kv_map)],
        out_shape=[jax.ShapeDtypeStruct((rows, qkw), BF16),
                   jax.ShapeDtypeStruct((n_batch, ctx_len + seq, qkw), BF16),
                   jax.ShapeDtypeStruct((n_batch, ctx_len + seq, vw), BF16)],
        compiler_params=_cparams(1, 48),
        name="qkv_rope",
    )(xs, g, shift, scale, w_qkv, cos_t, sin_t)


def _rope_tables(seq, tm):
    pos = np.arange(seq)
    n_freq = DA_QK_DIM // 4
    inv = np.power(ROPE_BASE, -np.arange(n_freq, dtype=np.float32) / n_freq).astype(np.float32)
    row = (pos // GRID_W).astype(np.float32)[:, None] * inv
    col = (pos % GRID_W).astype(np.float32)[:, None] * inv
    ang = np.concatenate([row, col], axis=-1).astype(np.float32)
    ang = np.tile(ang, (1, LANES // ang.shape[1]))
    cos = np.concatenate([np.cos(ang), np.ones((tm, LANES))], axis=0)
    sin = np.concatenate([np.sin(ang), np.zeros((tm, LANES))], axis=0)
    return jnp.asarray(cos, F32), jnp.asarray(sin, F32)


def _attn_kernel(lp_ref, q_ref, k_ref, v_ref, sg_ref, o_ref, vext, *, n_sub, lam_init):
    @pl.when(pl.program_id(2) == 0)
    def _():
        vext[:, :LANES] = v_ref[...]
        vext[:, LANES:] = jnp.ones((vext.shape[0], LANES), BF16)

    lp = lp_ref[...]
    lam = (jnp.exp(jnp.sum(lp[0:1] * lp[1:2], axis=1, keepdims=True))
           - jnp.exp(jnp.sum(lp[2:3] * lp[3:4], axis=1, keepdims=True)) + lam_init)
    ts = q_ref.shape[0] // n_sub
    lane = lax.broadcasted_iota(jnp.int32, (ts, LANES), 1)
    zero = jnp.zeros((ts, LANES), BF16)
    k = k_ref[...]
    scores = []
    for a in range(n_sub):
        q = q_ref[a * ts:(a + 1) * ts, :]
        qq = jnp.concatenate([jnp.where(lane < DA_QK_DIM, q, zero),
                              jnp.where(lane >= DA_QK_DIM, q, zero)], axis=0)
        scores.append(_dot_nt(qq, k))
    for a in range(n_sub):
        s = scores[a]
        p = jnp.exp2(s - jnp.max(s, axis=-1, keepdims=True)).astype(BF16)
        oe = _dot(p, vext[...])
        on = oe[:, :LANES] / oe[:, LANES:LANES + 1]
        o = on[:ts] - lam * on[ts:]
        o_ref[a * ts:(a + 1) * ts, :] = (_rms(o, sg_ref[...]) * (1.0 - lam_init)).astype(BF16)


def _attn_call(lam_p, q, k_all, v_all, subln_g, *, tq, seq, n_batch, lam_init):
    n_heads = q.shape[1] // LANES
    lk = k_all.shape[1]
    qt = seq // tq
    return pl.pallas_call(
        functools.partial(_attn_kernel, n_sub=tq // ATTN_SUB_ROWS, lam_init=lam_init),
        grid=(n_batch, n_heads, qt),
        in_specs=[pl.BlockSpec(lam_p.shape, lambda b, h, i: (0, 0)),
                  pl.BlockSpec((tq, LANES), lambda b, h, i: (b * qt + i, h)),
                  pl.BlockSpec((None, lk, LANES), lambda b, h, i: (b, 0, h)),
                  pl.BlockSpec((None, lk, LANES), lambda b, h, i: (b, 0, h)),
                  pl.BlockSpec((1, LANES), lambda b, h, i: (0, 0))],
        out_specs=pl.BlockSpec((tq, LANES), lambda b, h, i: (b * qt + i, h)),
        out_shape=jax.ShapeDtypeStruct((n_batch * seq, n_heads * LANES), BF16),
        scratch_shapes=[pltpu.VMEM((lk, 2 * LANES), BF16)],
        compiler_params=_cparams(3, 48),
        name="diff_attention",
    )(lam_p, q, k_all, v_all, subln_g)


def kernel(x, c, ctx, c_ctx, ada_w, ada_b, norm_g, cf_w_in, cf_conv_w, cf_conv_b, cf_ln_g, cf_ln_b,
           cf_w_out, da_w_qkv, da_lambda, da_subln_g, da_w_out, moe_router_w, moe_router_b,
           moe_w_gate, moe_w_up, moe_w_down, moe_sh_gate, moe_sh_up, moe_sh_down):
    n_batch, seq, d = x.shape
    ctx_len = ctx.shape[1]
    depth = ada_w.shape[0]
    assert depth == 2 and n_batch + 1 <= N_SEG
    assert DA_QK_DIM * 2 == LANES and da_subln_g.shape[1] == LANES
    rows_lat, rows_ctx = n_batch * seq, n_batch * ctx_len
    tm = 512
    tq = 256
    assert seq % tm == 0 and rows_ctx % tm == 0 and seq % tq == 0 and ctx_len % tq == 0

    xs = jnp.concatenate([x.reshape(rows_lat, d), ctx.reshape(rows_ctx, d)], axis=0)
    cond = jnp.concatenate([c, c_ctx[None, :], jnp.zeros((N_SEG - n_batch - 1, d), F32)], axis=0)
    mods = _mods_call(cond, ada_w, ada_b)
    mod = lambda layer, k: mods[layer, :, k * d:(k + 1) * d][:, None, :]
    gain = lambda layer, k: norm_g[layer, k][None, :]
    common = dict(seq=seq, n_batch=n_batch)

    cc = cf_conv_w.shape[-1]
    fc = cf_w_in.shape[2] - 2 * cc
    gw = fc // FOURIER_GROUPS
    ch_ang = 2.0 * np.pi * (np.outer(np.arange(gw), np.arange(gw)) % gw) / gw
    cs = jnp.asarray(np.concatenate([np.cos(ch_ang), np.sin(ch_ang)], axis=1) / math.sqrt(gw), BF16)
    a_glu, ab = _inproj_call(xs, gain(0, 0), mod(0, 0), mod(0, 1), cf_w_in[0].astype(BF16), cs,
                             cc=cc, gw=gw, tm=tm, **common)
    gsz = cc // CONV_GROUPS
    gid = np.arange(cc) // gsz
    gmean = jnp.asarray((gid[:, None] == gid[None, :]) / gsz, BF16)
    a_act = _conv_call(a_glu, cf_conv_w[0][:, 0, :], cf_conv_b[0][None, :], cf_ln_g[0][None, :],
                       cf_ln_b[0][None, :], gmean, tl=tq, seq=seq, ctx_len=ctx_len, n_batch=n_batch)
    c_lat, s_lat = _dft_tables(seq, 1.0 / math.sqrt(seq))
    c_ctx_m, s_ctx_m = _dft_tables(ctx_len, 1.0 / math.sqrt(ctx_len))
    fr = jnp.concatenate([
        _seqdft_call(ab, c_lat, s_lat, length=seq, row0=0, n_seq=n_batch, fc=fc, tm=min(seq, 256)),
        _seqdft_call(ab, c_ctx_m, s_ctx_m, length=ctx_len, row0=rows_lat, n_seq=n_batch, fc=fc,
                     tm=min(ctx_len, 256))], axis=0)
    w_out = cf_w_out[0].astype(BF16)
    xs, tok, tokp = _outproj_call([a_act, fr], [w_out[:cc], w_out[cc:]], xs, gain(0, 1), mod(0, 2),
                            gain(0, 2), mod(0, 3), mod(0, 4), rows=rows_lat + rows_ctx, tm=tm, **common)
    xs = _moe_layer(xs, tok, tokp, gain(0, 2), mod(0, 3), mod(0, 4), gain(0, 3), mod(0, 5),
                    moe_router_w[0], moe_router_b[0], moe_w_gate[0], moe_w_up[0], moe_w_down[0],
                    moe_sh_gate[0], moe_sh_up[0], moe_sh_down[0], rows=rows_lat + rows_ctx, tm=tm, **common)

    qkw = DA_HEADS * 2 * DA_QK_DIM
    lam_init = 0.8 - 0.6 * math.exp(-0.3 * 1)
    cos_t, sin_t = _rope_tables(seq, tq)
    q, k_all, v_all = _qkv_call(xs, gain(1, 0), mod(1, 0), mod(1, 1), da_w_qkv[0].astype(BF16),
                                cos_t, sin_t, qkw=qkw, tm=tq, ctx_len=ctx_len, **common)
    o = _attn_call(da_lambda[0], q, k_all, v_all, da_subln_g[0][None, :], tq=tm, seq=seq,
                   n_batch=n_batch, lam_init=lam_init)
    xs, tok, tokp = _outproj_call([o], [da_w_out[0].astype(BF16)], xs, gain(1, 1), mod(1, 2),
                            gain(1, 2), mod(1, 3), mod(1, 4), rows=rows_lat, tm=tm, **common)
    xs = _moe_layer(xs, tok, tokp, gain(1, 2), mod(1, 3), mod(1, 4), gain(1, 3), mod(1, 5),
                    moe_router_w[1], moe_router_b[1], moe_w_gate[1], moe_w_up[1], moe_w_down[1],
                    moe_sh_gate[1], moe_sh_up[1], moe_sh_down[1], rows=rows_lat, tm=tm, **common)
    return xs.reshape(n_batch, seq, d)
```

```python
import functools
import math

import numpy as np
import jax
import jax.numpy as jnp
from jax import lax
from jax.experimental import pallas as pl
from jax.experimental.pallas import tpu as pltpu
from jax.experimental.pallas import tpu_sc as plsc

F32 = jnp.float32
BF16 = jnp.bfloat16

EPS = 1e-6
GRID_W = 64
CONV_GROUPS = 8
FOURIER_GROUPS = 4
DA_HEADS = 8
DA_QK_DIM = 64
ROPE_BASE = 10000.0
N_GROUPS = 8
TOPK_GROUPS = 4
TOP_K = 6
ROUTED_SCALE = 2.5

LANES = 128
SUBLANES = 8
N_SEG = 16
HALO = 16
EXPERT_ROWS = 512
EXPERT_SUB_ROWS = 128
ATTN_SUB_ROWS = 128
MIB = 1024 * 1024


def _cparams(n_axes, vmem_mib):
    return pltpu.CompilerParams(dimension_semantics=("arbitrary",) * n_axes,
                                vmem_limit_bytes=vmem_mib * MIB)


def _sigmoid(v):
    return 1.0 / (1.0 + jnp.exp(-v))


def _rms(v, g):
    return v * lax.rsqrt(jnp.mean(v * v, axis=-1, keepdims=True) + EPS) * g


def _normmod(v, g, shift, scale):
    return _rms(v, g) * (1.0 + scale) + shift


def _split_bf16(v):
    hi = v.astype(BF16)
    lo = (v - hi.astype(F32)).astype(BF16)
    return hi, lo


def _pack_halves(v):
    half = v.shape[1] // 2
    word = pltpu.pack_elementwise([v[:, :half], v[:, half:]], packed_dtype=BF16)
    return lax.bitcast_convert_type(word, jnp.int32)


def _unpack_halves(w):
    u = lax.bitcast_convert_type(w, jnp.uint32)
    lo = lax.bitcast_convert_type(u << 16, F32)
    hi = lax.bitcast_convert_type(u & jnp.uint32(0xFFFF0000), F32)
    return lo, hi


SC_SPLIT = 2


def _store_pieces(ref, words):
    q = words.shape[1] // SC_SPLIT
    for j in range(SC_SPLIT):
        ref[j] = words[:, j * q:(j + 1) * q]


def _load_pieces(ref):
    return jnp.concatenate([ref[j] for j in range(SC_SPLIT)], axis=-1)


def _dot(a, b):
    return jnp.dot(a, b, preferred_element_type=F32)


def _dot_nt(a, b):
    return lax.dot_general(a, b, (((1,), (1,)), ((), ())), preferred_element_type=F32)


def _mods_kernel(c_ref, w_ref, b_ref, o_ref):
    cv = c_ref[...]
    o_ref[...] = _dot(cv * _sigmoid(cv), w_ref[...]) + b_ref[...]


def _mods_call(cond, ada_w, ada_b):
    depth, d, n = ada_w.shape
    tn = n // 4
    return pl.pallas_call(
        _mods_kernel,
        grid=(depth, n // tn),
        in_specs=[pl.BlockSpec((N_SEG, d), lambda l, j: (0, 0)),
                  pl.BlockSpec((None, d, tn), lambda l, j: (l, 0, j)),
                  pl.BlockSpec((None, 1, tn), lambda l, j: (l, 0, j))],
        out_specs=pl.BlockSpec((None, N_SEG, tn), lambda l, j: (l, 0, j)),
        out_shape=jax.ShapeDtypeStruct((depth, N_SEG, n), F32),
        compiler_params=_cparams(2, 40),
        name="adaln_mods",
    )(cond, ada_w, ada_b.reshape(depth, 1, n))


def _inproj_kernel(x_ref, g_ref, sh_ref, sc_ref, w_ref, cs_ref, a_ref, ab_ref, *, cc, gw):
    h = _normmod(x_ref[...], g_ref[...], sh_ref[...], sc_ref[...]).astype(BF16)
    u = _dot(h, w_ref[...])
    a_ref[...] = u[:, :cc] * _sigmoid(u[:, cc:2 * cc])
    f = u[:, 2 * cc:].astype(BF16)
    n_g = f.shape[1] // gw
    parts = [_dot(f[:, g * gw:(g + 1) * gw], cs_ref[...]) for g in range(n_g)]
    cos_part = [p[:, :gw] for p in parts]
    sin_part = [p[:, gw:] for p in parts]
    ab_ref[...] = jnp.concatenate(cos_part + sin_part, axis=-1).astype(BF16)


def _seg_map(tm, seq, n_batch):
    return lambda i: (jnp.minimum(i * tm // seq, n_batch), 0, 0)


def _inproj_call(xs, g, shift, scale, w_in, cs, *, cc, gw, tm, seq, n_batch):
    rows, d = xs.shape
    n = w_in.shape[1]
    fc = n - 2 * cc
    seg = _seg_map(tm, seq, n_batch)
    return pl.pallas_call(
        functools.partial(_inproj_kernel, cc=cc, gw=gw),
        grid=(rows // tm,),
        in_specs=[pl.BlockSpec((tm, d), lambda i: (i, 0)),
                  pl.BlockSpec((1, d), lambda i: (0, 0)),
                  pl.BlockSpec((None, 1, d), seg),
                  pl.BlockSpec((None, 1, d), seg),
                  pl.BlockSpec((d, n), lambda i: (0, 0)),
                  pl.BlockSpec((gw, 2 * gw), lambda i: (0, 0))],
        out_specs=[pl.BlockSpec((tm, cc), lambda i: (i, 0)),
                   pl.BlockSpec((tm, 2 * fc), lambda i: (i, 0))],
        out_shape=[jax.ShapeDtypeStruct((rows, cc), F32),
                   jax.ShapeDtypeStruct((rows, 2 * fc), BF16)],
        compiler_params=_cparams(1, 40),
        name="inproj_glu_chdft",
    )(xs, g, shift, scale, w_in, cs)


def _conv_kernel(prev_ref, main_ref, next_ref, w_ref, cb_ref, lg_ref, lb_ref, gm_ref, o_ref,
                 buf, cv, shifted, *, tl, width, lat_tiles, lat_tps, ctx_tps, chunk):
    i = pl.program_id(0)
    is_lat = i < lat_tiles
    tps = jnp.where(is_lat, lat_tps, ctx_tps)
    j = jnp.where(is_lat, i, i - lat_tiles) % tps
    zero = jnp.zeros((HALO, buf.shape[1]), F32)
    buf[0:HALO, :] = jnp.where(j > 0, prev_ref[...], zero)
    buf[HALO:HALO + tl, :] = main_ref[...]
    buf[HALO + tl:HALO + tl + HALO, :] = jnp.where(j < tps - 1, next_ref[...], zero)
    base = HALO - width // 2
    span = shifted.shape[1]
    for s in range(1, SUBLANES):
        shifted[s] = buf[s:s + span, :]

    def window(off, r0):
        s, m = off % SUBLANES, off - off % SUBLANES
        if s == 0:
            return buf[m + r0:m + r0 + chunk, :]
        return shifted[s, m + r0:m + r0 + chunk, :]

    for r0 in range(0, tl, chunk):
        acc = window(base, r0) * w_ref[0:1, :]
        for t in range(1, width):
            acc = acc + window(base + t, r0) * w_ref[t:t + 1, :]
        cv[r0:r0 + chunk, :] = acc
    a = cv[...] + cb_ref[...]
    gm = gm_ref[...]
    a_hi, a_lo = _split_bf16(a)
    mu = _dot(a_hi, gm) + _dot(a_lo, gm)
    dl = a - mu
    q_hi, q_lo = _split_bf16(dl * dl)
    var = _dot(q_hi, gm) + _dot(q_lo, gm)
    y = dl * lax.rsqrt(var + EPS) * lg_ref[...] + lb_ref[...]
    o_ref[...] = (y * _sigmoid(y)).astype(BF16)


def _conv_call(a_glu, conv_w, conv_b, ln_g, ln_b, gmean, *, tl, seq, ctx_len, n_batch):
    rows, ch = a_glu.shape
    width = conv_w.shape[0]
    assert width // 2 <= HALO and tl % HALO == 0
    hb = tl // HALO
    last_halo = rows // HALO - 1
    return pl.pallas_call(
        functools.partial(_conv_kernel, tl=tl, width=width, lat_tiles=n_batch * seq // tl,
                          lat_tps=seq // tl, ctx_tps=ctx_len // tl, chunk=32),
        grid=(rows // tl,),
        in_specs=[pl.BlockSpec((HALO, ch), lambda i: (jnp.maximum(i * hb - 1, 0), 0)),
                  pl.BlockSpec((tl, ch), lambda i: (i, 0)),
                  pl.BlockSpec((HALO, ch), lambda i: (jnp.minimum((i + 1) * hb, last_halo), 0)),
                  pl.BlockSpec((width, ch), lambda i: (0, 0)),
                  pl.BlockSpec((1, ch), lambda i: (0, 0)),
                  pl.BlockSpec((1, ch), lambda i: (0, 0)),
                  pl.BlockSpec((1, ch), lambda i: (0, 0)),
                  pl.BlockSpec((ch, ch), lambda i: (0, 0))],
        out_specs=pl.BlockSpec((tl, ch), lambda i: (i, 0)),
        out_shape=jax.ShapeDtypeStruct((rows, ch), BF16),
        scratch_shapes=[pltpu.VMEM((tl + 2 * HALO, ch), F32), pltpu.VMEM((tl, ch), F32),
                        pltpu.VMEM((SUBLANES, tl + 2 * HALO - SUBLANES, ch), F32)],
        compiler_params=_cparams(1, 40),
        name="dwconv_groupln_swish",
    )(a_glu, a_glu, a_glu, conv_w, conv_b, ln_g, ln_b, gmean)


def _seqdft_kernel(c_ref, s_ref, a_ref, b_ref, o_ref):
    o_ref[...] = (_dot(c_ref[...], a_ref[...]) - _dot(s_ref[...], b_ref[...])).astype(BF16)


def _seqdft_call(ab, cmat, smat, *, length, row0, n_seq, fc, tm):
    seq0 = row0 // length
    return pl.pallas_call(
        _seqdft_kernel,
        grid=(n_seq, length // tm),
        in_specs=[pl.BlockSpec((tm, length), lambda b, i: (i, 0)),
                  pl.BlockSpec((tm, length), lambda b, i: (i, 0)),
                  pl.BlockSpec((length, fc), lambda b, i: (seq0 + b, 0)),
                  pl.BlockSpec((length, fc), lambda b, i: (seq0 + b, 1))],
        out_specs=pl.BlockSpec((tm, fc), lambda b, i: (b * (length // tm) + i, 0)),
        out_shape=jax.ShapeDtypeStruct((n_seq * length, fc), BF16),
        compiler_params=_cparams(2, 48),
        name="seq_dft",
    )(cmat, smat, ab, ab)


def _dft_tables(length, scale):
    if length <= 512:
        kn = np.outer(np.arange(length), np.arange(length)) % length
        ang = 2.0 * np.pi * kn / length
        return (jnp.asarray(np.cos(ang) * scale, BF16), jnp.asarray(np.sin(ang) * scale, BF16))
    r = 64
    assert length % r == 0
    k = np.arange(length)[:, None]
    alpha = 2.0 * np.pi * ((k * np.arange(length // r)[None, :] * r) % length) / length
    beta = 2.0 * np.pi * ((k * np.arange(r)[None, :]) % length) / length
    ca, sa = jnp.asarray(np.cos(alpha), F32)[:, :, None], jnp.asarray(np.sin(alpha), F32)[:, :, None]
    cb, sb = jnp.asarray(np.cos(beta) * scale, F32)[:, None, :], jnp.asarray(np.sin(beta) * scale, F32)[:, None, :]
    cmat = (ca * cb - sa * sb).reshape(length, length).astype(BF16)
    smat = (sa * cb + ca * sb).reshape(length, length).astype(BF16)
    return cmat, smat


def _outproj_kernel(*refs, n_in):
    ins, ws = refs[:n_in], refs[n_in:2 * n_in]
    x_ref, g1_ref, gate_ref, g2_ref, sh_ref, sc_ref, xo_ref, tok_ref, tokp_ref = refs[2 * n_in:]
    y = _dot(ins[0][...], ws[0][...])
    for a_ref, w_ref in zip(ins[1:], ws[1:]):
        y = y + _dot(a_ref[...], w_ref[...])
    x1 = x_ref[...] + gate_ref[...] * _rms(y, g1_ref[...])
    xo_ref[...] = x1
    tok = _normmod(x1, g2_ref[...], sh_ref[...], sc_ref[...])
    tok_ref[...] = tok.astype(BF16)
    _store_pieces(tokp_ref, _pack_halves(tok))


def _outproj_call(ins, ws, xs, g1, gate, g2, shift, scale, *, rows, tm, seq, n_batch):
    d = xs.shape[1]
    seg = _seg_map(tm, seq, n_batch)
    n_in = len(ins)
    row_spec = lambda a: pl.BlockSpec((tm, a.shape[1]), lambda i: (i, 0))
    full_spec = lambda a: pl.BlockSpec(a.shape, lambda i: (0, 0))
    return pl.pallas_call(
        functools.partial(_outproj_kernel, n_in=n_in),
        grid=(rows // tm,),
        in_specs=[row_spec(a) for a in ins] + [full_spec(w) for w in ws] + [
            row_spec(xs), full_spec(g1), pl.BlockSpec((None, 1, d), seg), full_spec(g2),
            pl.BlockSpec((None, 1, d), seg), pl.BlockSpec((None, 1, d), seg)],
        out_specs=[pl.BlockSpec((tm, d), lambda i: (i, 0)), pl.BlockSpec((tm, d), lambda i: (i, 0)),
                   pl.BlockSpec((SC_SPLIT, tm, d // 2 // SC_SPLIT), lambda i: (0, i, 0))],
        out_shape=[jax.ShapeDtypeStruct((rows, d), F32), jax.ShapeDtypeStruct((rows, d), BF16),
                   jax.ShapeDtypeStruct((SC_SPLIT, rows, d // 2 // SC_SPLIT), jnp.int32)],
        compiler_params=_cparams(1, 40),
        name="outproj_residual",
    )(*ins, *ws, xs, g1, gate, g2, shift, scale)


def _pick_max(cur, idx):
    mx = jnp.max(cur, axis=0, keepdims=True)
    first = jnp.min(jnp.where(cur == mx, idx, float(cur.shape[0])), axis=0, keepdims=True)
    return first, idx == first


def _router_kernel(x_ref, g_ref, sh_ref, sc_ref, wh_ref, wl_ref, rb_ref,
                   e_ref, gt_ref, pos_ref, cnt_ref, carry, *, n_exp):
    i = pl.program_id(0)

    @pl.when(i == 0)
    def _():
        carry[...] = jnp.zeros_like(carry)

    tok = _normmod(x_ref[...], g_ref[...], sh_ref[...], sc_ref[...])
    t_hi, t_lo = _split_bf16(tok)
    wh, wl = wh_ref[...], wl_ref[...]
    logits = _dot_nt(wh, t_hi) + _dot_nt(wh, t_lo) + _dot_nt(wl, t_hi)
    tm = logits.shape[1]
    scores = _sigmoid(logits)
    biased = scores + rb_ref[...]
    gsz = n_exp // N_GROUPS
    neg = -jnp.inf

    b3 = biased.reshape(N_GROUPS, gsz, tm)
    im = lax.broadcasted_iota(jnp.int32, b3.shape, 1).astype(F32)
    m1 = jnp.max(b3, axis=1, keepdims=True)
    i1 = jnp.min(jnp.where(b3 == m1, im, float(gsz)), axis=1, keepdims=True)
    m2 = jnp.max(jnp.where(im == i1, neg, b3), axis=1, keepdims=True)
    gscore = (m1 + m2).reshape(N_GROUPS, tm)

    ig = lax.broadcasted_iota(jnp.int32, gscore.shape, 0).astype(F32)
    gsel = jnp.zeros_like(gscore)
    cur = gscore
    for _ in range(TOPK_GROUPS):
        _, hit = _pick_max(cur, ig)
        gsel = jnp.where(hit, 1.0, gsel)
        cur = jnp.where(hit, neg, cur)
    gsel3 = jnp.broadcast_to(gsel.reshape(N_GROUPS, 1, tm), b3.shape)
    cur = jnp.where(gsel3 > 0.0, b3, neg).reshape(n_exp, tm)

    ie = lax.broadcasted_iota(jnp.int32, (n_exp, tm), 0).astype(F32)
    sel = jnp.zeros((n_exp, tm), F32)
    picks, raw = [], []
    for _ in range(TOP_K):
        first, hit = _pick_max(cur, ie)
        picks.append(first)
        raw.append(jnp.sum(jnp.where(hit, scores, 0.0), axis=0, keepdims=True))
        sel = jnp.where(hit, 1.0, sel)
        cur = jnp.where(hit, neg, cur)
    total = raw[0]
    for r in raw[1:]:
        total = total + r

    ri = lax.broadcasted_iota(jnp.int32, (tm, tm), 0)
    ci = lax.broadcasted_iota(jnp.int32, (tm, tm), 1)
    upper = jnp.where(ri < ci, 1.0, 0.0).astype(BF16)
    rank = _dot(sel.astype(BF16), upper) + carry[...]
    ranks = [jnp.sum(jnp.where(ie == p, rank, 0.0), axis=0, keepdims=True) for p in picks]
    carry[...] = carry[...] + jnp.sum(sel, axis=1, keepdims=True)

    pad = jnp.zeros((SUBLANES - TOP_K, tm), F32)
    e_ref[...] = jnp.concatenate(picks + [pad], axis=0).astype(jnp.int32)
    gt_ref[...] = jnp.concatenate([r / total * ROUTED_SCALE for r in raw] + [pad], axis=0)
    pos_ref[...] = jnp.concatenate(ranks + [pad], axis=0).astype(jnp.int32)
    cnt_ref[...] = jnp.broadcast_to(carry[...], cnt_ref.shape)


def _router_call(xs, g, shift, scale, w_hi_t, w_lo_t, rbias, *, rows, tm, seq, n_batch):
    d = xs.shape[1]
    n_exp = w_hi_t.shape[0]
    seg = _seg_map(tm, seq, n_batch)
    tok_rows = lambda dt: jax.ShapeDtypeStruct((SUBLANES, rows), dt)
    lane_spec = pl.BlockSpec((SUBLANES, tm), lambda i: (0, i))
    return pl.pallas_call(
        functools.partial(_router_kernel, n_exp=n_exp),
        grid=(rows // tm,),
        in_specs=[pl.BlockSpec((tm, d), lambda i: (i, 0)),
                  pl.BlockSpec((1, d), lambda i: (0, 0)),
                  pl.BlockSpec((None, 1, d), seg),
                  pl.BlockSpec((None, 1, d), seg),
                  pl.BlockSpec((n_exp, d), lambda i: (0, 0)),
                  pl.BlockSpec((n_exp, d), lambda i: (0, 0)),
                  pl.BlockSpec((n_exp, 1), lambda i: (0, 0))],
        out_specs=[lane_spec, lane_spec, lane_spec,
                   pl.BlockSpec((n_exp, LANES), lambda i: (0, 0))],
        out_shape=[tok_rows(jnp.int32), tok_rows(F32), tok_rows(jnp.int32),
                   jax.ShapeDtypeStruct((n_exp, LANES), F32)],
        scratch_shapes=[pltpu.VMEM((n_exp, 1), F32)],
        compiler_params=_cparams(1, 40),
        name="moe_router",
    )(xs, g, shift, scale, w_hi_t, w_lo_t, rbias)


def _expert_kernel(be_ref, nv_ref, nu_ref, xs_ref, wg_ref, wu_ref, wd_ref, ys_ref, wgu, wdn, *, ff, n_sub):
    b = pl.program_id(0)
    changed = jnp.logical_or(b == 0, be_ref[b] != be_ref[jnp.maximum(b - 1, 0)])

    @pl.when(changed)
    def _():
        wgu[:, :ff] = wg_ref[...].astype(BF16)
        wgu[:, ff:] = wu_ref[...].astype(BF16)
        wdn[...] = wd_ref[...].astype(BF16)

    @pl.when(b < nu_ref[0])
    def _():
        n_valid = nv_ref[b]
        q = xs_ref.shape[2]
        ts = xs_ref.shape[1] // n_sub
        gus = []
        for a in range(n_sub):
            r0 = a * ts
            xw = jnp.concatenate([xs_ref[j, r0:r0 + ts, :] for j in range(SC_SPLIT)], axis=-1)
            row = lax.broadcasted_iota(jnp.int32, xw.shape, 0) + r0
            lo, hi = _unpack_halves(jnp.where(row < n_valid, xw, 0))
            half = lo.shape[1]
            gus.append(_dot(lo.astype(BF16), wgu[:half, :]) + _dot(hi.astype(BF16), wgu[half:, :]))
        outs = []
        for gu in gus:
            gate = gu[:, :ff]
            hmid = (gate * _sigmoid(gate) * gu[:, ff:]).astype(BF16)
            outs.append(_dot(hmid, wdn[...]))
        for a, y in enumerate(outs):
            words = _pack_halves(y)
            for j in range(SC_SPLIT):
                ys_ref[j, a * ts:(a + 1) * ts, :] = words[:, j * q:(j + 1) * q]


def _expert_call(block_e, n_valid, n_used, xs, layer, w_gate, w_up, w_down):
    _, n_rows, q = xs.shape
    d = 2 * SC_SPLIT * q
    ff = w_gate.shape[3]
    te = EXPERT_ROWS
    row_map = lambda b, be, nv, nu: (0, jnp.minimum(b, nu[0] - 1), 0)
    w_map = lambda b, be, nv, nu: (layer, be[b], 0, 0)
    grid_spec = pltpu.PrefetchScalarGridSpec(
        num_scalar_prefetch=3,
        grid=(n_rows // te,),
        in_specs=[pl.BlockSpec((SC_SPLIT, te, q), row_map),
                  pl.BlockSpec((None, None, d, ff), w_map),
                  pl.BlockSpec((None, None, d, ff), w_map),
                  pl.BlockSpec((None, None, ff, d), w_map)],
        out_specs=pl.BlockSpec((SC_SPLIT, te, q), row_map),
        scratch_shapes=[pltpu.VMEM((d, 2 * ff), BF16), pltpu.VMEM((ff, d), BF16)])
    return pl.pallas_call(
        functools.partial(_expert_kernel, ff=ff, n_sub=EXPERT_ROWS // EXPERT_SUB_ROWS),
        grid_spec=grid_spec,
        out_shape=jax.ShapeDtypeStruct(xs.shape, jnp.int32),
        compiler_params=_cparams(1, 40),
        name="moe_experts",
    )(block_e, n_valid, n_used, xs, w_gate, w_up, w_down)


SC_WINDOW = 128


def _sc_mesh():
    return plsc.VectorSubcoreMesh(core_axis_name="core", subcore_axis_name="subcore")


def _sc_scatter_rows(rows, dests, n_out):
    n, width = rows.shape
    n_k = len(dests)
    assert n % SC_WINDOW == 0

    @functools.partial(pl.kernel, out_type=jax.ShapeDtypeStruct((n_out, width), rows.dtype),
                       mesh=_sc_mesh(), name="sc_dispatch_rows")
    def scatter_kernel(x_hbm, *refs):
        idx_hbm, o_hbm = refs[:n_k], refs[n_k]

        def body(x_vmem, *idx_vmem):
            for iv in idx_vmem:
                pltpu.sync_copy(x_vmem, o_hbm.at[iv.at[0]])

        pltpu.emit_pipeline(
            body,
            grid=(n // SC_WINDOW,),
            in_specs=[pl.BlockSpec((SC_WINDOW, width), lambda i: (i, 0))]
            + [pl.BlockSpec((1, SC_WINDOW), lambda i: (0, i))] * n_k,
            out_specs=[],
            core_axis_name=("core", "subcore"),
            dimension_semantics=(pltpu.PARALLEL,),
        )(x_hbm, *idx_hbm)

    return scatter_kernel(rows, *dests)


def _sc_gather_rows(table, idx):
    n = idx.shape[1]
    width = table.shape[1]
    assert n % SC_WINDOW == 0

    @functools.partial(pl.kernel, out_type=jax.ShapeDtypeStruct((n, width), table.dtype),
                       mesh=_sc_mesh(), name="sc_collect_rows")
    def gather_kernel(t_hbm, i_hbm, o_hbm):
        def body(i_vmem, o_vmem):
            pltpu.sync_copy(t_hbm.at[i_vmem.at[0]], o_vmem)

        pltpu.emit_pipeline(
            body,
            grid=(n // SC_WINDOW,),
            in_specs=[pl.BlockSpec((1, SC_WINDOW), lambda i: (0, i))],
            out_specs=[pl.BlockSpec((SC_WINDOW, width), lambda i: (i, 0))],
            core_axis_name=("core", "subcore"),
            dimension_semantics=(pltpu.PARALLEL,),
        )(i_hbm, o_hbm)

    return gather_kernel(table, idx)


def _moe_out_kernel(tok_ref, yg_ref, gk_ref, x_ref, sgu_ref, sd_ref, g_ref, gate_ref, xo_ref, *, ff):
    gu = _dot(tok_ref[...], sgu_ref[...])
    gate = gu[:, :ff]
    hmid = (gate * _sigmoid(gate) * gu[:, ff:]).astype(BF16)
    shared = _dot(hmid, sd_ref[...])
    gk = gk_ref[...]
    lo_acc, hi_acc = None, None
    for k in range(TOP_K):
        lo, hi = _unpack_halves(jnp.concatenate([yg_ref[j, k] for j in range(SC_SPLIT)], axis=-1))
        w = gk[:, k:k + 1]
        lo_acc = lo * w if lo_acc is None else lo_acc + lo * w
        hi_acc = hi * w if hi_acc is None else hi_acc + hi * w
    y = jnp.concatenate([lo_acc, hi_acc], axis=-1) + shared
    xo_ref[...] = x_ref[...] + gate_ref[...] * _rms(y, g_ref[...])


def _moe_out_call(tok, yg, gk, xs, sgu, sd, g, gate, *, rows, tm, seq, n_batch):
    d = xs.shape[1]
    ff = sd.shape[0]
    seg = _seg_map(tm, seq, n_batch)
    row = lambda: pl.BlockSpec((tm, d), lambda i: (i, 0))
    return pl.pallas_call(
        functools.partial(_moe_out_kernel, ff=ff),
        grid=(rows // tm,),
        in_specs=[row(),
                  pl.BlockSpec((SC_SPLIT, TOP_K, tm, d // 2 // SC_SPLIT), lambda i: (0, 0, i, 0)),
                  pl.BlockSpec((tm, SUBLANES), lambda i: (i, 0)),
                  row(),
                  pl.BlockSpec((d, 2 * ff), lambda i: (0, 0)),
                  pl.BlockSpec((ff, d), lambda i: (0, 0)),
                  pl.BlockSpec((1, d), lambda i: (0, 0)),
                  pl.BlockSpec((None, 1, d), seg)],
        out_specs=row(),
        out_shape=jax.ShapeDtypeStruct((rows, d), F32),
        compiler_params=_cparams(1, 48),
        name="moe_shared_combine",
    )(tok, yg, gk, xs, sgu, sd, g, gate)


def _moe_layer(xs, tok, tokp, g_in, shift, scale, g_out, gate, router_w, router_b, layer, w_gate, w_up,
               w_down, sh_gate, sh_up, sh_down, *, rows, tm, seq, n_batch):
    n_exp = router_w.shape[1]
    te = EXPERT_ROWS
    w_t = router_w.T
    w_hi = w_t.astype(BF16)
    w_lo = (w_t - w_hi.astype(F32)).astype(BF16)
    e_tk, g_tk, pos_tk, counts = _router_call(xs, g_in, shift, scale, w_hi, w_lo,
                                              router_b.reshape(n_exp, 1),
                                              rows=rows, tm=tm, seq=seq, n_batch=n_batch)
    e_tk, pos_tk = e_tk[:TOP_K], pos_tk[:TOP_K]
    cnt = counts[:, 0].astype(jnp.int32)
    padded = (cnt + te - 1) // te * te
    pad_ends = jnp.cumsum(padded)
    pad_starts = pad_ends - padded
    expert_ids = jnp.arange(n_exp, dtype=jnp.int32)
    onehot = e_tk[:, :, None] == expert_ids[None, None, :]
    dest = jnp.sum(jnp.where(onehot, pad_starts[None, None, :], 0), axis=-1) + pos_tk
    n_blocks = -(-(rows * TOP_K + n_exp * (te - 1)) // te)
    n_used = pad_ends[-1] // te
    blk = jnp.arange(n_blocks, dtype=jnp.int32)
    last = jnp.minimum(blk, n_used - 1) * te
    owner = (last[:, None] >= pad_ends[None, :]).astype(jnp.int32)
    block_e = jnp.minimum(jnp.sum(owner, axis=1), n_exp - 1)
    own_hot = block_e[:, None] == expert_ids[None, :]
    n_valid = jnp.sum(jnp.where(own_hot, (pad_starts + cnt)[None, :], 0), axis=1) - last
    n_valid = jnp.clip(n_valid, 0, te)
    q = tokp.shape[2]
    n_rows = n_blocks * te
    piece = dest[None, :, :] + (jnp.arange(SC_SPLIT, dtype=jnp.int32) * n_rows)[:, None, None]
    xg = _sc_scatter_rows(tokp.reshape(SC_SPLIT * rows, q),
                          [piece[:, k, :].reshape(1, SC_SPLIT * rows) for k in range(TOP_K)],
                          SC_SPLIT * n_rows).reshape(SC_SPLIT, n_rows, q)
    ys = _expert_call(block_e, n_valid, n_used.reshape(1).astype(jnp.int32), xg, layer, w_gate, w_up, w_down)
    yg = _sc_gather_rows(ys.reshape(SC_SPLIT * n_rows, q),
                         piece.reshape(1, -1)).reshape(SC_SPLIT, TOP_K, rows, q)
    sgu = jnp.concatenate([sh_gate, sh_up], axis=1).astype(BF16)
    return _moe_out_call(tok, yg, g_tk.T, xs, sgu, sh_down.astype(BF16), g_out, gate,
                         rows=rows, tm=tm, seq=seq, n_batch=n_batch)


def _rope(t, cos, sin):
    half = DA_QK_DIM // 2
    up = pltpu.roll(t, shift=LANES - half, axis=1)
    dn = pltpu.roll(t, shift=half, axis=1)
    lane = lax.broadcasted_iota(jnp.int32, t.shape, 1) % DA_QK_DIM
    return t * cos + jnp.where(lane < half, -up, dn) * sin


def _qkv_kernel(x_ref, g_ref, sh_ref, sc_ref, w_ref, cos_ref, sin_ref, q_ref, k_ref, v_ref, *, qkw):
    h = _normmod(x_ref[...], g_ref[...], sh_ref[...], sc_ref[...]).astype(BF16)
    qkv = _dot(h, w_ref[...])
    cos, sin = cos_ref[...], sin_ref[...]
    q_scale = DA_QK_DIM ** -0.5 * math.log2(math.e)
    for hb in range(qkw // LANES):
        lo, hi = hb * LANES, (hb + 1) * LANES
        q_ref[:, lo:hi] = (_rope(qkv[:, lo:hi], cos, sin) * q_scale).astype(BF16)
        k_ref[:, lo:hi] = _rope(qkv[:, qkw + lo:qkw + hi], cos, sin).astype(BF16)
    v_ref[...] = qkv[:, 2 * qkw:].astype(BF16)


def _qkv_call(xs, g, shift, scale, w_qkv, cos_t, sin_t, *, qkw, tm, seq, ctx_len, n_batch):
    rows, d = xs.shape
    n = w_qkv.shape[1]
    vw = n - 2 * qkw
    lat_tiles, lat_tps, ctx_tps = n_batch * seq // tm, seq // tm, ctx_len // tm
    seg = _seg_map(tm, seq, n_batch)

    def kv_map(i):
        c = i - lat_tiles
        is_lat = i < lat_tiles
        return (jnp.where(is_lat, i // lat_tps, c // ctx_tps),
                jnp.where(is_lat, ctx_tps + i % lat_tps, c % ctx_tps), 0)

    rope_map = lambda i: (jnp.where(i < lat_tiles, i % lat_tps, lat_tps), 0)
    return pl.pallas_call(
        functools.partial(_qkv_kernel, qkw=qkw),
        grid=(rows // tm,),
        in_specs=[pl.BlockSpec((tm, d), lambda i: (i, 0)),
                  pl.BlockSpec((1, d), lambda i: (0, 0)),
                  pl.BlockSpec((None, 1, d), seg),
                  pl.BlockSpec((None, 1, d), seg),
                  pl.BlockSpec((d, n), lambda i: (0, 0)),
                  pl.BlockSpec((tm, LANES), rope_map),
                  pl.BlockSpec((tm, LANES), rope_map)],
        out_specs=[pl.BlockSpec((tm, qkw), lambda i: (i, 0)),
                   pl.BlockSpec((None, tm, qkw), kv_map),
                   pl.BlockSpec((None, tm, vw), kv_map)],
        out_shape=[jax.ShapeDtypeStruct((rows, qkw), BF16),
                   jax.ShapeDtypeStruct((n_batch, ctx_len + seq, qkw), BF16),
                   jax.ShapeDtypeStruct((n_batch, ctx_len + seq, vw), BF16)],
        compiler_params=_cparams(1, 48),
        name="qkv_rope",
    )(xs, g, shift, scale, w_qkv, cos_t, sin_t)


def _rope_tables(seq, tm):
    pos = np.arange(seq)
    n_freq = DA_QK_DIM // 4
    inv = np.power(ROPE_BASE, -np.arange(n_freq, dtype=np.float32) / n_freq).astype(np.float32)
    row = (pos // GRID_W).astype(np.float32)[:, None] * inv
    col = (pos % GRID_W).astype(np.float32)[:, None] * inv
    ang = np.concatenate([row, col], axis=-1).astype(np.float32)
    ang = np.tile(ang, (1, LANES // ang.shape[1]))
    cos = np.concatenate([np.cos(ang), np.ones((tm, LANES))], axis=0)
    sin = np.concatenate([np.sin(ang), np.zeros((tm, LANES))], axis=0)
    return jnp.asarray(cos, F32), jnp.asarray(sin, F32)


def _attn_kernel(lp_ref, q_ref, k_ref, v_ref, sg_ref, o_ref, vext, *, n_sub, lam_init):
    @pl.when(pl.program_id(2) == 0)
    def _():
        vext[:, :LANES] = v_ref[...]
        vext[:, LANES:] = jnp.ones((vext.shape[0], LANES), BF16)

    lp = lp_ref[...]
    lam = (jnp.exp(jnp.sum(lp[0:1] * lp[1:2], axis=1, keepdims=True))
           - jnp.exp(jnp.sum(lp[2:3] * lp[3:4], axis=1, keepdims=True)) + lam_init)
    ts = q_ref.shape[0] // n_sub
    lane = lax.broadcasted_iota(jnp.int32, (ts, LANES), 1)
    zero = jnp.zeros((ts, LANES), BF16)
    k = k_ref[...]
    scores = []
    for a in range(n_sub):
        q = q_ref[a * ts:(a + 1) * ts, :]
        qq = jnp.concatenate([jnp.where(lane < DA_QK_DIM, q, zero),
                              jnp.where(lane >= DA_QK_DIM, q, zero)], axis=0)
        scores.append(_dot_nt(qq, k))
    for a in range(n_sub):
        s = scores[a]
        p = jnp.exp2(s - jnp.max(s, axis=-1, keepdims=True)).astype(BF16)
        oe = _dot(p, vext[...])
        on = oe[:, :LANES] / oe[:, LANES:LANES + 1]
        o = on[:ts] - lam * on[ts:]
        o_ref[a * ts:(a + 1) * ts, :] = (_rms(o, sg_ref[...]) * (1.0 - lam_init)).astype(BF16)


def _attn_call(lam_p, q, k_all, v_all, subln_g, *, tq, seq, n_batch, lam_init):
    n_heads = q.shape[1] // LANES
    lk = k_all.shape[1]
    qt = seq // tq
    return pl.pallas_call(
        functools.partial(_attn_kernel, n_sub=tq // ATTN_SUB_ROWS, lam_init=lam_init),
        grid=(n_batch, n_heads, qt),
        in_specs=[pl.BlockSpec(lam_p.shape, lambda b, h, i: (0, 0)),
                  pl.BlockSpec((tq, LANES), lambda b, h, i: (b * qt + i, h)),
                  pl.BlockSpec((None, lk, LANES), lambda b, h, i: (b, 0, h)),
                  pl.BlockSpec((None, lk, LANES), lambda b, h, i: (b, 0, h)),
                  pl.BlockSpec((1, LANES), lambda b, h, i: (0, 0))],
        out_specs=pl.BlockSpec((tq, LANES), lambda b, h, i: (b * qt + i, h)),
        out_shape=jax.ShapeDtypeStruct((n_batch * seq, n_heads * LANES), BF16),
        scratch_shapes=[pltpu.VMEM((lk, 2 * LANES), BF16)],
        compiler_params=_cparams(3, 48),
        name="diff_attention",
    )(lam_p, q, k_all, v_all, subln_g)


def kernel(x, c, ctx, c_ctx, ada_w, ada_b, norm_g, cf_w_in, cf_conv_w, cf_conv_b, cf_ln_g, cf_ln_b,
           cf_w_out, da_w_qkv, da_lambda, da_subln_g, da_w_out, moe_router_w, moe_router_b,
           moe_w_gate, moe_w_up, moe_w_down, moe_sh_gate, moe_sh_up, moe_sh_down):
    n_batch, seq, d = x.shape
    ctx_len = ctx.shape[1]
    depth = ada_w.shape[0]
    assert depth == 2 and n_batch + 1 <= N_SEG
    assert DA_QK_DIM * 2 == LANES and da_subln_g.shape[1] == LANES
    rows_lat, rows_ctx = n_batch * seq, n_batch * ctx_len
    tm = 512
    tq = 256
    assert seq % tm == 0 and rows_ctx % tm == 0 and seq % tq == 0 and ctx_len % tq == 0

    xs = jnp.concatenate([x.reshape(rows_lat, d), ctx.reshape(rows_ctx, d)], axis=0)
    cond = jnp.concatenate([c, c_ctx[None, :], jnp.zeros((N_SEG - n_batch - 1, d), F32)], axis=0)
    mods = _mods_call(cond, ada_w, ada_b)
    mod = lambda layer, k: mods[layer, :, k * d:(k + 1) * d][:, None, :]
    gain = lambda layer, k: norm_g[layer, k][None, :]
    common = dict(seq=seq, n_batch=n_batch)

    cc = cf_conv_w.shape[-1]
    fc = cf_w_in.shape[2] - 2 * cc
    gw = fc // FOURIER_GROUPS
    ch_ang = 2.0 * np.pi * (np.outer(np.arange(gw), np.arange(gw)) % gw) / gw
    cs = jnp.asarray(np.concatenate([np.cos(ch_ang), np.sin(ch_ang)], axis=1) / math.sqrt(gw), BF16)
    a_glu, ab = _inproj_call(xs, gain(0, 0), mod(0, 0), mod(0, 1), cf_w_in[0].astype(BF16), cs,
                             cc=cc, gw=gw, tm=tm, **common)
    gsz = cc // CONV_GROUPS
    gid = np.arange(cc) // gsz
    gmean = jnp.asarray((gid[:, None] == gid[None, :]) / gsz, BF16)
    a_act = _conv_call(a_glu, cf_conv_w[0][:, 0, :], cf_conv_b[0][None, :], cf_ln_g[0][None, :],
                       cf_ln_b[0][None, :], gmean, tl=tq, seq=seq, ctx_len=ctx_len, n_batch=n_batch)
    c_lat, s_lat = _dft_tables(seq, 1.0 / math.sqrt(seq))
    c_ctx_m, s_ctx_m = _dft_tables(ctx_len, 1.0 / math.sqrt(ctx_len))
    fr = jnp.concatenate([
        _seqdft_call(ab, c_lat, s_lat, length=seq, row0=0, n_seq=n_batch, fc=fc, tm=min(seq, 256)),
        _seqdft_call(ab, c_ctx_m, s_ctx_m, length=ctx_len, row0=rows_lat, n_seq=n_batch, fc=fc,
                     tm=min(ctx_len, 256))], axis=0)
    w_out = cf_w_out[0].astype(BF16)
    xs, tok, tokp = _outproj_call([a_act, fr], [w_out[:cc], w_out[cc:]], xs, gain(0, 1), mod(0, 2),
                            gain(0, 2), mod(0, 3), mod(0, 4), rows=rows_lat + rows_ctx, tm=tm, **common)
    xs = _moe_layer(xs, tok, tokp, gain(0, 2), mod(0, 3), mod(0, 4), gain(0, 3), mod(0, 5),
                    moe_router_w[0], moe_router_b[0], 0, moe_w_gate, moe_w_up, moe_w_down,
                    moe_sh_gate[0], moe_sh_up[0], moe_sh_down[0], rows=rows_lat + rows_ctx, tm=tm, **common)

    qkw = DA_HEADS * 2 * DA_QK_DIM
    lam_init = 0.8 - 0.6 * math.exp(-0.3 * 1)
    cos_t, sin_t = _rope_tables(seq, tq)
    q, k_all, v_all = _qkv_call(xs, gain(1, 0), mod(1, 0), mod(1, 1), da_w_qkv[0].astype(BF16),
                                cos_t, sin_t, qkw=qkw, tm=tq, ctx_len=ctx_len, **common)
    o = _attn_call(da_lambda[0], q, k_all, v_all, da_subln_g[0][None, :], tq=tm, seq=seq,
                   n_batch=n_batch, lam_init=lam_init)
    xs, tok, tokp = _outproj_call([o], [da_w_out[0].astype(BF16)], xs, gain(1, 1), mod(1, 2),
                            gain(1, 2), mod(1, 3), mod(1, 4), rows=rows_lat, tm=tm, **common)
    xs = _moe_layer(xs, tok, tokp, gain(1, 2), mod(1, 3), mod(1, 4), gain(1, 3), mod(1, 5),
                    moe_router_w[1], moe_router_b[1], 1, moe_w_gate, moe_w_up, moe_w_down,
                    moe_sh_gate[1], moe_sh_up[1], moe_sh_down[1], rows=rows_lat, tm=tm, **common)
    return xs.reshape(n_batch, seq, d)
```

```python
import functools
import math

import numpy as np
import jax
import jax.numpy as jnp
from jax import lax
from jax.experimental import pallas as pl
from jax.experimental.pallas import tpu as pltpu
from jax.experimental.pallas import tpu_sc as plsc

F32 = jnp.float32
BF16 = jnp.bfloat16

EPS = 1e-6
GRID_W = 64
CONV_GROUPS = 8
FOURIER_GROUPS = 4
DA_HEADS = 8
DA_QK_DIM = 64
ROPE_BASE = 10000.0
N_GROUPS = 8
TOPK_GROUPS = 4
TOP_K = 6
ROUTED_SCALE = 2.5

LANES = 128
SUBLANES = 8
N_SEG = 16
HALO = 16
EXPERT_ROWS = 1024
EXPERT_SUB_ROWS = 128
ATTN_SUB_ROWS = 128
MIB = 1024 * 1024


def _cparams(n_axes, vmem_mib):
    return pltpu.CompilerParams(dimension_semantics=("arbitrary",) * n_axes,
                                vmem_limit_bytes=vmem_mib * MIB)


def _sigmoid(v):
    return 1.0 / (1.0 + jnp.exp(-v))


def _rms(v, g):
    return v * lax.rsqrt(jnp.mean(v * v, axis=-1, keepdims=True) + EPS) * g


def _normmod(v, g, shift, scale):
    return _rms(v, g) * (1.0 + scale) + shift


def _split_bf16(v):
    hi = v.astype(BF16)
    lo = (v - hi.astype(F32)).astype(BF16)
    return hi, lo


def _pack_halves(v):
    half = v.shape[1] // 2
    word = pltpu.pack_elementwise([v[:, :half], v[:, half:]], packed_dtype=BF16)
    return lax.bitcast_convert_type(word, jnp.int32)


def _unpack_halves(w):
    u = lax.bitcast_convert_type(w, jnp.uint32)
    lo = lax.bitcast_convert_type(u << 16, F32)
    hi = lax.bitcast_convert_type(u & jnp.uint32(0xFFFF0000), F32)
    return lo, hi


SC_SPLIT = 2


def _store_pieces(ref, words):
    q = words.shape[1] // SC_SPLIT
    for j in range(SC_SPLIT):
        ref[j] = words[:, j * q:(j + 1) * q]


def _load_pieces(ref):
    return jnp.concatenate([ref[j] for j in range(SC_SPLIT)], axis=-1)


def _dot(a, b):
    return jnp.dot(a, b, preferred_element_type=F32)


def _dot_nt(a, b):
    return lax.dot_general(a, b, (((1,), (1,)), ((), ())), preferred_element_type=F32)


def _mods_kernel(c_ref, w_ref, b_ref, o_ref):
    cv = c_ref[...]
    o_ref[...] = _dot(cv * _sigmoid(cv), w_ref[...]) + b_ref[...]


def _mods_call(cond, ada_w, ada_b):
    depth, d, n = ada_w.shape
    tn = n // 4
    return pl.pallas_call(
        _mods_kernel,
        grid=(depth, n // tn),
        in_specs=[pl.BlockSpec((N_SEG, d), lambda l, j: (0, 0)),
                  pl.BlockSpec((None, d, tn), lambda l, j: (l, 0, j)),
                  pl.BlockSpec((None, 1, tn), lambda l, j: (l, 0, j))],
        out_specs=pl.BlockSpec((None, N_SEG, tn), lambda l, j: (l, 0, j)),
        out_shape=jax.ShapeDtypeStruct((depth, N_SEG, n), F32),
        compiler_params=_cparams(2, 40),
        name="adaln_mods",
    )(cond, ada_w, ada_b.reshape(depth, 1, n))


def _inproj_kernel(x_ref, g_ref, sh_ref, sc_ref, w_ref, cs_ref, a_ref, ab_ref, *, cc, gw):
    h = _normmod(x_ref[...], g_ref[...], sh_ref[...], sc_ref[...]).astype(BF16)
    u = _dot(h, w_ref[...])
    a_ref[...] = u[:, :cc] * _sigmoid(u[:, cc:2 * cc])
    f = u[:, 2 * cc:].astype(BF16)
    n_g = f.shape[1] // gw
    parts = [_dot(f[:, g * gw:(g + 1) * gw], cs_ref[...]) for g in range(n_g)]
    cos_part = [p[:, :gw] for p in parts]
    sin_part = [p[:, gw:] for p in parts]
    ab_ref[...] = jnp.concatenate(cos_part + sin_part, axis=-1).astype(BF16)


def _seg_map(tm, seq, n_batch):
    return lambda i: (jnp.minimum(i * tm // seq, n_batch), 0, 0)


def _inproj_call(xs, g, shift, scale, w_in, cs, *, cc, gw, tm, seq, n_batch):
    rows, d = xs.shape
    n = w_in.shape[1]
    fc = n - 2 * cc
    seg = _seg_map(tm, seq, n_batch)
    return pl.pallas_call(
        functools.partial(_inproj_kernel, cc=cc, gw=gw),
        grid=(rows // tm,),
        in_specs=[pl.BlockSpec((tm, d), lambda i: (i, 0)),
                  pl.BlockSpec((1, d), lambda i: (0, 0)),
                  pl.BlockSpec((None, 1, d), seg),
                  pl.BlockSpec((None, 1, d), seg),
                  pl.BlockSpec((d, n), lambda i: (0, 0)),
                  pl.BlockSpec((gw, 2 * gw), lambda i: (0, 0))],
        out_specs=[pl.BlockSpec((tm, cc), lambda i: (i, 0)),
                   pl.BlockSpec((tm, 2 * fc), lambda i: (i, 0))],
        out_shape=[jax.ShapeDtypeStruct((rows, cc), F32),
                   jax.ShapeDtypeStruct((rows, 2 * fc), BF16)],
        compiler_params=_cparams(1, 40),
        name="inproj_glu_chdft",
    )(xs, g, shift, scale, w_in, cs)


def _conv_kernel(prev_ref, main_ref, next_ref, w_ref, cb_ref, lg_ref, lb_ref, gm_ref, o_ref,
                 buf, cv, shifted, *, tl, width, lat_tiles, lat_tps, ctx_tps, chunk):
    i = pl.program_id(0)
    is_lat = i < lat_tiles
    tps = jnp.where(is_lat, lat_tps, ctx_tps)
    j = jnp.where(is_lat, i, i - lat_tiles) % tps
    zero = jnp.zeros((HALO, buf.shape[1]), F32)
    buf[0:HALO, :] = jnp.where(j > 0, prev_ref[...], zero)
    buf[HALO:HALO + tl, :] = main_ref[...]
    buf[HALO + tl:HALO + tl + HALO, :] = jnp.where(j < tps - 1, next_ref[...], zero)
    base = HALO - width // 2
    span = shifted.shape[1]
    for s in range(1, SUBLANES):
        shifted[s] = buf[s:s + span, :]

    def window(off, r0):
        s, m = off % SUBLANES, off - off % SUBLANES
        if s == 0:
            return buf[m + r0:m + r0 + chunk, :]
        return shifted[s, m + r0:m + r0 + chunk, :]

    for r0 in range(0, tl, chunk):
        acc = window(base, r0) * w_ref[0:1, :]
        for t in range(1, width):
            acc = acc + window(base + t, r0) * w_ref[t:t + 1, :]
        cv[r0:r0 + chunk, :] = acc
    a = cv[...] + cb_ref[...]
    gm = gm_ref[...]
    a_hi, a_lo = _split_bf16(a)
    mu = _dot(a_hi, gm) + _dot(a_lo, gm)
    dl = a - mu
    q_hi, q_lo = _split_bf16(dl * dl)
    var = _dot(q_hi, gm) + _dot(q_lo, gm)
    y = dl * lax.rsqrt(var + EPS) * lg_ref[...] + lb_ref[...]
    o_ref[...] = (y * _sigmoid(y)).astype(BF16)


def _conv_call(a_glu, conv_w, conv_b, ln_g, ln_b, gmean, *, tl, seq, ctx_len, n_batch):
    rows, ch = a_glu.shape
    width = conv_w.shape[0]
    assert width // 2 <= HALO and tl % HALO == 0
    hb = tl // HALO
    last_halo = rows // HALO - 1
    return pl.pallas_call(
        functools.partial(_conv_kernel, tl=tl, width=width, lat_tiles=n_batch * seq // tl,
                          lat_tps=seq // tl, ctx_tps=ctx_len // tl, chunk=32),
        grid=(rows // tl,),
        in_specs=[pl.BlockSpec((HALO, ch), lambda i: (jnp.maximum(i * hb - 1, 0), 0)),
                  pl.BlockSpec((tl, ch), lambda i: (i, 0)),
                  pl.BlockSpec((HALO, ch), lambda i: (jnp.minimum((i + 1) * hb, last_halo), 0)),
                  pl.BlockSpec((width, ch), lambda i: (0, 0)),
                  pl.BlockSpec((1, ch), lambda i: (0, 0)),
                  pl.BlockSpec((1, ch), lambda i: (0, 0)),
                  pl.BlockSpec((1, ch), lambda i: (0, 0)),
                  pl.BlockSpec((ch, ch), lambda i: (0, 0))],
        out_specs=pl.BlockSpec((tl, ch), lambda i: (i, 0)),
        out_shape=jax.ShapeDtypeStruct((rows, ch), BF16),
        scratch_shapes=[pltpu.VMEM((tl + 2 * HALO, ch), F32), pltpu.VMEM((tl, ch), F32),
                        pltpu.VMEM((SUBLANES, tl + 2 * HALO - SUBLANES, ch), F32)],
        compiler_params=_cparams(1, 40),
        name="dwconv_groupln_swish",
    )(a_glu, a_glu, a_glu, conv_w, conv_b, ln_g, ln_b, gmean)


def _seqdft_kernel(c_ref, s_ref, a_ref, b_ref, o_ref):
    o_ref[...] = (_dot(c_ref[...], a_ref[...]) - _dot(s_ref[...], b_ref[...])).astype(BF16)


def _seqdft_call(ab, cmat, smat, *, length, row0, n_seq, fc, tm):
    seq0 = row0 // length
    return pl.pallas_call(
        _seqdft_kernel,
        grid=(n_seq, length // tm),
        in_specs=[pl.BlockSpec((tm, length), lambda b, i: (i, 0)),
                  pl.BlockSpec((tm, length), lambda b, i: (i, 0)),
                  pl.BlockSpec((length, fc), lambda b, i: (seq0 + b, 0)),
                  pl.BlockSpec((length, fc), lambda b, i: (seq0 + b, 1))],
        out_specs=pl.BlockSpec((tm, fc), lambda b, i: (b * (length // tm) + i, 0)),
        out_shape=jax.ShapeDtypeStruct((n_seq * length, fc), BF16),
        compiler_params=_cparams(2, 48),
        name="seq_dft",
    )(cmat, smat, ab, ab)


def _dft_tables(length, scale):
    if length <= 512:
        kn = np.outer(np.arange(length), np.arange(length)) % length
        ang = 2.0 * np.pi * kn / length
        return (jnp.asarray(np.cos(ang) * scale, BF16), jnp.asarray(np.sin(ang) * scale, BF16))
    r = 64
    assert length % r == 0
    k = np.arange(length)[:, None]
    alpha = 2.0 * np.pi * ((k * np.arange(length // r)[None, :] * r) % length) / length
    beta = 2.0 * np.pi * ((k * np.arange(r)[None, :]) % length) / length
    ca, sa = jnp.asarray(np.cos(alpha), F32)[:, :, None], jnp.asarray(np.sin(alpha), F32)[:, :, None]
    cb, sb = jnp.asarray(np.cos(beta) * scale, F32)[:, None, :], jnp.asarray(np.sin(beta) * scale, F32)[:, None, :]
    cmat = (ca * cb - sa * sb).reshape(length, length).astype(BF16)
    smat = (sa * cb + ca * sb).reshape(length, length).astype(BF16)
    return cmat, smat


def _outproj_kernel(*refs, n_in):
    ins, ws = refs[:n_in], refs[n_in:2 * n_in]
    x_ref, g1_ref, gate_ref, g2_ref, sh_ref, sc_ref, xo_ref, tok_ref, tokp_ref = refs[2 * n_in:]
    y = _dot(ins[0][...], ws[0][...])
    for a_ref, w_ref in zip(ins[1:], ws[1:]):
        y = y + _dot(a_ref[...], w_ref[...])
    x1 = x_ref[...] + gate_ref[...] * _rms(y, g1_ref[...])
    xo_ref[...] = x1
    tok = _normmod(x1, g2_ref[...], sh_ref[...], sc_ref[...])
    tok_ref[...] = tok.astype(BF16)
    _store_pieces(tokp_ref, _pack_halves(tok))


def _outproj_call(ins, ws, xs, g1, gate, g2, shift, scale, *, rows, tm, seq, n_batch):
    d = xs.shape[1]
    seg = _seg_map(tm, seq, n_batch)
    n_in = len(ins)
    row_spec = lambda a: pl.BlockSpec((tm, a.shape[1]), lambda i: (i, 0))
    full_spec = lambda a: pl.BlockSpec(a.shape, lambda i: (0, 0))
    return pl.pallas_call(
        functools.partial(_outproj_kernel, n_in=n_in),
        grid=(rows // tm,),
        in_specs=[row_spec(a) for a in ins] + [full_spec(w) for w in ws] + [
            row_spec(xs), full_spec(g1), pl.BlockSpec((None, 1, d), seg), full_spec(g2),
            pl.BlockSpec((None, 1, d), seg), pl.BlockSpec((None, 1, d), seg)],
        out_specs=[pl.BlockSpec((tm, d), lambda i: (i, 0)), pl.BlockSpec((tm, d), lambda i: (i, 0)),
                   pl.BlockSpec((SC_SPLIT, tm, d // 2 // SC_SPLIT), lambda i: (0, i, 0))],
        out_shape=[jax.ShapeDtypeStruct((rows, d), F32), jax.ShapeDtypeStruct((rows, d), BF16),
                   jax.ShapeDtypeStruct((SC_SPLIT, rows, d // 2 // SC_SPLIT), jnp.int32)],
        compiler_params=_cparams(1, 40),
        name="outproj_residual",
    )(*ins, *ws, xs, g1, gate, g2, shift, scale)


def _pick_max(cur, idx):
    mx = jnp.max(cur, axis=0, keepdims=True)
    first = jnp.min(jnp.where(cur == mx, idx, float(cur.shape[0])), axis=0, keepdims=True)
    return first, idx == first


def _router_kernel(x_ref, g_ref, sh_ref, sc_ref, wh_ref, wl_ref, rb_ref,
                   e_ref, gt_ref, pos_ref, cnt_ref, carry, *, n_exp):
    i = pl.program_id(0)

    @pl.when(i == 0)
    def _():
        carry[...] = jnp.zeros_like(carry)

    tok = _normmod(x_ref[...], g_ref[...], sh_ref[...], sc_ref[...])
    t_hi, t_lo = _split_bf16(tok)
    wh, wl = wh_ref[...], wl_ref[...]
    logits = _dot_nt(wh, t_hi) + _dot_nt(wh, t_lo) + _dot_nt(wl, t_hi)
    tm = logits.shape[1]
    scores = _sigmoid(logits)
    biased = scores + rb_ref[...]
    gsz = n_exp // N_GROUPS
    neg = -jnp.inf

    b3 = biased.reshape(N_GROUPS, gsz, tm)
    im = lax.broadcasted_iota(jnp.int32, b3.shape, 1).astype(F32)
    m1 = jnp.max(b3, axis=1, keepdims=True)
    i1 = jnp.min(jnp.where(b3 == m1, im, float(gsz)), axis=1, keepdims=True)
    m2 = jnp.max(jnp.where(im == i1, neg, b3), axis=1, keepdims=True)
    gscore = (m1 + m2).reshape(N_GROUPS, tm)

    ig = lax.broadcasted_iota(jnp.int32, gscore.shape, 0).astype(F32)
    gsel = jnp.zeros_like(gscore)
    cur = gscore
    for _ in range(TOPK_GROUPS):
        _, hit = _pick_max(cur, ig)
        gsel = jnp.where(hit, 1.0, gsel)
        cur = jnp.where(hit, neg, cur)
    gsel3 = jnp.broadcast_to(gsel.reshape(N_GROUPS, 1, tm), b3.shape)
    cur = jnp.where(gsel3 > 0.0, b3, neg).reshape(n_exp, tm)

    ie = lax.broadcasted_iota(jnp.int32, (n_exp, tm), 0).astype(F32)
    sel = jnp.zeros((n_exp, tm), F32)
    picks, raw = [], []
    for _ in range(TOP_K):
        first, hit = _pick_max(cur, ie)
        picks.append(first)
        raw.append(jnp.sum(jnp.where(hit, scores, 0.0), axis=0, keepdims=True))
        sel = jnp.where(hit, 1.0, sel)
        cur = jnp.where(hit, neg, cur)
    total = raw[0]
    for r in raw[1:]:
        total = total + r

    ri = lax.broadcasted_iota(jnp.int32, (tm, tm), 0)
    ci = lax.broadcasted_iota(jnp.int32, (tm, tm), 1)
    upper = jnp.where(ri < ci, 1.0, 0.0).astype(BF16)
    rank = _dot(sel.astype(BF16), upper) + carry[...]
    ranks = [jnp.sum(jnp.where(ie == p, rank, 0.0), axis=0, keepdims=True) for p in picks]
    carry[...] = carry[...] + jnp.sum(sel, axis=1, keepdims=True)

    pad = jnp.zeros((SUBLANES - TOP_K, tm), F32)
    e_ref[...] = jnp.concatenate(picks + [pad], axis=0).astype(jnp.int32)
    gt_ref[...] = jnp.concatenate([r / total * ROUTED_SCALE for r in raw] + [pad], axis=0)
    pos_ref[...] = jnp.concatenate(ranks + [pad], axis=0).astype(jnp.int32)
    cnt_ref[...] = jnp.broadcast_to(carry[...], cnt_ref.shape)


def _router_call(xs, g, shift, scale, w_hi_t, w_lo_t, rbias, *, rows, tm, seq, n_batch):
    d = xs.shape[1]
    n_exp = w_hi_t.shape[0]
    seg = _seg_map(tm, seq, n_batch)
    tok_rows = lambda dt: jax.ShapeDtypeStruct((SUBLANES, rows), dt)
    lane_spec = pl.BlockSpec((SUBLANES, tm), lambda i: (0, i))
    return pl.pallas_call(
        functools.partial(_router_kernel, n_exp=n_exp),
        grid=(rows // tm,),
        in_specs=[pl.BlockSpec((tm, d), lambda i: (i, 0)),
                  pl.BlockSpec((1, d), lambda i: (0, 0)),
                  pl.BlockSpec((None, 1, d), seg),
                  pl.BlockSpec((None, 1, d), seg),
                  pl.BlockSpec((n_exp, d), lambda i: (0, 0)),
                  pl.BlockSpec((n_exp, d), lambda i: (0, 0)),
                  pl.BlockSpec((n_exp, 1), lambda i: (0, 0))],
        out_specs=[lane_spec, lane_spec, lane_spec,
                   pl.BlockSpec((n_exp, LANES), lambda i: (0, 0))],
        out_shape=[tok_rows(jnp.int32), tok_rows(F32), tok_rows(jnp.int32),
                   jax.ShapeDtypeStruct((n_exp, LANES), F32)],
        scratch_shapes=[pltpu.VMEM((n_exp, 1), F32)],
        compiler_params=_cparams(1, 40),
        name="moe_router",
    )(xs, g, shift, scale, w_hi_t, w_lo_t, rbias)


def _expert_kernel(be_ref, nv_ref, nu_ref, xs_ref, wg_ref, wu_ref, wd_ref, ys_ref, wgu, wdn, *, ff, n_sub):
    b = pl.program_id(0)
    changed = jnp.logical_or(b == 0, be_ref[b] != be_ref[jnp.maximum(b - 1, 0)])

    @pl.when(changed)
    def _():
        wgu[:, :ff] = wg_ref[...].astype(BF16)
        wgu[:, ff:] = wu_ref[...].astype(BF16)
        wdn[...] = wd_ref[...].astype(BF16)

    @pl.when(b < nu_ref[0])
    def _():
        n_valid = nv_ref[b]
        q = xs_ref.shape[2]
        ts = xs_ref.shape[1] // n_sub
        gus = []
        for a in range(n_sub):
            r0 = a * ts
            xw = jnp.concatenate([xs_ref[j, r0:r0 + ts, :] for j in range(SC_SPLIT)], axis=-1)
            row = lax.broadcasted_iota(jnp.int32, xw.shape, 0) + r0
            lo, hi = _unpack_halves(jnp.where(row < n_valid, xw, 0))
            half = lo.shape[1]
            gus.append(_dot(lo.astype(BF16), wgu[:half, :]) + _dot(hi.astype(BF16), wgu[half:, :]))
        outs = []
        for gu in gus:
            gate = gu[:, :ff]
            hmid = (gate * _sigmoid(gate) * gu[:, ff:]).astype(BF16)
            outs.append(_dot(hmid, wdn[...]))
        for a, y in enumerate(outs):
            words = _pack_halves(y)
            for j in range(SC_SPLIT):
                ys_ref[j, a * ts:(a + 1) * ts, :] = words[:, j * q:(j + 1) * q]


def _expert_call(block_e, n_valid, n_used, xs, layer, w_gate, w_up, w_down):
    _, n_rows, q = xs.shape
    d = 2 * SC_SPLIT * q
    ff = w_gate.shape[3]
    te = EXPERT_ROWS
    row_map = lambda b, be, nv, nu: (0, jnp.minimum(b, nu[0] - 1), 0)
    w_map = lambda b, be, nv, nu: (layer, be[b], 0, 0)
    grid_spec = pltpu.PrefetchScalarGridSpec(
        num_scalar_prefetch=3,
        grid=(n_rows // te,),
        in_specs=[pl.BlockSpec((SC_SPLIT, te, q), row_map),
                  pl.BlockSpec((None, None, d, ff), w_map),
                  pl.BlockSpec((None, None, d, ff), w_map),
                  pl.BlockSpec((None, None, ff, d), w_map)],
        out_specs=pl.BlockSpec((SC_SPLIT, te, q), row_map),
        scratch_shapes=[pltpu.VMEM((d, 2 * ff), BF16), pltpu.VMEM((ff, d), BF16)])
    return pl.pallas_call(
        functools.partial(_expert_kernel, ff=ff, n_sub=EXPERT_ROWS // EXPERT_SUB_ROWS),
        grid_spec=grid_spec,
        out_shape=jax.ShapeDtypeStruct(xs.shape, jnp.int32),
        compiler_params=_cparams(1, 40),
        name="moe_experts",
    )(block_e, n_valid, n_used, xs, w_gate, w_up, w_down)


SC_WINDOW = 128


def _sc_mesh():
    return plsc.VectorSubcoreMesh(core_axis_name="core", subcore_axis_name="subcore")


def _sc_scatter_rows(rows, dests, n_out):
    n, width = rows.shape
    n_k = len(dests)
    assert n % SC_WINDOW == 0

    @functools.partial(pl.kernel, out_type=jax.ShapeDtypeStruct((n_out, width), rows.dtype),
                       mesh=_sc_mesh(), name="sc_dispatch_rows")
    def scatter_kernel(x_hbm, *refs):
        idx_hbm, o_hbm = refs[:n_k], refs[n_k]

        def body(x_vmem, *idx_vmem):
            for iv in idx_vmem:
                pltpu.sync_copy(x_vmem, o_hbm.at[iv.at[0]])

        pltpu.emit_pipeline(
            body,
            grid=(n // SC_WINDOW,),
            in_specs=[pl.BlockSpec((SC_WINDOW, width), lambda i: (i, 0))]
            + [pl.BlockSpec((1, SC_WINDOW), lambda i: (0, i))] * n_k,
            out_specs=[],
            core_axis_name=("core", "subcore"),
            dimension_semantics=(pltpu.PARALLEL,),
        )(x_hbm, *idx_hbm)

    return scatter_kernel(rows, *dests)


def _sc_gather_rows(table, idx):
    n = idx.shape[1]
    width = table.shape[1]
    assert n % SC_WINDOW == 0

    @functools.partial(pl.kernel, out_type=jax.ShapeDtypeStruct((n, width), table.dtype),
                       mesh=_sc_mesh(), name="sc_collect_rows")
    def gather_kernel(t_hbm, i_hbm, o_hbm):
        def body(i_vmem, o_vmem):
            pltpu.sync_copy(t_hbm.at[i_vmem.at[0]], o_vmem)

        pltpu.emit_pipeline(
            body,
            grid=(n // SC_WINDOW,),
            in_specs=[pl.BlockSpec((1, SC_WINDOW), lambda i: (0, i))],
            out_specs=[pl.BlockSpec((SC_WINDOW, width), lambda i: (i, 0))],
            core_axis_name=("core", "subcore"),
            dimension_semantics=(pltpu.PARALLEL,),
        )(i_hbm, o_hbm)

    return gather_kernel(table, idx)


def _shared_kernel(tok_ref, sgu_ref, sd_ref, o_ref, *, ff):
    gu = _dot(tok_ref[...], sgu_ref[...])
    gate = gu[:, :ff]
    hmid = (gate * _sigmoid(gate) * gu[:, ff:]).astype(BF16)
    o_ref[...] = _dot(hmid, sd_ref[...]).astype(BF16)


def _shared_call(tok, sgu, sd, *, rows, tm):
    d = tok.shape[1]
    ff = sd.shape[0]
    return pl.pallas_call(
        functools.partial(_shared_kernel, ff=ff),
        grid=(rows // tm,),
        in_specs=[pl.BlockSpec((tm, d), lambda i: (i, 0)),
                  pl.BlockSpec((d, 2 * ff), lambda i: (0, 0)),
                  pl.BlockSpec((ff, d), lambda i: (0, 0))],
        out_specs=pl.BlockSpec((tm, d), lambda i: (i, 0)),
        out_shape=jax.ShapeDtypeStruct((rows, d), BF16),
        compiler_params=_cparams(1, 40),
        name="moe_shared_expert",
    )(tok, sgu, sd)


def _moe_out_kernel(shared_ref, yg_ref, gk_ref, x_ref, g_ref, gate_ref, xo_ref):
    shared = shared_ref[...].astype(F32)
    gk = gk_ref[...]
    lo_acc, hi_acc = None, None
    for k in range(TOP_K):
        lo, hi = _unpack_halves(jnp.concatenate([yg_ref[j, k] for j in range(SC_SPLIT)], axis=-1))
        w = gk[:, k:k + 1]
        lo_acc = lo * w if lo_acc is None else lo_acc + lo * w
        hi_acc = hi * w if hi_acc is None else hi_acc + hi * w
    y = jnp.concatenate([lo_acc, hi_acc], axis=-1) + shared
    xo_ref[...] = x_ref[...] + gate_ref[...] * _rms(y, g_ref[...])


def _moe_out_call(shared, yg, gk, xs, g, gate, *, rows, tm, seq, n_batch):
    d = xs.shape[1]
    seg = _seg_map(tm, seq, n_batch)
    row = lambda: pl.BlockSpec((tm, d), lambda i: (i, 0))
    return pl.pallas_call(
        _moe_out_kernel,
        grid=(rows // tm,),
        in_specs=[row(),
                  pl.BlockSpec((SC_SPLIT, TOP_K, tm, d // 2 // SC_SPLIT), lambda i: (0, 0, i, 0)),
                  pl.BlockSpec((tm, SUBLANES), lambda i: (i, 0)),
                  row(),
                  pl.BlockSpec((1, d), lambda i: (0, 0)),
                  pl.BlockSpec((None, 1, d), seg)],
        out_specs=row(),
        out_shape=jax.ShapeDtypeStruct((rows, d), F32),
        compiler_params=_cparams(1, 48),
        name="moe_combine",
    )(shared, yg, gk, xs, g, gate)


def _moe_layer(xs, tok, tokp, g_in, shift, scale, g_out, gate, router_w, router_b, layer, w_gate, w_up,
               w_down, sh_gate, sh_up, sh_down, *, rows, tm, seq, n_batch):
    n_exp = router_w.shape[1]
    te = EXPERT_ROWS
    w_t = router_w.T
    w_hi = w_t.astype(BF16)
    w_lo = (w_t - w_hi.astype(F32)).astype(BF16)
    e_tk, g_tk, pos_tk, counts = _router_call(xs, g_in, shift, scale, w_hi, w_lo,
                                              router_b.reshape(n_exp, 1),
                                              rows=rows, tm=tm, seq=seq, n_batch=n_batch)
    e_tk, pos_tk = e_tk[:TOP_K], pos_tk[:TOP_K]
    cnt = counts[:, 0].astype(jnp.int32)
    padded = (cnt + te - 1) // te * te
    pad_ends = jnp.cumsum(padded)
    pad_starts = pad_ends - padded
    expert_ids = jnp.arange(n_exp, dtype=jnp.int32)
    onehot = e_tk[:, :, None] == expert_ids[None, None, :]
    dest = jnp.sum(jnp.where(onehot, pad_starts[None, None, :], 0), axis=-1) + pos_tk
    n_blocks = -(-(rows * TOP_K + n_exp * (te - 1)) // te)
    n_used = pad_ends[-1] // te
    blk = jnp.arange(n_blocks, dtype=jnp.int32)
    last = jnp.minimum(blk, n_used - 1) * te
    owner = (last[:, None] >= pad_ends[None, :]).astype(jnp.int32)
    block_e = jnp.minimum(jnp.sum(owner, axis=1), n_exp - 1)
    own_hot = block_e[:, None] == expert_ids[None, :]
    n_valid = jnp.sum(jnp.where(own_hot, (pad_starts + cnt)[None, :], 0), axis=1) - last
    n_valid = jnp.clip(n_valid, 0, te)
    q = tokp.shape[2]
    n_rows = n_blocks * te
    piece = dest[None, :, :] + (jnp.arange(SC_SPLIT, dtype=jnp.int32) * n_rows)[:, None, None]
    xg = _sc_scatter_rows(tokp.reshape(SC_SPLIT * rows, q),
                          [piece[:, k, :].reshape(1, SC_SPLIT * rows) for k in range(TOP_K)],
                          SC_SPLIT * n_rows).reshape(SC_SPLIT, n_rows, q)
    sgu = jnp.concatenate([sh_gate, sh_up], axis=1).astype(BF16)
    shared = _shared_call(tok, sgu, sh_down.astype(BF16), rows=rows, tm=tm)
    ys = _expert_call(block_e, n_valid, n_used.reshape(1).astype(jnp.int32), xg, layer, w_gate, w_up, w_down)
    yg = _sc_gather_rows(ys.reshape(SC_SPLIT * n_rows, q),
                         piece.reshape(1, -1)).reshape(SC_SPLIT, TOP_K, rows, q)
    return _moe_out_call(shared, yg, g_tk.T, xs, g_out, gate, rows=rows, tm=tm, seq=seq, n_batch=n_batch)


def _rope(t, cos, sin):
    half = DA_QK_DIM // 2
    up = pltpu.roll(t, shift=LANES - half, axis=1)
    dn = pltpu.roll(t, shift=half, axis=1)
    lane = lax.broadcasted_iota(jnp.int32, t.shape, 1) % DA_QK_DIM
    return t * cos + jnp.where(lane < half, -up, dn) * sin


def _qkv_kernel(x_ref, g_ref, sh_ref, sc_ref, w_ref, cos_ref, sin_ref, q_ref, k_ref, v_ref, *, qkw):
    h = _normmod(x_ref[...], g_ref[...], sh_ref[...], sc_ref[...]).astype(BF16)
    qkv = _dot(h, w_ref[...])
    cos, sin = cos_ref[...], sin_ref[...]
    q_scale = DA_QK_DIM ** -0.5 * math.log2(math.e)
    for hb in range(qkw // LANES):
        lo, hi = hb * LANES, (hb + 1) * LANES
        q_ref[:, lo:hi] = (_rope(qkv[:, lo:hi], cos, sin) * q_scale).astype(BF16)
        k_ref[:, lo:hi] = _rope(qkv[:, qkw + lo:qkw + hi], cos, sin).astype(BF16)
    v_ref[...] = qkv[:, 2 * qkw:].astype(BF16)


def _qkv_call(xs, g, shift, scale, w_qkv, cos_t, sin_t, *, qkw, tm, seq, ctx_len, n_batch):
    rows, d = xs.shape
    n = w_qkv.shape[1]
    vw = n - 2 * qkw
    lat_tiles, lat_tps, ctx_tps = n_batch * seq // tm, seq // tm, ctx_len // tm
    seg = _seg_map(tm, seq, n_batch)

    def kv_map(i):
        c = i - lat_tiles
        is_lat = i < lat_tiles
        return (jnp.where(is_lat, i // lat_tps, c // ctx_tps),
                jnp.where(is_lat, ctx_tps + i % lat_tps, c % ctx_tps), 0)

    rope_map = lambda i: (jnp.where(i < lat_tiles, i % lat_tps, lat_tps), 0)
    return pl.pallas_call(
        functools.partial(_qkv_kernel, qkw=qkw),
        grid=(rows // tm,),
        in_specs=[pl.BlockSpec((tm, d), lambda i: (i, 0)),
                  pl.BlockSpec((1, d), lambda i: (0, 0)),
                  pl.BlockSpec((None, 1, d), seg),
                  pl.BlockSpec((None, 1, d), seg),
                  pl.BlockSpec((d, n), lambda i: (0, 0)),
                  pl.BlockSpec((tm, LANES), rope_map),
                  pl.BlockSpec((tm, LANES), rope_map)],
        out_specs=[pl.BlockSpec((tm, qkw), lambda i: (i, 0)),
                   pl.BlockSpec((None, tm, qkw), kv_map),
                   pl.BlockSpec((None, tm, vw), kv_map)],
        out_shape=[jax.ShapeDtypeStruct((rows, qkw), BF16),
                   jax.ShapeDtypeStruct((n_batch, ctx_len + seq, qkw), BF16),
                   jax.ShapeDtypeStruct((n_batch, ctx_len + seq, vw), BF16)],
        compiler_params=_cparams(1, 48),
        name="qkv_rope",
    )(xs, g, shift, scale, w_qkv, cos_t, sin_t)


def _rope_tables(seq, tm):
    pos = np.arange(seq)
    n_freq = DA_QK_DIM // 4
    inv = np.power(ROPE_BASE, -np.arange(n_freq, dtype=np.float32) / n_freq).astype(np.float32)
    row = (pos // GRID_W).astype(np.float32)[:, None] * inv
    col = (pos % GRID_W).astype(np.float32)[:, None] * inv
    ang = np.concatenate([row, col], axis=-1).astype(np.float32)
    ang = np.tile(ang, (1, LANES // ang.shape[1]))
    cos = np.concatenate([np.cos(ang), np.ones((tm, LANES))], axis=0)
    sin = np.concatenate([np.sin(ang), np.zeros((tm, LANES))], axis=0)
    return jnp.asarray(cos, F32), jnp.asarray(sin, F32)


def _attn_kernel(lp_ref, q_ref, k_ref, v_ref, sg_ref, o_ref, vext, *, n_sub, lam_init):
    @pl.when(pl.program_id(2) == 0)
    def _():
        vext[:, :LANES] = v_ref[...]
        vext[:, LANES:] = jnp.ones((vext.shape[0], LANES), BF16)

    lp = lp_ref[...]
    lam = (jnp.exp(jnp.sum(lp[0:1] * lp[1:2], axis=1, keepdims=True))
           - jnp.exp(jnp.sum(lp[2:3] * lp[3:4], axis=1, keepdims=True)) + lam_init)
    ts = q_ref.shape[0] // n_sub
    lane = lax.broadcasted_iota(jnp.int32, (ts, LANES), 1)
    zero = jnp.zeros((ts, LANES), BF16)
    k = k_ref[...]
    scores = []
    for a in range(n_sub):
        q = q_ref[a * ts:(a + 1) * ts, :]
        qq = jnp.concatenate([jnp.where(lane < DA_QK_DIM, q, zero),
                              jnp.where(lane >= DA_QK_DIM, q, zero)], axis=0)
        scores.append(_dot_nt(qq, k))
    for a in range(n_sub):
        s = scores[a]
        p = jnp.exp2(s - jnp.max(s, axis=-1, keepdims=True)).astype(BF16)
        oe = _dot(p, vext[...])
        on = oe[:, :LANES] / oe[:, LANES:LANES + 1]
        o = on[:ts] - lam * on[ts:]
        o_ref[a * ts:(a + 1) * ts, :] = (_rms(o, sg_ref[...]) * (1.0 - lam_init)).astype(BF16)


def _attn_call(lam_p, q, k_all, v_all, subln_g, *, tq, seq, n_batch, lam_init):
    n_heads = q.shape[1] // LANES
    lk = k_all.shape[1]
    qt = seq // tq
    return pl.pallas_call(
        functools.partial(_attn_kernel, n_sub=tq // ATTN_SUB_ROWS, lam_init=lam_init),
        grid=(n_batch, n_heads, qt),
        in_specs=[pl.BlockSpec(lam_p.shape, lambda b, h, i: (0, 0)),
                  pl.BlockSpec((tq, LANES), lambda b, h, i: (b * qt + i, h)),
                  pl.BlockSpec((None, lk, LANES), lambda b, h, i: (b, 0, h)),
                  pl.BlockSpec((None, lk, LANES), lambda b, h, i: (b, 0, h)),
                  pl.BlockSpec((1, LANES), lambda b, h, i: (0, 0))],
        out_specs=pl.BlockSpec((tq, LANES), lambda b, h, i: (b * qt + i, h)),
        out_shape=jax.ShapeDtypeStruct((n_batch * seq, n_heads * LANES), BF16),
        scratch_shapes=[pltpu.VMEM((lk, 2 * LANES), BF16)],
        compiler_params=_cparams(3, 48),
        name="diff_attention",
    )(lam_p, q, k_all, v_all, subln_g)


def kernel(x, c, ctx, c_ctx, ada_w, ada_b, norm_g, cf_w_in, cf_conv_w, cf_conv_b, cf_ln_g, cf_ln_b,
           cf_w_out, da_w_qkv, da_lambda, da_subln_g, da_w_out, moe_router_w, moe_router_b,
           moe_w_gate, moe_w_up, moe_w_down, moe_sh_gate, moe_sh_up, moe_sh_down):
    n_batch, seq, d = x.shape
    ctx_len = ctx.shape[1]
    depth = ada_w.shape[0]
    assert depth == 2 and n_batch + 1 <= N_SEG
    assert DA_QK_DIM * 2 == LANES and da_subln_g.shape[1] == LANES
    rows_lat, rows_ctx = n_batch * seq, n_batch * ctx_len
    tm = 512
    tq = 256
    assert seq % tm == 0 and rows_ctx % tm == 0 and seq % tq == 0 and ctx_len % tq == 0

    xs = jnp.concatenate([x.reshape(rows_lat, d), ctx.reshape(rows_ctx, d)], axis=0)
    cond = jnp.concatenate([c, c_ctx[None, :], jnp.zeros((N_SEG - n_batch - 1, d), F32)], axis=0)
    mods = _mods_call(cond, ada_w, ada_b)
    mod = lambda layer, k: mods[layer, :, k * d:(k + 1) * d][:, None, :]
    gain = lambda layer, k: norm_g[layer, k][None, :]
    common = dict(seq=seq, n_batch=n_batch)

    cc = cf_conv_w.shape[-1]
    fc = cf_w_in.shape[2] - 2 * cc
    gw = fc // FOURIER_GROUPS
    ch_ang = 2.0 * np.pi * (np.outer(np.arange(gw), np.arange(gw)) % gw) / gw
    cs = jnp.asarray(np.concatenate([np.cos(ch_ang), np.sin(ch_ang)], axis=1) / math.sqrt(gw), BF16)
    a_glu, ab = _inproj_call(xs, gain(0, 0), mod(0, 0), mod(0, 1), cf_w_in[0].astype(BF16), cs,
                             cc=cc, gw=gw, tm=tm, **common)
    gsz = cc // CONV_GROUPS
    gid = np.arange(cc) // gsz
    gmean = jnp.asarray((gid[:, None] == gid[None, :]) / gsz, BF16)
    a_act = _conv_call(a_glu, cf_conv_w[0][:, 0, :], cf_conv_b[0][None, :], cf_ln_g[0][None, :],
                       cf_ln_b[0][None, :], gmean, tl=tq, seq=seq, ctx_len=ctx_len, n_batch=n_batch)
    c_lat, s_lat = _dft_tables(seq, 1.0 / math.sqrt(seq))
    c_ctx_m, s_ctx_m = _dft_tables(ctx_len, 1.0 / math.sqrt(ctx_len))
    fr = jnp.concatenate([
        _seqdft_call(ab, c_lat, s_lat, length=seq, row0=0, n_seq=n_batch, fc=fc, tm=min(seq, 256)),
        _seqdft_call(ab, c_ctx_m, s_ctx_m, length=ctx_len, row0=rows_lat, n_seq=n_batch, fc=fc,
                     tm=min(ctx_len, 256))], axis=0)
    w_out = cf_w_out[0].astype(BF16)
    xs, tok, tokp = _outproj_call([a_act, fr], [w_out[:cc], w_out[cc:]], xs, gain(0, 1), mod(0, 2),
                            gain(0, 2), mod(0, 3), mod(0, 4), rows=rows_lat + rows_ctx, tm=tm, **common)
    xs = _moe_layer(xs, tok, tokp, gain(0, 2), mod(0, 3), mod(0, 4), gain(0, 3), mod(0, 5),
                    moe_router_w[0], moe_router_b[0], 0, moe_w_gate, moe_w_up, moe_w_down,
                    moe_sh_gate[0], moe_sh_up[0], moe_sh_down[0], rows=rows_lat + rows_ctx, tm=tm, **common)

    qkw = DA_HEADS * 2 * DA_QK_DIM
    lam_init = 0.8 - 0.6 * math.exp(-0.3 * 1)
    cos_t, sin_t = _rope_tables(seq, tq)
    q, k_all, v_all = _qkv_call(xs, gain(1, 0), mod(1, 0), mod(1, 1), da_w_qkv[0].astype(BF16),
                                cos_t, sin_t, qkw=qkw, tm=tq, ctx_len=ctx_len, **common)
    o = _attn_call(da_lambda[0], q, k_all, v_all, da_subln_g[0][None, :], tq=tm, seq=seq,
                   n_batch=n_batch, lam_init=lam_init)
    xs, tok, tokp = _outproj_call([o], [da_w_out[0].astype(BF16)], xs, gain(1, 1), mod(1, 2),
                            gain(1, 2), mod(1, 3), mod(1, 4), rows=rows_lat, tm=tm, **common)
    xs = _moe_layer(xs, tok, tokp, gain(1, 2), mod(1, 3), mod(1, 4), gain(1, 3), mod(1, 5),
                    moe_router_w[1], moe_router_b[1], 1, moe_w_gate, moe_w_up, moe_w_down,
                    moe_sh_gate[1], moe_sh_up[1], moe_sh_down[1], rows=rows_lat, tm=tm, **common)
    return xs.reshape(n_batch, seq, d)
```

```python
import functools
import math

import numpy as np
import jax
import jax.numpy as jnp
from jax import lax
from jax.experimental import pallas as pl
from jax.experimental.pallas import tpu as pltpu
from jax.experimental.pallas import tpu_sc as plsc

F32 = jnp.float32
BF16 = jnp.bfloat16

EPS = 1e-6
GRID_W = 64
CONV_GROUPS = 8
FOURIER_GROUPS = 4
DA_HEADS = 8
DA_QK_DIM = 64
ROPE_BASE = 10000.0
N_GROUPS = 8
TOPK_GROUPS = 4
TOP_K = 6
ROUTED_SCALE = 2.5

LANES = 128
SUBLANES = 8
N_SEG = 16
HALO = 16
EXPERT_ROWS = 1024
EXPERT_SUB_ROWS = 128
ATTN_SUB_ROWS = 128
MIB = 1024 * 1024


def _cparams(n_axes, vmem_mib):
    return pltpu.CompilerParams(dimension_semantics=("arbitrary",) * n_axes,
                                vmem_limit_bytes=vmem_mib * MIB)


def _sigmoid(v):
    return 1.0 / (1.0 + jnp.exp(-v))


def _rms(v, g):
    return v * lax.rsqrt(jnp.mean(v * v, axis=-1, keepdims=True) + EPS) * g


def _normmod(v, g, shift, scale):
    return _rms(v, g) * (1.0 + scale) + shift


def _split_bf16(v):
    hi = v.astype(BF16)
    lo = (v - hi.astype(F32)).astype(BF16)
    return hi, lo


def _pack_halves(v):
    half = v.shape[1] // 2
    word = pltpu.pack_elementwise([v[:, :half], v[:, half:]], packed_dtype=BF16)
    return lax.bitcast_convert_type(word, jnp.int32)


def _unpack_halves(w):
    u = lax.bitcast_convert_type(w, jnp.uint32)
    lo = lax.bitcast_convert_type(u << 16, F32)
    hi = lax.bitcast_convert_type(u & jnp.uint32(0xFFFF0000), F32)
    return lo, hi


SC_SPLIT = 2


def _store_pieces(ref, words):
    q = words.shape[1] // SC_SPLIT
    for j in range(SC_SPLIT):
        ref[j] = words[:, j * q:(j + 1) * q]


def _load_pieces(ref):
    return jnp.concatenate([ref[j] for j in range(SC_SPLIT)], axis=-1)


def _dot(a, b):
    return jnp.dot(a, b, preferred_element_type=F32)


def _dot_nt(a, b):
    return lax.dot_general(a, b, (((1,), (1,)), ((), ())), preferred_element_type=F32)


def _mods_kernel(c_ref, w_ref, b_ref, o_ref):
    cv = c_ref[...]
    o_ref[...] = _dot(cv * _sigmoid(cv), w_ref[...]) + b_ref[...]


def _mods_call(cond, ada_w, ada_b):
    depth, d, n = ada_w.shape
    tn = n // 4
    return pl.pallas_call(
        _mods_kernel,
        grid=(depth, n // tn),
        in_specs=[pl.BlockSpec((N_SEG, d), lambda l, j: (0, 0)),
                  pl.BlockSpec((None, d, tn), lambda l, j: (l, 0, j)),
                  pl.BlockSpec((None, 1, tn), lambda l, j: (l, 0, j))],
        out_specs=pl.BlockSpec((None, N_SEG, tn), lambda l, j: (l, 0, j)),
        out_shape=jax.ShapeDtypeStruct((depth, N_SEG, n), F32),
        compiler_params=_cparams(2, 40),
        name="adaln_mods",
    )(cond, ada_w, ada_b.reshape(depth, 1, n))


def _two_source_specs(tm, d, n_head_tiles):
    return (pl.BlockSpec((tm, d), lambda i: (jnp.minimum(i, n_head_tiles - 1), 0)),
            pl.BlockSpec((tm, d), lambda i: (jnp.maximum(i - n_head_tiles, 0), 0)))


def _pick_rows(head_ref, tail_ref, n_head_tiles):
    return jnp.where(pl.program_id(0) < n_head_tiles, head_ref[...], tail_ref[...])


def _inproj_kernel(xh_ref, xt_ref, g_ref, sh_ref, sc_ref, w_ref, cs_ref, a_ref, ab_ref, *, cc, gw, n_head):
    x = _pick_rows(xh_ref, xt_ref, n_head)
    h = _normmod(x, g_ref[...], sh_ref[...], sc_ref[...]).astype(BF16)
    u = _dot(h, w_ref[...])
    a_ref[...] = u[:, :cc] * _sigmoid(u[:, cc:2 * cc])
    f = u[:, 2 * cc:].astype(BF16)
    n_g = f.shape[1] // gw
    parts = [_dot(f[:, g * gw:(g + 1) * gw], cs_ref[...]) for g in range(n_g)]
    cos_part = [p[:, :gw] for p in parts]
    sin_part = [p[:, gw:] for p in parts]
    ab_ref[...] = jnp.concatenate(cos_part + sin_part, axis=-1).astype(BF16)


def _seg_map(tm, seq, n_batch):
    return lambda i: (jnp.minimum(i * tm // seq, n_batch), 0, 0)


def _inproj_call(x_head, x_tail, g, shift, scale, w_in, cs, *, cc, gw, tm, seq, n_batch):
    d = x_head.shape[1]
    rows = x_head.shape[0] + x_tail.shape[0]
    n_head = x_head.shape[0] // tm
    n = w_in.shape[1]
    fc = n - 2 * cc
    seg = _seg_map(tm, seq, n_batch)
    return pl.pallas_call(
        functools.partial(_inproj_kernel, cc=cc, gw=gw, n_head=n_head),
        grid=(rows // tm,),
        in_specs=[*_two_source_specs(tm, d, n_head),
                  pl.BlockSpec((1, d), lambda i: (0, 0)),
                  pl.BlockSpec((None, 1, d), seg),
                  pl.BlockSpec((None, 1, d), seg),
                  pl.BlockSpec((d, n), lambda i: (0, 0)),
                  pl.BlockSpec((gw, 2 * gw), lambda i: (0, 0))],
        out_specs=[pl.BlockSpec((tm, cc), lambda i: (i, 0)),
                   pl.BlockSpec((tm, 2 * fc), lambda i: (i, 0))],
        out_shape=[jax.ShapeDtypeStruct((rows, cc), F32),
                   jax.ShapeDtypeStruct((rows, 2 * fc), BF16)],
        compiler_params=_cparams(1, 40),
        name="inproj_glu_chdft",
    )(x_head, x_tail, g, shift, scale, w_in, cs)


def _conv_kernel(prev_ref, main_ref, next_ref, w_ref, cb_ref, lg_ref, lb_ref, gm_ref, o_ref,
                 buf, cv, shifted, *, tl, width, lat_tiles, lat_tps, ctx_tps, chunk):
    i = pl.program_id(0)
    is_lat = i < lat_tiles
    tps = jnp.where(is_lat, lat_tps, ctx_tps)
    j = jnp.where(is_lat, i, i - lat_tiles) % tps
    zero = jnp.zeros((HALO, buf.shape[1]), F32)
    buf[0:HALO, :] = jnp.where(j > 0, prev_ref[...], zero)
    buf[HALO:HALO + tl, :] = main_ref[...]
    buf[HALO + tl:HALO + tl + HALO, :] = jnp.where(j < tps - 1, next_ref[...], zero)
    base = HALO - width // 2
    span = shifted.shape[1]
    for s in range(1, SUBLANES):
        shifted[s] = buf[s:s + span, :]

    def window(off, r0):
        s, m = off % SUBLANES, off - off % SUBLANES
        if s == 0:
            return buf[m + r0:m + r0 + chunk, :]
        return shifted[s, m + r0:m + r0 + chunk, :]

    for r0 in range(0, tl, chunk):
        acc = window(base, r0) * w_ref[0:1, :]
        for t in range(1, width):
            acc = acc + window(base + t, r0) * w_ref[t:t + 1, :]
        cv[r0:r0 + chunk, :] = acc
    a = cv[...] + cb_ref[...]
    gm = gm_ref[...]
    a_hi, a_lo = _split_bf16(a)
    mu = _dot(a_hi, gm) + _dot(a_lo, gm)
    dl = a - mu
    q_hi, q_lo = _split_bf16(dl * dl)
    var = _dot(q_hi, gm) + _dot(q_lo, gm)
    y = dl * lax.rsqrt(var + EPS) * lg_ref[...] + lb_ref[...]
    o_ref[...] = (y * _sigmoid(y)).astype(BF16)


def _conv_call(a_glu, conv_w, conv_b, ln_g, ln_b, gmean, *, tl, seq, ctx_len, n_batch):
    rows, ch = a_glu.shape
    width = conv_w.shape[0]
    assert width // 2 <= HALO and tl % HALO == 0
    hb = tl // HALO
    last_halo = rows // HALO - 1
    return pl.pallas_call(
        functools.partial(_conv_kernel, tl=tl, width=width, lat_tiles=n_batch * seq // tl,
                          lat_tps=seq // tl, ctx_tps=ctx_len // tl, chunk=32),
        grid=(rows // tl,),
        in_specs=[pl.BlockSpec((HALO, ch), lambda i: (jnp.maximum(i * hb - 1, 0), 0)),
                  pl.BlockSpec((tl, ch), lambda i: (i, 0)),
                  pl.BlockSpec((HALO, ch), lambda i: (jnp.minimum((i + 1) * hb, last_halo), 0)),
                  pl.BlockSpec((width, ch), lambda i: (0, 0)),
                  pl.BlockSpec((1, ch), lambda i: (0, 0)),
                  pl.BlockSpec((1, ch), lambda i: (0, 0)),
                  pl.BlockSpec((1, ch), lambda i: (0, 0)),
                  pl.BlockSpec((ch, ch), lambda i: (0, 0))],
        out_specs=pl.BlockSpec((tl, ch), lambda i: (i, 0)),
        out_shape=jax.ShapeDtypeStruct((rows, ch), BF16),
        scratch_shapes=[pltpu.VMEM((tl + 2 * HALO, ch), F32), pltpu.VMEM((tl, ch), F32),
                        pltpu.VMEM((SUBLANES, tl + 2 * HALO - SUBLANES, ch), F32)],
        compiler_params=_cparams(1, 40),
        name="dwconv_groupln_swish",
    )(a_glu, a_glu, a_glu, conv_w, conv_b, ln_g, ln_b, gmean)


def _seqdft_kernel(c_ref, s_ref, a_ref, b_ref, o_ref):
    o_ref[...] = (_dot(c_ref[...], a_ref[...]) - _dot(s_ref[...], b_ref[...])).astype(BF16)


def _seqdft_call(ab, cmat, smat, *, length, row0, n_seq, fc, tm):
    seq0 = row0 // length
    return pl.pallas_call(
        _seqdft_kernel,
        grid=(n_seq, length // tm),
        in_specs=[pl.BlockSpec((tm, length), lambda b, i: (i, 0)),
                  pl.BlockSpec((tm, length), lambda b, i: (i, 0)),
                  pl.BlockSpec((length, fc), lambda b, i: (seq0 + b, 0)),
                  pl.BlockSpec((length, fc), lambda b, i: (seq0 + b, 1))],
        out_specs=pl.BlockSpec((tm, fc), lambda b, i: (b * (length // tm) + i, 0)),
        out_shape=jax.ShapeDtypeStruct((n_seq * length, fc), BF16),
        compiler_params=_cparams(2, 48),
        name="seq_dft",
    )(cmat, smat, ab, ab)


def _dft_tables(length, scale):
    if length <= 512:
        kn = np.outer(np.arange(length), np.arange(length)) % length
        ang = 2.0 * np.pi * kn / length
        return (jnp.asarray(np.cos(ang) * scale, BF16), jnp.asarray(np.sin(ang) * scale, BF16))
    r = 64
    assert length % r == 0
    k = np.arange(length)[:, None]
    alpha = 2.0 * np.pi * ((k * np.arange(length // r)[None, :] * r) % length) / length
    beta = 2.0 * np.pi * ((k * np.arange(r)[None, :]) % length) / length
    ca, sa = jnp.asarray(np.cos(alpha), F32)[:, :, None], jnp.asarray(np.sin(alpha), F32)[:, :, None]
    cb, sb = jnp.asarray(np.cos(beta) * scale, F32)[:, None, :], jnp.asarray(np.sin(beta) * scale, F32)[:, None, :]
    cmat = (ca * cb - sa * sb).reshape(length, length).astype(BF16)
    smat = (sa * cb + ca * sb).reshape(length, length).astype(BF16)
    return cmat, smat


def _outproj_kernel(*refs, n_in, n_head):
    ins, ws = refs[:n_in], refs[n_in:2 * n_in]
    (xh_ref, xt_ref, g1_ref, gate_ref, g2_ref, sh_ref, sc_ref, wh_ref, wl_ref, rb_ref,
     xo_ref, tok_ref, tokp_ref, e_ref, gt_ref, pos_ref, cnt_ref, carry) = refs[2 * n_in:]
    y = _dot(ins[0][...], ws[0][...])
    for a_ref, w_ref in zip(ins[1:], ws[1:]):
        y = y + _dot(a_ref[...], w_ref[...])
    x1 = _pick_rows(xh_ref, xt_ref, n_head) + gate_ref[...] * _rms(y, g1_ref[...])
    xo_ref[...] = x1
    tok = _normmod(x1, g2_ref[...], sh_ref[...], sc_ref[...])
    tok_ref[...] = tok.astype(BF16)
    _store_pieces(tokp_ref, _pack_halves(tok))
    _route(tok, wh_ref, wl_ref, rb_ref, e_ref, gt_ref, pos_ref, cnt_ref, carry)


def _outproj_call(ins, ws, x_head, x_tail, g1, gate, g2, shift, scale, router_w, router_b,
                  *, rows, tm, seq, n_batch):
    d = x_head.shape[1]
    n_head = min(x_head.shape[0], rows) // tm
    n_exp = router_w.shape[1]
    w_t = router_w.T
    w_hi = w_t.astype(BF16)
    w_lo = (w_t - w_hi.astype(F32)).astype(BF16)
    seg = _seg_map(tm, seq, n_batch)
    n_in = len(ins)
    row_spec = lambda a: pl.BlockSpec((tm, a.shape[1]), lambda i: (i, 0))
    full_spec = lambda a: pl.BlockSpec(a.shape, lambda i: (0, 0))
    lane_spec = pl.BlockSpec((SUBLANES, tm), lambda i: (0, i))
    tok_rows = lambda dt: jax.ShapeDtypeStruct((SUBLANES, rows), dt)
    return pl.pallas_call(
        functools.partial(_outproj_kernel, n_in=n_in, n_head=n_head),
        grid=(rows // tm,),
        in_specs=[row_spec(a) for a in ins] + [full_spec(w) for w in ws] + [
            *_two_source_specs(tm, d, n_head), full_spec(g1), pl.BlockSpec((None, 1, d), seg), full_spec(g2),
            pl.BlockSpec((None, 1, d), seg), pl.BlockSpec((None, 1, d), seg),
            full_spec(w_hi), full_spec(w_lo), pl.BlockSpec((n_exp, 1), lambda i: (0, 0))],
        out_specs=[pl.BlockSpec((tm, d), lambda i: (i, 0)), pl.BlockSpec((tm, d), lambda i: (i, 0)),
                   pl.BlockSpec((SC_SPLIT, tm, d // 2 // SC_SPLIT), lambda i: (0, i, 0)),
                   lane_spec, lane_spec, lane_spec, pl.BlockSpec((n_exp, LANES), lambda i: (0, 0))],
        out_shape=[jax.ShapeDtypeStruct((rows, d), F32), jax.ShapeDtypeStruct((rows, d), BF16),
                   jax.ShapeDtypeStruct((SC_SPLIT, rows, d // 2 // SC_SPLIT), jnp.int32),
                   tok_rows(jnp.int32), tok_rows(F32), tok_rows(jnp.int32),
                   jax.ShapeDtypeStruct((n_exp, LANES), F32)],
        scratch_shapes=[pltpu.VMEM((n_exp, 1), F32)],
        compiler_params=_cparams(1, 48),
        name="outproj_residual_route",
    )(*ins, *ws, x_head, x_tail, g1, gate, g2, shift, scale, w_hi, w_lo, router_b.reshape(n_exp, 1))


def _pick_max(cur, idx):
    mx = jnp.max(cur, axis=0, keepdims=True)
    first = jnp.min(jnp.where(cur == mx, idx, float(cur.shape[0])), axis=0, keepdims=True)
    return first, idx == first


def _route(tok, wh_ref, wl_ref, rb_ref, e_ref, gt_ref, pos_ref, cnt_ref, carry):
    @pl.when(pl.program_id(0) == 0)
    def _():
        carry[...] = jnp.zeros_like(carry)

    t_hi, t_lo = _split_bf16(tok)
    wh, wl = wh_ref[...], wl_ref[...]
    logits = _dot_nt(wh, t_hi) + _dot_nt(wh, t_lo) + _dot_nt(wl, t_hi)
    n_exp, tm = logits.shape
    scores = _sigmoid(logits)
    biased = scores + rb_ref[...]
    gsz = n_exp // N_GROUPS
    neg = -jnp.inf

    b3 = biased.reshape(N_GROUPS, gsz, tm)
    im = lax.broadcasted_iota(jnp.int32, b3.shape, 1).astype(F32)
    m1 = jnp.max(b3, axis=1, keepdims=True)
    i1 = jnp.min(jnp.where(b3 == m1, im, float(gsz)), axis=1, keepdims=True)
    m2 = jnp.max(jnp.where(im == i1, neg, b3), axis=1, keepdims=True)
    gscore = (m1 + m2).reshape(N_GROUPS, tm)

    ig = lax.broadcasted_iota(jnp.int32, gscore.shape, 0).astype(F32)
    gsel = jnp.zeros_like(gscore)
    cur = gscore
    for _ in range(TOPK_GROUPS):
        _, hit = _pick_max(cur, ig)
        gsel = jnp.where(hit, 1.0, gsel)
        cur = jnp.where(hit, neg, cur)
    gsel3 = jnp.broadcast_to(gsel.reshape(N_GROUPS, 1, tm), b3.shape)
    cur = jnp.where(gsel3 > 0.0, b3, neg).reshape(n_exp, tm)

    ie = lax.broadcasted_iota(jnp.int32, (n_exp, tm), 0).astype(F32)
    sel = jnp.zeros((n_exp, tm), F32)
    picks, raw = [], []
    for _ in range(TOP_K):
        first, hit = _pick_max(cur, ie)
        picks.append(first)
        raw.append(jnp.sum(jnp.where(hit, scores, 0.0), axis=0, keepdims=True))
        sel = jnp.where(hit, 1.0, sel)
        cur = jnp.where(hit, neg, cur)
    total = raw[0]
    for r in raw[1:]:
        total = total + r

    ri = lax.broadcasted_iota(jnp.int32, (tm, tm), 0)
    ci = lax.broadcasted_iota(jnp.int32, (tm, tm), 1)
    upper = jnp.where(ri < ci, 1.0, 0.0).astype(BF16)
    rank = _dot(sel.astype(BF16), upper) + carry[...]
    ranks = [jnp.sum(jnp.where(ie == p, rank, 0.0), axis=0, keepdims=True) for p in picks]
    carry[...] = carry[...] + jnp.sum(sel, axis=1, keepdims=True)

    pad = jnp.zeros((SUBLANES - TOP_K, tm), F32)
    e_ref[...] = jnp.concatenate(picks + [pad], axis=0).astype(jnp.int32)
    gt_ref[...] = jnp.concatenate([r / total * ROUTED_SCALE for r in raw] + [pad], axis=0)
    pos_ref[...] = jnp.concatenate(ranks + [pad], axis=0).astype(jnp.int32)
    cnt_ref[...] = jnp.broadcast_to(carry[...], cnt_ref.shape)


def _expert_kernel(be_ref, nv_ref, nu_ref, xs_ref, wg_ref, wu_ref, wd_ref, ys_ref, wgu, wdn, *, ff, n_sub):
    b = pl.program_id(0)
    changed = jnp.logical_or(b == 0, be_ref[b] != be_ref[jnp.maximum(b - 1, 0)])

    @pl.when(changed)
    def _():
        wgu[:, :ff] = wg_ref[...].astype(BF16)
        wgu[:, ff:] = wu_ref[...].astype(BF16)
        wdn[...] = wd_ref[...].astype(BF16)

    @pl.when(b < nu_ref[0])
    def _():
        n_valid = nv_ref[b]
        q = xs_ref.shape[2]
        ts = xs_ref.shape[1] // n_sub
        gus = []
        for a in range(n_sub):
            r0 = a * ts
            xw = jnp.concatenate([xs_ref[j, r0:r0 + ts, :] for j in range(SC_SPLIT)], axis=-1)
            row = lax.broadcasted_iota(jnp.int32, xw.shape, 0) + r0
            lo, hi = _unpack_halves(jnp.where(row < n_valid, xw, 0))
            half = lo.shape[1]
            gus.append(_dot(lo.astype(BF16), wgu[:half, :]) + _dot(hi.astype(BF16), wgu[half:, :]))
        outs = []
        for gu in gus:
            gate = gu[:, :ff]
            hmid = (gate * _sigmoid(gate) * gu[:, ff:]).astype(BF16)
            outs.append(_dot(hmid, wdn[...]))
        for a, y in enumerate(outs):
            words = _pack_halves(y)
            for j in range(SC_SPLIT):
                ys_ref[j, a * ts:(a + 1) * ts, :] = words[:, j * q:(j + 1) * q]


def _expert_call(block_e, n_valid, n_used, xs, layer, w_gate, w_up, w_down):
    _, n_rows, q = xs.shape
    d = 2 * SC_SPLIT * q
    ff = w_gate.shape[3]
    te = EXPERT_ROWS
    row_map = lambda b, be, nv, nu: (0, jnp.minimum(b, nu[0] - 1), 0)
    w_map = lambda b, be, nv, nu: (layer, be[b], 0, 0)
    grid_spec = pltpu.PrefetchScalarGridSpec(
        num_scalar_prefetch=3,
        grid=(n_rows // te,),
        in_specs=[pl.BlockSpec((SC_SPLIT, te, q), row_map),
                  pl.BlockSpec((None, None, d, ff), w_map),
                  pl.BlockSpec((None, None, d, ff), w_map),
                  pl.BlockSpec((None, None, ff, d), w_map)],
        out_specs=pl.BlockSpec((SC_SPLIT, te, q), row_map),
        scratch_shapes=[pltpu.VMEM((d, 2 * ff), BF16), pltpu.VMEM((ff, d), BF16)])
    return pl.pallas_call(
        functools.partial(_expert_kernel, ff=ff, n_sub=EXPERT_ROWS // EXPERT_SUB_ROWS),
        grid_spec=grid_spec,
        out_shape=jax.ShapeDtypeStruct(xs.shape, jnp.int32),
        compiler_params=_cparams(1, 40),
        name="moe_experts",
    )(block_e, n_valid, n_used, xs, w_gate, w_up, w_down)


SC_WINDOW = 128


def _sc_mesh():
    return plsc.VectorSubcoreMesh(core_axis_name="core", subcore_axis_name="subcore")


def _sc_scatter_rows(rows, dests, n_out):
    n, width = rows.shape
    n_k = len(dests)
    assert n % SC_WINDOW == 0

    @functools.partial(pl.kernel, out_type=jax.ShapeDtypeStruct((n_out, width), rows.dtype),
                       mesh=_sc_mesh(), name="sc_dispatch_rows")
    def scatter_kernel(x_hbm, *refs):
        idx_hbm, o_hbm = refs[:n_k], refs[n_k]

        def body(x_vmem, *idx_vmem):
            for iv in idx_vmem:
                pltpu.sync_copy(x_vmem, o_hbm.at[iv.at[0]])

        pltpu.emit_pipeline(
            body,
            grid=(n // SC_WINDOW,),
            in_specs=[pl.BlockSpec((SC_WINDOW, width), lambda i: (i, 0))]
            + [pl.BlockSpec((1, SC_WINDOW), lambda i: (0, i))] * n_k,
            out_specs=[],
            core_axis_name=("core", "subcore"),
            dimension_semantics=(pltpu.PARALLEL,),
        )(x_hbm, *idx_hbm)

    return scatter_kernel(rows, *dests)


def _sc_gather_rows(table, idx):
    n = idx.shape[1]
    width = table.shape[1]
    assert n % SC_WINDOW == 0

    @functools.partial(pl.kernel, out_type=jax.ShapeDtypeStruct((n, width), table.dtype),
                       mesh=_sc_mesh(), name="sc_collect_rows")
    def gather_kernel(t_hbm, i_hbm, o_hbm):
        def body(i_vmem, o_vmem):
            pltpu.sync_copy(t_hbm.at[i_vmem.at[0]], o_vmem)

        pltpu.emit_pipeline(
            body,
            grid=(n // SC_WINDOW,),
            in_specs=[pl.BlockSpec((1, SC_WINDOW), lambda i: (0, i))],
            out_specs=[pl.BlockSpec((SC_WINDOW, width), lambda i: (i, 0))],
            core_axis_name=("core", "subcore"),
            dimension_semantics=(pltpu.PARALLEL,),
        )(i_hbm, o_hbm)

    return gather_kernel(table, idx)


def _shared_kernel(tok_ref, sgu_ref, sd_ref, o_ref, *, ff):
    gu = _dot(tok_ref[...], sgu_ref[...])
    gate = gu[:, :ff]
    hmid = (gate * _sigmoid(gate) * gu[:, ff:]).astype(BF16)
    o_ref[...] = _dot(hmid, sd_ref[...]).astype(BF16)


def _shared_call(tok, sgu, sd, *, rows, tm):
    d = tok.shape[1]
    ff = sd.shape[0]
    return pl.pallas_call(
        functools.partial(_shared_kernel, ff=ff),
        grid=(rows // tm,),
        in_specs=[pl.BlockSpec((tm, d), lambda i: (i, 0)),
                  pl.BlockSpec((d, 2 * ff), lambda i: (0, 0)),
                  pl.BlockSpec((ff, d), lambda i: (0, 0))],
        out_specs=pl.BlockSpec((tm, d), lambda i: (i, 0)),
        out_shape=jax.ShapeDtypeStruct((rows, d), BF16),
        compiler_params=_cparams(1, 40),
        name="moe_shared_expert",
    )(tok, sgu, sd)


def _moe_out_kernel(shared_ref, yg_ref, gk_ref, x_ref, g_ref, gate_ref, xo_ref):
    shared = shared_ref[...].astype(F32)
    gk = gk_ref[...]
    lo_acc, hi_acc = None, None
    for k in range(TOP_K):
        lo, hi = _unpack_halves(jnp.concatenate([yg_ref[j, k] for j in range(SC_SPLIT)], axis=-1))
        w = gk[:, k:k + 1]
        lo_acc = lo * w if lo_acc is None else lo_acc + lo * w
        hi_acc = hi * w if hi_acc is None else hi_acc + hi * w
    y = jnp.concatenate([lo_acc, hi_acc], axis=-1) + shared
    xo_ref[...] = x_ref[...] + gate_ref[...] * _rms(y, g_ref[...])


def _moe_out_call(shared, yg, gk, xs, g, gate, *, rows, tm, seq, n_batch):
    d = xs.shape[1]
    seg = _seg_map(tm, seq, n_batch)
    row = lambda: pl.BlockSpec((tm, d), lambda i: (i, 0))
    return pl.pallas_call(
        _moe_out_kernel,
        grid=(rows // tm,),
        in_specs=[row(),
                  pl.BlockSpec((SC_SPLIT, TOP_K, tm, d // 2 // SC_SPLIT), lambda i: (0, 0, i, 0)),
                  pl.BlockSpec((tm, SUBLANES), lambda i: (i, 0)),
                  row(),
                  pl.BlockSpec((1, d), lambda i: (0, 0)),
                  pl.BlockSpec((None, 1, d), seg)],
        out_specs=row(),
        out_shape=jax.ShapeDtypeStruct((rows, d), F32),
        compiler_params=_cparams(1, 48),
        name="moe_combine",
    )(shared, yg, gk, xs, g, gate)


def _dest_kernel(e_ref, pos_ref, ps_ref, o_ref, *, n_rows):
    e = e_ref[...].astype(F32)
    ps = ps_ref[...]
    n_exp, tm = ps.shape[0], e.shape[1]
    ids = lax.broadcasted_iota(jnp.int32, (n_exp, tm), 0).astype(F32)
    rows = [jnp.sum(jnp.where(ids == e[k:k + 1, :], ps, 0.0), axis=0, keepdims=True) for k in range(SUBLANES)]
    dest = jnp.concatenate(rows, axis=0).astype(jnp.int32) + pos_ref[...]
    for j in range(SC_SPLIT):
        o_ref[j] = dest + j * n_rows


def _dest_call(e_tk, pos_tk, pad_starts, *, n_rows, tm):
    rows = e_tk.shape[1]
    n_exp = pad_starts.shape[0]
    lane_spec = pl.BlockSpec((SUBLANES, tm), lambda i: (0, i))
    return pl.pallas_call(
        functools.partial(_dest_kernel, n_rows=n_rows),
        grid=(rows // tm,),
        in_specs=[lane_spec, lane_spec, pl.BlockSpec((n_exp, 1), lambda i: (0, 0))],
        out_specs=pl.BlockSpec((SC_SPLIT, SUBLANES, tm), lambda i: (0, 0, i)),
        out_shape=jax.ShapeDtypeStruct((SC_SPLIT, SUBLANES, rows), jnp.int32),
        compiler_params=_cparams(1, 32),
        name="moe_dest_rows",
    )(e_tk, pos_tk, pad_starts.astype(F32).reshape(n_exp, 1))


def _moe_layer(xs, tok, tokp, routing, g_out, gate, layer, w_gate, w_up, w_down, sh_gate, sh_up, sh_down,
               *, rows, tm, seq, n_batch):
    e_tk, g_tk, pos_tk, counts = routing
    n_exp = counts.shape[0]
    te = EXPERT_ROWS
    cnt = counts[:, 0].astype(jnp.int32)
    padded = (cnt + te - 1) // te * te
    pad_ends = jnp.cumsum(padded)
    pad_starts = pad_ends - padded
    expert_ids = jnp.arange(n_exp, dtype=jnp.int32)
    n_blocks = -(-(rows * TOP_K + n_exp * (te - 1)) // te)
    n_used = pad_ends[-1] // te
    blk = jnp.arange(n_blocks, dtype=jnp.int32)
    last = jnp.minimum(blk, n_used - 1) * te
    owner = (last[:, None] >= pad_ends[None, :]).astype(jnp.int32)
    block_e = jnp.minimum(jnp.sum(owner, axis=1), n_exp - 1)
    own_hot = block_e[:, None] == expert_ids[None, :]
    n_valid = jnp.sum(jnp.where(own_hot, (pad_starts + cnt)[None, :], 0), axis=1) - last
    n_valid = jnp.clip(n_valid, 0, te)
    q = tokp.shape[2]
    n_rows = n_blocks * te
    piece = _dest_call(e_tk, pos_tk, pad_starts, n_rows=n_rows, tm=tm)[:, :TOP_K, :]
    xg = _sc_scatter_rows(tokp.reshape(SC_SPLIT * rows, q),
                          [piece[:, k, :].reshape(1, SC_SPLIT * rows) for k in range(TOP_K)],
                          SC_SPLIT * n_rows).reshape(SC_SPLIT, n_rows, q)
    sgu = jnp.concatenate([sh_gate, sh_up], axis=1).astype(BF16)
    shared = _shared_call(tok, sgu, sh_down.astype(BF16), rows=rows, tm=tm)
    ys = _expert_call(block_e, n_valid, n_used.reshape(1).astype(jnp.int32), xg, layer, w_gate, w_up, w_down)
    yg = _sc_gather_rows(ys.reshape(SC_SPLIT * n_rows, q),
                         piece.reshape(1, -1)).reshape(SC_SPLIT, TOP_K, rows, q)
    return _moe_out_call(shared, yg, g_tk.T, xs, g_out, gate, rows=rows, tm=tm, seq=seq, n_batch=n_batch)


def _rope(t, cos, sin):
    half = DA_QK_DIM // 2
    up = pltpu.roll(t, shift=LANES - half, axis=1)
    dn = pltpu.roll(t, shift=half, axis=1)
    lane = lax.broadcasted_iota(jnp.int32, t.shape, 1) % DA_QK_DIM
    return t * cos + jnp.where(lane < half, -up, dn) * sin


def _qkv_kernel(x_ref, g_ref, sh_ref, sc_ref, w_ref, cos_ref, sin_ref, q_ref, k_ref, v_ref, *, qkw):
    h = _normmod(x_ref[...], g_ref[...], sh_ref[...], sc_ref[...]).astype(BF16)
    qkv = _dot(h, w_ref[...])
    cos, sin = cos_ref[...], sin_ref[...]
    q_scale = DA_QK_DIM ** -0.5 * math.log2(math.e)
    for hb in range(qkw // LANES):
        lo, hi = hb * LANES, (hb + 1) * LANES
        q_ref[:, lo:hi] = (_rope(qkv[:, lo:hi], cos, sin) * q_scale).astype(BF16)
        k_ref[:, lo:hi] = _rope(qkv[:, qkw + lo:qkw + hi], cos, sin).astype(BF16)
    v_ref[...] = qkv[:, 2 * qkw:].astype(BF16)


def _qkv_call(xs, g, shift, scale, w_qkv, cos_t, sin_t, *, qkw, tm, seq, ctx_len, n_batch):
    rows, d = xs.shape
    n = w_qkv.shape[1]
    vw = n - 2 * qkw
    lat_tiles, lat_tps, ctx_tps = n_batch * seq // tm, seq // tm, ctx_len // tm
    seg = _seg_map(tm, seq, n_batch)

    def kv_map(i):
        c = i - lat_tiles
        is_lat = i < lat_tiles
        return (jnp.where(is_lat, i // lat_tps, c // ctx_tps),
                jnp.where(is_lat, ctx_tps + i % lat_tps, c % ctx_tps), 0)

    rope_map = lambda i: (jnp.where(i < lat_tiles, i % lat_tps, lat_tps), 0)
    return pl.pallas_call(
        functools.partial(_qkv_kernel, qkw=qkw),
        grid=(rows // tm,),
        in_specs=[pl.BlockSpec((tm, d), lambda i: (i, 0)),
                  pl.BlockSpec((1, d), lambda i: (0, 0)),
                  pl.BlockSpec((None, 1, d), seg),
                  pl.BlockSpec((None, 1, d), seg),
                  pl.BlockSpec((d, n), lambda i: (0, 0)),
                  pl.BlockSpec((tm, LANES), rope_map),
                  pl.BlockSpec((tm, LANES), rope_map)],
        out_specs=[pl.BlockSpec((tm, qkw), lambda i: (i, 0)),
                   pl.BlockSpec((None, tm, qkw), kv_map),
                   pl.BlockSpec((None, tm, vw), kv_map)],
        out_shape=[jax.ShapeDtypeStruct((rows, qkw), BF16),
                   jax.ShapeDtypeStruct((n_batch, ctx_len + seq, qkw), BF16),
                   jax.ShapeDtypeStruct((n_batch, ctx_len + seq, vw), BF16)],
        compiler_params=_cparams(1, 48),
        name="qkv_rope",
    )(xs, g, shift, scale, w_qkv, cos_t, sin_t)


def _rope_tables(seq, tm):
    pos = np.arange(seq)
    n_freq = DA_QK_DIM // 4
    inv = np.power(ROPE_BASE, -np.arange(n_freq, dtype=np.float32) / n_freq).astype(np.float32)
    row = (pos // GRID_W).astype(np.float32)[:, None] * inv
    col = (pos % GRID_W).astype(np.float32)[:, None] * inv
    ang = np.concatenate([row, col], axis=-1).astype(np.float32)
    ang = np.tile(ang, (1, LANES // ang.shape[1]))
    cos = np.concatenate([np.cos(ang), np.ones((tm, LANES))], axis=0)
    sin = np.concatenate([np.sin(ang), np.zeros((tm, LANES))], axis=0)
    return jnp.asarray(cos, F32), jnp.asarray(sin, F32)


def _attn_kernel(lp_ref, q_ref, k_ref, v_ref, sg_ref, o_ref, vext, *, n_sub, lam_init):
    @pl.when(pl.program_id(2) == 0)
    def _():
        vext[:, :LANES] = v_ref[...]
        vext[:, LANES:] = jnp.ones((vext.shape[0], LANES), BF16)

    lp = lp_ref[...]
    lam = (jnp.exp(jnp.sum(lp[0:1] * lp[1:2], axis=1, keepdims=True))
           - jnp.exp(jnp.sum(lp[2:3] * lp[3:4], axis=1, keepdims=True)) + lam_init)
    ts = q_ref.shape[0] // n_sub
    lane = lax.broadcasted_iota(jnp.int32, (ts, LANES), 1)
    zero = jnp.zeros((ts, LANES), BF16)
    k = k_ref[...]
    scores = []
    for a in range(n_sub):
        q = q_ref[a * ts:(a + 1) * ts, :]
        qq = jnp.concatenate([jnp.where(lane < DA_QK_DIM, q, zero),
                              jnp.where(lane >= DA_QK_DIM, q, zero)], axis=0)
        scores.append(_dot_nt(qq, k))
    for a in range(n_sub):
        s = scores[a]
        p = jnp.exp2(s - jnp.max(s, axis=-1, keepdims=True)).astype(BF16)
        oe = _dot(p, vext[...])
        on = oe[:, :LANES] / oe[:, LANES:LANES + 1]
        o = on[:ts] - lam * on[ts:]
        o_ref[a * ts:(a + 1) * ts, :] = (_rms(o, sg_ref[...]) * (1.0 - lam_init)).astype(BF16)


def _attn_call(lam_p, q, k_all, v_all, subln_g, *, tq, seq, n_batch, lam_init):
    n_heads = q.shape[1] // LANES
    lk = k_all.shape[1]
    qt = seq // tq
    return pl.pallas_call(
        functools.partial(_attn_kernel, n_sub=tq // ATTN_SUB_ROWS, lam_init=lam_init),
        grid=(n_batch, n_heads, qt),
        in_specs=[pl.BlockSpec(lam_p.shape, lambda b, h, i: (0, 0)),
                  pl.BlockSpec((tq, LANES), lambda b, h, i: (b * qt + i, h)),
                  pl.BlockSpec((None, lk, LANES), lambda b, h, i: (b, 0, h)),
                  pl.BlockSpec((None, lk, LANES), lambda b, h, i: (b, 0, h)),
                  pl.BlockSpec((1, LANES), lambda b, h, i: (0, 0))],
        out_specs=pl.BlockSpec((tq, LANES), lambda b, h, i: (b * qt + i, h)),
        out_shape=jax.ShapeDtypeStruct((n_batch * seq, n_heads * LANES), BF16),
        scratch_shapes=[pltpu.VMEM((lk, 2 * LANES), BF16)],
        compiler_params=_cparams(3, 48),
        name="diff_attention",
    )(lam_p, q, k_all, v_all, subln_g)


def kernel(x, c, ctx, c_ctx, ada_w, ada_b, norm_g, cf_w_in, cf_conv_w, cf_conv_b, cf_ln_g, cf_ln_b,
           cf_w_out, da_w_qkv, da_lambda, da_subln_g, da_w_out, moe_router_w, moe_router_b,
           moe_w_gate, moe_w_up, moe_w_down, moe_sh_gate, moe_sh_up, moe_sh_down):
    n_batch, seq, d = x.shape
    ctx_len = ctx.shape[1]
    depth = ada_w.shape[0]
    assert depth == 2 and n_batch + 1 <= N_SEG
    assert DA_QK_DIM * 2 == LANES and da_subln_g.shape[1] == LANES
    rows_lat, rows_ctx = n_batch * seq, n_batch * ctx_len
    tm = 512
    tq = 256
    assert seq % tm == 0 and rows_ctx % tm == 0 and seq % tq == 0 and ctx_len % tq == 0

    x_lat, x_ctx = x.reshape(rows_lat, d), ctx.reshape(rows_ctx, d)
    cond = jnp.concatenate([c, c_ctx[None, :], jnp.zeros((N_SEG - n_batch - 1, d), F32)], axis=0)
    mods = _mods_call(cond, ada_w, ada_b)
    mod = lambda layer, k: mods[layer, :, k * d:(k + 1) * d][:, None, :]
    gain = lambda layer, k: norm_g[layer, k][None, :]
    common = dict(seq=seq, n_batch=n_batch)

    cc = cf_conv_w.shape[-1]
    fc = cf_w_in.shape[2] - 2 * cc
    gw = fc // FOURIER_GROUPS
    ch_ang = 2.0 * np.pi * (np.outer(np.arange(gw), np.arange(gw)) % gw) / gw
    cs = jnp.asarray(np.concatenate([np.cos(ch_ang), np.sin(ch_ang)], axis=1) / math.sqrt(gw), BF16)
    a_glu, ab = _inproj_call(x_lat, x_ctx, gain(0, 0), mod(0, 0), mod(0, 1), cf_w_in[0].astype(BF16), cs,
                             cc=cc, gw=gw, tm=tm, **common)
    gsz = cc // CONV_GROUPS
    gid = np.arange(cc) // gsz
    gmean = jnp.asarray((gid[:, None] == gid[None, :]) / gsz, BF16)
    a_act = _conv_call(a_glu, cf_conv_w[0][:, 0, :], cf_conv_b[0][None, :], cf_ln_g[0][None, :],
                       cf_ln_b[0][None, :], gmean, tl=tq, seq=seq, ctx_len=ctx_len, n_batch=n_batch)
    c_lat, s_lat = _dft_tables(seq, 1.0 / math.sqrt(seq))
    c_ctx_m, s_ctx_m = _dft_tables(ctx_len, 1.0 / math.sqrt(ctx_len))
    fr = jnp.concatenate([
        _seqdft_call(ab, c_lat, s_lat, length=seq, row0=0, n_seq=n_batch, fc=fc, tm=min(seq, 256)),
        _seqdft_call(ab, c_ctx_m, s_ctx_m, length=ctx_len, row0=rows_lat, n_seq=n_batch, fc=fc,
                     tm=min(ctx_len, 256))], axis=0)
    w_out = cf_w_out[0].astype(BF16)
    xs, tok, tokp, *routing = _outproj_call(
        [a_act, fr], [w_out[:cc], w_out[cc:]], x_lat, x_ctx, gain(0, 1), mod(0, 2), gain(0, 2), mod(0, 3),
        mod(0, 4), moe_router_w[0], moe_router_b[0], rows=rows_lat + rows_ctx, tm=tm, **common)
    xs = _moe_layer(xs, tok, tokp, routing, gain(0, 3), mod(0, 5), 0, moe_w_gate, moe_w_up, moe_w_down,
                    moe_sh_gate[0], moe_sh_up[0], moe_sh_down[0], rows=rows_lat + rows_ctx, tm=tm, **common)

    qkw = DA_HEADS * 2 * DA_QK_DIM
    lam_init = 0.8 - 0.6 * math.exp(-0.3 * 1)
    cos_t, sin_t = _rope_tables(seq, tq)
    q, k_all, v_all = _qkv_call(xs, gain(1, 0), mod(1, 0), mod(1, 1), da_w_qkv[0].astype(BF16),
                                cos_t, sin_t, qkw=qkw, tm=tq, ctx_len=ctx_len, **common)
    o = _attn_call(da_lambda[0], q, k_all, v_all, da_subln_g[0][None, :], tq=tm, seq=seq,
                   n_batch=n_batch, lam_init=lam_init)
    xs, tok, tokp, *routing = _outproj_call(
        [o], [da_w_out[0].astype(BF16)], xs, xs, gain(1, 1), mod(1, 2), gain(1, 2), mod(1, 3), mod(1, 4),
        moe_router_w[1], moe_router_b[1], rows=rows_lat, tm=tm, **common)
    xs = _moe_layer(xs, tok, tokp, routing, gain(1, 3), mod(1, 5), 1, moe_w_gate, moe_w_up, moe_w_down,
                    moe_sh_gate[1], moe_sh_up[1], moe_sh_down[1], rows=rows_lat, tm=tm, **common)
    return xs.reshape(n_batch, seq, d)
```

```python
import functools
import math

import numpy as np
import jax
import jax.numpy as jnp
from jax import lax
from jax.experimental import pallas as pl
from jax.experimental.pallas import tpu as pltpu
from jax.experimental.pallas import tpu_sc as plsc

F32 = jnp.float32
BF16 = jnp.bfloat16

EPS = 1e-6
GRID_W = 64
CONV_GROUPS = 8
FOURIER_GROUPS = 4
DA_HEADS = 8
DA_QK_DIM = 64
ROPE_BASE = 10000.0
N_GROUPS = 8
TOPK_GROUPS = 4
TOP_K = 6
ROUTED_SCALE = 2.5

LANES = 128
SUBLANES = 8
N_SEG = 16
HALO = 16
EXPERT_ROWS = 1024
EXPERT_SUB_ROWS = 128
EXPERT_RING = 3
ATTN_SUB_ROWS = 128
MIB = 1024 * 1024


def _cparams(n_axes, vmem_mib):
    return pltpu.CompilerParams(dimension_semantics=("arbitrary",) * n_axes,
                                vmem_limit_bytes=vmem_mib * MIB)


def _sigmoid(v):
    return 1.0 / (1.0 + jnp.exp(-v))


def _rms(v, g):
    return v * lax.rsqrt(jnp.mean(v * v, axis=-1, keepdims=True) + EPS) * g


def _normmod(v, g, shift, scale):
    return _rms(v, g) * (1.0 + scale) + shift


def _split_bf16(v):
    hi = v.astype(BF16)
    lo = (v - hi.astype(F32)).astype(BF16)
    return hi, lo


def _pack_halves(v):
    half = v.shape[1] // 2
    word = pltpu.pack_elementwise([v[:, :half], v[:, half:]], packed_dtype=BF16)
    return lax.bitcast_convert_type(word, jnp.int32)


def _unpack_halves(w):
    u = lax.bitcast_convert_type(w, jnp.uint32)
    lo = lax.bitcast_convert_type(u << 16, F32)
    hi = lax.bitcast_convert_type(u & jnp.uint32(0xFFFF0000), F32)
    return lo, hi


SC_SPLIT = 2


def _store_pieces(ref, words):
    q = words.shape[1] // SC_SPLIT
    for j in range(SC_SPLIT):
        ref[j] = words[:, j * q:(j + 1) * q]


def _load_pieces(ref):
    return jnp.concatenate([ref[j] for j in range(SC_SPLIT)], axis=-1)


def _dot(a, b):
    return jnp.dot(a, b, preferred_element_type=F32)


def _dot_nt(a, b):
    return lax.dot_general(a, b, (((1,), (1,)), ((), ())), preferred_element_type=F32)


def _mods_kernel(c_ref, w_ref, b_ref, o_ref):
    cv = c_ref[...]
    o_ref[...] = _dot(cv * _sigmoid(cv), w_ref[...]) + b_ref[...]


def _mods_call(cond, ada_w, ada_b):
    depth, d, n = ada_w.shape
    tn = n // 4
    return pl.pallas_call(
        _mods_kernel,
        grid=(depth, n // tn),
        in_specs=[pl.BlockSpec((N_SEG, d), lambda l, j: (0, 0)),
                  pl.BlockSpec((None, d, tn), lambda l, j: (l, 0, j)),
                  pl.BlockSpec((None, 1, tn), lambda l, j: (l, 0, j))],
        out_specs=pl.BlockSpec((None, N_SEG, tn), lambda l, j: (l, 0, j)),
        out_shape=jax.ShapeDtypeStruct((depth, N_SEG, n), F32),
        compiler_params=_cparams(2, 40),
        name="adaln_mods",
    )(cond, ada_w, ada_b.reshape(depth, 1, n))


def _two_source_specs(tm, d, n_head_tiles):
    return (pl.BlockSpec((tm, d), lambda i: (jnp.minimum(i, n_head_tiles - 1), 0)),
            pl.BlockSpec((tm, d), lambda i: (jnp.maximum(i - n_head_tiles, 0), 0)))


def _pick_rows(head_ref, tail_ref, n_head_tiles):
    return jnp.where(pl.program_id(0) < n_head_tiles, head_ref[...], tail_ref[...])


def _inproj_kernel(xh_ref, xt_ref, g_ref, sh_ref, sc_ref, w_ref, cs_ref, a_ref, ab_ref, *, cc, gw, n_head):
    x = _pick_rows(xh_ref, xt_ref, n_head)
    h = _normmod(x, g_ref[...], sh_ref[...], sc_ref[...]).astype(BF16)
    u = _dot(h, w_ref[...])
    a_ref[...] = u[:, :cc] * _sigmoid(u[:, cc:2 * cc])
    f = u[:, 2 * cc:].astype(BF16)
    n_g = f.shape[1] // gw
    parts = [_dot(f[:, g * gw:(g + 1) * gw], cs_ref[...]) for g in range(n_g)]
    cos_part = [p[:, :gw] for p in parts]
    sin_part = [p[:, gw:] for p in parts]
    ab_ref[...] = jnp.concatenate(cos_part + sin_part, axis=-1).astype(BF16)


def _seg_map(tm, seq, n_batch):
    return lambda i: (jnp.minimum(i * tm // seq, n_batch), 0, 0)


def _inproj_call(x_head, x_tail, g, shift, scale, w_in, cs, *, cc, gw, tm, seq, n_batch):
    d = x_head.shape[1]
    rows = x_head.shape[0] + x_tail.shape[0]
    n_head = x_head.shape[0] // tm
    n = w_in.shape[1]
    fc = n - 2 * cc
    seg = _seg_map(tm, seq, n_batch)
    return pl.pallas_call(
        functools.partial(_inproj_kernel, cc=cc, gw=gw, n_head=n_head),
        grid=(rows // tm,),
        in_specs=[*_two_source_specs(tm, d, n_head),
                  pl.BlockSpec((1, d), lambda i: (0, 0)),
                  pl.BlockSpec((None, 1, d), seg),
                  pl.BlockSpec((None, 1, d), seg),
                  pl.BlockSpec((d, n), lambda i: (0, 0)),
                  pl.BlockSpec((gw, 2 * gw), lambda i: (0, 0))],
        out_specs=[pl.BlockSpec((tm, cc), lambda i: (i, 0)),
                   pl.BlockSpec((tm, 2 * fc), lambda i: (i, 0))],
        out_shape=[jax.ShapeDtypeStruct((rows, cc), F32),
                   jax.ShapeDtypeStruct((rows, 2 * fc), BF16)],
        compiler_params=_cparams(1, 40),
        name="inproj_glu_chdft",
    )(x_head, x_tail, g, shift, scale, w_in, cs)


def _conv_kernel(prev_ref, main_ref, next_ref, w_ref, cb_ref, lg_ref, lb_ref, gm_ref, o_ref,
                 buf, cv, shifted, *, tl, width, lat_tiles, lat_tps, ctx_tps, chunk):
    i = pl.program_id(0)
    is_lat = i < lat_tiles
    tps = jnp.where(is_lat, lat_tps, ctx_tps)
    j = jnp.where(is_lat, i, i - lat_tiles) % tps
    zero = jnp.zeros((HALO, buf.shape[1]), F32)
    buf[0:HALO, :] = jnp.where(j > 0, prev_ref[...], zero)
    buf[HALO:HALO + tl, :] = main_ref[...]
    buf[HALO + tl:HALO + tl + HALO, :] = jnp.where(j < tps - 1, next_ref[...], zero)
    base = HALO - width // 2
    span = shifted.shape[1]
    for s in range(1, SUBLANES):
        shifted[s] = buf[s:s + span, :]

    def window(off, r0):
        s, m = off % SUBLANES, off - off % SUBLANES
        if s == 0:
            return buf[m + r0:m + r0 + chunk, :]
        return shifted[s, m + r0:m + r0 + chunk, :]

    for r0 in range(0, tl, chunk):
        acc = window(base, r0) * w_ref[0:1, :]
        for t in range(1, width):
            acc = acc + window(base + t, r0) * w_ref[t:t + 1, :]
        cv[r0:r0 + chunk, :] = acc
    a = cv[...] + cb_ref[...]
    gm = gm_ref[...]
    a_hi, a_lo = _split_bf16(a)
    mu = _dot(a_hi, gm) + _dot(a_lo, gm)
    dl = a - mu
    q_hi, q_lo = _split_bf16(dl * dl)
    var = _dot(q_hi, gm) + _dot(q_lo, gm)
    y = dl * lax.rsqrt(var + EPS) * lg_ref[...] + lb_ref[...]
    o_ref[...] = (y * _sigmoid(y)).astype(BF16)


def _conv_call(a_glu, conv_w, conv_b, ln_g, ln_b, gmean, *, tl, seq, ctx_len, n_batch):
    rows, ch = a_glu.shape
    width = conv_w.shape[0]
    assert width // 2 <= HALO and tl % HALO == 0
    hb = tl // HALO
    last_halo = rows // HALO - 1
    return pl.pallas_call(
        functools.partial(_conv_kernel, tl=tl, width=width, lat_tiles=n_batch * seq // tl,
                          lat_tps=seq // tl, ctx_tps=ctx_len // tl, chunk=32),
        grid=(rows // tl,),
        in_specs=[pl.BlockSpec((HALO, ch), lambda i: (jnp.maximum(i * hb - 1, 0), 0)),
                  pl.BlockSpec((tl, ch), lambda i: (i, 0)),
                  pl.BlockSpec((HALO, ch), lambda i: (jnp.minimum((i + 1) * hb, last_halo), 0)),
                  pl.BlockSpec((width, ch), lambda i: (0, 0)),
                  pl.BlockSpec((1, ch), lambda i: (0, 0)),
                  pl.BlockSpec((1, ch), lambda i: (0, 0)),
                  pl.BlockSpec((1, ch), lambda i: (0, 0)),
                  pl.BlockSpec((ch, ch), lambda i: (0, 0))],
        out_specs=pl.BlockSpec((tl, ch), lambda i: (i, 0)),
        out_shape=jax.ShapeDtypeStruct((rows, ch), BF16),
        scratch_shapes=[pltpu.VMEM((tl + 2 * HALO, ch), F32), pltpu.VMEM((tl, ch), F32),
                        pltpu.VMEM((SUBLANES, tl + 2 * HALO - SUBLANES, ch), F32)],
        compiler_params=_cparams(1, 40),
        name="dwconv_groupln_swish",
    )(a_glu, a_glu, a_glu, conv_w, conv_b, ln_g, ln_b, gmean)


def _seqdft_kernel(c_ref, s_ref, a_ref, b_ref, o_ref):
    o_ref[...] = (_dot(c_ref[...], a_ref[...]) - _dot(s_ref[...], b_ref[...])).astype(BF16)


def _seqdft_call(ab, cmat, smat, *, length, row0, n_seq, fc, tm):
    seq0 = row0 // length
    return pl.pallas_call(
        _seqdft_kernel,
        grid=(n_seq, length // tm),
        in_specs=[pl.BlockSpec((tm, length), lambda b, i: (i, 0)),
                  pl.BlockSpec((tm, length), lambda b, i: (i, 0)),
                  pl.BlockSpec((length, fc), lambda b, i: (seq0 + b, 0)),
                  pl.BlockSpec((length, fc), lambda b, i: (seq0 + b, 1))],
        out_specs=pl.BlockSpec((tm, fc), lambda b, i: (b * (length // tm) + i, 0)),
        out_shape=jax.ShapeDtypeStruct((n_seq * length, fc), BF16),
        compiler_params=_cparams(2, 48),
        name="seq_dft",
    )(cmat, smat, ab, ab)


def _dft_tables(length, scale):
    if length <= 512:
        kn = np.outer(np.arange(length), np.arange(length)) % length
        ang = 2.0 * np.pi * kn / length
        return (jnp.asarray(np.cos(ang) * scale, BF16), jnp.asarray(np.sin(ang) * scale, BF16))
    r = 64
    assert length % r == 0
    k = np.arange(length)[:, None]
    alpha = 2.0 * np.pi * ((k * np.arange(length // r)[None, :] * r) % length) / length
    beta = 2.0 * np.pi * ((k * np.arange(r)[None, :]) % length) / length
    ca, sa = jnp.asarray(np.cos(alpha), F32)[:, :, None], jnp.asarray(np.sin(alpha), F32)[:, :, None]
    cb, sb = jnp.asarray(np.cos(beta) * scale, F32)[:, None, :], jnp.asarray(np.sin(beta) * scale, F32)[:, None, :]
    cmat = (ca * cb - sa * sb).reshape(length, length).astype(BF16)
    smat = (sa * cb + ca * sb).reshape(length, length).astype(BF16)
    return cmat, smat


def _outproj_kernel(*refs, n_in, n_head):
    ins, ws = refs[:n_in], refs[n_in:2 * n_in]
    (xh_ref, xt_ref, g1_ref, gate_ref, g2_ref, sh_ref, sc_ref, wh_ref, wl_ref, rb_ref,
     xo_ref, tok_ref, tokp_ref, e_ref, gt_ref, pos_ref, cnt_ref, carry) = refs[2 * n_in:]
    y = _dot(ins[0][...], ws[0][...])
    for a_ref, w_ref in zip(ins[1:], ws[1:]):
        y = y + _dot(a_ref[...], w_ref[...])
    x1 = _pick_rows(xh_ref, xt_ref, n_head) + gate_ref[...] * _rms(y, g1_ref[...])
    xo_ref[...] = x1
    tok = _normmod(x1, g2_ref[...], sh_ref[...], sc_ref[...])
    tok_ref[...] = tok.astype(BF16)
    _store_pieces(tokp_ref, _pack_halves(tok))
    _route(tok, wh_ref, wl_ref, rb_ref, e_ref, gt_ref, pos_ref, cnt_ref, carry)


def _outproj_call(ins, ws, x_head, x_tail, g1, gate, g2, shift, scale, router_w, router_b,
                  *, rows, tm, seq, n_batch):
    d = x_head.shape[1]
    n_head = min(x_head.shape[0], rows) // tm
    n_exp = router_w.shape[1]
    w_t = router_w.T
    w_hi = w_t.astype(BF16)
    w_lo = (w_t - w_hi.astype(F32)).astype(BF16)
    seg = _seg_map(tm, seq, n_batch)
    n_in = len(ins)
    row_spec = lambda a: pl.BlockSpec((tm, a.shape[1]), lambda i: (i, 0))
    full_spec = lambda a: pl.BlockSpec(a.shape, lambda i: (0, 0))
    lane_spec = pl.BlockSpec((SUBLANES, tm), lambda i: (0, i))
    tok_rows = lambda dt: jax.ShapeDtypeStruct((SUBLANES, rows), dt)
    return pl.pallas_call(
        functools.partial(_outproj_kernel, n_in=n_in, n_head=n_head),
        grid=(rows // tm,),
        in_specs=[row_spec(a) for a in ins] + [full_spec(w) for w in ws] + [
            *_two_source_specs(tm, d, n_head), full_spec(g1), pl.BlockSpec((None, 1, d), seg), full_spec(g2),
            pl.BlockSpec((None, 1, d), seg), pl.BlockSpec((None, 1, d), seg),
            full_spec(w_hi), full_spec(w_lo), pl.BlockSpec((n_exp, 1), lambda i: (0, 0))],
        out_specs=[pl.BlockSpec((tm, d), lambda i: (i, 0)), pl.BlockSpec((tm, d), lambda i: (i, 0)),
                   pl.BlockSpec((SC_SPLIT, tm, d // 2 // SC_SPLIT), lambda i: (0, i, 0)),
                   lane_spec, lane_spec, lane_spec, pl.BlockSpec((n_exp, LANES), lambda i: (0, 0))],
        out_shape=[jax.ShapeDtypeStruct((rows, d), F32), jax.ShapeDtypeStruct((rows, d), BF16),
                   jax.ShapeDtypeStruct((SC_SPLIT, rows, d // 2 // SC_SPLIT), jnp.int32),
                   tok_rows(jnp.int32), tok_rows(F32), tok_rows(jnp.int32),
                   jax.ShapeDtypeStruct((n_exp, LANES), F32)],
        scratch_shapes=[pltpu.VMEM((n_exp, 1), F32)],
        compiler_params=_cparams(1, 48),
        name="outproj_residual_route",
    )(*ins, *ws, x_head, x_tail, g1, gate, g2, shift, scale, w_hi, w_lo, router_b.reshape(n_exp, 1))


def _pick_max(cur, idx):
    mx = jnp.max(cur, axis=0, keepdims=True)
    first = jnp.min(jnp.where(cur == mx, idx, float(cur.shape[0])), axis=0, keepdims=True)
    return first, idx == first


def _route(tok, wh_ref, wl_ref, rb_ref, e_ref, gt_ref, pos_ref, cnt_ref, carry):
    @pl.when(pl.program_id(0) == 0)
    def _():
        carry[...] = jnp.zeros_like(carry)

    t_hi, t_lo = _split_bf16(tok)
    wh, wl = wh_ref[...], wl_ref[...]
    logits = _dot_nt(wh, t_hi) + _dot_nt(wh, t_lo) + _dot_nt(wl, t_hi)
    n_exp, tm = logits.shape
    scores = _sigmoid(logits)
    biased = scores + rb_ref[...]
    gsz = n_exp // N_GROUPS
    neg = -jnp.inf

    b3 = biased.reshape(N_GROUPS, gsz, tm)
    im = lax.broadcasted_iota(jnp.int32, b3.shape, 1).astype(F32)
    m1 = jnp.max(b3, axis=1, keepdims=True)
    i1 = jnp.min(jnp.where(b3 == m1, im, float(gsz)), axis=1, keepdims=True)
    m2 = jnp.max(jnp.where(im == i1, neg, b3), axis=1, keepdims=True)
    gscore = (m1 + m2).reshape(N_GROUPS, tm)

    ig = lax.broadcasted_iota(jnp.int32, gscore.shape, 0).astype(F32)
    gsel = jnp.zeros_like(gscore)
    cur = gscore
    for _ in range(TOPK_GROUPS):
        _, hit = _pick_max(cur, ig)
        gsel = jnp.where(hit, 1.0, gsel)
        cur = jnp.where(hit, neg, cur)
    gsel3 = jnp.broadcast_to(gsel.reshape(N_GROUPS, 1, tm), b3.shape)
    cur = jnp.where(gsel3 > 0.0, b3, neg).reshape(n_exp, tm)

    ie = lax.broadcasted_iota(jnp.int32, (n_exp, tm), 0).astype(F32)
    sel = jnp.zeros((n_exp, tm), F32)
    picks, raw = [], []
    for _ in range(TOP_K):
        first, hit = _pick_max(cur, ie)
        picks.append(first)
        raw.append(jnp.sum(jnp.where(hit, scores, 0.0), axis=0, keepdims=True))
        sel = jnp.where(hit, 1.0, sel)
        cur = jnp.where(hit, neg, cur)
    total = raw[0]
    for r in raw[1:]:
        total = total + r

    ri = lax.broadcasted_iota(jnp.int32, (tm, tm), 0)
    ci = lax.broadcasted_iota(jnp.int32, (tm, tm), 1)
    upper = jnp.where(ri < ci, 1.0, 0.0).astype(BF16)
    rank = _dot(sel.astype(BF16), upper) + carry[...]
    ranks = [jnp.sum(jnp.where(ie == p, rank, 0.0), axis=0, keepdims=True) for p in picks]
    carry[...] = carry[...] + jnp.sum(sel, axis=1, keepdims=True)

    pad = jnp.zeros((SUBLANES - TOP_K, tm), F32)
    e_ref[...] = jnp.concatenate(picks + [pad], axis=0).astype(jnp.int32)
    gt_ref[...] = jnp.concatenate([r / total * ROUTED_SCALE for r in raw] + [pad], axis=0)
    pos_ref[...] = jnp.concatenate(ranks + [pad], axis=0).astype(jnp.int32)
    cnt_ref[...] = jnp.broadcast_to(carry[...], cnt_ref.shape)


def _expert_kernel(be_ref, nv_ref, nu_ref, xs_hbm, wg_ref, wu_ref, wd_ref, ys_ref, wgu, wdn, xbuf, sem,
                   *, ff, n_sub):
    b = pl.program_id(0)
    n_used = nu_ref[0]
    depth, _, te, q = xbuf.shape

    def row_block_copy(blk):
        slot = blk % depth
        return pltpu.make_async_copy(xs_hbm.at[:, pl.ds(blk * te, te), :], xbuf.at[slot], sem.at[slot])

    @pl.when(b == 0)
    def _():
        for j in range(depth - 1):
            @pl.when(j < n_used)
            def _():
                row_block_copy(j).start()

    @pl.when(b + depth - 1 < n_used)
    def _():
        row_block_copy(b + depth - 1).start()

    changed = jnp.logical_or(b == 0, be_ref[b] != be_ref[jnp.maximum(b - 1, 0)])

    @pl.when(changed)
    def _():
        wgu[:, :ff] = wg_ref[...].astype(BF16)
        wgu[:, ff:] = wu_ref[...].astype(BF16)
        wdn[...] = wd_ref[...].astype(BF16)

    @pl.when(b < n_used)
    def _():
        row_block_copy(b).wait()
        slot = b % depth
        n_valid = nv_ref[b]
        ts = te // n_sub
        gus = []
        for a in range(n_sub):
            r0 = a * ts
            xw = jnp.concatenate([xbuf[slot, j, r0:r0 + ts, :] for j in range(SC_SPLIT)], axis=-1)
            row = lax.broadcasted_iota(jnp.int32, xw.shape, 0) + r0
            lo, hi = _unpack_halves(jnp.where(row < n_valid, xw, 0))
            half = lo.shape[1]
            gus.append(_dot(lo.astype(BF16), wgu[:half, :]) + _dot(hi.astype(BF16), wgu[half:, :]))
        outs = []
        for gu in gus:
            gate = gu[:, :ff]
            hmid = (gate * _sigmoid(gate) * gu[:, ff:]).astype(BF16)
            outs.append(_dot(hmid, wdn[...]))
        for a, y in enumerate(outs):
            words = _pack_halves(y)
            for j in range(SC_SPLIT):
                ys_ref[j, a * ts:(a + 1) * ts, :] = words[:, j * q:(j + 1) * q]


def _expert_call(block_e, n_valid, n_used, xs, layer, w_gate, w_up, w_down):
    _, n_rows, q = xs.shape
    d = 2 * SC_SPLIT * q
    ff = w_gate.shape[3]
    te = EXPERT_ROWS
    row_map = lambda b, be, nv, nu: (0, jnp.minimum(b, nu[0] - 1), 0)
    w_map = lambda b, be, nv, nu: (layer, be[b], 0, 0)
    grid_spec = pltpu.PrefetchScalarGridSpec(
        num_scalar_prefetch=3,
        grid=(n_rows // te,),
        in_specs=[pl.BlockSpec(memory_space=pl.ANY),
                  pl.BlockSpec((None, None, d, ff), w_map),
                  pl.BlockSpec((None, None, d, ff), w_map),
                  pl.BlockSpec((None, None, ff, d), w_map)],
        out_specs=pl.BlockSpec((SC_SPLIT, te, q), row_map),
        scratch_shapes=[pltpu.VMEM((d, 2 * ff), BF16), pltpu.VMEM((ff, d), BF16),
                        pltpu.VMEM((EXPERT_RING, SC_SPLIT, te, q), jnp.int32),
                        pltpu.SemaphoreType.DMA((EXPERT_RING,))])
    return pl.pallas_call(
        functools.partial(_expert_kernel, ff=ff, n_sub=EXPERT_ROWS // EXPERT_SUB_ROWS),
        grid_spec=grid_spec,
        out_shape=jax.ShapeDtypeStruct(xs.shape, jnp.int32),
        compiler_params=_cparams(1, 40),
        name="moe_experts",
    )(block_e, n_valid, n_used, xs, w_gate, w_up, w_down)


SC_WINDOW = 128


def _sc_mesh():
    return plsc.VectorSubcoreMesh(core_axis_name="core", subcore_axis_name="subcore")


def _sc_scatter_rows(rows, dests, n_out):
    n, width = rows.shape
    n_k = len(dests)
    assert n % SC_WINDOW == 0

    @functools.partial(pl.kernel, out_type=jax.ShapeDtypeStruct((n_out, width), rows.dtype),
                       mesh=_sc_mesh(), name="sc_dispatch_rows")
    def scatter_kernel(x_hbm, *refs):
        idx_hbm, o_hbm = refs[:n_k], refs[n_k]

        def body(x_vmem, *idx_vmem):
            for iv in idx_vmem:
                pltpu.sync_copy(x_vmem, o_hbm.at[iv.at[0]])

        pltpu.emit_pipeline(
            body,
            grid=(n // SC_WINDOW,),
            in_specs=[pl.BlockSpec((SC_WINDOW, width), lambda i: (i, 0))]
            + [pl.BlockSpec((1, SC_WINDOW), lambda i: (0, i))] * n_k,
            out_specs=[],
            core_axis_name=("core", "subcore"),
            dimension_semantics=(pltpu.PARALLEL,),
        )(x_hbm, *idx_hbm)

    return scatter_kernel(rows, *dests)


def _sc_gather_rows(table, idx):
    n = idx.shape[1]
    width = table.shape[1]
    assert n % SC_WINDOW == 0

    @functools.partial(pl.kernel, out_type=jax.ShapeDtypeStruct((n, width), table.dtype),
                       mesh=_sc_mesh(), name="sc_collect_rows")
    def gather_kernel(t_hbm, i_hbm, o_hbm):
        def body(i_vmem, o_vmem):
            pltpu.sync_copy(t_hbm.at[i_vmem.at[0]], o_vmem)

        pltpu.emit_pipeline(
            body,
            grid=(n // SC_WINDOW,),
            in_specs=[pl.BlockSpec((1, SC_WINDOW), lambda i: (0, i))],
            out_specs=[pl.BlockSpec((SC_WINDOW, width), lambda i: (i, 0))],
            core_axis_name=("core", "subcore"),
            dimension_semantics=(pltpu.PARALLEL,),
        )(i_hbm, o_hbm)

    return gather_kernel(table, idx)


def _shared_kernel(tok_ref, sgu_ref, sd_ref, o_ref, *, ff):
    gu = _dot(tok_ref[...], sgu_ref[...])
    gate = gu[:, :ff]
    hmid = (gate * _sigmoid(gate) * gu[:, ff:]).astype(BF16)
    o_ref[...] = _dot(hmid, sd_ref[...]).astype(BF16)


def _shared_call(tok, sgu, sd, *, rows, tm):
    d = tok.shape[1]
    ff = sd.shape[0]
    return pl.pallas_call(
        functools.partial(_shared_kernel, ff=ff),
        grid=(rows // tm,),
        in_specs=[pl.BlockSpec((tm, d), lambda i: (i, 0)),
                  pl.BlockSpec((d, 2 * ff), lambda i: (0, 0)),
                  pl.BlockSpec((ff, d), lambda i: (0, 0))],
        out_specs=pl.BlockSpec((tm, d), lambda i: (i, 0)),
        out_shape=jax.ShapeDtypeStruct((rows, d), BF16),
        compiler_params=_cparams(1, 40),
        name="moe_shared_expert",
    )(tok, sgu, sd)


def _moe_out_kernel(shared_ref, yg_ref, gk_ref, x_ref, g_ref, gate_ref, xo_ref):
    shared = shared_ref[...].astype(F32)
    gk = gk_ref[...]
    lo_acc, hi_acc = None, None
    for k in range(TOP_K):
        lo, hi = _unpack_halves(jnp.concatenate([yg_ref[j, k] for j in range(SC_SPLIT)], axis=-1))
        w = gk[:, k:k + 1]
        lo_acc = lo * w if lo_acc is None else lo_acc + lo * w
        hi_acc = hi * w if hi_acc is None else hi_acc + hi * w
    y = jnp.concatenate([lo_acc, hi_acc], axis=-1) + shared
    xo_ref[...] = x_ref[...] + gate_ref[...] * _rms(y, g_ref[...])


def _moe_out_call(shared, yg, gk, xs, g, gate, *, rows, tm, seq, n_batch):
    d = xs.shape[1]
    seg = _seg_map(tm, seq, n_batch)
    row = lambda: pl.BlockSpec((tm, d), lambda i: (i, 0))
    return pl.pallas_call(
        _moe_out_kernel,
        grid=(rows // tm,),
        in_specs=[row(),
                  pl.BlockSpec((SC_SPLIT, TOP_K, tm, d // 2 // SC_SPLIT), lambda i: (0, 0, i, 0)),
                  pl.BlockSpec((tm, SUBLANES), lambda i: (i, 0)),
                  row(),
                  pl.BlockSpec((1, d), lambda i: (0, 0)),
                  pl.BlockSpec((None, 1, d), seg)],
        out_specs=row(),
        out_shape=jax.ShapeDtypeStruct((rows, d), F32),
        compiler_params=_cparams(1, 48),
        name="moe_combine",
    )(shared, yg, gk, xs, g, gate)


def _dest_kernel(e_ref, pos_ref, ps_ref, o_ref, *, n_rows):
    e = e_ref[...].astype(F32)
    ps = ps_ref[...]
    n_exp, tm = ps.shape[0], e.shape[1]
    ids = lax.broadcasted_iota(jnp.int32, (n_exp, tm), 0).astype(F32)
    rows = [jnp.sum(jnp.where(ids == e[k:k + 1, :], ps, 0.0), axis=0, keepdims=True) for k in range(SUBLANES)]
    dest = jnp.concatenate(rows, axis=0).astype(jnp.int32) + pos_ref[...]
    for j in range(SC_SPLIT):
        o_ref[j] = dest + j * n_rows


def _dest_call(e_tk, pos_tk, pad_starts, *, n_rows, tm):
    rows = e_tk.shape[1]
    n_exp = pad_starts.shape[0]
    lane_spec = pl.BlockSpec((SUBLANES, tm), lambda i: (0, i))
    return pl.pallas_call(
        functools.partial(_dest_kernel, n_rows=n_rows),
        grid=(rows // tm,),
        in_specs=[lane_spec, lane_spec, pl.BlockSpec((n_exp, 1), lambda i: (0, 0))],
        out_specs=pl.BlockSpec((SC_SPLIT, SUBLANES, tm), lambda i: (0, 0, i)),
        out_shape=jax.ShapeDtypeStruct((SC_SPLIT, SUBLANES, rows), jnp.int32),
        compiler_params=_cparams(1, 32),
        name="moe_dest_rows",
    )(e_tk, pos_tk, pad_starts.astype(F32).reshape(n_exp, 1))


def _moe_layer(xs, tok, tokp, routing, g_out, gate, layer, w_gate, w_up, w_down, sh_gate, sh_up, sh_down,
               *, rows, tm, seq, n_batch):
    e_tk, g_tk, pos_tk, counts = routing
    n_exp = counts.shape[0]
    te = EXPERT_ROWS
    cnt = counts[:, 0].astype(jnp.int32)
    padded = (cnt + te - 1) // te * te
    pad_ends = jnp.cumsum(padded)
    pad_starts = pad_ends - padded
    expert_ids = jnp.arange(n_exp, dtype=jnp.int32)
    n_blocks = -(-(rows * TOP_K + n_exp * (te - 1)) // te)
    n_used = pad_ends[-1] // te
    blk = jnp.arange(n_blocks, dtype=jnp.int32)
    last = jnp.minimum(blk, n_used - 1) * te
    owner = (last[:, None] >= pad_ends[None, :]).astype(jnp.int32)
    block_e = jnp.minimum(jnp.sum(owner, axis=1), n_exp - 1)
    own_hot = block_e[:, None] == expert_ids[None, :]
    n_valid = jnp.sum(jnp.where(own_hot, (pad_starts + cnt)[None, :], 0), axis=1) - last
    n_valid = jnp.clip(n_valid, 0, te)
    q = tokp.shape[2]
    n_rows = n_blocks * te
    piece = _dest_call(e_tk, pos_tk, pad_starts, n_rows=n_rows, tm=tm)[:, :TOP_K, :]
    xg = _sc_scatter_rows(tokp.reshape(SC_SPLIT * rows, q),
                          [piece[:, k, :].reshape(1, SC_SPLIT * rows) for k in range(TOP_K)],
                          SC_SPLIT * n_rows).reshape(SC_SPLIT, n_rows, q)
    sgu = jnp.concatenate([sh_gate, sh_up], axis=1).astype(BF16)
    shared = _shared_call(tok, sgu, sh_down.astype(BF16), rows=rows, tm=tm)
    ys = _expert_call(block_e, n_valid, n_used.reshape(1).astype(jnp.int32), xg, layer, w_gate, w_up, w_down)
    yg = _sc_gather_rows(ys.reshape(SC_SPLIT * n_rows, q),
                         piece.reshape(1, -1)).reshape(SC_SPLIT, TOP_K, rows, q)
    return _moe_out_call(shared, yg, g_tk.T, xs, g_out, gate, rows=rows, tm=tm, seq=seq, n_batch=n_batch)


def _rope(t, cos, sin):
    half = DA_QK_DIM // 2
    up = pltpu.roll(t, shift=LANES - half, axis=1)
    dn = pltpu.roll(t, shift=half, axis=1)
    lane = lax.broadcasted_iota(jnp.int32, t.shape, 1) % DA_QK_DIM
    return t * cos + jnp.where(lane < half, -up, dn) * sin


def _qkv_kernel(x_ref, g_ref, sh_ref, sc_ref, w_ref, cos_ref, sin_ref, q_ref, k_ref, v_ref, *, qkw):
    h = _normmod(x_ref[...], g_ref[...], sh_ref[...], sc_ref[...]).astype(BF16)
    qkv = _dot(h, w_ref[...])
    cos, sin = cos_ref[...], sin_ref[...]
    q_scale = DA_QK_DIM ** -0.5 * math.log2(math.e)
    for hb in range(qkw // LANES):
        lo, hi = hb * LANES, (hb + 1) * LANES
        q_ref[:, lo:hi] = (_rope(qkv[:, lo:hi], cos, sin) * q_scale).astype(BF16)
        k_ref[:, lo:hi] = _rope(qkv[:, qkw + lo:qkw + hi], cos, sin).astype(BF16)
    v_ref[...] = qkv[:, 2 * qkw:].astype(BF16)


def _qkv_call(xs, g, shift, scale, w_qkv, cos_t, sin_t, *, qkw, tm, seq, ctx_len, n_batch):
    rows, d = xs.shape
    n = w_qkv.shape[1]
    vw = n - 2 * qkw
    lat_tiles, lat_tps, ctx_tps = n_batch * seq // tm, seq // tm, ctx_len // tm
    seg = _seg_map(tm, seq, n_batch)

    def kv_map(i):
        c = i - lat_tiles
        is_lat = i < lat_tiles
        return (jnp.where(is_lat, i // lat_tps, c // ctx_tps),
                jnp.where(is_lat, ctx_tps + i % lat_tps, c % ctx_tps), 0)

    rope_map = lambda i: (jnp.where(i < lat_tiles, i % lat_tps, lat_tps), 0)
    return pl.pallas_call(
        functools.partial(_qkv_kernel, qkw=qkw),
        grid=(rows // tm,),
        in_specs=[pl.BlockSpec((tm, d), lambda i: (i, 0)),
                  pl.BlockSpec((1, d), lambda i: (0, 0)),
                  pl.BlockSpec((None, 1, d), seg),
                  pl.BlockSpec((None, 1, d), seg),
                  pl.BlockSpec((d, n), lambda i: (0, 0)),
                  pl.BlockSpec((tm, LANES), rope_map),
                  pl.BlockSpec((tm, LANES), rope_map)],
        out_specs=[pl.BlockSpec((tm, qkw), lambda i: (i, 0)),
                   pl.BlockSpec((None, tm, qkw), kv_map),
                   pl.BlockSpec((None, tm, vw), kv_map)],
        out_shape=[jax.ShapeDtypeStruct((rows, qkw), BF16),
                   jax.ShapeDtypeStruct((n_batch, ctx_len + seq, qkw), BF16),
                   jax.ShapeDtypeStruct((n_batch, ctx_len + seq, vw), BF16)],
        compiler_params=_cparams(1, 48),
        name="qkv_rope",
    )(xs, g, shift, scale, w_qkv, cos_t, sin_t)


def _rope_tables(seq, tm):
    pos = np.arange(seq)
    n_freq = DA_QK_DIM // 4
    inv = np.power(ROPE_BASE, -np.arange(n_freq, dtype=np.float32) / n_freq).astype(np.float32)
    row = (pos // GRID_W).astype(np.float32)[:, None] * inv
    col = (pos % GRID_W).astype(np.float32)[:, None] * inv
    ang = np.concatenate([row, col], axis=-1).astype(np.float32)
    ang = np.tile(ang, (1, LANES // ang.shape[1]))
    cos = np.concatenate([np.cos(ang), np.ones((tm, LANES))], axis=0)
    sin = np.concatenate([np.sin(ang), np.zeros((tm, LANES))], axis=0)
    return jnp.asarray(cos, F32), jnp.asarray(sin, F32)


def _attn_kernel(lp_ref, q_ref, k_ref, v_ref, sg_ref, o_ref, vext, *, n_sub, lam_init):
    @pl.when(pl.program_id(2) == 0)
    def _():
        vext[:, :LANES] = v_ref[...]
        vext[:, LANES:] = jnp.ones((vext.shape[0], LANES), BF16)

    lp = lp_ref[...]
    lam = (jnp.exp(jnp.sum(lp[0:1] * lp[1:2], axis=1, keepdims=True))
           - jnp.exp(jnp.sum(lp[2:3] * lp[3:4], axis=1, keepdims=True)) + lam_init)
    ts = q_ref.shape[0] // n_sub
    lane = lax.broadcasted_iota(jnp.int32, (ts, LANES), 1)
    zero = jnp.zeros((ts, LANES), BF16)
    k = k_ref[...]
    scores = []
    for a in range(n_sub):
        q = q_ref[a * ts:(a + 1) * ts, :]
        qq = jnp.concatenate([jnp.where(lane < DA_QK_DIM, q, zero),
                              jnp.where(lane >= DA_QK_DIM, q, zero)], axis=0)
        scores.append(_dot_nt(qq, k))
    for a in range(n_sub):
        s = scores[a]
        p = jnp.exp2(s - jnp.max(s, axis=-1, keepdims=True)).astype(BF16)
        oe = _dot(p, vext[...])
        on = oe[:, :LANES] / oe[:, LANES:LANES + 1]
        o = on[:ts] - lam * on[ts:]
        o_ref[a * ts:(a + 1) * ts, :] = (_rms(o, sg_ref[...]) * (1.0 - lam_init)).astype(BF16)


def _attn_call(lam_p, q, k_all, v_all, subln_g, *, tq, seq, n_batch, lam_init):
    n_heads = q.shape[1] // LANES
    lk = k_all.shape[1]
    qt = seq // tq
    return pl.pallas_call(
        functools.partial(_attn_kernel, n_sub=tq // ATTN_SUB_ROWS, lam_init=lam_init),
        grid=(n_batch, n_heads, qt),
        in_specs=[pl.BlockSpec(lam_p.shape, lambda b, h, i: (0, 0)),
                  pl.BlockSpec((tq, LANES), lambda b, h, i: (b * qt + i, h)),
                  pl.BlockSpec((None, lk, LANES), lambda b, h, i: (b, 0, h)),
                  pl.BlockSpec((None, lk, LANES), lambda b, h, i: (b, 0, h)),
                  pl.BlockSpec((1, LANES), lambda b, h, i: (0, 0))],
        out_specs=pl.BlockSpec((tq, LANES), lambda b, h, i: (b * qt + i, h)),
        out_shape=jax.ShapeDtypeStruct((n_batch * seq, n_heads * LANES), BF16),
        scratch_shapes=[pltpu.VMEM((lk, 2 * LANES), BF16)],
        compiler_params=_cparams(3, 48),
        name="diff_attention",
    )(lam_p, q, k_all, v_all, subln_g)


def kernel(x, c, ctx, c_ctx, ada_w, ada_b, norm_g, cf_w_in, cf_conv_w, cf_conv_b, cf_ln_g, cf_ln_b,
           cf_w_out, da_w_qkv, da_lambda, da_subln_g, da_w_out, moe_router_w, moe_router_b,
           moe_w_gate, moe_w_up, moe_w_down, moe_sh_gate, moe_sh_up, moe_sh_down):
    n_batch, seq, d = x.shape
    ctx_len = ctx.shape[1]
    depth = ada_w.shape[0]
    assert depth == 2 and n_batch + 1 <= N_SEG
    assert DA_QK_DIM * 2 == LANES and da_subln_g.shape[1] == LANES
    rows_lat, rows_ctx = n_batch * seq, n_batch * ctx_len
    tm = 512
    tq = 256
    assert seq % tm == 0 and rows_ctx % tm == 0 and seq % tq == 0 and ctx_len % tq == 0

    x_lat, x_ctx = x.reshape(rows_lat, d), ctx.reshape(rows_ctx, d)
    cond = jnp.concatenate([c, c_ctx[None, :], jnp.zeros((N_SEG - n_batch - 1, d), F32)], axis=0)
    mods = _mods_call(cond, ada_w, ada_b)
    mod = lambda layer, k: mods[layer, :, k * d:(k + 1) * d][:, None, :]
    gain = lambda layer, k: norm_g[layer, k][None, :]
    common = dict(seq=seq, n_batch=n_batch)

    cc = cf_conv_w.shape[-1]
    fc = cf_w_in.shape[2] - 2 * cc
    gw = fc // FOURIER_GROUPS
    ch_ang = 2.0 * np.pi * (np.outer(np.arange(gw), np.arange(gw)) % gw) / gw
    cs = jnp.asarray(np.concatenate([np.cos(ch_ang), np.sin(ch_ang)], axis=1) / math.sqrt(gw), BF16)
    a_glu, ab = _inproj_call(x_lat, x_ctx, gain(0, 0), mod(0, 0), mod(0, 1), cf_w_in[0].astype(BF16), cs,
                             cc=cc, gw=gw, tm=tm, **common)
    gsz = cc // CONV_GROUPS
    gid = np.arange(cc) // gsz
    gmean = jnp.asarray((gid[:, None] == gid[None, :]) / gsz, BF16)
    a_act = _conv_call(a_glu, cf_conv_w[0][:, 0, :], cf_conv_b[0][None, :], cf_ln_g[0][None, :],
                       cf_ln_b[0][None, :], gmean, tl=tq, seq=seq, ctx_len=ctx_len, n_batch=n_batch)
    c_lat, s_lat = _dft_tables(seq, 1.0 / math.sqrt(seq))
    c_ctx_m, s_ctx_m = _dft_tables(ctx_len, 1.0 / math.sqrt(ctx_len))
    fr = jnp.concatenate([
        _seqdft_call(ab, c_lat, s_lat, length=seq, row0=0, n_seq=n_batch, fc=fc, tm=min(seq, 256)),
        _seqdft_call(ab, c_ctx_m, s_ctx_m, length=ctx_len, row0=rows_lat, n_seq=n_batch, fc=fc,
                     tm=min(ctx_len, 256))], axis=0)
    w_out = cf_w_out[0].astype(BF16)
    xs, tok, tokp, *routing = _outproj_call(
        [a_act, fr], [w_out[:cc], w_out[cc:]], x_lat, x_ctx, gain(0, 1), mod(0, 2), gain(0, 2), mod(0, 3),
        mod(0, 4), moe_router_w[0], moe_router_b[0], rows=rows_lat + rows_ctx, tm=tm, **common)
    xs = _moe_layer(xs, tok, tokp, routing, gain(0, 3), mod(0, 5), 0, moe_w_gate, moe_w_up, moe_w_down,
                    moe_sh_gate[0], moe_sh_up[0], moe_sh_down[0], rows=rows_lat + rows_ctx, tm=tm, **common)

    qkw = DA_HEADS * 2 * DA_QK_DIM
    lam_init = 0.8 - 0.6 * math.exp(-0.3 * 1)
    cos_t, sin_t = _rope_tables(seq, tq)
    q, k_all, v_all = _qkv_call(xs, gain(1, 0), mod(1, 0), mod(1, 1), da_w_qkv[0].astype(BF16),
                                cos_t, sin_t, qkw=qkw, tm=tq, ctx_len=ctx_len, **common)
    o = _attn_call(da_lambda[0], q, k_all, v_all, da_subln_g[0][None, :], tq=tm, seq=seq,
                   n_batch=n_batch, lam_init=lam_init)
    xs, tok, tokp, *routing = _outproj_call(
        [o], [da_w_out[0].astype(BF16)], xs, xs, gain(1, 1), mod(1, 2), gain(1, 2), mod(1, 3), mod(1, 4),
        moe_router_w[1], moe_router_b[1], rows=rows_lat, tm=tm, **common)
    xs = _moe_layer(xs, tok, tokp, routing, gain(1, 3), mod(1, 5), 1, moe_w_gate, moe_w_up, moe_w_down,
                    moe_sh_gate[1], moe_sh_up[1], moe_sh_down[1], rows=rows_lat, tm=tm, **common)
    return xs.reshape(n_batch, seq, d)
```

```python
import functools
import math

import numpy as np
import jax
import jax.numpy as jnp
from jax import lax
from jax.experimental import pallas as pl
from jax.experimental.pallas import tpu as pltpu
from jax.experimental.pallas import tpu_sc as plsc

F32 = jnp.float32
BF16 = jnp.bfloat16

EPS = 1e-6
GRID_W = 64
CONV_GROUPS = 8
FOURIER_GROUPS = 4
DA_HEADS = 8
DA_QK_DIM = 64
ROPE_BASE = 10000.0
N_GROUPS = 8
TOPK_GROUPS = 4
TOP_K = 6
ROUTED_SCALE = 2.5

LANES = 128
SUBLANES = 8
N_SEG = 16
HALO = 16
EXPERT_ROWS = 1024
EXPERT_SUB_ROWS = 128
EXPERT_RING = 3
COMBINE_RANGES = 2
ATTN_SUB_ROWS = 128
MIB = 1024 * 1024


def _cparams(n_axes, vmem_mib):
    return pltpu.CompilerParams(dimension_semantics=("arbitrary",) * n_axes,
                                vmem_limit_bytes=vmem_mib * MIB)


def _sigmoid(v):
    return 1.0 / (1.0 + jnp.exp(-v))


def _rms(v, g):
    return v * lax.rsqrt(jnp.mean(v * v, axis=-1, keepdims=True) + EPS) * g


def _normmod(v, g, shift, scale):
    return _rms(v, g) * (1.0 + scale) + shift


def _split_bf16(v):
    hi = v.astype(BF16)
    lo = (v - hi.astype(F32)).astype(BF16)
    return hi, lo


def _pack_halves(v):
    half = v.shape[1] // 2
    word = pltpu.pack_elementwise([v[:, :half], v[:, half:]], packed_dtype=BF16)
    return lax.bitcast_convert_type(word, jnp.int32)


def _unpack_halves(w):
    u = lax.bitcast_convert_type(w, jnp.uint32)
    lo = lax.bitcast_convert_type(u << 16, F32)
    hi = lax.bitcast_convert_type(u & jnp.uint32(0xFFFF0000), F32)
    return lo, hi


SC_SPLIT = 2


def _store_pieces(ref, words):
    q = words.shape[1] // SC_SPLIT
    for j in range(SC_SPLIT):
        ref[j] = words[:, j * q:(j + 1) * q]


def _load_pieces(ref):
    return jnp.concatenate([ref[j] for j in range(SC_SPLIT)], axis=-1)


def _dot(a, b):
    return jnp.dot(a, b, preferred_element_type=F32)


def _dot_nt(a, b):
    return lax.dot_general(a, b, (((1,), (1,)), ((), ())), preferred_element_type=F32)


def _mods_kernel(c_ref, w_ref, b_ref, o_ref):
    cv = c_ref[...]
    o_ref[...] = _dot(cv * _sigmoid(cv), w_ref[...]) + b_ref[...]


def _mods_call(cond, ada_w, ada_b):
    depth, d, n = ada_w.shape
    tn = n // 4
    return pl.pallas_call(
        _mods_kernel,
        grid=(depth, n // tn),
        in_specs=[pl.BlockSpec((N_SEG, d), lambda l, j: (0, 0)),
                  pl.BlockSpec((None, d, tn), lambda l, j: (l, 0, j)),
                  pl.BlockSpec((None, 1, tn), lambda l, j: (l, 0, j))],
        out_specs=pl.BlockSpec((None, N_SEG, tn), lambda l, j: (l, 0, j)),
        out_shape=jax.ShapeDtypeStruct((depth, N_SEG, n), F32),
        compiler_params=_cparams(2, 40),
        name="adaln_mods",
    )(cond, ada_w, ada_b.reshape(depth, 1, n))


def _two_source_specs(tm, d, n_head_tiles):
    return (pl.BlockSpec((tm, d), lambda i: (jnp.minimum(i, n_head_tiles - 1), 0)),
            pl.BlockSpec((tm, d), lambda i: (jnp.maximum(i - n_head_tiles, 0), 0)))


def _pick_rows(head_ref, tail_ref, n_head_tiles):
    return jnp.where(pl.program_id(0) < n_head_tiles, head_ref[...], tail_ref[...])


def _inproj_kernel(xh_ref, xt_ref, g_ref, sh_ref, sc_ref, w_ref, cs_ref, a_ref, ab_ref, *, cc, gw, n_head):
    x = _pick_rows(xh_ref, xt_ref, n_head)
    h = _normmod(x, g_ref[...], sh_ref[...], sc_ref[...]).astype(BF16)
    u = _dot(h, w_ref[...])
    a_ref[...] = u[:, :cc] * _sigmoid(u[:, cc:2 * cc])
    f = u[:, 2 * cc:].astype(BF16)
    n_g = f.shape[1] // gw
    parts = [_dot(f[:, g * gw:(g + 1) * gw], cs_ref[...]) for g in range(n_g)]
    cos_part = [p[:, :gw] for p in parts]
    sin_part = [p[:, gw:] for p in parts]
    ab_ref[...] = jnp.concatenate(cos_part + sin_part, axis=-1).astype(BF16)


def _seg_map(tm, seq, n_batch):
    return lambda i: (jnp.minimum(i * tm // seq, n_batch), 0, 0)


def _inproj_call(x_head, x_tail, g, shift, scale, w_in, cs, *, cc, gw, tm, seq, n_batch):
    d = x_head.shape[1]
    rows = x_head.shape[0] + x_tail.shape[0]
    n_head = x_head.shape[0] // tm
    n = w_in.shape[1]
    fc = n - 2 * cc
    seg = _seg_map(tm, seq, n_batch)
    return pl.pallas_call(
        functools.partial(_inproj_kernel, cc=cc, gw=gw, n_head=n_head),
        grid=(rows // tm,),
        in_specs=[*_two_source_specs(tm, d, n_head),
                  pl.BlockSpec((1, d), lambda i: (0, 0)),
                  pl.BlockSpec((None, 1, d), seg),
                  pl.BlockSpec((None, 1, d), seg),
                  pl.BlockSpec((d, n), lambda i: (0, 0)),
                  pl.BlockSpec((gw, 2 * gw), lambda i: (0, 0))],
        out_specs=[pl.BlockSpec((tm, cc), lambda i: (i, 0)),
                   pl.BlockSpec((tm, 2 * fc), lambda i: (i, 0))],
        out_shape=[jax.ShapeDtypeStruct((rows, cc), F32),
                   jax.ShapeDtypeStruct((rows, 2 * fc), BF16)],
        compiler_params=_cparams(1, 40),
        name="inproj_glu_chdft",
    )(x_head, x_tail, g, shift, scale, w_in, cs)


def _conv_kernel(prev_ref, main_ref, next_ref, w_ref, cb_ref, lg_ref, lb_ref, gm_ref, o_ref,
                 buf, cv, shifted, *, tl, width, lat_tiles, lat_tps, ctx_tps, chunk):
    i = pl.program_id(0)
    is_lat = i < lat_tiles
    tps = jnp.where(is_lat, lat_tps, ctx_tps)
    j = jnp.where(is_lat, i, i - lat_tiles) % tps
    zero = jnp.zeros((HALO, buf.shape[1]), F32)
    buf[0:HALO, :] = jnp.where(j > 0, prev_ref[...], zero)
    buf[HALO:HALO + tl, :] = main_ref[...]
    buf[HALO + tl:HALO + tl + HALO, :] = jnp.where(j < tps - 1, next_ref[...], zero)
    base = HALO - width // 2
    span = shifted.shape[1]
    for s in range(1, SUBLANES):
        shifted[s] = buf[s:s + span, :]

    def window(off, r0):
        s, m = off % SUBLANES, off - off % SUBLANES
        if s == 0:
            return buf[m + r0:m + r0 + chunk, :]
        return shifted[s, m + r0:m + r0 + chunk, :]

    for r0 in range(0, tl, chunk):
        acc = window(base, r0) * w_ref[0:1, :]
        for t in range(1, width):
            acc = acc + window(base + t, r0) * w_ref[t:t + 1, :]
        cv[r0:r0 + chunk, :] = acc
    a = cv[...] + cb_ref[...]
    gm = gm_ref[...]
    a_hi, a_lo = _split_bf16(a)
    mu = _dot(a_hi, gm) + _dot(a_lo, gm)
    dl = a - mu
    q_hi, q_lo = _split_bf16(dl * dl)
    var = _dot(q_hi, gm) + _dot(q_lo, gm)
    y = dl * lax.rsqrt(var + EPS) * lg_ref[...] + lb_ref[...]
    o_ref[...] = (y * _sigmoid(y)).astype(BF16)


def _conv_call(a_glu, conv_w, conv_b, ln_g, ln_b, gmean, *, tl, seq, ctx_len, n_batch):
    rows, ch = a_glu.shape
    width = conv_w.shape[0]
    assert width // 2 <= HALO and tl % HALO == 0
    hb = tl // HALO
    last_halo = rows // HALO - 1
    return pl.pallas_call(
        functools.partial(_conv_kernel, tl=tl, width=width, lat_tiles=n_batch * seq // tl,
                          lat_tps=seq // tl, ctx_tps=ctx_len // tl, chunk=32),
        grid=(rows // tl,),
        in_specs=[pl.BlockSpec((HALO, ch), lambda i: (jnp.maximum(i * hb - 1, 0), 0)),
                  pl.BlockSpec((tl, ch), lambda i: (i, 0)),
                  pl.BlockSpec((HALO, ch), lambda i: (jnp.minimum((i + 1) * hb, last_halo), 0)),
                  pl.BlockSpec((width, ch), lambda i: (0, 0)),
                  pl.BlockSpec((1, ch), lambda i: (0, 0)),
                  pl.BlockSpec((1, ch), lambda i: (0, 0)),
                  pl.BlockSpec((1, ch), lambda i: (0, 0)),
                  pl.BlockSpec((ch, ch), lambda i: (0, 0))],
        out_specs=pl.BlockSpec((tl, ch), lambda i: (i, 0)),
        out_shape=jax.ShapeDtypeStruct((rows, ch), BF16),
        scratch_shapes=[pltpu.VMEM((tl + 2 * HALO, ch), F32), pltpu.VMEM((tl, ch), F32),
                        pltpu.VMEM((SUBLANES, tl + 2 * HALO - SUBLANES, ch), F32)],
        compiler_params=_cparams(1, 40),
        name="dwconv_groupln_swish",
    )(a_glu, a_glu, a_glu, conv_w, conv_b, ln_g, ln_b, gmean)


def _seqdft_kernel(c_ref, s_ref, a_ref, b_ref, o_ref):
    o_ref[...] = (_dot(c_ref[...], a_ref[...]) - _dot(s_ref[...], b_ref[...])).astype(BF16)


def _seqdft_call(ab, cmat, smat, *, length, row0, n_seq, fc, tm):
    seq0 = row0 // length
    return pl.pallas_call(
        _seqdft_kernel,
        grid=(n_seq, length // tm),
        in_specs=[pl.BlockSpec((tm, length), lambda b, i: (i, 0)),
                  pl.BlockSpec((tm, length), lambda b, i: (i, 0)),
                  pl.BlockSpec((length, fc), lambda b, i: (seq0 + b, 0)),
                  pl.BlockSpec((length, fc), lambda b, i: (seq0 + b, 1))],
        out_specs=pl.BlockSpec((tm, fc), lambda b, i: (b * (length // tm) + i, 0)),
        out_shape=jax.ShapeDtypeStruct((n_seq * length, fc), BF16),
        compiler_params=_cparams(2, 48),
        name="seq_dft",
    )(cmat, smat, ab, ab)


def _dft_tables(length, scale):
    if length <= 512:
        kn = np.outer(np.arange(length), np.arange(length)) % length
        ang = 2.0 * np.pi * kn / length
        return (jnp.asarray(np.cos(ang) * scale, BF16), jnp.asarray(np.sin(ang) * scale, BF16))
    r = 64
    assert length % r == 0
    k = np.arange(length)[:, None]
    alpha = 2.0 * np.pi * ((k * np.arange(length // r)[None, :] * r) % length) / length
    beta = 2.0 * np.pi * ((k * np.arange(r)[None, :]) % length) / length
    ca, sa = jnp.asarray(np.cos(alpha), F32)[:, :, None], jnp.asarray(np.sin(alpha), F32)[:, :, None]
    cb, sb = jnp.asarray(np.cos(beta) * scale, F32)[:, None, :], jnp.asarray(np.sin(beta) * scale, F32)[:, None, :]
    cmat = (ca * cb - sa * sb).reshape(length, length).astype(BF16)
    smat = (sa * cb + ca * sb).reshape(length, length).astype(BF16)
    return cmat, smat


def _outproj_kernel(*refs, n_in, n_head):
    ins, ws = refs[:n_in], refs[n_in:2 * n_in]
    (xh_ref, xt_ref, g1_ref, gate_ref, g2_ref, sh_ref, sc_ref, wh_ref, wl_ref, rb_ref,
     xo_ref, tok_ref, tokp_ref, e_ref, gt_ref, pos_ref, cnt_ref, carry) = refs[2 * n_in:]
    y = _dot(ins[0][...], ws[0][...])
    for a_ref, w_ref in zip(ins[1:], ws[1:]):
        y = y + _dot(a_ref[...], w_ref[...])
    x1 = _pick_rows(xh_ref, xt_ref, n_head) + gate_ref[...] * _rms(y, g1_ref[...])
    xo_ref[...] = x1
    tok = _normmod(x1, g2_ref[...], sh_ref[...], sc_ref[...])
    tok_ref[...] = tok.astype(BF16)
    _store_pieces(tokp_ref, _pack_halves(tok))
    _route(tok, wh_ref, wl_ref, rb_ref, e_ref, gt_ref, pos_ref, cnt_ref, carry)


def _outproj_call(ins, ws, x_head, x_tail, g1, gate, g2, shift, scale, router_w, router_b,
                  *, rows, tm, seq, n_batch):
    d = x_head.shape[1]
    n_head = min(x_head.shape[0], rows) // tm
    n_exp = router_w.shape[1]
    w_t = router_w.T
    w_hi = w_t.astype(BF16)
    w_lo = (w_t - w_hi.astype(F32)).astype(BF16)
    seg = _seg_map(tm, seq, n_batch)
    n_in = len(ins)
    row_spec = lambda a: pl.BlockSpec((tm, a.shape[1]), lambda i: (i, 0))
    full_spec = lambda a: pl.BlockSpec(a.shape, lambda i: (0, 0))
    lane_spec = pl.BlockSpec((SUBLANES, tm), lambda i: (0, i))
    tok_rows = lambda dt: jax.ShapeDtypeStruct((SUBLANES, rows), dt)
    return pl.pallas_call(
        functools.partial(_outproj_kernel, n_in=n_in, n_head=n_head),
        grid=(rows // tm,),
        in_specs=[row_spec(a) for a in ins] + [full_spec(w) for w in ws] + [
            *_two_source_specs(tm, d, n_head), full_spec(g1), pl.BlockSpec((None, 1, d), seg), full_spec(g2),
            pl.BlockSpec((None, 1, d), seg), pl.BlockSpec((None, 1, d), seg),
            full_spec(w_hi), full_spec(w_lo), pl.BlockSpec((n_exp, 1), lambda i: (0, 0))],
        out_specs=[pl.BlockSpec((tm, d), lambda i: (i, 0)), pl.BlockSpec((tm, d), lambda i: (i, 0)),
                   pl.BlockSpec((SC_SPLIT, tm, d // 2 // SC_SPLIT), lambda i: (0, i, 0)),
                   lane_spec, lane_spec, lane_spec, pl.BlockSpec((n_exp, LANES), lambda i: (0, 0))],
        out_shape=[jax.ShapeDtypeStruct((rows, d), F32), jax.ShapeDtypeStruct((rows, d), BF16),
                   jax.ShapeDtypeStruct((SC_SPLIT, rows, d // 2 // SC_SPLIT), jnp.int32),
                   tok_rows(jnp.int32), tok_rows(F32), tok_rows(jnp.int32),
                   jax.ShapeDtypeStruct((n_exp, LANES), F32)],
        scratch_shapes=[pltpu.VMEM((n_exp, 1), F32)],
        compiler_params=_cparams(1, 48),
        name="outproj_residual_route",
    )(*ins, *ws, x_head, x_tail, g1, gate, g2, shift, scale, w_hi, w_lo, router_b.reshape(n_exp, 1))


def _pick_max(cur, idx):
    mx = jnp.max(cur, axis=0, keepdims=True)
    first = jnp.min(jnp.where(cur == mx, idx, float(cur.shape[0])), axis=0, keepdims=True)
    return first, idx == first


def _route(tok, wh_ref, wl_ref, rb_ref, e_ref, gt_ref, pos_ref, cnt_ref, carry):
    @pl.when(pl.program_id(0) == 0)
    def _():
        carry[...] = jnp.zeros_like(carry)

    t_hi, t_lo = _split_bf16(tok)
    wh, wl = wh_ref[...], wl_ref[...]
    logits = _dot_nt(wh, t_hi) + _dot_nt(wh, t_lo) + _dot_nt(wl, t_hi)
    n_exp, tm = logits.shape
    scores = _sigmoid(logits)
    biased = scores + rb_ref[...]
    gsz = n_exp // N_GROUPS
    neg = -jnp.inf

    b3 = biased.reshape(N_GROUPS, gsz, tm)
    im = lax.broadcasted_iota(jnp.int32, b3.shape, 1).astype(F32)
    m1 = jnp.max(b3, axis=1, keepdims=True)
    i1 = jnp.min(jnp.where(b3 == m1, im, float(gsz)), axis=1, keepdims=True)
    m2 = jnp.max(jnp.where(im == i1, neg, b3), axis=1, keepdims=True)
    gscore = (m1 + m2).reshape(N_GROUPS, tm)

    ig = lax.broadcasted_iota(jnp.int32, gscore.shape, 0).astype(F32)
    gsel = jnp.zeros_like(gscore)
    cur = gscore
    for _ in range(TOPK_GROUPS):
        _, hit = _pick_max(cur, ig)
        gsel = jnp.where(hit, 1.0, gsel)
        cur = jnp.where(hit, neg, cur)
    gsel3 = jnp.broadcast_to(gsel.reshape(N_GROUPS, 1, tm), b3.shape)
    cur = jnp.where(gsel3 > 0.0, b3, neg).reshape(n_exp, tm)

    ie = lax.broadcasted_iota(jnp.int32, (n_exp, tm), 0).astype(F32)
    sel = jnp.zeros((n_exp, tm), F32)
    picks, raw = [], []
    for _ in range(TOP_K):
        first, hit = _pick_max(cur, ie)
        picks.append(first)
        raw.append(jnp.sum(jnp.where(hit, scores, 0.0), axis=0, keepdims=True))
        sel = jnp.where(hit, 1.0, sel)
        cur = jnp.where(hit, neg, cur)
    total = raw[0]
    for r in raw[1:]:
        total = total + r

    ri = lax.broadcasted_iota(jnp.int32, (tm, tm), 0)
    ci = lax.broadcasted_iota(jnp.int32, (tm, tm), 1)
    upper = jnp.where(ri < ci, 1.0, 0.0).astype(BF16)
    rank = _dot(sel.astype(BF16), upper) + carry[...]
    ranks = [jnp.sum(jnp.where(ie == p, rank, 0.0), axis=0, keepdims=True) for p in picks]
    carry[...] = carry[...] + jnp.sum(sel, axis=1, keepdims=True)

    pad = jnp.zeros((SUBLANES - TOP_K, tm), F32)
    e_ref[...] = jnp.concatenate(picks + [pad], axis=0).astype(jnp.int32)
    gt_ref[...] = jnp.concatenate([r / total * ROUTED_SCALE for r in raw] + [pad], axis=0)
    pos_ref[...] = jnp.concatenate(ranks + [pad], axis=0).astype(jnp.int32)
    cnt_ref[...] = jnp.broadcast_to(carry[...], cnt_ref.shape)


def _expert_kernel(be_ref, nv_ref, nu_ref, xs_hbm, wg_ref, wu_ref, wd_ref, after_ref, ys_ref, wgu, wdn, xbuf,
                   sem, *, ff, n_sub):
    del after_ref
    b = pl.program_id(0)
    n_used = nu_ref[0]
    depth, _, te, q = xbuf.shape

    def row_block_copy(blk):
        slot = blk % depth
        return pltpu.make_async_copy(xs_hbm.at[:, pl.ds(blk * te, te), :], xbuf.at[slot], sem.at[slot])

    @pl.when(b == 0)
    def _():
        for j in range(depth - 1):
            @pl.when(j < n_used)
            def _():
                row_block_copy(j).start()

    @pl.when(b + depth - 1 < n_used)
    def _():
        row_block_copy(b + depth - 1).start()

    changed = jnp.logical_or(b == 0, be_ref[b] != be_ref[jnp.maximum(b - 1, 0)])

    @pl.when(changed)
    def _():
        wgu[:, :ff] = wg_ref[...].astype(BF16)
        wgu[:, ff:] = wu_ref[...].astype(BF16)
        wdn[...] = wd_ref[...].astype(BF16)

    @pl.when(b < n_used)
    def _():
        row_block_copy(b).wait()
        slot = b % depth
        n_valid = nv_ref[b]
        ts = te // n_sub
        gus = []
        for a in range(n_sub):
            r0 = a * ts
            xw = jnp.concatenate([xbuf[slot, j, r0:r0 + ts, :] for j in range(SC_SPLIT)], axis=-1)
            row = lax.broadcasted_iota(jnp.int32, xw.shape, 0) + r0
            lo, hi = _unpack_halves(jnp.where(row < n_valid, xw, 0))
            half = lo.shape[1]
            gus.append(_dot(lo.astype(BF16), wgu[:half, :]) + _dot(hi.astype(BF16), wgu[half:, :]))
        outs = []
        for gu in gus:
            gate = gu[:, :ff]
            hmid = (gate * _sigmoid(gate) * gu[:, ff:]).astype(BF16)
            outs.append(_dot(hmid, wdn[...]))
        for a, y in enumerate(outs):
            words = _pack_halves(y)
            for j in range(SC_SPLIT):
                ys_ref[j, a * ts:(a + 1) * ts, :] = words[:, j * q:(j + 1) * q]


def _expert_call(block_e, n_valid, n_used, xs, layer, w_gate, w_up, w_down, run_after):
    _, n_rows, q = xs.shape
    d = 2 * SC_SPLIT * q
    ff = w_gate.shape[3]
    te = EXPERT_ROWS
    row_map = lambda b, be, nv, nu: (0, jnp.minimum(b, nu[0] - 1), 0)
    w_map = lambda b, be, nv, nu: (layer, be[b], 0, 0)
    grid_spec = pltpu.PrefetchScalarGridSpec(
        num_scalar_prefetch=3,
        grid=(n_rows // te,),
        in_specs=[pl.BlockSpec(memory_space=pl.ANY),
                  pl.BlockSpec((None, None, d, ff), w_map),
                  pl.BlockSpec((None, None, d, ff), w_map),
                  pl.BlockSpec((None, None, ff, d), w_map),
                  pl.BlockSpec(memory_space=pl.ANY)],
        out_specs=pl.BlockSpec((SC_SPLIT, te, q), row_map),
        scratch_shapes=[pltpu.VMEM((d, 2 * ff), BF16), pltpu.VMEM((ff, d), BF16),
                        pltpu.VMEM((EXPERT_RING, SC_SPLIT, te, q), jnp.int32),
                        pltpu.SemaphoreType.DMA((EXPERT_RING,))])
    return pl.pallas_call(
        functools.partial(_expert_kernel, ff=ff, n_sub=EXPERT_ROWS // EXPERT_SUB_ROWS),
        grid_spec=grid_spec,
        out_shape=jax.ShapeDtypeStruct(xs.shape, jnp.int32),
        compiler_params=_cparams(1, 40),
        name="moe_experts",
    )(block_e, n_valid, n_used, xs, w_gate, w_up, w_down, run_after)


SC_WINDOW = 128


def _sc_mesh():
    return plsc.VectorSubcoreMesh(core_axis_name="core", subcore_axis_name="subcore")


def _sc_scatter_rows(rows, dests, n_out):
    n, width = rows.shape
    n_k = len(dests)
    assert n % SC_WINDOW == 0

    @functools.partial(pl.kernel, out_type=jax.ShapeDtypeStruct((n_out, width), rows.dtype),
                       mesh=_sc_mesh(), name="sc_dispatch_rows")
    def scatter_kernel(x_hbm, *refs):
        idx_hbm, o_hbm = refs[:n_k], refs[n_k]

        def body(x_vmem, *idx_vmem):
            for iv in idx_vmem:
                pltpu.sync_copy(x_vmem, o_hbm.at[iv.at[0]])

        pltpu.emit_pipeline(
            body,
            grid=(n // SC_WINDOW,),
            in_specs=[pl.BlockSpec((SC_WINDOW, width), lambda i: (i, 0))]
            + [pl.BlockSpec((1, SC_WINDOW), lambda i: (0, i))] * n_k,
            out_specs=[],
            core_axis_name=("core", "subcore"),
            dimension_semantics=(pltpu.PARALLEL,),
        )(x_hbm, *idx_hbm)

    return scatter_kernel(rows, *dests)


def _sc_gather_rows(table, idx):
    n = idx.shape[1]
    width = table.shape[1]
    assert n % SC_WINDOW == 0

    @functools.partial(pl.kernel, out_type=jax.ShapeDtypeStruct((n, width), table.dtype),
                       mesh=_sc_mesh(), name="sc_collect_rows")
    def gather_kernel(t_hbm, i_hbm, o_hbm):
        def body(i_vmem, o_vmem):
            pltpu.sync_copy(t_hbm.at[i_vmem.at[0]], o_vmem)

        pltpu.emit_pipeline(
            body,
            grid=(n // SC_WINDOW,),
            in_specs=[pl.BlockSpec((1, SC_WINDOW), lambda i: (0, i))],
            out_specs=[pl.BlockSpec((SC_WINDOW, width), lambda i: (i, 0))],
            core_axis_name=("core", "subcore"),
            dimension_semantics=(pltpu.PARALLEL,),
        )(i_hbm, o_hbm)

    return gather_kernel(table, idx)


def _shared_kernel(tok_ref, sgu_ref, sd_ref, o_ref, *, ff):
    gu = _dot(tok_ref[...], sgu_ref[...])
    gate = gu[:, :ff]
    hmid = (gate * _sigmoid(gate) * gu[:, ff:]).astype(BF16)
    o_ref[...] = _dot(hmid, sd_ref[...]).astype(BF16)


def _shared_call(tok, sgu, sd, *, rows, tm):
    d = tok.shape[1]
    ff = sd.shape[0]
    return pl.pallas_call(
        functools.partial(_shared_kernel, ff=ff),
        grid=(rows // tm,),
        in_specs=[pl.BlockSpec((tm, d), lambda i: (i, 0)),
                  pl.BlockSpec((d, 2 * ff), lambda i: (0, 0)),
                  pl.BlockSpec((ff, d), lambda i: (0, 0))],
        out_specs=pl.BlockSpec((tm, d), lambda i: (i, 0)),
        out_shape=jax.ShapeDtypeStruct((rows, d), BF16),
        compiler_params=_cparams(1, 40),
        name="moe_shared_expert",
    )(tok, sgu, sd)


def _moe_out_kernel(shared_ref, yg_ref, gk_ref, x_ref, g_ref, gate_ref, *rest):
    xo_ref = rest[-1]
    shared = shared_ref[...].astype(F32)
    gk = gk_ref[...]
    lo_acc, hi_acc = None, None
    for k in range(TOP_K):
        lo, hi = _unpack_halves(jnp.concatenate([yg_ref[j, k] for j in range(SC_SPLIT)], axis=-1))
        w = gk[:, k:k + 1]
        lo_acc = lo * w if lo_acc is None else lo_acc + lo * w
        hi_acc = hi * w if hi_acc is None else hi_acc + hi * w
    y = jnp.concatenate([lo_acc, hi_acc], axis=-1) + shared
    xo_ref[...] = x_ref[...] + gate_ref[...] * _rms(y, g_ref[...])


def _moe_out_call(shared, yg, gk, xs, g, gate, prev, *, rows, tile0, tm, seq, n_batch):
    d = xs.shape[1]
    seg = _seg_map(tm, seq, n_batch)
    row = lambda: pl.BlockSpec((tm, d), lambda i: (i + tile0, 0))
    n_tiles = yg.shape[2] // tm
    in_specs = [row(),
                pl.BlockSpec((SC_SPLIT, TOP_K, tm, d // 2 // SC_SPLIT), lambda i: (0, 0, i, 0)),
                pl.BlockSpec((tm, SUBLANES), lambda i: (i + tile0, 0)),
                row(),
                pl.BlockSpec((1, d), lambda i: (0, 0)),
                pl.BlockSpec((None, 1, d), lambda i: seg(i + tile0))]
    args = [shared, yg, gk, xs, g, gate]
    aliases = {}
    if prev is not None:
        in_specs.append(pl.BlockSpec(memory_space=pl.ANY))
        args.append(prev)
        aliases = {len(args) - 1: 0}
    return pl.pallas_call(
        _moe_out_kernel,
        grid=(n_tiles,),
        in_specs=in_specs,
        out_specs=row(),
        out_shape=jax.ShapeDtypeStruct((rows, d), F32),
        input_output_aliases=aliases,
        compiler_params=_cparams(1, 48),
        name="moe_combine",
    )(*args)


def _dest_kernel(e_ref, pos_ref, ps_ref, o_ref, *, n_rows):
    e = e_ref[...].astype(F32)
    ps = ps_ref[...]
    n_exp, tm = ps.shape[0], e.shape[1]
    ids = lax.broadcasted_iota(jnp.int32, (n_exp, tm), 0).astype(F32)
    rows = [jnp.sum(jnp.where(ids == e[k:k + 1, :], ps, 0.0), axis=0, keepdims=True) for k in range(SUBLANES)]
    dest = jnp.concatenate(rows, axis=0).astype(jnp.int32) + pos_ref[...]
    for j in range(SC_SPLIT):
        o_ref[j] = dest + j * n_rows


def _dest_call(e_tk, pos_tk, pad_starts, *, n_rows, tm):
    rows = e_tk.shape[1]
    n_exp = pad_starts.shape[0]
    lane_spec = pl.BlockSpec((SUBLANES, tm), lambda i: (0, i))
    return pl.pallas_call(
        functools.partial(_dest_kernel, n_rows=n_rows),
        grid=(rows // tm,),
        in_specs=[lane_spec, lane_spec, pl.BlockSpec((n_exp, 1), lambda i: (0, 0))],
        out_specs=pl.BlockSpec((SC_SPLIT, SUBLANES, tm), lambda i: (0, 0, i)),
        out_shape=jax.ShapeDtypeStruct((SC_SPLIT, SUBLANES, rows), jnp.int32),
        compiler_params=_cparams(1, 32),
        name="moe_dest_rows",
    )(e_tk, pos_tk, pad_starts.astype(F32).reshape(n_exp, 1))


def _moe_layer(xs, tok, tokp, routing, g_out, gate, layer, w_gate, w_up, w_down, sh_gate, sh_up, sh_down,
               *, rows, tm, seq, n_batch):
    e_tk, g_tk, pos_tk, counts = routing
    n_exp = counts.shape[0]
    te = EXPERT_ROWS
    cnt = counts[:, 0].astype(jnp.int32)
    padded = (cnt + te - 1) // te * te
    pad_ends = jnp.cumsum(padded)
    pad_starts = pad_ends - padded
    expert_ids = jnp.arange(n_exp, dtype=jnp.int32)
    n_blocks = -(-(rows * TOP_K + n_exp * (te - 1)) // te)
    n_used = pad_ends[-1] // te
    blk = jnp.arange(n_blocks, dtype=jnp.int32)
    last = jnp.minimum(blk, n_used - 1) * te
    owner = (last[:, None] >= pad_ends[None, :]).astype(jnp.int32)
    block_e = jnp.minimum(jnp.sum(owner, axis=1), n_exp - 1)
    own_hot = block_e[:, None] == expert_ids[None, :]
    n_valid = jnp.sum(jnp.where(own_hot, (pad_starts + cnt)[None, :], 0), axis=1) - last
    n_valid = jnp.clip(n_valid, 0, te)
    q = tokp.shape[2]
    n_rows = n_blocks * te
    piece = _dest_call(e_tk, pos_tk, pad_starts, n_rows=n_rows, tm=tm)[:, :TOP_K, :]
    xg = _sc_scatter_rows(tokp.reshape(SC_SPLIT * rows, q),
                          [piece[:, k, :].reshape(1, SC_SPLIT * rows) for k in range(TOP_K)],
                          SC_SPLIT * n_rows).reshape(SC_SPLIT, n_rows, q)
    sgu = jnp.concatenate([sh_gate, sh_up], axis=1).astype(BF16)
    shared = _shared_call(tok, sgu, sh_down.astype(BF16), rows=rows, tm=tm)
    ys = _expert_call(block_e, n_valid, n_used.reshape(1).astype(jnp.int32), xg, layer, w_gate, w_up, w_down,
                      shared)
    ys_flat = ys.reshape(SC_SPLIT * n_rows, q)
    gk = g_tk.T
    n_tiles = rows // tm
    out = None
    for r in range(COMBINE_RANGES):
        t0, t1 = n_tiles * r // COMBINE_RANGES, n_tiles * (r + 1) // COMBINE_RANGES
        idx = piece[:, :, t0 * tm:t1 * tm].reshape(1, -1)
        yg = _sc_gather_rows(ys_flat, idx).reshape(SC_SPLIT, TOP_K, (t1 - t0) * tm, q)
        out = _moe_out_call(shared, yg, gk, xs, g_out, gate, out, rows=rows, tile0=t0, tm=tm, seq=seq,
                            n_batch=n_batch)
    return out


def _rope(t, cos, sin):
    half = DA_QK_DIM // 2
    up = pltpu.roll(t, shift=LANES - half, axis=1)
    dn = pltpu.roll(t, shift=half, axis=1)
    lane = lax.broadcasted_iota(jnp.int32, t.shape, 1) % DA_QK_DIM
    return t * cos + jnp.where(lane < half, -up, dn) * sin


def _qkv_kernel(x_ref, g_ref, sh_ref, sc_ref, w_ref, cos_ref, sin_ref, q_ref, k_ref, v_ref, *, qkw):
    h = _normmod(x_ref[...], g_ref[...], sh_ref[...], sc_ref[...]).astype(BF16)
    qkv = _dot(h, w_ref[...])
    cos, sin = cos_ref[...], sin_ref[...]
    q_scale = DA_QK_DIM ** -0.5 * math.log2(math.e)
    for hb in range(qkw // LANES):
        lo, hi = hb * LANES, (hb + 1) * LANES
        q_ref[:, lo:hi] = (_rope(qkv[:, lo:hi], cos, sin) * q_scale).astype(BF16)
        k_ref[:, lo:hi] = _rope(qkv[:, qkw + lo:qkw + hi], cos, sin).astype(BF16)
    v_ref[...] = qkv[:, 2 * qkw:].astype(BF16)


def _qkv_call(xs, g, shift, scale, w_qkv, cos_t, sin_t, *, qkw, tm, seq, ctx_len, n_batch):
    rows, d = xs.shape
    n = w_qkv.shape[1]
    vw = n - 2 * qkw
    lat_tiles, lat_tps, ctx_tps = n_batch * seq // tm, seq // tm, ctx_len // tm
    seg = _seg_map(tm, seq, n_batch)

    def kv_map(i):
        c = i - lat_tiles
        is_lat = i < lat_tiles
        return (jnp.where(is_lat, i // lat_tps, c // ctx_tps),
                jnp.where(is_lat, ctx_tps + i % lat_tps, c % ctx_tps), 0)

    rope_map = lambda i: (jnp.where(i < lat_tiles, i % lat_tps, lat_tps), 0)
    return pl.pallas_call(
        functools.partial(_qkv_kernel, qkw=qkw),
        grid=(rows // tm,),
        in_specs=[pl.BlockSpec((tm, d), lambda i: (i, 0)),
                  pl.BlockSpec((1, d), lambda i: (0, 0)),
                  pl.BlockSpec((None, 1, d), seg),
                  pl.BlockSpec((None, 1, d), seg),
                  pl.BlockSpec((d, n), lambda i: (0, 0)),
                  pl.BlockSpec((tm, LANES), rope_map),
                  pl.BlockSpec((tm, LANES), rope_map)],
        out_specs=[pl.BlockSpec((tm, qkw), lambda i: (i, 0)),
                   pl.BlockSpec((None, tm, qkw), kv_map),
                   pl.BlockSpec((None, tm, vw), kv_map)],
        out_shape=[jax.ShapeDtypeStruct((rows, qkw), BF16),
                   jax.ShapeDtypeStruct((n_batch, ctx_len + seq, qkw), BF16),
                   jax.ShapeDtypeStruct((n_batch, ctx_len + seq, vw), BF16)],
        compiler_params=_cparams(1, 48),
        name="qkv_rope",
    )(xs, g, shift, scale, w_qkv, cos_t, sin_t)


def _rope_tables(seq, tm):
    pos = np.arange(seq)
    n_freq = DA_QK_DIM // 4
    inv = np.power(ROPE_BASE, -np.arange(n_freq, dtype=np.float32) / n_freq).astype(np.float32)
    row = (pos // GRID_W).astype(np.float32)[:, None] * inv
    col = (pos % GRID_W).astype(np.float32)[:, None] * inv
    ang = np.concatenate([row, col], axis=-1).astype(np.float32)
    ang = np.tile(ang, (1, LANES // ang.shape[1]))
    cos = np.concatenate([np.cos(ang), np.ones((tm, LANES))], axis=0)
    sin = np.concatenate([np.sin(ang), np.zeros((tm, LANES))], axis=0)
    return jnp.asarray(cos, F32), jnp.asarray(sin, F32)


def _attn_kernel(lp_ref, q_ref, k_ref, v_ref, sg_ref, o_ref, vext, *, n_sub, lam_init):
    @pl.when(pl.program_id(2) == 0)
    def _():
        vext[:, :LANES] = v_ref[...]
        vext[:, LANES:] = jnp.ones((vext.shape[0], LANES), BF16)

    lp = lp_ref[...]
    lam = (jnp.exp(jnp.sum(lp[0:1] * lp[1:2], axis=1, keepdims=True))
           - jnp.exp(jnp.sum(lp[2:3] * lp[3:4], axis=1, keepdims=True)) + lam_init)
    ts = q_ref.shape[0] // n_sub
    lane = lax.broadcasted_iota(jnp.int32, (ts, LANES), 1)
    zero = jnp.zeros((ts, LANES), BF16)
    k = k_ref[...]
    scores = []
    for a in range(n_sub):
        q = q_ref[a * ts:(a + 1) * ts, :]
        qq = jnp.concatenate([jnp.where(lane < DA_QK_DIM, q, zero),
                              jnp.where(lane >= DA_QK_DIM, q, zero)], axis=0)
        scores.append(_dot_nt(qq, k))
    for a in range(n_sub):
        s = scores[a]
        p = jnp.exp2(s - jnp.max(s, axis=-1, keepdims=True)).astype(BF16)
        oe = _dot(p, vext[...])
        on = oe[:, :LANES] / oe[:, LANES:LANES + 1]
        o = on[:ts] - lam * on[ts:]
        o_ref[a * ts:(a + 1) * ts, :] = (_rms(o, sg_ref[...]) * (1.0 - lam_init)).astype(BF16)


def _attn_call(lam_p, q, k_all, v_all, subln_g, *, tq, seq, n_batch, lam_init):
    n_heads = q.shape[1] // LANES
    lk = k_all.shape[1]
    qt = seq // tq
    return pl.pallas_call(
        functools.partial(_attn_kernel, n_sub=tq // ATTN_SUB_ROWS, lam_init=lam_init),
        grid=(n_batch, n_heads, qt),
        in_specs=[pl.BlockSpec(lam_p.shape, lambda b, h, i: (0, 0)),
                  pl.BlockSpec((tq, LANES), lambda b, h, i: (b * qt + i, h)),
                  pl.BlockSpec((None, lk, LANES), lambda b, h, i: (b, 0, h)),
                  pl.BlockSpec((None, lk, LANES), lambda b, h, i: (b, 0, h)),
                  pl.BlockSpec((1, LANES), lambda b, h, i: (0, 0))],
        out_specs=pl.BlockSpec((tq, LANES), lambda b, h, i: (b * qt + i, h)),
        out_shape=jax.ShapeDtypeStruct((n_batch * seq, n_heads * LANES), BF16),
        scratch_shapes=[pltpu.VMEM((lk, 2 * LANES), BF16)],
        compiler_params=_cparams(3, 48),
        name="diff_attention",
    )(lam_p, q, k_all, v_all, subln_g)


def kernel(x, c, ctx, c_ctx, ada_w, ada_b, norm_g, cf_w_in, cf_conv_w, cf_conv_b, cf_ln_g, cf_ln_b,
           cf_w_out, da_w_qkv, da_lambda, da_subln_g, da_w_out, moe_router_w, moe_router_b,
           moe_w_gate, moe_w_up, moe_w_down, moe_sh_gate, moe_sh_up, moe_sh_down):
    n_batch, seq, d = x.shape
    ctx_len = ctx.shape[1]
    depth = ada_w.shape[0]
    assert depth == 2 and n_batch + 1 <= N_SEG
    assert DA_QK_DIM * 2 == LANES and da_subln_g.shape[1] == LANES
    rows_lat, rows_ctx = n_batch * seq, n_batch * ctx_len
    tm = 512
    tq = 256
    assert seq % tm == 0 and rows_ctx % tm == 0 and seq % tq == 0 and ctx_len % tq == 0

    x_lat, x_ctx = x.reshape(rows_lat, d), ctx.reshape(rows_ctx, d)
    cond = jnp.concatenate([c, c_ctx[None, :], jnp.zeros((N_SEG - n_batch - 1, d), F32)], axis=0)
    mods = _mods_call(cond, ada_w, ada_b)
    mod = lambda layer, k: mods[layer, :, k * d:(k + 1) * d][:, None, :]
    gain = lambda layer, k: norm_g[layer, k][None, :]
    common = dict(seq=seq, n_batch=n_batch)

    cc = cf_conv_w.shape[-1]
    fc = cf_w_in.shape[2] - 2 * cc
    gw = fc // FOURIER_GROUPS
    ch_ang = 2.0 * np.pi * (np.outer(np.arange(gw), np.arange(gw)) % gw) / gw
    cs = jnp.asarray(np.concatenate([np.cos(ch_ang), np.sin(ch_ang)], axis=1) / math.sqrt(gw), BF16)
    a_glu, ab = _inproj_call(x_lat, x_ctx, gain(0, 0), mod(0, 0), mod(0, 1), cf_w_in[0].astype(BF16), cs,
                             cc=cc, gw=gw, tm=tm, **common)
    gsz = cc // CONV_GROUPS
    gid = np.arange(cc) // gsz
    gmean = jnp.asarray((gid[:, None] == gid[None, :]) / gsz, BF16)
    a_act = _conv_call(a_glu, cf_conv_w[0][:, 0, :], cf_conv_b[0][None, :], cf_ln_g[0][None, :],
                       cf_ln_b[0][None, :], gmean, tl=tq, seq=seq, ctx_len=ctx_len, n_batch=n_batch)
    c_lat, s_lat = _dft_tables(seq, 1.0 / math.sqrt(seq))
    c_ctx_m, s_ctx_m = _dft_tables(ctx_len, 1.0 / math.sqrt(ctx_len))
    fr = jnp.concatenate([
        _seqdft_call(ab, c_lat, s_lat, length=seq, row0=0, n_seq=n_batch, fc=fc, tm=min(seq, 256)),
        _seqdft_call(ab, c_ctx_m, s_ctx_m, length=ctx_len, row0=rows_lat, n_seq=n_batch, fc=fc,
                     tm=min(ctx_len, 256))], axis=0)
    w_out = cf_w_out[0].astype(BF16)
    xs, tok, tokp, *routing = _outproj_call(
        [a_act, fr], [w_out[:cc], w_out[cc:]], x_lat, x_ctx, gain(0, 1), mod(0, 2), gain(0, 2), mod(0, 3),
        mod(0, 4), moe_router_w[0], moe_router_b[0], rows=rows_lat + rows_ctx, tm=tm, **common)
    xs = _moe_layer(xs, tok, tokp, routing, gain(0, 3), mod(0, 5), 0, moe_w_gate, moe_w_up, moe_w_down,
                    moe_sh_gate[0], moe_sh_up[0], moe_sh_down[0], rows=rows_lat + rows_ctx, tm=tm, **common)

    qkw = DA_HEADS * 2 * DA_QK_DIM
    lam_init = 0.8 - 0.6 * math.exp(-0.3 * 1)
    cos_t, sin_t = _rope_tables(seq, tq)
    q, k_all, v_all = _qkv_call(xs, gain(1, 0), mod(1, 0), mod(1, 1), da_w_qkv[0].astype(BF16),
                                cos_t, sin_t, qkw=qkw, tm=tq, ctx_len=ctx_len, **common)
    o = _attn_call(da_lambda[0], q, k_all, v_all, da_subln_g[0][None, :], tq=tm, seq=seq,
                   n_batch=n_batch, lam_init=lam_init)
    xs, tok, tokp, *routing = _outproj_call(
        [o], [da_w_out[0].astype(BF16)], xs, xs, gain(1, 1), mod(1, 2), gain(1, 2), mod(1, 3), mod(1, 4),
        moe_router_w[1], moe_router_b[1], rows=rows_lat, tm=tm, **common)
    xs = _moe_layer(xs, tok, tokp, routing, gain(1, 3), mod(1, 5), 1, moe_w_gate, moe_w_up, moe_w_down,
                    moe_sh_gate[1], moe_sh_up[1], moe_sh_down[1], rows=rows_lat, tm=tm, **common)
    return xs.reshape(n_batch, seq, d)
```

```python
import functools
import math

import numpy as np
import jax
import jax.numpy as jnp
from jax import lax
from jax.experimental import pallas as pl
from jax.experimental.pallas import tpu as pltpu
from jax.experimental.pallas import tpu_sc as plsc

F32 = jnp.float32
BF16 = jnp.bfloat16

EPS = 1e-6
GRID_W = 64
CONV_GROUPS = 8
FOURIER_GROUPS = 4
DA_HEADS = 8
DA_QK_DIM = 64
ROPE_BASE = 10000.0
N_GROUPS = 8
TOPK_GROUPS = 4
TOP_K = 6
ROUTED_SCALE = 2.5

LANES = 128
SUBLANES = 8
N_SEG = 16
HALO = 16
EXPERT_ROWS = 1024
EXPERT_SUB_ROWS = 128
EXPERT_RING = 3
ATTN_ROWS = 1024
ATTN_SUB_ROWS = 128
MIB = 1024 * 1024


def _cparams(n_axes, vmem_mib):
    return pltpu.CompilerParams(dimension_semantics=("arbitrary",) * n_axes,
                                vmem_limit_bytes=vmem_mib * MIB)


def _sigmoid(v):
    return 1.0 / (1.0 + jnp.exp(-v))


def _rms(v, g):
    return v * lax.rsqrt(jnp.mean(v * v, axis=-1, keepdims=True) + EPS) * g


def _normmod(v, g, shift, scale):
    return _rms(v, g) * (1.0 + scale) + shift


def _split_bf16(v):
    hi = v.astype(BF16)
    lo = (v - hi.astype(F32)).astype(BF16)
    return hi, lo


def _pack_halves(v):
    half = v.shape[1] // 2
    word = pltpu.pack_elementwise([v[:, :half], v[:, half:]], packed_dtype=BF16)
    return lax.bitcast_convert_type(word, jnp.int32)


def _unpack_halves(w):
    u = lax.bitcast_convert_type(w, jnp.uint32)
    lo = lax.bitcast_convert_type(u << 16, F32)
    hi = lax.bitcast_convert_type(u & jnp.uint32(0xFFFF0000), F32)
    return lo, hi


SC_SPLIT = 2


def _store_pieces(ref, words):
    q = words.shape[1] // SC_SPLIT
    for j in range(SC_SPLIT):
        ref[j] = words[:, j * q:(j + 1) * q]


def _load_pieces(ref):
    return jnp.concatenate([ref[j] for j in range(SC_SPLIT)], axis=-1)


def _dot(a, b):
    return jnp.dot(a, b, preferred_element_type=F32)


def _dot_nt(a, b):
    return lax.dot_general(a, b, (((1,), (1,)), ((), ())), preferred_element_type=F32)


def _mods_kernel(c_ref, w_ref, b_ref, o_ref):
    cv = c_ref[...]
    o_ref[...] = _dot(cv * _sigmoid(cv), w_ref[...]) + b_ref[...]


def _mods_call(cond, ada_w, ada_b):
    depth, d, n = ada_w.shape
    tn = n // 4
    return pl.pallas_call(
        _mods_kernel,
        grid=(depth, n // tn),
        in_specs=[pl.BlockSpec((N_SEG, d), lambda l, j: (0, 0)),
                  pl.BlockSpec((None, d, tn), lambda l, j: (l, 0, j)),
                  pl.BlockSpec((None, 1, tn), lambda l, j: (l, 0, j))],
        out_specs=pl.BlockSpec((None, N_SEG, tn), lambda l, j: (l, 0, j)),
        out_shape=jax.ShapeDtypeStruct((depth, N_SEG, n), F32),
        compiler_params=_cparams(2, 40),
        name="adaln_mods",
    )(cond, ada_w, ada_b.reshape(depth, 1, n))


def _two_source_specs(tm, d, n_head_tiles):
    return (pl.BlockSpec((tm, d), lambda i: (jnp.minimum(i, n_head_tiles - 1), 0)),
            pl.BlockSpec((tm, d), lambda i: (jnp.maximum(i - n_head_tiles, 0), 0)))


def _pick_rows(head_ref, tail_ref, n_head_tiles):
    return jnp.where(pl.program_id(0) < n_head_tiles, head_ref[...], tail_ref[...])


def _inproj_kernel(xh_ref, xt_ref, g_ref, sh_ref, sc_ref, w_ref, cs_ref, a_ref, ab_ref, *, cc, gw, n_head):
    x = _pick_rows(xh_ref, xt_ref, n_head)
    h = _normmod(x, g_ref[...], sh_ref[...], sc_ref[...]).astype(BF16)
    u = _dot(h, w_ref[...])
    a_ref[...] = u[:, :cc] * _sigmoid(u[:, cc:2 * cc])
    f = u[:, 2 * cc:].astype(BF16)
    n_g = f.shape[1] // gw
    parts = [_dot(f[:, g * gw:(g + 1) * gw], cs_ref[...]) for g in range(n_g)]
    cos_part = [p[:, :gw] for p in parts]
    sin_part = [p[:, gw:] for p in parts]
    ab_ref[...] = jnp.concatenate(cos_part + sin_part, axis=-1).astype(BF16)


def _seg_map(tm, seq, n_batch):
    return lambda i: (jnp.minimum(i * tm // seq, n_batch), 0, 0)


def _inproj_call(x_head, x_tail, g, shift, scale, w_in, cs, *, cc, gw, tm, seq, n_batch):
    d = x_head.shape[1]
    rows = x_head.shape[0] + x_tail.shape[0]
    n_head = x_head.shape[0] // tm
    n = w_in.shape[1]
    fc = n - 2 * cc
    seg = _seg_map(tm, seq, n_batch)
    return pl.pallas_call(
        functools.partial(_inproj_kernel, cc=cc, gw=gw, n_head=n_head),
        grid=(rows // tm,),
        in_specs=[*_two_source_specs(tm, d, n_head),
                  pl.BlockSpec((1, d), lambda i: (0, 0)),
                  pl.BlockSpec((None, 1, d), seg),
                  pl.BlockSpec((None, 1, d), seg),
                  pl.BlockSpec((d, n), lambda i: (0, 0)),
                  pl.BlockSpec((gw, 2 * gw), lambda i: (0, 0))],
        out_specs=[pl.BlockSpec((tm, cc), lambda i: (i, 0)),
                   pl.BlockSpec((tm, 2 * fc), lambda i: (i, 0))],
        out_shape=[jax.ShapeDtypeStruct((rows, cc), F32),
                   jax.ShapeDtypeStruct((rows, 2 * fc), BF16)],
        compiler_params=_cparams(1, 40),
        name="inproj_glu_chdft",
    )(x_head, x_tail, g, shift, scale, w_in, cs)


def _conv_kernel(prev_ref, main_ref, next_ref, w_ref, cb_ref, lg_ref, lb_ref, gm_ref, o_ref,
                 buf, cv, shifted, *, tl, width, lat_tiles, lat_tps, ctx_tps, chunk):
    i = pl.program_id(0)
    is_lat = i < lat_tiles
    tps = jnp.where(is_lat, lat_tps, ctx_tps)
    j = jnp.where(is_lat, i, i - lat_tiles) % tps
    zero = jnp.zeros((HALO, buf.shape[1]), F32)
    buf[0:HALO, :] = jnp.where(j > 0, prev_ref[...], zero)
    buf[HALO:HALO + tl, :] = main_ref[...]
    buf[HALO + tl:HALO + tl + HALO, :] = jnp.where(j < tps - 1, next_ref[...], zero)
    base = HALO - width // 2
    span = shifted.shape[1]
    for s in range(1, SUBLANES):
        shifted[s] = buf[s:s + span, :]

    def window(off, r0):
        s, m = off % SUBLANES, off - off % SUBLANES
        if s == 0:
            return buf[m + r0:m + r0 + chunk, :]
        return shifted[s, m + r0:m + r0 + chunk, :]

    for r0 in range(0, tl, chunk):
        acc = window(base, r0) * w_ref[0:1, :]
        for t in range(1, width):
            acc = acc + window(base + t, r0) * w_ref[t:t + 1, :]
        cv[r0:r0 + chunk, :] = acc
    a = cv[...] + cb_ref[...]
    gm = gm_ref[...]
    a_hi, a_lo = _split_bf16(a)
    mu = _dot(a_hi, gm) + _dot(a_lo, gm)
    dl = a - mu
    q_hi, q_lo = _split_bf16(dl * dl)
    var = _dot(q_hi, gm) + _dot(q_lo, gm)
    y = dl * lax.rsqrt(var + EPS) * lg_ref[...] + lb_ref[...]
    o_ref[...] = (y * _sigmoid(y)).astype(BF16)


def _conv_call(a_glu, conv_w, conv_b, ln_g, ln_b, gmean, *, tl, seq, ctx_len, n_batch):
    rows, ch = a_glu.shape
    width = conv_w.shape[0]
    assert width // 2 <= HALO and tl % HALO == 0
    hb = tl // HALO
    last_halo = rows // HALO - 1
    return pl.pallas_call(
        functools.partial(_conv_kernel, tl=tl, width=width, lat_tiles=n_batch * seq // tl,
                          lat_tps=seq // tl, ctx_tps=ctx_len // tl, chunk=32),
        grid=(rows // tl,),
        in_specs=[pl.BlockSpec((HALO, ch), lambda i: (jnp.maximum(i * hb - 1, 0), 0)),
                  pl.BlockSpec((tl, ch), lambda i: (i, 0)),
                  pl.BlockSpec((HALO, ch), lambda i: (jnp.minimum((i + 1) * hb, last_halo), 0)),
                  pl.BlockSpec((width, ch), lambda i: (0, 0)),
                  pl.BlockSpec((1, ch), lambda i: (0, 0)),
                  pl.BlockSpec((1, ch), lambda i: (0, 0)),
                  pl.BlockSpec((1, ch), lambda i: (0, 0)),
                  pl.BlockSpec((ch, ch), lambda i: (0, 0))],
        out_specs=pl.BlockSpec((tl, ch), lambda i: (i, 0)),
        out_shape=jax.ShapeDtypeStruct((rows, ch), BF16),
        scratch_shapes=[pltpu.VMEM((tl + 2 * HALO, ch), F32), pltpu.VMEM((tl, ch), F32),
                        pltpu.VMEM((SUBLANES, tl + 2 * HALO - SUBLANES, ch), F32)],
        compiler_params=_cparams(1, 40),
        name="dwconv_groupln_swish",
    )(a_glu, a_glu, a_glu, conv_w, conv_b, ln_g, ln_b, gmean)


def _seqdft_kernel(c_ref, s_ref, a_ref, b_ref, o_ref):
    o_ref[...] = (_dot(c_ref[...], a_ref[...]) - _dot(s_ref[...], b_ref[...])).astype(BF16)


def _seqdft_call(ab, cmat, smat, *, length, row0, n_seq, fc, tm):
    seq0 = row0 // length
    return pl.pallas_call(
        _seqdft_kernel,
        grid=(n_seq, length // tm),
        in_specs=[pl.BlockSpec((tm, length), lambda b, i: (i, 0)),
                  pl.BlockSpec((tm, length), lambda b, i: (i, 0)),
                  pl.BlockSpec((length, fc), lambda b, i: (seq0 + b, 0)),
                  pl.BlockSpec((length, fc), lambda b, i: (seq0 + b, 1))],
        out_specs=pl.BlockSpec((tm, fc), lambda b, i: (b * (length // tm) + i, 0)),
        out_shape=jax.ShapeDtypeStruct((n_seq * length, fc), BF16),
        compiler_params=_cparams(2, 48),
        name="seq_dft",
    )(cmat, smat, ab, ab)


def _dft_tables(length, scale):
    if length <= 512:
        kn = np.outer(np.arange(length), np.arange(length)) % length
        ang = 2.0 * np.pi * kn / length
        return (jnp.asarray(np.cos(ang) * scale, BF16), jnp.asarray(np.sin(ang) * scale, BF16))
    r = 64
    assert length % r == 0
    k = np.arange(length)[:, None]
    alpha = 2.0 * np.pi * ((k * np.arange(length // r)[None, :] * r) % length) / length
    beta = 2.0 * np.pi * ((k * np.arange(r)[None, :]) % length) / length
    ca, sa = jnp.asarray(np.cos(alpha), F32)[:, :, None], jnp.asarray(np.sin(alpha), F32)[:, :, None]
    cb, sb = jnp.asarray(np.cos(beta) * scale, F32)[:, None, :], jnp.asarray(np.sin(beta) * scale, F32)[:, None, :]
    cmat = (ca * cb - sa * sb).reshape(length, length).astype(BF16)
    smat = (sa * cb + ca * sb).reshape(length, length).astype(BF16)
    return cmat, smat


def _outproj_kernel(*refs, n_in, n_head):
    ins, ws = refs[:n_in], refs[n_in:2 * n_in]
    (xh_ref, xt_ref, g1_ref, gate_ref, g2_ref, sh_ref, sc_ref, wh_ref, wl_ref, rb_ref,
     xo_ref, tok_ref, tokp_ref, e_ref, gt_ref, pos_ref, cnt_ref, carry) = refs[2 * n_in:]
    y = _dot(ins[0][...], ws[0][...])
    for a_ref, w_ref in zip(ins[1:], ws[1:]):
        y = y + _dot(a_ref[...], w_ref[...])
    x1 = _pick_rows(xh_ref, xt_ref, n_head) + gate_ref[...] * _rms(y, g1_ref[...])
    xo_ref[...] = x1
    tok = _normmod(x1, g2_ref[...], sh_ref[...], sc_ref[...])
    tok_ref[...] = tok.astype(BF16)
    _store_pieces(tokp_ref, _pack_halves(tok))
    _route(tok, wh_ref, wl_ref, rb_ref, e_ref, gt_ref, pos_ref, cnt_ref, carry)


def _outproj_call(ins, ws, x_head, x_tail, g1, gate, g2, shift, scale, router_w, router_b,
                  *, rows, tm, seq, n_batch):
    d = x_head.shape[1]
    n_head = min(x_head.shape[0], rows) // tm
    n_exp = router_w.shape[1]
    w_t = router_w.T
    w_hi = w_t.astype(BF16)
    w_lo = (w_t - w_hi.astype(F32)).astype(BF16)
    seg = _seg_map(tm, seq, n_batch)
    n_in = len(ins)
    row_spec = lambda a: pl.BlockSpec((tm, a.shape[1]), lambda i: (i, 0))
    full_spec = lambda a: pl.BlockSpec(a.shape, lambda i: (0, 0))
    lane_spec = pl.BlockSpec((SUBLANES, tm), lambda i: (0, i))
    tok_rows = lambda dt: jax.ShapeDtypeStruct((SUBLANES, rows), dt)
    return pl.pallas_call(
        functools.partial(_outproj_kernel, n_in=n_in, n_head=n_head),
        grid=(rows // tm,),
        in_specs=[row_spec(a) for a in ins] + [full_spec(w) for w in ws] + [
            *_two_source_specs(tm, d, n_head), full_spec(g1), pl.BlockSpec((None, 1, d), seg), full_spec(g2),
            pl.BlockSpec((None, 1, d), seg), pl.BlockSpec((None, 1, d), seg),
            full_spec(w_hi), full_spec(w_lo), pl.BlockSpec((n_exp, 1), lambda i: (0, 0))],
        out_specs=[pl.BlockSpec((tm, d), lambda i: (i, 0)), pl.BlockSpec((tm, d), lambda i: (i, 0)),
                   pl.BlockSpec((SC_SPLIT, tm, d // 2 // SC_SPLIT), lambda i: (0, i, 0)),
                   lane_spec, pl.BlockSpec((tm, SUBLANES), lambda i: (i, 0)), lane_spec,
                   pl.BlockSpec((n_exp, LANES), lambda i: (0, 0))],
        out_shape=[jax.ShapeDtypeStruct((rows, d), F32), jax.ShapeDtypeStruct((rows, d), BF16),
                   jax.ShapeDtypeStruct((SC_SPLIT, rows, d // 2 // SC_SPLIT), jnp.int32),
                   tok_rows(jnp.int32), jax.ShapeDtypeStruct((rows, SUBLANES), F32), tok_rows(jnp.int32),
                   jax.ShapeDtypeStruct((n_exp, LANES), F32)],
        scratch_shapes=[pltpu.VMEM((n_exp, 1), F32)],
        compiler_params=_cparams(1, 48),
        name="outproj_residual_route",
    )(*ins, *ws, x_head, x_tail, g1, gate, g2, shift, scale, w_hi, w_lo, router_b.reshape(n_exp, 1))


def _pick_max(cur, idx):
    mx = jnp.max(cur, axis=0, keepdims=True)
    first = jnp.min(jnp.where(cur == mx, idx, float(cur.shape[0])), axis=0, keepdims=True)
    return first, idx == first


def _route(tok, wh_ref, wl_ref, rb_ref, e_ref, gt_ref, pos_ref, cnt_ref, carry):
    @pl.when(pl.program_id(0) == 0)
    def _():
        carry[...] = jnp.zeros_like(carry)

    t_hi, t_lo = _split_bf16(tok)
    wh, wl = wh_ref[...], wl_ref[...]
    logits = _dot_nt(wh, t_hi) + _dot_nt(wh, t_lo) + _dot_nt(wl, t_hi)
    n_exp, tm = logits.shape
    scores = _sigmoid(logits)
    biased = scores + rb_ref[...]
    gsz = n_exp // N_GROUPS
    neg = -jnp.inf

    b3 = biased.reshape(N_GROUPS, gsz, tm)
    im = lax.broadcasted_iota(jnp.int32, b3.shape, 1).astype(F32)
    m1 = jnp.max(b3, axis=1, keepdims=True)
    i1 = jnp.min(jnp.where(b3 == m1, im, float(gsz)), axis=1, keepdims=True)
    m2 = jnp.max(jnp.where(im == i1, neg, b3), axis=1, keepdims=True)
    gscore = (m1 + m2).reshape(N_GROUPS, tm)

    ig = lax.broadcasted_iota(jnp.int32, gscore.shape, 0).astype(F32)
    gsel = jnp.zeros_like(gscore)
    cur = gscore
    for _ in range(TOPK_GROUPS):
        _, hit = _pick_max(cur, ig)
        gsel = jnp.where(hit, 1.0, gsel)
        cur = jnp.where(hit, neg, cur)
    gsel3 = jnp.broadcast_to(gsel.reshape(N_GROUPS, 1, tm), b3.shape)
    cur = jnp.where(gsel3 > 0.0, b3, neg).reshape(n_exp, tm)

    ie = lax.broadcasted_iota(jnp.int32, (n_exp, tm), 0).astype(F32)
    sel = jnp.zeros((n_exp, tm), F32)
    picks, raw = [], []
    for _ in range(TOP_K):
        first, hit = _pick_max(cur, ie)
        picks.append(first)
        raw.append(jnp.sum(jnp.where(hit, scores, 0.0), axis=0, keepdims=True))
        sel = jnp.where(hit, 1.0, sel)
        cur = jnp.where(hit, neg, cur)
    total = raw[0]
    for r in raw[1:]:
        total = total + r

    ri = lax.broadcasted_iota(jnp.int32, (tm, tm), 0)
    ci = lax.broadcasted_iota(jnp.int32, (tm, tm), 1)
    upper = jnp.where(ri < ci, 1.0, 0.0).astype(BF16)
    rank = _dot(sel.astype(BF16), upper) + carry[...]
    ranks = [jnp.sum(jnp.where(ie == p, rank, 0.0), axis=0, keepdims=True) for p in picks]
    carry[...] = carry[...] + jnp.sum(sel, axis=1, keepdims=True)

    pad = jnp.zeros((SUBLANES - TOP_K, tm), F32)
    e_ref[...] = jnp.concatenate(picks + [pad], axis=0).astype(jnp.int32)
    gates = jnp.concatenate([r / total * ROUTED_SCALE for r in raw]
                            + [jnp.zeros((LANES - TOP_K, tm), F32)], axis=0)
    gt_ref[...] = gates.T[:, :SUBLANES]
    pos_ref[...] = jnp.concatenate(ranks + [pad], axis=0).astype(jnp.int32)
    cnt_ref[...] = jnp.broadcast_to(carry[...], cnt_ref.shape)


def _expert_kernel(be_ref, nv_ref, nu_ref, xs_hbm, wg_ref, wu_ref, wd_ref, ys_ref, wgu, wdn, xbuf, sem,
                   *, ff, n_sub):
    b = pl.program_id(0)
    n_used = nu_ref[0]
    depth, _, te, q = xbuf.shape

    def row_block_copy(blk):
        slot = blk % depth
        return pltpu.make_async_copy(xs_hbm.at[:, pl.ds(blk * te, te), :], xbuf.at[slot], sem.at[slot])

    @pl.when(b == 0)
    def _():
        for j in range(depth - 1):
            @pl.when(j < n_used)
            def _():
                row_block_copy(j).start()

    @pl.when(b + depth - 1 < n_used)
    def _():
        row_block_copy(b + depth - 1).start()

    changed = jnp.logical_or(b == 0, be_ref[b] != be_ref[jnp.maximum(b - 1, 0)])

    @pl.when(changed)
    def _():
        wgu[:, :ff] = wg_ref[...].astype(BF16)
        wgu[:, ff:] = wu_ref[...].astype(BF16)
        wdn[...] = wd_ref[...].astype(BF16)

    @pl.when(b < n_used)
    def _():
        row_block_copy(b).wait()
        slot = b % depth
        n_valid = nv_ref[b]
        ts = te // n_sub
        gus = []
        for a in range(n_sub):
            r0 = a * ts
            xw = jnp.concatenate([xbuf[slot, j, r0:r0 + ts, :] for j in range(SC_SPLIT)], axis=-1)
            row = lax.broadcasted_iota(jnp.int32, xw.shape, 0) + r0
            lo, hi = _unpack_halves(jnp.where(row < n_valid, xw, 0))
            half = lo.shape[1]
            gus.append(_dot(lo.astype(BF16), wgu[:half, :]) + _dot(hi.astype(BF16), wgu[half:, :]))
        outs = []
        for gu in gus:
            gate = gu[:, :ff]
            hmid = (gate * _sigmoid(gate) * gu[:, ff:]).astype(BF16)
            outs.append(_dot(hmid, wdn[...]))
        for a, y in enumerate(outs):
            words = _pack_halves(y)
            for j in range(SC_SPLIT):
                ys_ref[j, a * ts:(a + 1) * ts, :] = words[:, j * q:(j + 1) * q]


def _expert_call(block_e, n_valid, n_used, xs, layer, w_gate, w_up, w_down):
    _, n_rows, q = xs.shape
    d = 2 * SC_SPLIT * q
    ff = w_gate.shape[3]
    te = EXPERT_ROWS
    row_map = lambda b, be, nv, nu: (0, jnp.minimum(b, nu[0] - 1), 0)
    w_map = lambda b, be, nv, nu: (layer, be[b], 0, 0)
    grid_spec = pltpu.PrefetchScalarGridSpec(
        num_scalar_prefetch=3,
        grid=(n_rows // te,),
        in_specs=[pl.BlockSpec(memory_space=pl.ANY),
                  pl.BlockSpec((None, None, d, ff), w_map),
                  pl.BlockSpec((None, None, d, ff), w_map),
                  pl.BlockSpec((None, None, ff, d), w_map)],
        out_specs=pl.BlockSpec((SC_SPLIT, te, q), row_map),
        scratch_shapes=[pltpu.VMEM((d, 2 * ff), BF16), pltpu.VMEM((ff, d), BF16),
                        pltpu.VMEM((EXPERT_RING, SC_SPLIT, te, q), jnp.int32),
                        pltpu.SemaphoreType.DMA((EXPERT_RING,))])
    return pl.pallas_call(
        functools.partial(_expert_kernel, ff=ff, n_sub=EXPERT_ROWS // EXPERT_SUB_ROWS),
        grid_spec=grid_spec,
        out_shape=jax.ShapeDtypeStruct(xs.shape, jnp.int32),
        compiler_params=_cparams(1, 40),
        name="moe_experts",
    )(block_e, n_valid, n_used, xs, w_gate, w_up, w_down)


SC_WINDOW = 128


def _sc_mesh():
    return plsc.VectorSubcoreMesh(core_axis_name="core", subcore_axis_name="subcore")


def _sc_scatter_rows(rows, dests, n_out):
    n, width = rows.shape
    n_k = len(dests)
    assert n % SC_WINDOW == 0

    @functools.partial(pl.kernel, out_type=jax.ShapeDtypeStruct((n_out, width), rows.dtype),
                       mesh=_sc_mesh(), name="sc_dispatch_rows")
    def scatter_kernel(x_hbm, *refs):
        idx_hbm, o_hbm = refs[:n_k], refs[n_k]

        def body(x_vmem, *idx_vmem):
            for iv in idx_vmem:
                pltpu.sync_copy(x_vmem, o_hbm.at[iv.at[0]])

        pltpu.emit_pipeline(
            body,
            grid=(n // SC_WINDOW,),
            in_specs=[pl.BlockSpec((SC_WINDOW, width), lambda i: (i, 0))]
            + [pl.BlockSpec((1, SC_WINDOW), lambda i: (0, i))] * n_k,
            out_specs=[],
            core_axis_name=("core", "subcore"),
            dimension_semantics=(pltpu.PARALLEL,),
        )(x_hbm, *idx_hbm)

    return scatter_kernel(rows, *dests)


def _sc_gather_rows(table, idx):
    n = idx.shape[1]
    width = table.shape[1]
    assert n % SC_WINDOW == 0

    @functools.partial(pl.kernel, out_type=jax.ShapeDtypeStruct((n, width), table.dtype),
                       mesh=_sc_mesh(), name="sc_collect_rows")
    def gather_kernel(t_hbm, i_hbm, o_hbm):
        def body(i_vmem, o_vmem):
            pltpu.sync_copy(t_hbm.at[i_vmem.at[0]], o_vmem)

        pltpu.emit_pipeline(
            body,
            grid=(n // SC_WINDOW,),
            in_specs=[pl.BlockSpec((1, SC_WINDOW), lambda i: (0, i))],
            out_specs=[pl.BlockSpec((SC_WINDOW, width), lambda i: (i, 0))],
            core_axis_name=("core", "subcore"),
            dimension_semantics=(pltpu.PARALLEL,),
        )(i_hbm, o_hbm)

    return gather_kernel(table, idx)


def _shared_kernel(tok_ref, sgu_ref, sd_ref, o_ref, *, ff):
    gu = _dot(tok_ref[...], sgu_ref[...])
    gate = gu[:, :ff]
    hmid = (gate * _sigmoid(gate) * gu[:, ff:]).astype(BF16)
    o_ref[...] = _dot(hmid, sd_ref[...]).astype(BF16)


def _shared_call(tok, sgu, sd, *, rows, tm):
    d = tok.shape[1]
    ff = sd.shape[0]
    return pl.pallas_call(
        functools.partial(_shared_kernel, ff=ff),
        grid=(rows // tm,),
        in_specs=[pl.BlockSpec((tm, d), lambda i: (i, 0)),
                  pl.BlockSpec((d, 2 * ff), lambda i: (0, 0)),
                  pl.BlockSpec((ff, d), lambda i: (0, 0))],
        out_specs=pl.BlockSpec((tm, d), lambda i: (i, 0)),
        out_shape=jax.ShapeDtypeStruct((rows, d), BF16),
        compiler_params=_cparams(1, 40),
        name="moe_shared_expert",
    )(tok, sgu, sd)


def _moe_out_kernel(shared_ref, yg_ref, gk_ref, x_ref, g_ref, gate_ref, xo_ref):
    shared = shared_ref[...].astype(F32)
    gk = gk_ref[...]
    lo_acc, hi_acc = None, None
    for k in range(TOP_K):
        lo, hi = _unpack_halves(jnp.concatenate([yg_ref[j, k] for j in range(SC_SPLIT)], axis=-1))
        w = gk[:, k:k + 1]
        lo_acc = lo * w if lo_acc is None else lo_acc + lo * w
        hi_acc = hi * w if hi_acc is None else hi_acc + hi * w
    y = jnp.concatenate([lo_acc, hi_acc], axis=-1) + shared
    xo_ref[...] = x_ref[...] + gate_ref[...] * _rms(y, g_ref[...])


def _moe_out_call(shared, yg, gk, xs, g, gate, *, rows, tm, seq, n_batch):
    d = xs.shape[1]
    seg = _seg_map(tm, seq, n_batch)
    row = lambda: pl.BlockSpec((tm, d), lambda i: (i, 0))
    return pl.pallas_call(
        _moe_out_kernel,
        grid=(rows // tm,),
        in_specs=[row(),
                  pl.BlockSpec((SC_SPLIT, TOP_K, tm, d // 2 // SC_SPLIT), lambda i: (0, 0, i, 0)),
                  pl.BlockSpec((tm, SUBLANES), lambda i: (i, 0)),
                  row(),
                  pl.BlockSpec((1, d), lambda i: (0, 0)),
                  pl.BlockSpec((None, 1, d), seg)],
        out_specs=row(),
        out_shape=jax.ShapeDtypeStruct((rows, d), F32),
        compiler_params=_cparams(1, 48),
        name="moe_combine",
    )(shared, yg, gk, xs, g, gate)


def _dest_kernel(e_ref, pos_ref, ps_ref, o_ref, *, n_rows):
    e = e_ref[...].astype(F32)
    ps = ps_ref[...]
    n_exp, tm = ps.shape[0], e.shape[1]
    ids = lax.broadcasted_iota(jnp.int32, (n_exp, tm), 0).astype(F32)
    rows = [jnp.sum(jnp.where(ids == e[k:k + 1, :], ps, 0.0), axis=0, keepdims=True) for k in range(SUBLANES)]
    dest = jnp.concatenate(rows, axis=0).astype(jnp.int32) + pos_ref[...]
    for j in range(SC_SPLIT):
        o_ref[j] = dest + j * n_rows


def _dest_call(e_tk, pos_tk, pad_starts, *, n_rows, tm):
    rows = e_tk.shape[1]
    n_exp = pad_starts.shape[0]
    lane_spec = pl.BlockSpec((SUBLANES, tm), lambda i: (0, i))
    return pl.pallas_call(
        functools.partial(_dest_kernel, n_rows=n_rows),
        grid=(rows // tm,),
        in_specs=[lane_spec, lane_spec, pl.BlockSpec((n_exp, 1), lambda i: (0, 0))],
        out_specs=pl.BlockSpec((SC_SPLIT, SUBLANES, tm), lambda i: (0, 0, i)),
        out_shape=jax.ShapeDtypeStruct((SC_SPLIT, SUBLANES, rows), jnp.int32),
        compiler_params=_cparams(1, 32),
        name="moe_dest_rows",
    )(e_tk, pos_tk, pad_starts.astype(F32).reshape(n_exp, 1))


def _moe_layer(xs, tok, tokp, routing, g_out, gate, layer, w_gate, w_up, w_down, sh_gate, sh_up, sh_down,
               *, rows, tm, seq, n_batch):
    e_tk, g_tk, pos_tk, counts = routing
    n_exp = counts.shape[0]
    te = EXPERT_ROWS
    cnt = counts[:, 0].astype(jnp.int32)
    padded = (cnt + te - 1) // te * te
    pad_ends = jnp.cumsum(padded)
    pad_starts = pad_ends - padded
    expert_ids = jnp.arange(n_exp, dtype=jnp.int32)
    n_blocks = -(-(rows * TOP_K + n_exp * (te - 1)) // te)
    n_used = pad_ends[-1] // te
    blk = jnp.arange(n_blocks, dtype=jnp.int32)
    last = jnp.minimum(blk, n_used - 1) * te
    owner = (last[:, None] >= pad_ends[None, :]).astype(jnp.int32)
    block_e = jnp.minimum(jnp.sum(owner, axis=1), n_exp - 1)
    own_hot = block_e[:, None] == expert_ids[None, :]
    n_valid = jnp.sum(jnp.where(own_hot, (pad_starts + cnt)[None, :], 0), axis=1) - last
    n_valid = jnp.clip(n_valid, 0, te)
    q = tokp.shape[2]
    n_rows = n_blocks * te
    piece = _dest_call(e_tk, pos_tk, pad_starts, n_rows=n_rows, tm=tm)[:, :TOP_K, :]
    xg = _sc_scatter_rows(tokp.reshape(SC_SPLIT * rows, q),
                          [piece[:, k, :].reshape(1, SC_SPLIT * rows) for k in range(TOP_K)],
                          SC_SPLIT * n_rows).reshape(SC_SPLIT, n_rows, q)
    sgu = jnp.concatenate([sh_gate, sh_up], axis=1).astype(BF16)
    shared = _shared_call(tok, sgu, sh_down.astype(BF16), rows=rows, tm=tm)
    ys = _expert_call(block_e, n_valid, n_used.reshape(1).astype(jnp.int32), xg, layer, w_gate, w_up, w_down)
    yg = _sc_gather_rows(ys.reshape(SC_SPLIT * n_rows, q),
                         piece.reshape(1, -1)).reshape(SC_SPLIT, TOP_K, rows, q)
    return _moe_out_call(shared, yg, g_tk, xs, g_out, gate, rows=rows, tm=tm, seq=seq, n_batch=n_batch)


def _rope(t, cos, sin):
    half = DA_QK_DIM // 2
    up = pltpu.roll(t, shift=LANES - half, axis=1)
    dn = pltpu.roll(t, shift=half, axis=1)
    lane = lax.broadcasted_iota(jnp.int32, t.shape, 1) % DA_QK_DIM
    return t * cos + jnp.where(lane < half, -up, dn) * sin


def _qkv_kernel(x_ref, g_ref, sh_ref, sc_ref, w_ref, cos_ref, sin_ref, q_ref, k_ref, v_ref, *, qkw):
    h = _normmod(x_ref[...], g_ref[...], sh_ref[...], sc_ref[...]).astype(BF16)
    qkv = _dot(h, w_ref[...])
    cos, sin = cos_ref[...], sin_ref[...]
    q_scale = DA_QK_DIM ** -0.5 * math.log2(math.e)
    for hb in range(qkw // LANES):
        lo, hi = hb * LANES, (hb + 1) * LANES
        q_ref[:, lo:hi] = (_rope(qkv[:, lo:hi], cos, sin) * q_scale).astype(BF16)
        k_ref[:, lo:hi] = _rope(qkv[:, qkw + lo:qkw + hi], cos, sin).astype(BF16)
    v_ref[...] = qkv[:, 2 * qkw:].astype(BF16)


def _qkv_call(xs, g, shift, scale, w_qkv, cos_t, sin_t, *, qkw, tm, seq, ctx_len, n_batch):
    rows, d = xs.shape
    n = w_qkv.shape[1]
    vw = n - 2 * qkw
    lat_tiles, lat_tps, ctx_tps = n_batch * seq // tm, seq // tm, ctx_len // tm
    seg = _seg_map(tm, seq, n_batch)

    def kv_map(i):
        c = i - lat_tiles
        is_lat = i < lat_tiles
        return (jnp.where(is_lat, i // lat_tps, c // ctx_tps),
                jnp.where(is_lat, ctx_tps + i % lat_tps, c % ctx_tps), 0)

    rope_map = lambda i: (jnp.where(i < lat_tiles, i % lat_tps, lat_tps), 0)
    return pl.pallas_call(
        functools.partial(_qkv_kernel, qkw=qkw),
        grid=(rows // tm,),
        in_specs=[pl.BlockSpec((tm, d), lambda i: (i, 0)),
                  pl.BlockSpec((1, d), lambda i: (0, 0)),
                  pl.BlockSpec((None, 1, d), seg),
                  pl.BlockSpec((None, 1, d), seg),
                  pl.BlockSpec((d, n), lambda i: (0, 0)),
                  pl.BlockSpec((tm, LANES), rope_map),
                  pl.BlockSpec((tm, LANES), rope_map)],
        out_specs=[pl.BlockSpec((tm, qkw), lambda i: (i, 0)),
                   pl.BlockSpec((None, tm, qkw), kv_map),
                   pl.BlockSpec((None, tm, vw), kv_map)],
        out_shape=[jax.ShapeDtypeStruct((rows, qkw), BF16),
                   jax.ShapeDtypeStruct((n_batch, ctx_len + seq, qkw), BF16),
                   jax.ShapeDtypeStruct((n_batch, ctx_len + seq, vw), BF16)],
        compiler_params=_cparams(1, 48),
        name="qkv_rope",
    )(xs, g, shift, scale, w_qkv, cos_t, sin_t)


def _rope_tables(seq, tm):
    pos = np.arange(seq)
    n_freq = DA_QK_DIM // 4
    inv = np.power(ROPE_BASE, -np.arange(n_freq, dtype=np.float32) / n_freq).astype(np.float32)
    row = (pos // GRID_W).astype(np.float32)[:, None] * inv
    col = (pos % GRID_W).astype(np.float32)[:, None] * inv
    ang = np.concatenate([row, col], axis=-1).astype(np.float32)
    ang = np.tile(ang, (1, LANES // ang.shape[1]))
    cos = np.concatenate([np.cos(ang), np.ones((tm, LANES))], axis=0)
    sin = np.concatenate([np.sin(ang), np.zeros((tm, LANES))], axis=0)
    return jnp.asarray(cos, F32), jnp.asarray(sin, F32)


def _attn_kernel(lp_ref, q_ref, k_ref, v_ref, sg_ref, o_ref, vext, *, n_sub, lam_init):
    @pl.when(pl.program_id(2) == 0)
    def _():
        vext[:, :LANES] = v_ref[...]
        vext[:, LANES:] = jnp.ones((vext.shape[0], LANES), BF16)

    lp = lp_ref[...]
    lam = (jnp.exp(jnp.sum(lp[0:1] * lp[1:2], axis=1, keepdims=True))
           - jnp.exp(jnp.sum(lp[2:3] * lp[3:4], axis=1, keepdims=True)) + lam_init)
    ts = q_ref.shape[0] // n_sub
    lane = lax.broadcasted_iota(jnp.int32, (ts, LANES), 1)
    zero = jnp.zeros((ts, LANES), BF16)
    k = k_ref[...]
    scores = []
    for a in range(n_sub):
        q = q_ref[a * ts:(a + 1) * ts, :]
        qq = jnp.concatenate([jnp.where(lane < DA_QK_DIM, q, zero),
                              jnp.where(lane >= DA_QK_DIM, q, zero)], axis=0)
        scores.append(_dot_nt(qq, k))
    for a in range(n_sub):
        s = scores[a]
        p = jnp.exp2(s - jnp.max(s, axis=-1, keepdims=True)).astype(BF16)
        oe = _dot(p, vext[...])
        on = oe[:, :LANES] / oe[:, LANES:LANES + 1]
        o = on[:ts] - lam * on[ts:]
        o_ref[a * ts:(a + 1) * ts, :] = (_rms(o, sg_ref[...]) * (1.0 - lam_init)).astype(BF16)


def _attn_call(lam_p, q, k_all, v_all, subln_g, *, tq, seq, n_batch, lam_init):
    n_heads = q.shape[1] // LANES
    lk = k_all.shape[1]
    qt = seq // tq
    return pl.pallas_call(
        functools.partial(_attn_kernel, n_sub=tq // ATTN_SUB_ROWS, lam_init=lam_init),
        grid=(n_batch, n_heads, qt),
        in_specs=[pl.BlockSpec(lam_p.shape, lambda b, h, i: (0, 0)),
                  pl.BlockSpec((tq, LANES), lambda b, h, i: (b * qt + i, h)),
                  pl.BlockSpec((None, lk, LANES), lambda b, h, i: (b, 0, h)),
                  pl.BlockSpec((None, lk, LANES), lambda b, h, i: (b, 0, h)),
                  pl.BlockSpec((1, LANES), lambda b, h, i: (0, 0))],
        out_specs=pl.BlockSpec((tq, LANES), lambda b, h, i: (b * qt + i, h)),
        out_shape=jax.ShapeDtypeStruct((n_batch * seq, n_heads * LANES), BF16),
        scratch_shapes=[pltpu.VMEM((lk, 2 * LANES), BF16)],
        compiler_params=_cparams(3, 48),
        name="diff_attention",
    )(lam_p, q, k_all, v_all, subln_g)


def kernel(x, c, ctx, c_ctx, ada_w, ada_b, norm_g, cf_w_in, cf_conv_w, cf_conv_b, cf_ln_g, cf_ln_b,
           cf_w_out, da_w_qkv, da_lambda, da_subln_g, da_w_out, moe_router_w, moe_router_b,
           moe_w_gate, moe_w_up, moe_w_down, moe_sh_gate, moe_sh_up, moe_sh_down):
    n_batch, seq, d = x.shape
    ctx_len = ctx.shape[1]
    depth = ada_w.shape[0]
    assert depth == 2 and n_batch + 1 <= N_SEG
    assert DA_QK_DIM * 2 == LANES and da_subln_g.shape[1] == LANES
    rows_lat, rows_ctx = n_batch * seq, n_batch * ctx_len
    tm = 512
    tq = 256
    assert seq % tm == 0 and rows_ctx % tm == 0 and seq % tq == 0 and ctx_len % tq == 0

    x_lat, x_ctx = x.reshape(rows_lat, d), ctx.reshape(rows_ctx, d)
    cond = jnp.concatenate([c, c_ctx[None, :], jnp.zeros((N_SEG - n_batch - 1, d), F32)], axis=0)
    mods = _mods_call(cond, ada_w, ada_b)
    mod = lambda layer, k: mods[layer, :, k * d:(k + 1) * d][:, None, :]
    gain = lambda layer, k: norm_g[layer, k][None, :]
    common = dict(seq=seq, n_batch=n_batch)

    cc = cf_conv_w.shape[-1]
    fc = cf_w_in.shape[2] - 2 * cc
    gw = fc // FOURIER_GROUPS
    ch_ang = 2.0 * np.pi * (np.outer(np.arange(gw), np.arange(gw)) % gw) / gw
    cs = jnp.asarray(np.concatenate([np.cos(ch_ang), np.sin(ch_ang)], axis=1) / math.sqrt(gw), BF16)
    a_glu, ab = _inproj_call(x_lat, x_ctx, gain(0, 0), mod(0, 0), mod(0, 1), cf_w_in[0].astype(BF16), cs,
                             cc=cc, gw=gw, tm=tm, **common)
    gsz = cc // CONV_GROUPS
    gid = np.arange(cc) // gsz
    gmean = jnp.asarray((gid[:, None] == gid[None, :]) / gsz, BF16)
    a_act = _conv_call(a_glu, cf_conv_w[0][:, 0, :], cf_conv_b[0][None, :], cf_ln_g[0][None, :],
                       cf_ln_b[0][None, :], gmean, tl=tq, seq=seq, ctx_len=ctx_len, n_batch=n_batch)
    c_lat, s_lat = _dft_tables(seq, 1.0 / math.sqrt(seq))
    c_ctx_m, s_ctx_m = _dft_tables(ctx_len, 1.0 / math.sqrt(ctx_len))
    fr = jnp.concatenate([
        _seqdft_call(ab, c_lat, s_lat, length=seq, row0=0, n_seq=n_batch, fc=fc, tm=min(seq, 256)),
        _seqdft_call(ab, c_ctx_m, s_ctx_m, length=ctx_len, row0=rows_lat, n_seq=n_batch, fc=fc,
                     tm=min(ctx_len, 256))], axis=0)
    w_out = cf_w_out[0].astype(BF16)
    xs, tok, tokp, *routing = _outproj_call(
        [a_act, fr], [w_out[:cc], w_out[cc:]], x_lat, x_ctx, gain(0, 1), mod(0, 2), gain(0, 2), mod(0, 3),
        mod(0, 4), moe_router_w[0], moe_router_b[0], rows=rows_lat + rows_ctx, tm=tm, **common)
    xs = _moe_layer(xs, tok, tokp, routing, gain(0, 3), mod(0, 5), 0, moe_w_gate, moe_w_up, moe_w_down,
                    moe_sh_gate[0], moe_sh_up[0], moe_sh_down[0], rows=rows_lat + rows_ctx, tm=tm, **common)

    qkw = DA_HEADS * 2 * DA_QK_DIM
    lam_init = 0.8 - 0.6 * math.exp(-0.3 * 1)
    cos_t, sin_t = _rope_tables(seq, tq)
    q, k_all, v_all = _qkv_call(xs, gain(1, 0), mod(1, 0), mod(1, 1), da_w_qkv[0].astype(BF16),
                                cos_t, sin_t, qkw=qkw, tm=tq, ctx_len=ctx_len, **common)
    o = _attn_call(da_lambda[0], q, k_all, v_all, da_subln_g[0][None, :], tq=math.gcd(seq, ATTN_ROWS), seq=seq,
                   n_batch=n_batch, lam_init=lam_init)
    xs, tok, tokp, *routing = _outproj_call(
        [o], [da_w_out[0].astype(BF16)], xs, xs, gain(1, 1), mod(1, 2), gain(1, 2), mod(1, 3), mod(1, 4),
        moe_router_w[1], moe_router_b[1], rows=rows_lat, tm=tm, **common)
    xs = _moe_layer(xs, tok, tokp, routing, gain(1, 3), mod(1, 5), 1, moe_w_gate, moe_w_up, moe_w_down,
                    moe_sh_gate[1], moe_sh_up[1], moe_sh_down[1], rows=rows_lat, tm=tm, **common)
    return xs.reshape(n_batch, seq, d)
```

```python
import functools
import math

import numpy as np
import jax
import jax.numpy as jnp
from jax import lax
from jax.experimental import pallas as pl
from jax.experimental.pallas import tpu as pltpu
from jax.experimental.pallas import tpu_sc as plsc

F32 = jnp.float32
BF16 = jnp.bfloat16

EPS = 1e-6
GRID_W = 64
CONV_GROUPS = 8
FOURIER_GROUPS = 4
DA_HEADS = 8
DA_QK_DIM = 64
ROPE_BASE = 10000.0
N_GROUPS = 8
TOPK_GROUPS = 4
TOP_K = 6
ROUTED_SCALE = 2.5

LANES = 128
SUBLANES = 8
N_SEG = 16
HALO = 16
EXPERT_ROWS = 1024
EXPERT_SUB_ROWS = 128
EXPERT_RING = 3
ATTN_ROWS = 1024
ATTN_SUB_ROWS = 128
MIB = 1024 * 1024


def _cparams(n_axes, vmem_mib):
    return pltpu.CompilerParams(dimension_semantics=("arbitrary",) * n_axes,
                                vmem_limit_bytes=vmem_mib * MIB)


def _sigmoid(v):
    return 1.0 / (1.0 + jnp.exp(-v))


def _rms(v, g):
    return v * lax.rsqrt(jnp.mean(v * v, axis=-1, keepdims=True) + EPS) * g


def _normmod(v, g, shift, scale):
    return _rms(v, g) * (1.0 + scale) + shift


def _split_bf16(v):
    hi = v.astype(BF16)
    lo = (v - hi.astype(F32)).astype(BF16)
    return hi, lo


def _pack_halves(v):
    half = v.shape[1] // 2
    word = pltpu.pack_elementwise([v[:, :half], v[:, half:]], packed_dtype=BF16)
    return lax.bitcast_convert_type(word, jnp.int32)


def _unpack_halves(w):
    u = lax.bitcast_convert_type(w, jnp.uint32)
    lo = lax.bitcast_convert_type(u << 16, F32)
    hi = lax.bitcast_convert_type(u & jnp.uint32(0xFFFF0000), F32)
    return lo, hi


SC_SPLIT = 2


def _store_pieces(ref, words):
    q = words.shape[1] // SC_SPLIT
    for j in range(SC_SPLIT):
        ref[j] = words[:, j * q:(j + 1) * q]


def _load_pieces(ref):
    return jnp.concatenate([ref[j] for j in range(SC_SPLIT)], axis=-1)


def _dot(a, b):
    return jnp.dot(a, b, preferred_element_type=F32)


def _dot_nt(a, b):
    return lax.dot_general(a, b, (((1,), (1,)), ((), ())), preferred_element_type=F32)


def _mods_kernel(c_ref, w_ref, b_ref, o_ref):
    cv = c_ref[...]
    o_ref[...] = _dot(cv * _sigmoid(cv), w_ref[...]) + b_ref[...]


def _mods_call(cond, ada_w, ada_b):
    depth, d, n = ada_w.shape
    tn = n // 4
    return pl.pallas_call(
        _mods_kernel,
        grid=(depth, n // tn),
        in_specs=[pl.BlockSpec((N_SEG, d), lambda l, j: (0, 0)),
                  pl.BlockSpec((None, d, tn), lambda l, j: (l, 0, j)),
                  pl.BlockSpec((None, 1, tn), lambda l, j: (l, 0, j))],
        out_specs=pl.BlockSpec((None, N_SEG, tn), lambda l, j: (l, 0, j)),
        out_shape=jax.ShapeDtypeStruct((depth, N_SEG, n), F32),
        compiler_params=_cparams(2, 40),
        name="adaln_mods",
    )(cond, ada_w, ada_b.reshape(depth, 1, n))


def _two_source_specs(tm, d, n_head_tiles):
    return (pl.BlockSpec((tm, d), lambda i: (jnp.minimum(i, n_head_tiles - 1), 0)),
            pl.BlockSpec((tm, d), lambda i: (jnp.maximum(i - n_head_tiles, 0), 0)))


def _pick_rows(head_ref, tail_ref, n_head_tiles):
    return jnp.where(pl.program_id(0) < n_head_tiles, head_ref[...], tail_ref[...])


def _inproj_kernel(xh_ref, xt_ref, g_ref, sh_ref, sc_ref, w_ref, cs_ref, a_ref, ab_ref, *, cc, gw, n_head):
    x = _pick_rows(xh_ref, xt_ref, n_head)
    h = _normmod(x, g_ref[...], sh_ref[...], sc_ref[...]).astype(BF16)
    u = _dot(h, w_ref[...])
    a_ref[...] = u[:, :cc] * _sigmoid(u[:, cc:2 * cc])
    f = u[:, 2 * cc:].astype(BF16)
    n_g = f.shape[1] // gw
    parts = [_dot(f[:, g * gw:(g + 1) * gw], cs_ref[...]) for g in range(n_g)]
    cos_part = [p[:, :gw] for p in parts]
    sin_part = [p[:, gw:] for p in parts]
    ab_ref[...] = jnp.concatenate(cos_part + sin_part, axis=-1).astype(BF16)


def _seg_map(tm, seq, n_batch):
    return lambda i: (jnp.minimum(i * tm // seq, n_batch), 0, 0)


def _inproj_call(x_head, x_tail, g, shift, scale, w_in, cs, *, cc, gw, tm, seq, n_batch):
    d = x_head.shape[1]
    rows = x_head.shape[0] + x_tail.shape[0]
    n_head = x_head.shape[0] // tm
    n = w_in.shape[1]
    fc = n - 2 * cc
    seg = _seg_map(tm, seq, n_batch)
    return pl.pallas_call(
        functools.partial(_inproj_kernel, cc=cc, gw=gw, n_head=n_head),
        grid=(rows // tm,),
        in_specs=[*_two_source_specs(tm, d, n_head),
                  pl.BlockSpec((1, d), lambda i: (0, 0)),
                  pl.BlockSpec((None, 1, d), seg),
                  pl.BlockSpec((None, 1, d), seg),
                  pl.BlockSpec((d, n), lambda i: (0, 0)),
                  pl.BlockSpec((gw, 2 * gw), lambda i: (0, 0))],
        out_specs=[pl.BlockSpec((tm, cc), lambda i: (i, 0)),
                   pl.BlockSpec((tm, 2 * fc), lambda i: (i, 0))],
        out_shape=[jax.ShapeDtypeStruct((rows, cc), F32),
                   jax.ShapeDtypeStruct((rows, 2 * fc), BF16)],
        compiler_params=_cparams(1, 40),
        name="inproj_glu_chdft",
    )(x_head, x_tail, g, shift, scale, w_in, cs)


def _conv_kernel(prev_ref, main_ref, next_ref, w_ref, cb_ref, lg_ref, lb_ref, gm_ref, o_ref,
                 buf, cv, shifted, *, tl, width, lat_tiles, lat_tps, ctx_tps, chunk):
    i = pl.program_id(0)
    is_lat = i < lat_tiles
    tps = jnp.where(is_lat, lat_tps, ctx_tps)
    j = jnp.where(is_lat, i, i - lat_tiles) % tps
    zero = jnp.zeros((HALO, buf.shape[1]), F32)
    buf[0:HALO, :] = jnp.where(j > 0, prev_ref[...], zero)
    buf[HALO:HALO + tl, :] = main_ref[...]
    buf[HALO + tl:HALO + tl + HALO, :] = jnp.where(j < tps - 1, next_ref[...], zero)
    base = HALO - width // 2
    span = shifted.shape[1]
    for s in range(1, SUBLANES):
        shifted[s] = buf[s:s + span, :]

    def window(off, r0):
        s, m = off % SUBLANES, off - off % SUBLANES
        if s == 0:
            return buf[m + r0:m + r0 + chunk, :]
        return shifted[s, m + r0:m + r0 + chunk, :]

    for r0 in range(0, tl, chunk):
        acc = window(base, r0) * w_ref[0:1, :]
        for t in range(1, width):
            acc = acc + window(base + t, r0) * w_ref[t:t + 1, :]
        cv[r0:r0 + chunk, :] = acc
    a = cv[...] + cb_ref[...]
    gm = gm_ref[...]
    a_hi, a_lo = _split_bf16(a)
    mu = _dot(a_hi, gm) + _dot(a_lo, gm)
    dl = a - mu
    q_hi, q_lo = _split_bf16(dl * dl)
    var = _dot(q_hi, gm) + _dot(q_lo, gm)
    y = dl * lax.rsqrt(var + EPS) * lg_ref[...] + lb_ref[...]
    o_ref[...] = (y * _sigmoid(y)).astype(BF16)


def _conv_call(a_glu, conv_w, conv_b, ln_g, ln_b, gmean, *, tl, seq, ctx_len, n_batch):
    rows, ch = a_glu.shape
    width = conv_w.shape[0]
    assert width // 2 <= HALO and tl % HALO == 0
    hb = tl // HALO
    last_halo = rows // HALO - 1
    return pl.pallas_call(
        functools.partial(_conv_kernel, tl=tl, width=width, lat_tiles=n_batch * seq // tl,
                          lat_tps=seq // tl, ctx_tps=ctx_len // tl, chunk=32),
        grid=(rows // tl,),
        in_specs=[pl.BlockSpec((HALO, ch), lambda i: (jnp.maximum(i * hb - 1, 0), 0)),
                  pl.BlockSpec((tl, ch), lambda i: (i, 0)),
                  pl.BlockSpec((HALO, ch), lambda i: (jnp.minimum((i + 1) * hb, last_halo), 0)),
                  pl.BlockSpec((width, ch), lambda i: (0, 0)),
                  pl.BlockSpec((1, ch), lambda i: (0, 0)),
                  pl.BlockSpec((1, ch), lambda i: (0, 0)),
                  pl.BlockSpec((1, ch), lambda i: (0, 0)),
                  pl.BlockSpec((ch, ch), lambda i: (0, 0))],
        out_specs=pl.BlockSpec((tl, ch), lambda i: (i, 0)),
        out_shape=jax.ShapeDtypeStruct((rows, ch), BF16),
        scratch_shapes=[pltpu.VMEM((tl + 2 * HALO, ch), F32), pltpu.VMEM((tl, ch), F32),
                        pltpu.VMEM((SUBLANES, tl + 2 * HALO - SUBLANES, ch), F32)],
        compiler_params=_cparams(1, 40),
        name="dwconv_groupln_swish",
    )(a_glu, a_glu, a_glu, conv_w, conv_b, ln_g, ln_b, gmean)


EDGE_ROWS = 16


def _seqdft_kernel(c_ref, s_ref, cx_ref, sx_ref, jm_ref, a_ref, b_ref, o_ref, *, tm, n_tiles):
    i = pl.program_id(1)
    a, b = a_ref[...], b_ref[...]
    p, qv = _dot(c_ref[...], a), _dot(s_ref[...], b)
    o_ref[pl.ds(pl.multiple_of(i * tm, tm), tm), :] = (p - qv).astype(BF16)
    edge = _dot(cx_ref[...], a) + _dot(sx_ref[...], b)
    pick = lax.broadcasted_iota(jnp.int32, edge.shape, 0) == i
    edge_row = jnp.sum(jnp.where(pick, edge, 0.0), axis=0, keepdims=True)
    mirrored = _dot(jm_ref[...], (p + qv).astype(BF16))
    srow = lax.broadcasted_iota(jnp.int32, mirrored.shape, 0)
    mirrored = jnp.where(srow == 0, edge_row, mirrored)
    o_ref[pl.ds(pl.multiple_of((n_tiles - 1 - i) * tm, tm), tm), :] = mirrored.astype(BF16)


def _seqdft_call(ab, length, *, row0, n_seq, fc, tm):
    n_tiles = length // tm
    n_half = n_tiles // 2
    assert n_tiles % 2 == 0 and n_half <= EDGE_ROWS
    scale = 1.0 / math.sqrt(length)
    cmat, smat = _dft_tables(length, scale, np.arange(length // 2))
    edge_k = np.zeros((EDGE_ROWS,), np.int64)
    edge_k[:n_half] = tm * (np.arange(n_half) + 1)
    cx, sx = _dft_tables(length, scale, edge_k)
    s_idx = np.arange(tm)
    reversal = jnp.asarray((s_idx[None, :] == (tm - s_idx)[:, None]) & (s_idx[:, None] > 0), BF16)
    seq0 = row0 // length
    whole = lambda arr: pl.BlockSpec(arr.shape, lambda b, i: (0, 0))
    return pl.pallas_call(
        functools.partial(_seqdft_kernel, tm=tm, n_tiles=n_tiles),
        grid=(n_seq, n_half),
        in_specs=[pl.BlockSpec((tm, length), lambda b, i: (i, 0)),
                  pl.BlockSpec((tm, length), lambda b, i: (i, 0)),
                  whole(cx), whole(sx), whole(reversal),
                  pl.BlockSpec((length, fc), lambda b, i: (seq0 + b, 0)),
                  pl.BlockSpec((length, fc), lambda b, i: (seq0 + b, 1))],
        out_specs=pl.BlockSpec((length, fc), lambda b, i: (b, 0)),
        out_shape=jax.ShapeDtypeStruct((n_seq * length, fc), BF16),
        compiler_params=_cparams(2, 48),
        name="seq_dft",
    )(cmat, smat, cx, sx, reversal, ab, ab)


def _dft_tables(length, scale, rows):
    k = np.asarray(rows, np.int64)[:, None]
    if length <= 512:
        ang = 2.0 * np.pi * ((k * np.arange(length)[None, :]) % length) / length
        return (jnp.asarray(np.cos(ang) * scale, BF16), jnp.asarray(np.sin(ang) * scale, BF16))
    r = 64
    assert length % r == 0
    alpha = 2.0 * np.pi * ((k * np.arange(length // r)[None, :] * r) % length) / length
    beta = 2.0 * np.pi * ((k * np.arange(r)[None, :]) % length) / length
    ca, sa = jnp.asarray(np.cos(alpha), F32)[:, :, None], jnp.asarray(np.sin(alpha), F32)[:, :, None]
    cb, sb = jnp.asarray(np.cos(beta) * scale, F32)[:, None, :], jnp.asarray(np.sin(beta) * scale, F32)[:, None, :]
    cmat = (ca * cb - sa * sb).reshape(k.shape[0], length).astype(BF16)
    smat = (sa * cb + ca * sb).reshape(k.shape[0], length).astype(BF16)
    return cmat, smat


def _outproj_kernel(*refs, n_in, n_head):
    ins, ws = refs[:n_in], refs[n_in:2 * n_in]
    (xh_ref, xt_ref, g1_ref, gate_ref, g2_ref, sh_ref, sc_ref, wh_ref, wl_ref, rb_ref,
     xo_ref, tok_ref, tokp_ref, e_ref, gt_ref, pos_ref, cnt_ref, carry) = refs[2 * n_in:]
    y = _dot(ins[0][...], ws[0][...])
    for a_ref, w_ref in zip(ins[1:], ws[1:]):
        y = y + _dot(a_ref[...], w_ref[...])
    x1 = _pick_rows(xh_ref, xt_ref, n_head) + gate_ref[...] * _rms(y, g1_ref[...])
    xo_ref[...] = x1
    tok = _normmod(x1, g2_ref[...], sh_ref[...], sc_ref[...])
    tok_ref[...] = tok.astype(BF16)
    _store_pieces(tokp_ref, _pack_halves(tok))
    _route(tok, wh_ref, wl_ref, rb_ref, e_ref, gt_ref, pos_ref, cnt_ref, carry)


def _outproj_call(ins, ws, x_head, x_tail, g1, gate, g2, shift, scale, router_w, router_b,
                  *, rows, tm, seq, n_batch):
    d = x_head.shape[1]
    n_head = min(x_head.shape[0], rows) // tm
    n_exp = router_w.shape[1]
    w_t = router_w.T
    w_hi = w_t.astype(BF16)
    w_lo = (w_t - w_hi.astype(F32)).astype(BF16)
    seg = _seg_map(tm, seq, n_batch)
    n_in = len(ins)
    row_spec = lambda a: pl.BlockSpec((tm, a.shape[1]), lambda i: (i, 0))
    full_spec = lambda a: pl.BlockSpec(a.shape, lambda i: (0, 0))
    lane_spec = pl.BlockSpec((SUBLANES, tm), lambda i: (0, i))
    tok_rows = lambda dt: jax.ShapeDtypeStruct((SUBLANES, rows), dt)
    return pl.pallas_call(
        functools.partial(_outproj_kernel, n_in=n_in, n_head=n_head),
        grid=(rows // tm,),
        in_specs=[row_spec(a) for a in ins] + [full_spec(w) for w in ws] + [
            *_two_source_specs(tm, d, n_head), full_spec(g1), pl.BlockSpec((None, 1, d), seg), full_spec(g2),
            pl.BlockSpec((None, 1, d), seg), pl.BlockSpec((None, 1, d), seg),
            full_spec(w_hi), full_spec(w_lo), pl.BlockSpec((n_exp, 1), lambda i: (0, 0))],
        out_specs=[pl.BlockSpec((tm, d), lambda i: (i, 0)), pl.BlockSpec((tm, d), lambda i: (i, 0)),
                   pl.BlockSpec((SC_SPLIT, tm, d // 2 // SC_SPLIT), lambda i: (0, i, 0)),
                   lane_spec, pl.BlockSpec((tm, SUBLANES), lambda i: (i, 0)), lane_spec,
                   pl.BlockSpec((n_exp, LANES), lambda i: (0, 0))],
        out_shape=[jax.ShapeDtypeStruct((rows, d), F32), jax.ShapeDtypeStruct((rows, d), BF16),
                   jax.ShapeDtypeStruct((SC_SPLIT, rows, d // 2 // SC_SPLIT), jnp.int32),
                   tok_rows(jnp.int32), jax.ShapeDtypeStruct((rows, SUBLANES), F32), tok_rows(jnp.int32),
                   jax.ShapeDtypeStruct((n_exp, LANES), F32)],
        scratch_shapes=[pltpu.VMEM((n_exp, 1), F32)],
        compiler_params=_cparams(1, 48),
        name="outproj_residual_route",
    )(*ins, *ws, x_head, x_tail, g1, gate, g2, shift, scale, w_hi, w_lo, router_b.reshape(n_exp, 1))


def _pick_max(cur, idx):
    mx = jnp.max(cur, axis=0, keepdims=True)
    first = jnp.min(jnp.where(cur == mx, idx, float(cur.shape[0])), axis=0, keepdims=True)
    return first, idx == first


def _route(tok, wh_ref, wl_ref, rb_ref, e_ref, gt_ref, pos_ref, cnt_ref, carry):
    @pl.when(pl.program_id(0) == 0)
    def _():
        carry[...] = jnp.zeros_like(carry)

    t_hi, t_lo = _split_bf16(tok)
    wh, wl = wh_ref[...], wl_ref[...]
    logits = _dot_nt(wh, t_hi) + _dot_nt(wh, t_lo) + _dot_nt(wl, t_hi)
    n_exp, tm = logits.shape
    scores = _sigmoid(logits)
    biased = scores + rb_ref[...]
    gsz = n_exp // N_GROUPS
    neg = -jnp.inf

    b3 = biased.reshape(N_GROUPS, gsz, tm)
    im = lax.broadcasted_iota(jnp.int32, b3.shape, 1).astype(F32)
    m1 = jnp.max(b3, axis=1, keepdims=True)
    i1 = jnp.min(jnp.where(b3 == m1, im, float(gsz)), axis=1, keepdims=True)
    m2 = jnp.max(jnp.where(im == i1, neg, b3), axis=1, keepdims=True)
    gscore = (m1 + m2).reshape(N_GROUPS, tm)

    ig = lax.broadcasted_iota(jnp.int32, gscore.shape, 0).astype(F32)
    gsel = jnp.zeros_like(gscore)
    cur = gscore
    for _ in range(TOPK_GROUPS):
        _, hit = _pick_max(cur, ig)
        gsel = jnp.where(hit, 1.0, gsel)
        cur = jnp.where(hit, neg, cur)
    gsel3 = jnp.broadcast_to(gsel.reshape(N_GROUPS, 1, tm), b3.shape)
    cur = jnp.where(gsel3 > 0.0, b3, neg).reshape(n_exp, tm)

    ie = lax.broadcasted_iota(jnp.int32, (n_exp, tm), 0).astype(F32)
    sel = jnp.zeros((n_exp, tm), F32)
    picks, raw = [], []
    for _ in range(TOP_K):
        first, hit = _pick_max(cur, ie)
        picks.append(first)
        raw.append(jnp.sum(jnp.where(hit, scores, 0.0), axis=0, keepdims=True))
        sel = jnp.where(hit, 1.0, sel)
        cur = jnp.where(hit, neg, cur)
    total = raw[0]
    for r in raw[1:]:
        total = total + r

    ri = lax.broadcasted_iota(jnp.int32, (tm, tm), 0)
    ci = lax.broadcasted_iota(jnp.int32, (tm, tm), 1)
    upper = jnp.where(ri < ci, 1.0, 0.0).astype(BF16)
    rank = _dot(sel.astype(BF16), upper) + carry[...]
    ranks = [jnp.sum(jnp.where(ie == p, rank, 0.0), axis=0, keepdims=True) for p in picks]
    carry[...] = carry[...] + jnp.sum(sel, axis=1, keepdims=True)

    pad = jnp.zeros((SUBLANES - TOP_K, tm), F32)
    e_ref[...] = jnp.concatenate(picks + [pad], axis=0).astype(jnp.int32)
    gates = jnp.concatenate([r / total * ROUTED_SCALE for r in raw]
                            + [jnp.zeros((LANES - TOP_K, tm), F32)], axis=0)
    gt_ref[...] = gates.T[:, :SUBLANES]
    pos_ref[...] = jnp.concatenate(ranks + [pad], axis=0).astype(jnp.int32)
    cnt_ref[...] = jnp.broadcast_to(carry[...], cnt_ref.shape)


def _expert_kernel(be_ref, nv_ref, nu_ref, xs_hbm, wg_ref, wu_ref, wd_ref, ys_ref, wgu, wdn, xbuf, sem,
                   *, ff, n_sub):
    b = pl.program_id(0)
    n_used = nu_ref[0]
    depth, _, te, q = xbuf.shape

    def row_block_copy(blk):
        slot = blk % depth
        return pltpu.make_async_copy(xs_hbm.at[:, pl.ds(blk * te, te), :], xbuf.at[slot], sem.at[slot])

    @pl.when(b == 0)
    def _():
        for j in range(depth - 1):
            @pl.when(j < n_used)
            def _():
                row_block_copy(j).start()

    @pl.when(b + depth - 1 < n_used)
    def _():
        row_block_copy(b + depth - 1).start()

    changed = jnp.logical_or(b == 0, be_ref[b] != be_ref[jnp.maximum(b - 1, 0)])

    @pl.when(changed)
    def _():
        wgu[:, :ff] = wg_ref[...].astype(BF16)
        wgu[:, ff:] = wu_ref[...].astype(BF16)
        wdn[...] = wd_ref[...].astype(BF16)

    @pl.when(b < n_used)
    def _():
        row_block_copy(b).wait()
        slot = b % depth
        n_valid = nv_ref[b]
        ts = te // n_sub
        gus = []
        for a in range(n_sub):
            r0 = a * ts
            xw = jnp.concatenate([xbuf[slot, j, r0:r0 + ts, :] for j in range(SC_SPLIT)], axis=-1)
            row = lax.broadcasted_iota(jnp.int32, xw.shape, 0) + r0
            lo, hi = _unpack_halves(jnp.where(row < n_valid, xw, 0))
            half = lo.shape[1]
            gus.append(_dot(lo.astype(BF16), wgu[:half, :]) + _dot(hi.astype(BF16), wgu[half:, :]))
        outs = []
        for gu in gus:
            gate = gu[:, :ff]
            hmid = (gate * _sigmoid(gate) * gu[:, ff:]).astype(BF16)
            outs.append(_dot(hmid, wdn[...]))
        for a, y in enumerate(outs):
            words = _pack_halves(y)
            for j in range(SC_SPLIT):
                ys_ref[j, a * ts:(a + 1) * ts, :] = words[:, j * q:(j + 1) * q]


def _expert_call(block_e, n_valid, n_used, xs, layer, w_gate, w_up, w_down):
    _, n_rows, q = xs.shape
    d = 2 * SC_SPLIT * q
    ff = w_gate.shape[3]
    te = EXPERT_ROWS
    row_map = lambda b, be, nv, nu: (0, jnp.minimum(b, nu[0] - 1), 0)
    w_map = lambda b, be, nv, nu: (layer, be[b], 0, 0)
    grid_spec = pltpu.PrefetchScalarGridSpec(
        num_scalar_prefetch=3,
        grid=(n_rows // te,),
        in_specs=[pl.BlockSpec(memory_space=pl.ANY),
                  pl.BlockSpec((None, None, d, ff), w_map),
                  pl.BlockSpec((None, None, d, ff), w_map),
                  pl.BlockSpec((None, None, ff, d), w_map)],
        out_specs=pl.BlockSpec((SC_SPLIT, te, q), row_map),
        scratch_shapes=[pltpu.VMEM((d, 2 * ff), BF16), pltpu.VMEM((ff, d), BF16),
                        pltpu.VMEM((EXPERT_RING, SC_SPLIT, te, q), jnp.int32),
                        pltpu.SemaphoreType.DMA((EXPERT_RING,))])
    return pl.pallas_call(
        functools.partial(_expert_kernel, ff=ff, n_sub=EXPERT_ROWS // EXPERT_SUB_ROWS),
        grid_spec=grid_spec,
        out_shape=jax.ShapeDtypeStruct(xs.shape, jnp.int32),
        compiler_params=_cparams(1, 40),
        name="moe_experts",
    )(block_e, n_valid, n_used, xs, w_gate, w_up, w_down)


SC_WINDOW = 128


def _sc_mesh():
    return plsc.VectorSubcoreMesh(core_axis_name="core", subcore_axis_name="subcore")


def _sc_scatter_rows(rows, dests, n_out):
    n, width = rows.shape
    n_k = len(dests)
    assert n % SC_WINDOW == 0

    @functools.partial(pl.kernel, out_type=jax.ShapeDtypeStruct((n_out, width), rows.dtype),
                       mesh=_sc_mesh(), name="sc_dispatch_rows")
    def scatter_kernel(x_hbm, *refs):
        idx_hbm, o_hbm = refs[:n_k], refs[n_k]

        def body(x_vmem, *idx_vmem):
            for iv in idx_vmem:
                pltpu.sync_copy(x_vmem, o_hbm.at[iv.at[0]])

        pltpu.emit_pipeline(
            body,
            grid=(n // SC_WINDOW,),
            in_specs=[pl.BlockSpec((SC_WINDOW, width), lambda i: (i, 0))]
            + [pl.BlockSpec((1, SC_WINDOW), lambda i: (0, i))] * n_k,
            out_specs=[],
            core_axis_name=("core", "subcore"),
            dimension_semantics=(pltpu.PARALLEL,),
        )(x_hbm, *idx_hbm)

    return scatter_kernel(rows, *dests)


def _sc_gather_rows(table, idx):
    n = idx.shape[1]
    width = table.shape[1]
    assert n % SC_WINDOW == 0

    @functools.partial(pl.kernel, out_type=jax.ShapeDtypeStruct((n, width), table.dtype),
                       mesh=_sc_mesh(), name="sc_collect_rows")
    def gather_kernel(t_hbm, i_hbm, o_hbm):
        def body(i_vmem, o_vmem):
            pltpu.sync_copy(t_hbm.at[i_vmem.at[0]], o_vmem)

        pltpu.emit_pipeline(
            body,
            grid=(n // SC_WINDOW,),
            in_specs=[pl.BlockSpec((1, SC_WINDOW), lambda i: (0, i))],
            out_specs=[pl.BlockSpec((SC_WINDOW, width), lambda i: (i, 0))],
            core_axis_name=("core", "subcore"),
            dimension_semantics=(pltpu.PARALLEL,),
        )(i_hbm, o_hbm)

    return gather_kernel(table, idx)


def _shared_kernel(tok_ref, sgu_ref, sd_ref, o_ref, *, ff):
    gu = _dot(tok_ref[...], sgu_ref[...])
    gate = gu[:, :ff]
    hmid = (gate * _sigmoid(gate) * gu[:, ff:]).astype(BF16)
    o_ref[...] = _dot(hmid, sd_ref[...]).astype(BF16)


def _shared_call(tok, sgu, sd, *, rows, tm):
    d = tok.shape[1]
    ff = sd.shape[0]
    return pl.pallas_call(
        functools.partial(_shared_kernel, ff=ff),
        grid=(rows // tm,),
        in_specs=[pl.BlockSpec((tm, d), lambda i: (i, 0)),
                  pl.BlockSpec((d, 2 * ff), lambda i: (0, 0)),
                  pl.BlockSpec((ff, d), lambda i: (0, 0))],
        out_specs=pl.BlockSpec((tm, d), lambda i: (i, 0)),
        out_shape=jax.ShapeDtypeStruct((rows, d), BF16),
        compiler_params=_cparams(1, 40),
        name="moe_shared_expert",
    )(tok, sgu, sd)


def _moe_out_kernel(shared_ref, yg_ref, gk_ref, x_ref, g_ref, gate_ref, xo_ref):
    shared = shared_ref[...].astype(F32)
    gk = gk_ref[...]
    lo_acc, hi_acc = None, None
    for k in range(TOP_K):
        lo, hi = _unpack_halves(jnp.concatenate([yg_ref[j, k] for j in range(SC_SPLIT)], axis=-1))
        w = gk[:, k:k + 1]
        lo_acc = lo * w if lo_acc is None else lo_acc + lo * w
        hi_acc = hi * w if hi_acc is None else hi_acc + hi * w
    y = jnp.concatenate([lo_acc, hi_acc], axis=-1) + shared
    xo_ref[...] = x_ref[...] + gate_ref[...] * _rms(y, g_ref[...])


def _moe_out_call(shared, yg, gk, xs, g, gate, *, rows, tm, seq, n_batch):
    d = xs.shape[1]
    seg = _seg_map(tm, seq, n_batch)
    row = lambda: pl.BlockSpec((tm, d), lambda i: (i, 0))
    return pl.pallas_call(
        _moe_out_kernel,
        grid=(rows // tm,),
        in_specs=[row(),
                  pl.BlockSpec((SC_SPLIT, TOP_K, tm, d // 2 // SC_SPLIT), lambda i: (0, 0, i, 0)),
                  pl.BlockSpec((tm, SUBLANES), lambda i: (i, 0)),
                  row(),
                  pl.BlockSpec((1, d), lambda i: (0, 0)),
                  pl.BlockSpec((None, 1, d), seg)],
        out_specs=row(),
        out_shape=jax.ShapeDtypeStruct((rows, d), F32),
        compiler_params=_cparams(1, 48),
        name="moe_combine",
    )(shared, yg, gk, xs, g, gate)


def _dest_kernel(e_ref, pos_ref, ps_ref, o_ref, *, n_rows):
    e = e_ref[...].astype(F32)
    ps = ps_ref[...]
    n_exp, tm = ps.shape[0], e.shape[1]
    ids = lax.broadcasted_iota(jnp.int32, (n_exp, tm), 0).astype(F32)
    rows = [jnp.sum(jnp.where(ids == e[k:k + 1, :], ps, 0.0), axis=0, keepdims=True) for k in range(SUBLANES)]
    dest = jnp.concatenate(rows, axis=0).astype(jnp.int32) + pos_ref[...]
    for j in range(SC_SPLIT):
        o_ref[j] = dest + j * n_rows


def _dest_call(e_tk, pos_tk, pad_starts, *, n_rows, tm):
    rows = e_tk.shape[1]
    n_exp = pad_starts.shape[0]
    lane_spec = pl.BlockSpec((SUBLANES, tm), lambda i: (0, i))
    return pl.pallas_call(
        functools.partial(_dest_kernel, n_rows=n_rows),
        grid=(rows // tm,),
        in_specs=[lane_spec, lane_spec, pl.BlockSpec((n_exp, 1), lambda i: (0, 0))],
        out_specs=pl.BlockSpec((SC_SPLIT, SUBLANES, tm), lambda i: (0, 0, i)),
        out_shape=jax.ShapeDtypeStruct((SC_SPLIT, SUBLANES, rows), jnp.int32),
        compiler_params=_cparams(1, 32),
        name="moe_dest_rows",
    )(e_tk, pos_tk, pad_starts.astype(F32).reshape(n_exp, 1))


def _moe_layer(xs, tok, tokp, routing, g_out, gate, layer, w_gate, w_up, w_down, sh_gate, sh_up, sh_down,
               *, rows, tm, seq, n_batch):
    e_tk, g_tk, pos_tk, counts = routing
    n_exp = counts.shape[0]
    te = EXPERT_ROWS
    cnt = counts[:, 0].astype(jnp.int32)
    padded = (cnt + te - 1) // te * te
    pad_ends = jnp.cumsum(padded)
    pad_starts = pad_ends - padded
    expert_ids = jnp.arange(n_exp, dtype=jnp.int32)
    n_blocks = -(-(rows * TOP_K + n_exp * (te - 1)) // te)
    n_used = pad_ends[-1] // te
    blk = jnp.arange(n_blocks, dtype=jnp.int32)
    last = jnp.minimum(blk, n_used - 1) * te
    owner = (last[:, None] >= pad_ends[None, :]).astype(jnp.int32)
    block_e = jnp.minimum(jnp.sum(owner, axis=1), n_exp - 1)
    own_hot = block_e[:, None] == expert_ids[None, :]
    n_valid = jnp.sum(jnp.where(own_hot, (pad_starts + cnt)[None, :], 0), axis=1) - last
    n_valid = jnp.clip(n_valid, 0, te)
    q = tokp.shape[2]
    n_rows = n_blocks * te
    piece = _dest_call(e_tk, pos_tk, pad_starts, n_rows=n_rows, tm=tm)[:, :TOP_K, :]
    xg = _sc_scatter_rows(tokp.reshape(SC_SPLIT * rows, q),
                          [piece[:, k, :].reshape(1, SC_SPLIT * rows) for k in range(TOP_K)],
                          SC_SPLIT * n_rows).reshape(SC_SPLIT, n_rows, q)
    sgu = jnp.concatenate([sh_gate, sh_up], axis=1).astype(BF16)
    shared = _shared_call(tok, sgu, sh_down.astype(BF16), rows=rows, tm=tm)
    ys = _expert_call(block_e, n_valid, n_used.reshape(1).astype(jnp.int32), xg, layer, w_gate, w_up, w_down)
    yg = _sc_gather_rows(ys.reshape(SC_SPLIT * n_rows, q),
                         piece.reshape(1, -1)).reshape(SC_SPLIT, TOP_K, rows, q)
    return _moe_out_call(shared, yg, g_tk, xs, g_out, gate, rows=rows, tm=tm, seq=seq, n_batch=n_batch)


def _rope(t, cos, sin):
    half = DA_QK_DIM // 2
    up = pltpu.roll(t, shift=LANES - half, axis=1)
    dn = pltpu.roll(t, shift=half, axis=1)
    lane = lax.broadcasted_iota(jnp.int32, t.shape, 1) % DA_QK_DIM
    return t * cos + jnp.where(lane < half, -up, dn) * sin


def _qkv_kernel(x_ref, g_ref, sh_ref, sc_ref, w_ref, cos_ref, sin_ref, q_ref, k_ref, v_ref, *, qkw):
    h = _normmod(x_ref[...], g_ref[...], sh_ref[...], sc_ref[...]).astype(BF16)
    qkv = _dot(h, w_ref[...])
    cos, sin = cos_ref[...], sin_ref[...]
    q_scale = DA_QK_DIM ** -0.5 * math.log2(math.e)
    for hb in range(qkw // LANES):
        lo, hi = hb * LANES, (hb + 1) * LANES
        q_ref[:, lo:hi] = (_rope(qkv[:, lo:hi], cos, sin) * q_scale).astype(BF16)
        k_ref[:, lo:hi] = _rope(qkv[:, qkw + lo:qkw + hi], cos, sin).astype(BF16)
    v_ref[...] = qkv[:, 2 * qkw:].astype(BF16)


def _qkv_call(xs, g, shift, scale, w_qkv, cos_t, sin_t, *, qkw, tm, seq, ctx_len, n_batch):
    rows, d = xs.shape
    n = w_qkv.shape[1]
    vw = n - 2 * qkw
    lat_tiles, lat_tps, ctx_tps = n_batch * seq // tm, seq // tm, ctx_len // tm
    seg = _seg_map(tm, seq, n_batch)

    def kv_map(i):
        c = i - lat_tiles
        is_lat = i < lat_tiles
        return (jnp.where(is_lat, i // lat_tps, c // ctx_tps),
                jnp.where(is_lat, ctx_tps + i % lat_tps, c % ctx_tps), 0)

    rope_map = lambda i: (jnp.where(i < lat_tiles, i % lat_tps, lat_tps), 0)
    return pl.pallas_call(
        functools.partial(_qkv_kernel, qkw=qkw),
        grid=(rows // tm,),
        in_specs=[pl.BlockSpec((tm, d), lambda i: (i, 0)),
                  pl.BlockSpec((1, d), lambda i: (0, 0)),
                  pl.BlockSpec((None, 1, d), seg),
                  pl.BlockSpec((None, 1, d), seg),
                  pl.BlockSpec((d, n), lambda i: (0, 0)),
                  pl.BlockSpec((tm, LANES), rope_map),
                  pl.BlockSpec((tm, LANES), rope_map)],
        out_specs=[pl.BlockSpec((tm, qkw), lambda i: (i, 0)),
                   pl.BlockSpec((None, tm, qkw), kv_map),
                   pl.BlockSpec((None, tm, vw), kv_map)],
        out_shape=[jax.ShapeDtypeStruct((rows, qkw), BF16),
                   jax.ShapeDtypeStruct((n_batch, ctx_len + seq, qkw), BF16),
                   jax.ShapeDtypeStruct((n_batch, ctx_len + seq, vw), BF16)],
        compiler_params=_cparams(1, 48),
        name="qkv_rope",
    )(xs, g, shift, scale, w_qkv, cos_t, sin_t)


def _rope_tables(seq, tm):
    pos = np.arange(seq)
    n_freq = DA_QK_DIM // 4
    inv = np.power(ROPE_BASE, -np.arange(n_freq, dtype=np.float32) / n_freq).astype(np.float32)
    row = (pos // GRID_W).astype(np.float32)[:, None] * inv
    col = (pos % GRID_W).astype(np.float32)[:, None] * inv
    ang = np.concatenate([row, col], axis=-1).astype(np.float32)
    ang = np.tile(ang, (1, LANES // ang.shape[1]))
    cos = np.concatenate([np.cos(ang), np.ones((tm, LANES))], axis=0)
    sin = np.concatenate([np.sin(ang), np.zeros((tm, LANES))], axis=0)
    return jnp.asarray(cos, F32), jnp.asarray(sin, F32)


def _attn_kernel(lp_ref, q_ref, k_ref, v_ref, sg_ref, o_ref, vext, *, n_sub, lam_init):
    @pl.when(pl.program_id(2) == 0)
    def _():
        vext[:, :LANES] = v_ref[...]
        vext[:, LANES:] = jnp.ones((vext.shape[0], LANES), BF16)

    lp = lp_ref[...]
    lam = (jnp.exp(jnp.sum(lp[0:1] * lp[1:2], axis=1, keepdims=True))
           - jnp.exp(jnp.sum(lp[2:3] * lp[3:4], axis=1, keepdims=True)) + lam_init)
    ts = q_ref.shape[0] // n_sub
    lane = lax.broadcasted_iota(jnp.int32, (ts, LANES), 1)
    zero = jnp.zeros((ts, LANES), BF16)
    k = k_ref[...]
    scores = []
    for a in range(n_sub):
        q = q_ref[a * ts:(a + 1) * ts, :]
        qq = jnp.concatenate([jnp.where(lane < DA_QK_DIM, q, zero),
                              jnp.where(lane >= DA_QK_DIM, q, zero)], axis=0)
        scores.append(_dot_nt(qq, k))
    for a in range(n_sub):
        s = scores[a]
        p = jnp.exp2(s - jnp.max(s, axis=-1, keepdims=True)).astype(BF16)
        oe = _dot(p, vext[...])
        on = oe[:, :LANES] / oe[:, LANES:LANES + 1]
        o = on[:ts] - lam * on[ts:]
        o_ref[a * ts:(a + 1) * ts, :] = (_rms(o, sg_ref[...]) * (1.0 - lam_init)).astype(BF16)


def _attn_call(lam_p, q, k_all, v_all, subln_g, *, tq, seq, n_batch, lam_init):
    n_heads = q.shape[1] // LANES
    lk = k_all.shape[1]
    qt = seq // tq
    return pl.pallas_call(
        functools.partial(_attn_kernel, n_sub=tq // ATTN_SUB_ROWS, lam_init=lam_init),
        grid=(n_batch, n_heads, qt),
        in_specs=[pl.BlockSpec(lam_p.shape, lambda b, h, i: (0, 0)),
                  pl.BlockSpec((tq, LANES), lambda b, h, i: (b * qt + i, h)),
                  pl.BlockSpec((None, lk, LANES), lambda b, h, i: (b, 0, h)),
                  pl.BlockSpec((None, lk, LANES), lambda b, h, i: (b, 0, h)),
                  pl.BlockSpec((1, LANES), lambda b, h, i: (0, 0))],
        out_specs=pl.BlockSpec((tq, LANES), lambda b, h, i: (b * qt + i, h)),
        out_shape=jax.ShapeDtypeStruct((n_batch * seq, n_heads * LANES), BF16),
        scratch_shapes=[pltpu.VMEM((lk, 2 * LANES), BF16)],
        compiler_params=_cparams(3, 48),
        name="diff_attention",
    )(lam_p, q, k_all, v_all, subln_g)


def kernel(x, c, ctx, c_ctx, ada_w, ada_b, norm_g, cf_w_in, cf_conv_w, cf_conv_b, cf_ln_g, cf_ln_b,
           cf_w_out, da_w_qkv, da_lambda, da_subln_g, da_w_out, moe_router_w, moe_router_b,
           moe_w_gate, moe_w_up, moe_w_down, moe_sh_gate, moe_sh_up, moe_sh_down):
    n_batch, seq, d = x.shape
    ctx_len = ctx.shape[1]
    depth = ada_w.shape[0]
    assert depth == 2 and n_batch + 1 <= N_SEG
    assert DA_QK_DIM * 2 == LANES and da_subln_g.shape[1] == LANES
    rows_lat, rows_ctx = n_batch * seq, n_batch * ctx_len
    tm = 512
    tq = 256
    assert seq % tm == 0 and rows_ctx % tm == 0 and seq % tq == 0 and ctx_len % tq == 0

    x_lat, x_ctx = x.reshape(rows_lat, d), ctx.reshape(rows_ctx, d)
    cond = jnp.concatenate([c, c_ctx[None, :], jnp.zeros((N_SEG - n_batch - 1, d), F32)], axis=0)
    mods = _mods_call(cond, ada_w, ada_b)
    mod = lambda layer, k: mods[layer, :, k * d:(k + 1) * d][:, None, :]
    gain = lambda layer, k: norm_g[layer, k][None, :]
    common = dict(seq=seq, n_batch=n_batch)

    cc = cf_conv_w.shape[-1]
    fc = cf_w_in.shape[2] - 2 * cc
    gw = fc // FOURIER_GROUPS
    ch_ang = 2.0 * np.pi * (np.outer(np.arange(gw), np.arange(gw)) % gw) / gw
    cs = jnp.asarray(np.concatenate([np.cos(ch_ang), np.sin(ch_ang)], axis=1) / math.sqrt(gw), BF16)
    a_glu, ab = _inproj_call(x_lat, x_ctx, gain(0, 0), mod(0, 0), mod(0, 1), cf_w_in[0].astype(BF16), cs,
                             cc=cc, gw=gw, tm=tm, **common)
    gsz = cc // CONV_GROUPS
    gid = np.arange(cc) // gsz
    gmean = jnp.asarray((gid[:, None] == gid[None, :]) / gsz, BF16)
    a_act = _conv_call(a_glu, cf_conv_w[0][:, 0, :], cf_conv_b[0][None, :], cf_ln_g[0][None, :],
                       cf_ln_b[0][None, :], gmean, tl=tq, seq=seq, ctx_len=ctx_len, n_batch=n_batch)
    fr = jnp.concatenate([
        _seqdft_call(ab, seq, row0=0, n_seq=n_batch, fc=fc, tm=min(seq // 2, 256)),
        _seqdft_call(ab, ctx_len, row0=rows_lat, n_seq=n_batch, fc=fc, tm=min(ctx_len // 2, 256))], axis=0)
    w_out = cf_w_out[0].astype(BF16)
    xs, tok, tokp, *routing = _outproj_call(
        [a_act, fr], [w_out[:cc], w_out[cc:]], x_lat, x_ctx, gain(0, 1), mod(0, 2), gain(0, 2), mod(0, 3),
        mod(0, 4), moe_router_w[0], moe_router_b[0], rows=rows_lat + rows_ctx, tm=tm, **common)
    xs = _moe_layer(xs, tok, tokp, routing, gain(0, 3), mod(0, 5), 0, moe_w_gate, moe_w_up, moe_w_down,
                    moe_sh_gate[0], moe_sh_up[0], moe_sh_down[0], rows=rows_lat + rows_ctx, tm=tm, **common)

    qkw = DA_HEADS * 2 * DA_QK_DIM
    lam_init = 0.8 - 0.6 * math.exp(-0.3 * 1)
    cos_t, sin_t = _rope_tables(seq, tq)
    q, k_all, v_all = _qkv_call(xs, gain(1, 0), mod(1, 0), mod(1, 1), da_w_qkv[0].astype(BF16),
                                cos_t, sin_t, qkw=qkw, tm=tq, ctx_len=ctx_len, **common)
    o = _attn_call(da_lambda[0], q, k_all, v_all, da_subln_g[0][None, :], tq=math.gcd(seq, ATTN_ROWS), seq=seq,
                   n_batch=n_batch, lam_init=lam_init)
    xs, tok, tokp, *routing = _outproj_call(
        [o], [da_w_out[0].astype(BF16)], xs, xs, gain(1, 1), mod(1, 2), gain(1, 2), mod(1, 3), mod(1, 4),
        moe_router_w[1], moe_router_b[1], rows=rows_lat, tm=tm, **common)
    xs = _moe_layer(xs, tok, tokp, routing, gain(1, 3), mod(1, 5), 1, moe_w_gate, moe_w_up, moe_w_down,
                    moe_sh_gate[1], moe_sh_up[1], moe_sh_down[1], rows=rows_lat, tm=tm, **common)
    return xs.reshape(n_batch, seq, d)
```

```python
import functools
import math

import numpy as np
import jax
import jax.numpy as jnp
from jax import lax
from jax.experimental import pallas as pl
from jax.experimental.pallas import tpu as pltpu
from jax.experimental.pallas import tpu_sc as plsc

F32 = jnp.float32
BF16 = jnp.bfloat16

EPS = 1e-6
GRID_W = 64
CONV_GROUPS = 8
FOURIER_GROUPS = 4
DA_HEADS = 8
DA_QK_DIM = 64
ROPE_BASE = 10000.0
N_GROUPS = 8
TOPK_GROUPS = 4
TOP_K = 6
ROUTED_SCALE = 2.5

LANES = 128
SUBLANES = 8
ROW_TILE = 512
SEQ_TILE = 256
N_SEG = 16
HALO = 16
EXPERT_ROWS = 1024
EXPERT_SUB_ROWS = 128
EXPERT_RING = 3
ATTN_ROWS = 1024
ATTN_SUB_ROWS = 128
MIB = 1024 * 1024


def _cparams(n_axes, vmem_mib):
    return pltpu.CompilerParams(dimension_semantics=("arbitrary",) * n_axes,
                                vmem_limit_bytes=vmem_mib * MIB)


def _sigmoid(v):
    return 1.0 / (1.0 + jnp.exp(-v))


def _rms(v, g):
    return v * lax.rsqrt(jnp.mean(v * v, axis=-1, keepdims=True) + EPS) * g


def _normmod(v, g, shift, scale):
    return _rms(v, g) * (1.0 + scale) + shift


def _split_bf16(v):
    hi = v.astype(BF16)
    lo = (v - hi.astype(F32)).astype(BF16)
    return hi, lo


def _pack_halves(v):
    half = v.shape[1] // 2
    word = pltpu.pack_elementwise([v[:, :half], v[:, half:]], packed_dtype=BF16)
    return lax.bitcast_convert_type(word, jnp.int32)


def _unpack_halves(w):
    u = lax.bitcast_convert_type(w, jnp.uint32)
    lo = lax.bitcast_convert_type(u << 16, F32)
    hi = lax.bitcast_convert_type(u & jnp.uint32(0xFFFF0000), F32)
    return lo, hi


SC_SPLIT = 2


def _store_pieces(ref, words):
    q = words.shape[1] // SC_SPLIT
    for j in range(SC_SPLIT):
        ref[j] = words[:, j * q:(j + 1) * q]


def _load_pieces(ref):
    return jnp.concatenate([ref[j] for j in range(SC_SPLIT)], axis=-1)


def _dot(a, b):
    return jnp.dot(a, b, preferred_element_type=F32)


def _dot_nt(a, b):
    return lax.dot_general(a, b, (((1,), (1,)), ((), ())), preferred_element_type=F32)


def _mods_kernel(c_ref, w_ref, b_ref, o_ref):
    cv = c_ref[...]
    o_ref[...] = _dot(cv * _sigmoid(cv), w_ref[...]) + b_ref[...]


def _mods_call(cond, ada_w, ada_b):
    depth, d, n = ada_w.shape
    tn = n // 4
    return pl.pallas_call(
        _mods_kernel,
        grid=(depth, n // tn),
        in_specs=[pl.BlockSpec((N_SEG, d), lambda l, j: (0, 0)),
                  pl.BlockSpec((None, d, tn), lambda l, j: (l, 0, j)),
                  pl.BlockSpec((None, 1, tn), lambda l, j: (l, 0, j))],
        out_specs=pl.BlockSpec((None, N_SEG, tn), lambda l, j: (l, 0, j)),
        out_shape=jax.ShapeDtypeStruct((depth, N_SEG, n), F32),
        compiler_params=_cparams(2, 40),
        name="adaln_mods",
    )(cond, ada_w, ada_b.reshape(depth, 1, n))


def _two_source_specs(tm, d, n_head_tiles):
    return (pl.BlockSpec((tm, d), lambda i: (jnp.minimum(i, n_head_tiles - 1), 0)),
            pl.BlockSpec((tm, d), lambda i: (jnp.maximum(i - n_head_tiles, 0), 0)))


def _pick_rows(head_ref, tail_ref, n_head_tiles):
    return jnp.where(pl.program_id(0) < n_head_tiles, head_ref[...], tail_ref[...])


def _inproj_kernel(xh_ref, xt_ref, g_ref, sh_ref, sc_ref, w_ref, cs_ref, a_ref, ab_ref, *, cc, gw, n_head):
    x = _pick_rows(xh_ref, xt_ref, n_head)
    h = _normmod(x, g_ref[...], sh_ref[...], sc_ref[...]).astype(BF16)
    u = _dot(h, w_ref[...])
    a_ref[...] = u[:, :cc] * _sigmoid(u[:, cc:2 * cc])
    f = u[:, 2 * cc:].astype(BF16)
    n_g = f.shape[1] // gw
    parts = [_dot(f[:, g * gw:(g + 1) * gw], cs_ref[...]) for g in range(n_g)]
    cos_part = [p[:, :gw] for p in parts]
    sin_part = [p[:, gw:] for p in parts]
    ab_ref[...] = jnp.concatenate(cos_part + sin_part, axis=-1).astype(BF16)


def _seg_map(tm, seq, n_batch):
    return lambda i: (jnp.minimum(i * tm // seq, n_batch), 0, 0)


def _inproj_call(x_head, x_tail, g, shift, scale, w_in, cs, *, cc, gw, tm, seq, n_batch):
    d = x_head.shape[1]
    rows = x_head.shape[0] + x_tail.shape[0]
    n_head = x_head.shape[0] // tm
    n = w_in.shape[1]
    fc = n - 2 * cc
    seg = _seg_map(tm, seq, n_batch)
    return pl.pallas_call(
        functools.partial(_inproj_kernel, cc=cc, gw=gw, n_head=n_head),
        grid=(rows // tm,),
        in_specs=[*_two_source_specs(tm, d, n_head),
                  pl.BlockSpec((1, d), lambda i: (0, 0)),
                  pl.BlockSpec((None, 1, d), seg),
                  pl.BlockSpec((None, 1, d), seg),
                  pl.BlockSpec((d, n), lambda i: (0, 0)),
                  pl.BlockSpec((gw, 2 * gw), lambda i: (0, 0))],
        out_specs=[pl.BlockSpec((tm, cc), lambda i: (i, 0)),
                   pl.BlockSpec((tm, 2 * fc), lambda i: (i, 0))],
        out_shape=[jax.ShapeDtypeStruct((rows, cc), F32),
                   jax.ShapeDtypeStruct((rows, 2 * fc), BF16)],
        compiler_params=_cparams(1, 40),
        name="inproj_glu_chdft",
    )(x_head, x_tail, g, shift, scale, w_in, cs)


def _conv_kernel(prev_ref, main_ref, next_ref, w_ref, cb_ref, lg_ref, lb_ref, gm_ref, o_ref,
                 buf, cv, shifted, *, tl, width, lat_tiles, lat_tps, ctx_tps, chunk):
    i = pl.program_id(0)
    is_lat = i < lat_tiles
    tps = jnp.where(is_lat, lat_tps, ctx_tps)
    j = jnp.where(is_lat, i, i - lat_tiles) % tps
    zero = jnp.zeros((HALO, buf.shape[1]), F32)
    buf[0:HALO, :] = jnp.where(j > 0, prev_ref[...], zero)
    buf[HALO:HALO + tl, :] = main_ref[...]
    buf[HALO + tl:HALO + tl + HALO, :] = jnp.where(j < tps - 1, next_ref[...], zero)
    base = HALO - width // 2
    span = shifted.shape[1]
    for s in range(1, SUBLANES):
        shifted[s] = buf[s:s + span, :]

    def window(off, r0):
        s, m = off % SUBLANES, off - off % SUBLANES
        if s == 0:
            return buf[m + r0:m + r0 + chunk, :]
        return shifted[s, m + r0:m + r0 + chunk, :]

    for r0 in range(0, tl, chunk):
        acc = window(base, r0) * w_ref[0:1, :]
        for t in range(1, width):
            acc = acc + window(base + t, r0) * w_ref[t:t + 1, :]
        cv[r0:r0 + chunk, :] = acc
    a = cv[...] + cb_ref[...]
    gm = gm_ref[...]
    a_hi, a_lo = _split_bf16(a)
    mu = _dot(a_hi, gm) + _dot(a_lo, gm)
    dl = a - mu
    q_hi, q_lo = _split_bf16(dl * dl)
    var = _dot(q_hi, gm) + _dot(q_lo, gm)
    y = dl * lax.rsqrt(var + EPS) * lg_ref[...] + lb_ref[...]
    o_ref[...] = (y * _sigmoid(y)).astype(BF16)


def _conv_call(a_glu, conv_w, conv_b, ln_g, ln_b, gmean, *, tl, seq, ctx_len, n_batch):
    rows, ch = a_glu.shape
    width = conv_w.shape[0]
    assert width // 2 <= HALO and tl % HALO == 0
    hb = tl // HALO
    last_halo = rows // HALO - 1
    return pl.pallas_call(
        functools.partial(_conv_kernel, tl=tl, width=width, lat_tiles=n_batch * seq // tl,
                          lat_tps=seq // tl, ctx_tps=ctx_len // tl, chunk=32),
        grid=(rows // tl,),
        in_specs=[pl.BlockSpec((HALO, ch), lambda i: (jnp.maximum(i * hb - 1, 0), 0)),
                  pl.BlockSpec((tl, ch), lambda i: (i, 0)),
                  pl.BlockSpec((HALO, ch), lambda i: (jnp.minimum((i + 1) * hb, last_halo), 0)),
                  pl.BlockSpec((width, ch), lambda i: (0, 0)),
                  pl.BlockSpec((1, ch), lambda i: (0, 0)),
                  pl.BlockSpec((1, ch), lambda i: (0, 0)),
                  pl.BlockSpec((1, ch), lambda i: (0, 0)),
                  pl.BlockSpec((ch, ch), lambda i: (0, 0))],
        out_specs=pl.BlockSpec((tl, ch), lambda i: (i, 0)),
        out_shape=jax.ShapeDtypeStruct((rows, ch), BF16),
        scratch_shapes=[pltpu.VMEM((tl + 2 * HALO, ch), F32), pltpu.VMEM((tl, ch), F32),
                        pltpu.VMEM((SUBLANES, tl + 2 * HALO - SUBLANES, ch), F32)],
        compiler_params=_cparams(1, 40),
        name="dwconv_groupln_swish",
    )(a_glu, a_glu, a_glu, conv_w, conv_b, ln_g, ln_b, gmean)


EDGE_ROWS = 16


def _seqdft_kernel(c_ref, s_ref, cx_ref, sx_ref, jm_ref, a_ref, b_ref, o_ref, *, tm, n_tiles):
    i = pl.program_id(1)
    a, b = a_ref[...], b_ref[...]
    p, qv = _dot(c_ref[...], a), _dot(s_ref[...], b)
    o_ref[pl.ds(pl.multiple_of(i * tm, tm), tm), :] = (p - qv).astype(BF16)
    edge = _dot(cx_ref[...], a) + _dot(sx_ref[...], b)
    pick = lax.broadcasted_iota(jnp.int32, edge.shape, 0) == i
    edge_row = jnp.sum(jnp.where(pick, edge, 0.0), axis=0, keepdims=True)
    mirrored = _dot(jm_ref[...], (p + qv).astype(BF16))
    srow = lax.broadcasted_iota(jnp.int32, mirrored.shape, 0)
    mirrored = jnp.where(srow == 0, edge_row, mirrored)
    o_ref[pl.ds(pl.multiple_of((n_tiles - 1 - i) * tm, tm), tm), :] = mirrored.astype(BF16)


def _seqdft_call(ab, length, *, row0, n_seq, fc, tm):
    n_tiles = length // tm
    n_half = n_tiles // 2
    assert n_tiles % 2 == 0 and n_half <= EDGE_ROWS
    scale = 1.0 / math.sqrt(length)
    cmat, smat = _dft_tables(length, scale, np.arange(length // 2))
    edge_k = np.zeros((EDGE_ROWS,), np.int64)
    edge_k[:n_half] = tm * (np.arange(n_half) + 1)
    cx, sx = _dft_tables(length, scale, edge_k)
    s_idx = np.arange(tm)
    reversal = jnp.asarray((s_idx[None, :] == (tm - s_idx)[:, None]) & (s_idx[:, None] > 0), BF16)
    assert row0 % length == 0
    seq0 = row0 // length
    whole = lambda arr: pl.BlockSpec(arr.shape, lambda b, i: (0, 0))
    return pl.pallas_call(
        functools.partial(_seqdft_kernel, tm=tm, n_tiles=n_tiles),
        grid=(n_seq, n_half),
        in_specs=[pl.BlockSpec((tm, length), lambda b, i: (i, 0)),
                  pl.BlockSpec((tm, length), lambda b, i: (i, 0)),
                  whole(cx), whole(sx), whole(reversal),
                  pl.BlockSpec((length, fc), lambda b, i: (seq0 + b, 0)),
                  pl.BlockSpec((length, fc), lambda b, i: (seq0 + b, 1))],
        out_specs=pl.BlockSpec((length, fc), lambda b, i: (b, 0)),
        out_shape=jax.ShapeDtypeStruct((n_seq * length, fc), BF16),
        compiler_params=_cparams(2, 48),
        name="seq_dft",
    )(cmat, smat, cx, sx, reversal, ab, ab)


def _dft_tables(length, scale, rows):
    k = np.asarray(rows, np.int64)[:, None]
    if length <= 512:
        ang = 2.0 * np.pi * ((k * np.arange(length)[None, :]) % length) / length
        return (jnp.asarray(np.cos(ang) * scale, BF16), jnp.asarray(np.sin(ang) * scale, BF16))
    r = 64
    assert length % r == 0
    alpha = 2.0 * np.pi * ((k * np.arange(length // r)[None, :] * r) % length) / length
    beta = 2.0 * np.pi * ((k * np.arange(r)[None, :]) % length) / length
    ca, sa = jnp.asarray(np.cos(alpha), F32)[:, :, None], jnp.asarray(np.sin(alpha), F32)[:, :, None]
    cb, sb = jnp.asarray(np.cos(beta) * scale, F32)[:, None, :], jnp.asarray(np.sin(beta) * scale, F32)[:, None, :]
    cmat = (ca * cb - sa * sb).reshape(k.shape[0], length).astype(BF16)
    smat = (sa * cb + ca * sb).reshape(k.shape[0], length).astype(BF16)
    return cmat, smat


def _outproj_kernel(*refs, in_heads, n_head):
    n_in = len(in_heads)
    ins, ws = refs[:2 * n_in], refs[2 * n_in:3 * n_in]
    (xh_ref, xt_ref, g1_ref, gate_ref, g2_ref, sh_ref, sc_ref, wh_ref, wl_ref, rb_ref,
     xo_ref, tok_ref, tokp_ref, e_ref, gt_ref, pos_ref, cnt_ref, carry) = refs[3 * n_in:]
    y = None
    for j, w_ref in enumerate(ws):
        part = _dot(_pick_rows(ins[2 * j], ins[2 * j + 1], in_heads[j]), w_ref[...])
        y = part if y is None else y + part
    x1 = _pick_rows(xh_ref, xt_ref, n_head) + gate_ref[...] * _rms(y, g1_ref[...])
    xo_ref[...] = x1
    tok = _normmod(x1, g2_ref[...], sh_ref[...], sc_ref[...])
    tok_ref[...] = tok.astype(BF16)
    _store_pieces(tokp_ref, _pack_halves(tok))
    _route(tok, wh_ref, wl_ref, rb_ref, e_ref, gt_ref, pos_ref, cnt_ref, carry)


def _outproj_call(ins, ws, x_head, x_tail, g1, gate, g2, shift, scale, router_w, router_b,
                  *, rows, tm, seq, n_batch):
    d = x_head.shape[1]
    n_head = min(x_head.shape[0], rows) // tm
    n_exp = router_w.shape[1]
    w_t = router_w.T
    w_hi = w_t.astype(BF16)
    w_lo = (w_t - w_hi.astype(F32)).astype(BF16)
    seg = _seg_map(tm, seq, n_batch)
    in_heads = tuple(min(head.shape[0], rows) // tm for head, _ in ins)
    pair_specs = [s for (head, _), nh in zip(ins, in_heads) for s in _two_source_specs(tm, head.shape[1], nh)]
    full_spec = lambda a: pl.BlockSpec(a.shape, lambda i: (0, 0))
    lane_spec = pl.BlockSpec((SUBLANES, tm), lambda i: (0, i))
    tok_rows = lambda dt: jax.ShapeDtypeStruct((SUBLANES, rows), dt)
    return pl.pallas_call(
        functools.partial(_outproj_kernel, in_heads=in_heads, n_head=n_head),
        grid=(rows // tm,),
        in_specs=pair_specs + [full_spec(w) for w in ws] + [
            *_two_source_specs(tm, d, n_head), full_spec(g1), pl.BlockSpec((None, 1, d), seg), full_spec(g2),
            pl.BlockSpec((None, 1, d), seg), pl.BlockSpec((None, 1, d), seg),
            full_spec(w_hi), full_spec(w_lo), pl.BlockSpec((n_exp, 1), lambda i: (0, 0))],
        out_specs=[pl.BlockSpec((tm, d), lambda i: (i, 0)), pl.BlockSpec((tm, d), lambda i: (i, 0)),
                   pl.BlockSpec((SC_SPLIT, tm, d // 2 // SC_SPLIT), lambda i: (0, i, 0)),
                   lane_spec, pl.BlockSpec((tm, SUBLANES), lambda i: (i, 0)), lane_spec,
                   pl.BlockSpec((n_exp, LANES), lambda i: (0, 0))],
        out_shape=[jax.ShapeDtypeStruct((rows, d), F32), jax.ShapeDtypeStruct((rows, d), BF16),
                   jax.ShapeDtypeStruct((SC_SPLIT, rows, d // 2 // SC_SPLIT), jnp.int32),
                   tok_rows(jnp.int32), jax.ShapeDtypeStruct((rows, SUBLANES), F32), tok_rows(jnp.int32),
                   jax.ShapeDtypeStruct((n_exp, LANES), F32)],
        scratch_shapes=[pltpu.VMEM((n_exp, 1), F32)],
        compiler_params=_cparams(1, 48),
        name="outproj_residual_route",
    )(*[a for pair in ins for a in pair], *ws, x_head, x_tail, g1, gate, g2, shift, scale, w_hi, w_lo,
      router_b.reshape(n_exp, 1))


def _pick_max(cur, idx):
    mx = jnp.max(cur, axis=0, keepdims=True)
    first = jnp.min(jnp.where(cur == mx, idx, float(cur.shape[0])), axis=0, keepdims=True)
    return first, idx == first


def _route(tok, wh_ref, wl_ref, rb_ref, e_ref, gt_ref, pos_ref, cnt_ref, carry):
    @pl.when(pl.program_id(0) == 0)
    def _():
        carry[...] = jnp.zeros_like(carry)

    t_hi, t_lo = _split_bf16(tok)
    wh, wl = wh_ref[...], wl_ref[...]
    logits = _dot_nt(wh, t_hi) + _dot_nt(wh, t_lo) + _dot_nt(wl, t_hi)
    n_exp, tm = logits.shape
    scores = _sigmoid(logits)
    biased = scores + rb_ref[...]
    gsz = n_exp // N_GROUPS
    neg = -jnp.inf

    b3 = biased.reshape(N_GROUPS, gsz, tm)
    im = lax.broadcasted_iota(jnp.int32, b3.shape, 1).astype(F32)
    m1 = jnp.max(b3, axis=1, keepdims=True)
    i1 = jnp.min(jnp.where(b3 == m1, im, float(gsz)), axis=1, keepdims=True)
    m2 = jnp.max(jnp.where(im == i1, neg, b3), axis=1, keepdims=True)
    gscore = (m1 + m2).reshape(N_GROUPS, tm)

    ig = lax.broadcasted_iota(jnp.int32, gscore.shape, 0).astype(F32)
    gsel = jnp.zeros_like(gscore)
    cur = gscore
    for _ in range(TOPK_GROUPS):
        _, hit = _pick_max(cur, ig)
        gsel = jnp.where(hit, 1.0, gsel)
        cur = jnp.where(hit, neg, cur)
    gsel3 = jnp.broadcast_to(gsel.reshape(N_GROUPS, 1, tm), b3.shape)
    cur = jnp.where(gsel3 > 0.0, b3, neg).reshape(n_exp, tm)

    ie = lax.broadcasted_iota(jnp.int32, (n_exp, tm), 0).astype(F32)
    sel = jnp.zeros((n_exp, tm), F32)
    picks, raw = [], []
    for _ in range(TOP_K):
        first, hit = _pick_max(cur, ie)
        picks.append(first)
        raw.append(jnp.sum(jnp.where(hit, scores, 0.0), axis=0, keepdims=True))
        sel = jnp.where(hit, 1.0, sel)
        cur = jnp.where(hit, neg, cur)
    total = raw[0]
    for r in raw[1:]:
        total = total + r

    ri = lax.broadcasted_iota(jnp.int32, (tm, tm), 0)
    ci = lax.broadcasted_iota(jnp.int32, (tm, tm), 1)
    upper = jnp.where(ri < ci, 1.0, 0.0).astype(BF16)
    rank = _dot(sel.astype(BF16), upper) + carry[...]
    ranks = [jnp.sum(jnp.where(ie == p, rank, 0.0), axis=0, keepdims=True) for p in picks]
    carry[...] = carry[...] + jnp.sum(sel, axis=1, keepdims=True)

    pad = jnp.zeros((SUBLANES - TOP_K, tm), F32)
    e_ref[...] = jnp.concatenate(picks + [pad], axis=0).astype(jnp.int32)
    gates = jnp.concatenate([r / total * ROUTED_SCALE for r in raw]
                            + [jnp.zeros((LANES - TOP_K, tm), F32)], axis=0)
    gt_ref[...] = gates.T[:, :SUBLANES]
    pos_ref[...] = jnp.concatenate(ranks + [pad], axis=0).astype(jnp.int32)
    cnt_ref[...] = jnp.broadcast_to(carry[...], cnt_ref.shape)


def _expert_kernel(be_ref, nv_ref, nu_ref, xs_hbm, wg_ref, wu_ref, wd_ref, ys_ref, wgu, wdn, xbuf, sem,
                   *, ff, n_sub):
    b = pl.program_id(0)
    n_used = nu_ref[0]
    depth, _, te, q = xbuf.shape

    def row_block_copy(blk):
        slot = blk % depth
        return pltpu.make_async_copy(xs_hbm.at[:, pl.ds(blk * te, te), :], xbuf.at[slot], sem.at[slot])

    @pl.when(b == 0)
    def _():
        for j in range(depth - 1):
            @pl.when(j < n_used)
            def _():
                row_block_copy(j).start()

    @pl.when(b + depth - 1 < n_used)
    def _():
        row_block_copy(b + depth - 1).start()

    changed = jnp.logical_or(b == 0, be_ref[b] != be_ref[jnp.maximum(b - 1, 0)])

    @pl.when(changed)
    def _():
        wgu[:, :ff] = wg_ref[...].astype(BF16)
        wgu[:, ff:] = wu_ref[...].astype(BF16)
        wdn[...] = wd_ref[...].astype(BF16)

    @pl.when(b < n_used)
    def _():
        row_block_copy(b).wait()
        slot = b % depth
        n_valid = nv_ref[b]
        ts = te // n_sub
        gus = []
        for a in range(n_sub):
            r0 = a * ts
            xw = jnp.concatenate([xbuf[slot, j, r0:r0 + ts, :] for j in range(SC_SPLIT)], axis=-1)
            row = lax.broadcasted_iota(jnp.int32, xw.shape, 0) + r0
            lo, hi = _unpack_halves(jnp.where(row < n_valid, xw, 0))
            half = lo.shape[1]
            gus.append(_dot(lo.astype(BF16), wgu[:half, :]) + _dot(hi.astype(BF16), wgu[half:, :]))
        outs = []
        for gu in gus:
            gate = gu[:, :ff]
            hmid = (gate * _sigmoid(gate) * gu[:, ff:]).astype(BF16)
            outs.append(_dot(hmid, wdn[...]))
        for a, y in enumerate(outs):
            words = _pack_halves(y)
            for j in range(SC_SPLIT):
                ys_ref[j, a * ts:(a + 1) * ts, :] = words[:, j * q:(j + 1) * q]


def _expert_call(block_e, n_valid, n_used, xs, layer, w_gate, w_up, w_down):
    _, n_rows, q = xs.shape
    d = 2 * SC_SPLIT * q
    ff = w_gate.shape[3]
    te = EXPERT_ROWS
    row_map = lambda b, be, nv, nu: (0, jnp.minimum(b, nu[0] - 1), 0)
    w_map = lambda b, be, nv, nu: (layer, be[b], 0, 0)
    grid_spec = pltpu.PrefetchScalarGridSpec(
        num_scalar_prefetch=3,
        grid=(n_rows // te,),
        in_specs=[pl.BlockSpec(memory_space=pl.ANY),
                  pl.BlockSpec((None, None, d, ff), w_map),
                  pl.BlockSpec((None, None, d, ff), w_map),
                  pl.BlockSpec((None, None, ff, d), w_map)],
        out_specs=pl.BlockSpec((SC_SPLIT, te, q), row_map),
        scratch_shapes=[pltpu.VMEM((d, 2 * ff), BF16), pltpu.VMEM((ff, d), BF16),
                        pltpu.VMEM((EXPERT_RING, SC_SPLIT, te, q), jnp.int32),
                        pltpu.SemaphoreType.DMA((EXPERT_RING,))])
    return pl.pallas_call(
        functools.partial(_expert_kernel, ff=ff, n_sub=EXPERT_ROWS // EXPERT_SUB_ROWS),
        grid_spec=grid_spec,
        out_shape=jax.ShapeDtypeStruct(xs.shape, jnp.int32),
        compiler_params=_cparams(1, 40),
        name="moe_experts",
    )(block_e, n_valid, n_used, xs, w_gate, w_up, w_down)


SC_WINDOW = 128


def _sc_mesh():
    return plsc.VectorSubcoreMesh(core_axis_name="core", subcore_axis_name="subcore")


def _sc_scatter_rows(rows, dests, n_out):
    n, width = rows.shape
    n_k = len(dests)
    assert n % SC_WINDOW == 0

    @functools.partial(pl.kernel, out_type=jax.ShapeDtypeStruct((n_out, width), rows.dtype),
                       mesh=_sc_mesh(), name="sc_dispatch_rows")
    def scatter_kernel(x_hbm, *refs):
        idx_hbm, o_hbm = refs[:n_k], refs[n_k]

        def body(x_vmem, *idx_vmem):
            for iv in idx_vmem:
                pltpu.sync_copy(x_vmem, o_hbm.at[iv.at[0]])

        pltpu.emit_pipeline(
            body,
            grid=(n // SC_WINDOW,),
            in_specs=[pl.BlockSpec((SC_WINDOW, width), lambda i: (i, 0))]
            + [pl.BlockSpec((1, SC_WINDOW), lambda i: (0, i))] * n_k,
            out_specs=[],
            core_axis_name=("core", "subcore"),
            dimension_semantics=(pltpu.PARALLEL,),
        )(x_hbm, *idx_hbm)

    return scatter_kernel(rows, *dests)


def _sc_gather_rows(table, idx):
    n = idx.shape[1]
    width = table.shape[1]
    assert n % SC_WINDOW == 0

    @functools.partial(pl.kernel, out_type=jax.ShapeDtypeStruct((n, width), table.dtype),
                       mesh=_sc_mesh(), name="sc_collect_rows")
    def gather_kernel(t_hbm, i_hbm, o_hbm):
        def body(i_vmem, o_vmem):
            pltpu.sync_copy(t_hbm.at[i_vmem.at[0]], o_vmem)

        pltpu.emit_pipeline(
            body,
            grid=(n // SC_WINDOW,),
            in_specs=[pl.BlockSpec((1, SC_WINDOW), lambda i: (0, i))],
            out_specs=[pl.BlockSpec((SC_WINDOW, width), lambda i: (i, 0))],
            core_axis_name=("core", "subcore"),
            dimension_semantics=(pltpu.PARALLEL,),
        )(i_hbm, o_hbm)

    return gather_kernel(table, idx)


def _shared_kernel(tok_ref, sgu_ref, sd_ref, o_ref, *, ff):
    gu = _dot(tok_ref[...], sgu_ref[...])
    gate = gu[:, :ff]
    hmid = (gate * _sigmoid(gate) * gu[:, ff:]).astype(BF16)
    o_ref[...] = _dot(hmid, sd_ref[...]).astype(BF16)


def _shared_call(tok, sgu, sd, *, rows, tm):
    d = tok.shape[1]
    ff = sd.shape[0]
    return pl.pallas_call(
        functools.partial(_shared_kernel, ff=ff),
        grid=(rows // tm,),
        in_specs=[pl.BlockSpec((tm, d), lambda i: (i, 0)),
                  pl.BlockSpec((d, 2 * ff), lambda i: (0, 0)),
                  pl.BlockSpec((ff, d), lambda i: (0, 0))],
        out_specs=pl.BlockSpec((tm, d), lambda i: (i, 0)),
        out_shape=jax.ShapeDtypeStruct((rows, d), BF16),
        compiler_params=_cparams(1, 40),
        name="moe_shared_expert",
    )(tok, sgu, sd)


def _moe_out_kernel(shared_ref, yg_ref, gk_ref, x_ref, g_ref, gate_ref, xo_ref):
    shared = shared_ref[...].astype(F32)
    gk = gk_ref[...]
    lo_acc, hi_acc = None, None
    for k in range(TOP_K):
        lo, hi = _unpack_halves(jnp.concatenate([yg_ref[j, k] for j in range(SC_SPLIT)], axis=-1))
        w = gk[:, k:k + 1]
        lo_acc = lo * w if lo_acc is None else lo_acc + lo * w
        hi_acc = hi * w if hi_acc is None else hi_acc + hi * w
    y = jnp.concatenate([lo_acc, hi_acc], axis=-1) + shared
    xo_ref[...] = x_ref[...] + gate_ref[...] * _rms(y, g_ref[...])


def _moe_out_call(shared, yg, gk, xs, g, gate, *, rows, tm, seq, n_batch):
    d = xs.shape[1]
    seg = _seg_map(tm, seq, n_batch)
    row = lambda: pl.BlockSpec((tm, d), lambda i: (i, 0))
    return pl.pallas_call(
        _moe_out_kernel,
        grid=(rows // tm,),
        in_specs=[row(),
                  pl.BlockSpec((SC_SPLIT, TOP_K, tm, d // 2 // SC_SPLIT), lambda i: (0, 0, i, 0)),
                  pl.BlockSpec((tm, SUBLANES), lambda i: (i, 0)),
                  row(),
                  pl.BlockSpec((1, d), lambda i: (0, 0)),
                  pl.BlockSpec((None, 1, d), seg)],
        out_specs=row(),
        out_shape=jax.ShapeDtypeStruct((rows, d), F32),
        compiler_params=_cparams(1, 48),
        name="moe_combine",
    )(shared, yg, gk, xs, g, gate)


def _dest_kernel(e_ref, pos_ref, ps_ref, o_ref, *, n_rows):
    e = e_ref[...].astype(F32)
    ps = ps_ref[...]
    n_exp, tm = ps.shape[0], e.shape[1]
    ids = lax.broadcasted_iota(jnp.int32, (n_exp, tm), 0).astype(F32)
    rows = [jnp.sum(jnp.where(ids == e[k:k + 1, :], ps, 0.0), axis=0, keepdims=True) for k in range(SUBLANES)]
    dest = jnp.concatenate(rows, axis=0).astype(jnp.int32) + pos_ref[...]
    for j in range(SC_SPLIT):
        o_ref[j] = dest + j * n_rows


def _dest_call(e_tk, pos_tk, pad_starts, *, n_rows, tm):
    rows = e_tk.shape[1]
    n_exp = pad_starts.shape[0]
    lane_spec = pl.BlockSpec((SUBLANES, tm), lambda i: (0, i))
    return pl.pallas_call(
        functools.partial(_dest_kernel, n_rows=n_rows),
        grid=(rows // tm,),
        in_specs=[lane_spec, lane_spec, pl.BlockSpec((n_exp, 1), lambda i: (0, 0))],
        out_specs=pl.BlockSpec((SC_SPLIT, SUBLANES, tm), lambda i: (0, 0, i)),
        out_shape=jax.ShapeDtypeStruct((SC_SPLIT, SUBLANES, rows), jnp.int32),
        compiler_params=_cparams(1, 32),
        name="moe_dest_rows",
    )(e_tk, pos_tk, pad_starts.astype(F32).reshape(n_exp, 1))


def _moe_layer(xs, tok, tokp, routing, g_out, gate, layer, w_gate, w_up, w_down, sh_gate, sh_up, sh_down,
               *, rows, tm, seq, n_batch):
    e_tk, g_tk, pos_tk, counts = routing
    n_exp = counts.shape[0]
    te = EXPERT_ROWS
    cnt = counts[:, 0].astype(jnp.int32)
    padded = (cnt + te - 1) // te * te
    pad_ends = jnp.cumsum(padded)
    pad_starts = pad_ends - padded
    expert_ids = jnp.arange(n_exp, dtype=jnp.int32)
    n_blocks = -(-(rows * TOP_K + n_exp * (te - 1)) // te)
    n_used = pad_ends[-1] // te
    blk = jnp.arange(n_blocks, dtype=jnp.int32)
    last = jnp.minimum(blk, n_used - 1) * te
    owner = (last[:, None] >= pad_ends[None, :]).astype(jnp.int32)
    block_e = jnp.minimum(jnp.sum(owner, axis=1), n_exp - 1)
    own_hot = block_e[:, None] == expert_ids[None, :]
    n_valid = jnp.sum(jnp.where(own_hot, (pad_starts + cnt)[None, :], 0), axis=1) - last
    n_valid = jnp.clip(n_valid, 0, te)
    q = tokp.shape[2]
    n_rows = n_blocks * te
    piece = _dest_call(e_tk, pos_tk, pad_starts, n_rows=n_rows, tm=tm)[:, :TOP_K, :]
    xg = _sc_scatter_rows(tokp.reshape(SC_SPLIT * rows, q),
                          [piece[:, k, :].reshape(1, SC_SPLIT * rows) for k in range(TOP_K)],
                          SC_SPLIT * n_rows).reshape(SC_SPLIT, n_rows, q)
    sgu = jnp.concatenate([sh_gate, sh_up], axis=1).astype(BF16)
    shared = _shared_call(tok, sgu, sh_down.astype(BF16), rows=rows, tm=tm)
    ys = _expert_call(block_e, n_valid, n_used.reshape(1).astype(jnp.int32), xg, layer, w_gate, w_up, w_down)
    yg = _sc_gather_rows(ys.reshape(SC_SPLIT * n_rows, q),
                         piece.reshape(1, -1)).reshape(SC_SPLIT, TOP_K, rows, q)
    return _moe_out_call(shared, yg, g_tk, xs, g_out, gate, rows=rows, tm=tm, seq=seq, n_batch=n_batch)


def _rope(t, cos, sin):
    half = DA_QK_DIM // 2
    up = pltpu.roll(t, shift=LANES - half, axis=1)
    dn = pltpu.roll(t, shift=half, axis=1)
    lane = lax.broadcasted_iota(jnp.int32, t.shape, 1) % DA_QK_DIM
    return t * cos + jnp.where(lane < half, -up, dn) * sin


def _qkv_kernel(x_ref, g_ref, sh_ref, sc_ref, w_ref, cos_ref, sin_ref, q_ref, k_ref, v_ref, *, qkw):
    h = _normmod(x_ref[...], g_ref[...], sh_ref[...], sc_ref[...]).astype(BF16)
    qkv = _dot(h, w_ref[...])
    cos, sin = cos_ref[...], sin_ref[...]
    q_scale = DA_QK_DIM ** -0.5 * math.log2(math.e)
    for hb in range(qkw // LANES):
        lo, hi = hb * LANES, (hb + 1) * LANES
        q_ref[:, lo:hi] = (_rope(qkv[:, lo:hi], cos, sin) * q_scale).astype(BF16)
        k_ref[:, lo:hi] = _rope(qkv[:, qkw + lo:qkw + hi], cos, sin).astype(BF16)
    v_ref[...] = qkv[:, 2 * qkw:].astype(BF16)


def _qkv_call(xs, g, shift, scale, w_qkv, cos_t, sin_t, *, qkw, tm, seq, ctx_len, n_batch):
    rows, d = xs.shape
    n = w_qkv.shape[1]
    vw = n - 2 * qkw
    lat_tiles, lat_tps, ctx_tps = n_batch * seq // tm, seq // tm, ctx_len // tm
    seg = _seg_map(tm, seq, n_batch)

    def kv_map(i):
        c = i - lat_tiles
        is_lat = i < lat_tiles
        return (jnp.where(is_lat, i // lat_tps, c // ctx_tps),
                jnp.where(is_lat, ctx_tps + i % lat_tps, c % ctx_tps), 0)

    rope_map = lambda i: (jnp.where(i < lat_tiles, i % lat_tps, lat_tps), 0)
    return pl.pallas_call(
        functools.partial(_qkv_kernel, qkw=qkw),
        grid=(rows // tm,),
        in_specs=[pl.BlockSpec((tm, d), lambda i: (i, 0)),
                  pl.BlockSpec((1, d), lambda i: (0, 0)),
                  pl.BlockSpec((None, 1, d), seg),
                  pl.BlockSpec((None, 1, d), seg),
                  pl.BlockSpec((d, n), lambda i: (0, 0)),
                  pl.BlockSpec((tm, LANES), rope_map),
                  pl.BlockSpec((tm, LANES), rope_map)],
        out_specs=[pl.BlockSpec((tm, qkw), lambda i: (i, 0)),
                   pl.BlockSpec((None, tm, qkw), kv_map),
                   pl.BlockSpec((None, tm, vw), kv_map)],
        out_shape=[jax.ShapeDtypeStruct((rows, qkw), BF16),
                   jax.ShapeDtypeStruct((n_batch, ctx_len + seq, qkw), BF16),
                   jax.ShapeDtypeStruct((n_batch, ctx_len + seq, vw), BF16)],
        compiler_params=_cparams(1, 48),
        name="qkv_rope",
    )(xs, g, shift, scale, w_qkv, cos_t, sin_t)


def _rope_tables(seq, tm):
    pos = np.arange(seq)
    n_freq = DA_QK_DIM // 4
    inv = np.power(ROPE_BASE, -np.arange(n_freq, dtype=np.float32) / n_freq).astype(np.float32)
    row = (pos // GRID_W).astype(np.float32)[:, None] * inv
    col = (pos % GRID_W).astype(np.float32)[:, None] * inv
    ang = np.concatenate([row, col], axis=-1).astype(np.float32)
    ang = np.tile(ang, (1, LANES // ang.shape[1]))
    cos = np.concatenate([np.cos(ang), np.ones((tm, LANES))], axis=0)
    sin = np.concatenate([np.sin(ang), np.zeros((tm, LANES))], axis=0)
    return jnp.asarray(cos, F32), jnp.asarray(sin, F32)


def _attn_kernel(lp_ref, q_ref, k_ref, v_ref, sg_ref, o_ref, vext, *, n_sub, lam_init):
    @pl.when(pl.program_id(2) == 0)
    def _():
        vext[:, :LANES] = v_ref[...]
        vext[:, LANES:] = jnp.ones((vext.shape[0], LANES), BF16)

    lp = lp_ref[...]
    lam = (jnp.exp(jnp.sum(lp[0:1] * lp[1:2], axis=1, keepdims=True))
           - jnp.exp(jnp.sum(lp[2:3] * lp[3:4], axis=1, keepdims=True)) + lam_init)
    ts = q_ref.shape[0] // n_sub
    lane = lax.broadcasted_iota(jnp.int32, (ts, LANES), 1)
    zero = jnp.zeros((ts, LANES), BF16)
    k = k_ref[...]
    scores = []
    for a in range(n_sub):
        q = q_ref[a * ts:(a + 1) * ts, :]
        qq = jnp.concatenate([jnp.where(lane < DA_QK_DIM, q, zero),
                              jnp.where(lane >= DA_QK_DIM, q, zero)], axis=0)
        scores.append(_dot_nt(qq, k))
    for a in range(n_sub):
        s = scores[a]
        p = jnp.exp2(s - jnp.max(s, axis=-1, keepdims=True)).astype(BF16)
        oe = _dot(p, vext[...])
        on = oe[:, :LANES] / oe[:, LANES:LANES + 1]
        o = on[:ts] - lam * on[ts:]
        o_ref[a * ts:(a + 1) * ts, :] = (_rms(o, sg_ref[...]) * (1.0 - lam_init)).astype(BF16)


def _attn_call(lam_p, q, k_all, v_all, subln_g, *, tq, seq, n_batch, lam_init):
    n_heads = q.shape[1] // LANES
    lk = k_all.shape[1]
    qt = seq // tq
    return pl.pallas_call(
        functools.partial(_attn_kernel, n_sub=tq // ATTN_SUB_ROWS, lam_init=lam_init),
        grid=(n_batch, n_heads, qt),
        in_specs=[pl.BlockSpec(lam_p.shape, lambda b, h, i: (0, 0)),
                  pl.BlockSpec((tq, LANES), lambda b, h, i: (b * qt + i, h)),
                  pl.BlockSpec((None, lk, LANES), lambda b, h, i: (b, 0, h)),
                  pl.BlockSpec((None, lk, LANES), lambda b, h, i: (b, 0, h)),
                  pl.BlockSpec((1, LANES), lambda b, h, i: (0, 0))],
        out_specs=pl.BlockSpec((tq, LANES), lambda b, h, i: (b * qt + i, h)),
        out_shape=jax.ShapeDtypeStruct((n_batch * seq, n_heads * LANES), BF16),
        scratch_shapes=[pltpu.VMEM((lk, 2 * LANES), BF16)],
        compiler_params=_cparams(3, 48),
        name="diff_attention",
    )(lam_p, q, k_all, v_all, subln_g)


def kernel(x, c, ctx, c_ctx, ada_w, ada_b, norm_g, cf_w_in, cf_conv_w, cf_conv_b, cf_ln_g, cf_ln_b,
           cf_w_out, da_w_qkv, da_lambda, da_subln_g, da_w_out, moe_router_w, moe_router_b,
           moe_w_gate, moe_w_up, moe_w_down, moe_sh_gate, moe_sh_up, moe_sh_down):
    n_batch, seq, d = x.shape
    ctx_len = ctx.shape[1]
    depth = ada_w.shape[0]
    assert depth == 2 and n_batch + 1 <= N_SEG
    assert DA_QK_DIM * 2 == LANES and da_subln_g.shape[1] == LANES
    rows_lat, rows_ctx = n_batch * seq, n_batch * ctx_len
    tm = ROW_TILE
    tq = SEQ_TILE
    assert seq % tm == 0 and rows_ctx % tm == 0 and seq % tq == 0 and ctx_len % tq == 0

    x_lat, x_ctx = x.reshape(rows_lat, d), ctx.reshape(rows_ctx, d)
    cond = jnp.concatenate([c, c_ctx[None, :], jnp.zeros((N_SEG - n_batch - 1, d), F32)], axis=0)
    mods = _mods_call(cond, ada_w, ada_b)
    mod = lambda layer, k: mods[layer, :, k * d:(k + 1) * d][:, None, :]
    gain = lambda layer, k: norm_g[layer, k][None, :]
    common = dict(seq=seq, n_batch=n_batch)

    cc = cf_conv_w.shape[-1]
    fc = cf_w_in.shape[2] - 2 * cc
    gw = fc // FOURIER_GROUPS
    ch_ang = 2.0 * np.pi * (np.outer(np.arange(gw), np.arange(gw)) % gw) / gw
    cs = jnp.asarray(np.concatenate([np.cos(ch_ang), np.sin(ch_ang)], axis=1) / math.sqrt(gw), BF16)
    a_glu, ab = _inproj_call(x_lat, x_ctx, gain(0, 0), mod(0, 0), mod(0, 1), cf_w_in[0].astype(BF16), cs,
                             cc=cc, gw=gw, tm=tm, **common)
    gsz = cc // CONV_GROUPS
    gid = np.arange(cc) // gsz
    gmean = jnp.asarray((gid[:, None] == gid[None, :]) / gsz, BF16)
    a_act = _conv_call(a_glu, cf_conv_w[0][:, 0, :], cf_conv_b[0][None, :], cf_ln_g[0][None, :],
                       cf_ln_b[0][None, :], gmean, tl=tq, seq=seq, ctx_len=ctx_len, n_batch=n_batch)
    fr_lat = _seqdft_call(ab, seq, row0=0, n_seq=n_batch, fc=fc, tm=min(seq // 2, 256))
    fr_ctx = _seqdft_call(ab, ctx_len, row0=rows_lat, n_seq=n_batch, fc=fc, tm=min(ctx_len // 2, 256))
    w_out = cf_w_out[0].astype(BF16)
    xs, tok, tokp, *routing = _outproj_call(
        [(a_act, a_act), (fr_lat, fr_ctx)], [w_out[:cc], w_out[cc:]], x_lat, x_ctx, gain(0, 1), mod(0, 2), gain(0, 2), mod(0, 3),
        mod(0, 4), moe_router_w[0], moe_router_b[0], rows=rows_lat + rows_ctx, tm=tm, **common)
    xs = _moe_layer(xs, tok, tokp, routing, gain(0, 3), mod(0, 5), 0, moe_w_gate, moe_w_up, moe_w_down,
                    moe_sh_gate[0], moe_sh_up[0], moe_sh_down[0], rows=rows_lat + rows_ctx, tm=tm, **common)

    qkw = DA_HEADS * 2 * DA_QK_DIM
    lam_init = 0.8 - 0.6 * math.exp(-0.3 * 1)
    cos_t, sin_t = _rope_tables(seq, tq)
    q, k_all, v_all = _qkv_call(xs, gain(1, 0), mod(1, 0), mod(1, 1), da_w_qkv[0].astype(BF16),
                                cos_t, sin_t, qkw=qkw, tm=tq, ctx_len=ctx_len, **common)
    o = _attn_call(da_lambda[0], q, k_all, v_all, da_subln_g[0][None, :], tq=math.gcd(seq, ATTN_ROWS), seq=seq,
                   n_batch=n_batch, lam_init=lam_init)
    xs, tok, tokp, *routing = _outproj_call(
        [(o, o)], [da_w_out[0].astype(BF16)], xs, xs, gain(1, 1), mod(1, 2), gain(1, 2), mod(1, 3), mod(1, 4),
        moe_router_w[1], moe_router_b[1], rows=rows_lat, tm=tm, **common)
    xs = _moe_layer(xs, tok, tokp, routing, gain(1, 3), mod(1, 5), 1, moe_w_gate, moe_w_up, moe_w_down,
                    moe_sh_gate[1], moe_sh_up[1], moe_sh_down[1], rows=rows_lat, tm=tm, **common)
    return xs.reshape(n_batch, seq, d)
```

```python
import functools
import math

import numpy as np
import jax
import jax.numpy as jnp
from jax import lax
from jax.experimental import pallas as pl
from jax.experimental.pallas import tpu as pltpu
from jax.experimental.pallas import tpu_sc as plsc

F32 = jnp.float32
BF16 = jnp.bfloat16

EPS = 1e-6
GRID_W = 64
CONV_GROUPS = 8
FOURIER_GROUPS = 4
DA_HEADS = 8
DA_QK_DIM = 64
ROPE_BASE = 10000.0
N_GROUPS = 8
TOPK_GROUPS = 4
TOP_K = 6
ROUTED_SCALE = 2.5

LANES = 128
SUBLANES = 8
ROW_TILE = 512
SEQ_TILE = 256
N_SEG = 16
HALO = 16
CONV_GROUP_ROWS = 128
EXPERT_ROWS = 1024
EXPERT_SUB_ROWS = 128
EXPERT_RING = 3
ATTN_ROWS = 1024
ATTN_SUB_ROWS = 128
MIB = 1024 * 1024


def _cparams(n_axes, vmem_mib):
    return pltpu.CompilerParams(dimension_semantics=("arbitrary",) * n_axes,
                                vmem_limit_bytes=vmem_mib * MIB)


def _sigmoid(v):
    return 1.0 / (1.0 + jnp.exp(-v))


def _rms(v, g):
    return v * lax.rsqrt(jnp.mean(v * v, axis=-1, keepdims=True) + EPS) * g


def _normmod(v, g, shift, scale):
    return _rms(v, g) * (1.0 + scale) + shift


def _split_bf16(v):
    hi = v.astype(BF16)
    lo = (v - hi.astype(F32)).astype(BF16)
    return hi, lo


def _pack_halves(v):
    half = v.shape[1] // 2
    word = pltpu.pack_elementwise([v[:, :half], v[:, half:]], packed_dtype=BF16)
    return lax.bitcast_convert_type(word, jnp.int32)


def _unpack_halves(w):
    u = lax.bitcast_convert_type(w, jnp.uint32)
    lo = lax.bitcast_convert_type(u << 16, F32)
    hi = lax.bitcast_convert_type(u & jnp.uint32(0xFFFF0000), F32)
    return lo, hi


SC_SPLIT = 2


def _store_pieces(ref, words):
    q = words.shape[1] // SC_SPLIT
    for j in range(SC_SPLIT):
        ref[j] = words[:, j * q:(j + 1) * q]


def _load_pieces(ref):
    return jnp.concatenate([ref[j] for j in range(SC_SPLIT)], axis=-1)


def _dot(a, b):
    return jnp.dot(a, b, preferred_element_type=F32)


def _dot_nt(a, b):
    return lax.dot_general(a, b, (((1,), (1,)), ((), ())), preferred_element_type=F32)


def _mods_kernel(c_ref, w_ref, b_ref, o_ref):
    cv = c_ref[...]
    o_ref[...] = _dot(cv * _sigmoid(cv), w_ref[...]) + b_ref[...]


def _mods_call(cond, ada_w, ada_b):
    depth, d, n = ada_w.shape
    tn = n // 4
    return pl.pallas_call(
        _mods_kernel,
        grid=(depth, n // tn),
        in_specs=[pl.BlockSpec((N_SEG, d), lambda l, j: (0, 0)),
                  pl.BlockSpec((None, d, tn), lambda l, j: (l, 0, j)),
                  pl.BlockSpec((None, 1, tn), lambda l, j: (l, 0, j))],
        out_specs=pl.BlockSpec((None, N_SEG, tn), lambda l, j: (l, 0, j)),
        out_shape=jax.ShapeDtypeStruct((depth, N_SEG, n), F32),
        compiler_params=_cparams(2, 40),
        name="adaln_mods",
    )(cond, ada_w, ada_b.reshape(depth, 1, n))


def _two_source_specs(tm, d, n_head_tiles):
    return (pl.BlockSpec((tm, d), lambda i: (jnp.minimum(i, n_head_tiles - 1), 0)),
            pl.BlockSpec((tm, d), lambda i: (jnp.maximum(i - n_head_tiles, 0), 0)))


def _pick_rows(head_ref, tail_ref, n_head_tiles):
    return jnp.where(pl.program_id(0) < n_head_tiles, head_ref[...], tail_ref[...])


def _inproj_kernel(xh_ref, xt_ref, g_ref, sh_ref, sc_ref, w_ref, cs_ref, a_ref, ab_ref, *, cc, gw, n_head):
    x = _pick_rows(xh_ref, xt_ref, n_head)
    h = _normmod(x, g_ref[...], sh_ref[...], sc_ref[...]).astype(BF16)
    u = _dot(h, w_ref[...])
    a_ref[...] = u[:, :cc] * _sigmoid(u[:, cc:2 * cc])
    f = u[:, 2 * cc:].astype(BF16)
    n_g = f.shape[1] // gw
    parts = [_dot(f[:, g * gw:(g + 1) * gw], cs_ref[...]) for g in range(n_g)]
    cos_part = [p[:, :gw] for p in parts]
    sin_part = [p[:, gw:] for p in parts]
    ab_ref[...] = jnp.concatenate(cos_part + sin_part, axis=-1).astype(BF16)


def _seg_map(tm, seq, n_batch):
    return lambda i: (jnp.minimum(i * tm // seq, n_batch), 0, 0)


def _inproj_call(x_head, x_tail, g, shift, scale, w_in, cs, *, cc, gw, tm, seq, n_batch):
    d = x_head.shape[1]
    rows = x_head.shape[0] + x_tail.shape[0]
    n_head = x_head.shape[0] // tm
    n = w_in.shape[1]
    fc = n - 2 * cc
    seg = _seg_map(tm, seq, n_batch)
    return pl.pallas_call(
        functools.partial(_inproj_kernel, cc=cc, gw=gw, n_head=n_head),
        grid=(rows // tm,),
        in_specs=[*_two_source_specs(tm, d, n_head),
                  pl.BlockSpec((1, d), lambda i: (0, 0)),
                  pl.BlockSpec((None, 1, d), seg),
                  pl.BlockSpec((None, 1, d), seg),
                  pl.BlockSpec((d, n), lambda i: (0, 0)),
                  pl.BlockSpec((gw, 2 * gw), lambda i: (0, 0))],
        out_specs=[pl.BlockSpec((tm, cc), lambda i: (i, 0)),
                   pl.BlockSpec((tm, 2 * fc), lambda i: (i, 0))],
        out_shape=[jax.ShapeDtypeStruct((rows, cc), F32),
                   jax.ShapeDtypeStruct((rows, 2 * fc), BF16)],
        compiler_params=_cparams(1, 40),
        name="inproj_glu_chdft",
    )(x_head, x_tail, g, shift, scale, w_in, cs)


def _conv_kernel(prev_ref, main_ref, next_ref, w_ref, cb_ref, lg_ref, lb_ref, gm_ref, o_ref,
                 buf, cv, shifted, *, tl, width, lat_tiles, lat_tps, ctx_tps, chunk):
    i = pl.program_id(0)
    is_lat = i < lat_tiles
    tps = jnp.where(is_lat, lat_tps, ctx_tps)
    j = jnp.where(is_lat, i, i - lat_tiles) % tps
    zero = jnp.zeros((HALO, buf.shape[1]), F32)
    buf[0:HALO, :] = jnp.where(j > 0, prev_ref[...], zero)
    buf[HALO:HALO + tl, :] = main_ref[...]
    buf[HALO + tl:HALO + tl + HALO, :] = jnp.where(j < tps - 1, next_ref[...], zero)
    base = HALO - width // 2
    span = shifted.shape[1]
    for s in range(1, SUBLANES):
        shifted[s] = buf[s:s + span, :]

    def window(off, r0):
        s, m = off % SUBLANES, off - off % SUBLANES
        rows = buf[m + r0:m + r0 + chunk, :] if s == 0 else shifted[s, m + r0:m + r0 + chunk, :]
        return rows.reshape(chunk // SUBLANES, SUBLANES, rows.shape[1])

    gm = gm_ref[...]
    for g0 in range(0, tl, CONV_GROUP_ROWS):
        for r0 in range(g0, g0 + CONV_GROUP_ROWS, chunk):
            acc = window(base, r0) * w_ref[0]
            for t in range(1, width):
                acc = acc + window(base + t, r0) * w_ref[t]
            cv[r0:r0 + chunk, :] = acc.reshape(chunk, acc.shape[2])
        a = cv[g0:g0 + CONV_GROUP_ROWS, :] + cb_ref[...]
        a_hi, a_lo = _split_bf16(a)
        mu = _dot(a_hi, gm) + _dot(a_lo, gm)
        dl = a - mu
        q_hi, q_lo = _split_bf16(dl * dl)
        var = _dot(q_hi, gm) + _dot(q_lo, gm)
        y = dl * lax.rsqrt(var + EPS) * lg_ref[...] + lb_ref[...]
        o_ref[g0:g0 + CONV_GROUP_ROWS, :] = (y * _sigmoid(y)).astype(BF16)


def _conv_call(a_glu, conv_w, conv_b, ln_g, ln_b, gmean, *, tl, seq, ctx_len, n_batch):
    rows, ch = a_glu.shape
    width = conv_w.shape[0]
    assert width // 2 <= HALO and tl % HALO == 0 and tl % CONV_GROUP_ROWS == 0
    hb = tl // HALO
    last_halo = rows // HALO - 1
    return pl.pallas_call(
        functools.partial(_conv_kernel, tl=tl, width=width, lat_tiles=n_batch * seq // tl,
                          lat_tps=seq // tl, ctx_tps=ctx_len // tl, chunk=32),
        grid=(rows // tl,),
        in_specs=[pl.BlockSpec((HALO, ch), lambda i: (jnp.maximum(i * hb - 1, 0), 0)),
                  pl.BlockSpec((tl, ch), lambda i: (i, 0)),
                  pl.BlockSpec((HALO, ch), lambda i: (jnp.minimum((i + 1) * hb, last_halo), 0)),
                  pl.BlockSpec((width, SUBLANES, ch), lambda i: (0, 0, 0)),
                  pl.BlockSpec((1, ch), lambda i: (0, 0)),
                  pl.BlockSpec((1, ch), lambda i: (0, 0)),
                  pl.BlockSpec((1, ch), lambda i: (0, 0)),
                  pl.BlockSpec((ch, ch), lambda i: (0, 0))],
        out_specs=pl.BlockSpec((tl, ch), lambda i: (i, 0)),
        out_shape=jax.ShapeDtypeStruct((rows, ch), BF16),
        scratch_shapes=[pltpu.VMEM((tl + 2 * HALO, ch), F32), pltpu.VMEM((tl, ch), F32),
                        pltpu.VMEM((SUBLANES, tl + 2 * HALO - SUBLANES, ch), F32)],
        compiler_params=_cparams(1, 40),
        name="dwconv_groupln_swish",
    )(a_glu, a_glu, a_glu, jnp.broadcast_to(conv_w[:, None, :], (width, SUBLANES, ch)), conv_b, ln_g, ln_b, gmean)


EDGE_ROWS = 16


def _seqdft_kernel(c_ref, s_ref, cx_ref, sx_ref, jm_ref, a_ref, b_ref, o_ref, *, tm, n_tiles):
    i = pl.program_id(1)
    a, b = a_ref[...], b_ref[...]
    p, qv = _dot(c_ref[...], a), _dot(s_ref[...], b)
    o_ref[pl.ds(pl.multiple_of(i * tm, tm), tm), :] = (p - qv).astype(BF16)
    edge = _dot(cx_ref[...], a) + _dot(sx_ref[...], b)
    pick = lax.broadcasted_iota(jnp.int32, edge.shape, 0) == i
    edge_row = jnp.sum(jnp.where(pick, edge, 0.0), axis=0, keepdims=True)
    mirrored = _dot(jm_ref[...], (p + qv).astype(BF16))
    srow = lax.broadcasted_iota(jnp.int32, mirrored.shape, 0)
    mirrored = jnp.where(srow == 0, edge_row, mirrored)
    o_ref[pl.ds(pl.multiple_of((n_tiles - 1 - i) * tm, tm), tm), :] = mirrored.astype(BF16)


def _seqdft_call(ab, length, *, row0, n_seq, fc, tm):
    n_tiles = length // tm
    n_half = n_tiles // 2
    assert n_tiles % 2 == 0 and n_half <= EDGE_ROWS
    scale = 1.0 / math.sqrt(length)
    cmat, smat = _dft_tables(length, scale, np.arange(length // 2))
    edge_k = np.zeros((EDGE_ROWS,), np.int64)
    edge_k[:n_half] = tm * (np.arange(n_half) + 1)
    cx, sx = _dft_tables(length, scale, edge_k)
    s_idx = np.arange(tm)
    reversal = jnp.asarray((s_idx[None, :] == (tm - s_idx)[:, None]) & (s_idx[:, None] > 0), BF16)
    assert row0 % length == 0
    seq0 = row0 // length
    whole = lambda arr: pl.BlockSpec(arr.shape, lambda b, i: (0, 0))
    return pl.pallas_call(
        functools.partial(_seqdft_kernel, tm=tm, n_tiles=n_tiles),
        grid=(n_seq, n_half),
        in_specs=[pl.BlockSpec((tm, length), lambda b, i: (i, 0)),
                  pl.BlockSpec((tm, length), lambda b, i: (i, 0)),
                  whole(cx), whole(sx), whole(reversal),
                  pl.BlockSpec((length, fc), lambda b, i: (seq0 + b, 0)),
                  pl.BlockSpec((length, fc), lambda b, i: (seq0 + b, 1))],
        out_specs=pl.BlockSpec((length, fc), lambda b, i: (b, 0)),
        out_shape=jax.ShapeDtypeStruct((n_seq * length, fc), BF16),
        compiler_params=_cparams(2, 48),
        name="seq_dft",
    )(cmat, smat, cx, sx, reversal, ab, ab)


def _dft_tables(length, scale, rows):
    k = np.asarray(rows, np.int64)[:, None]
    if length <= 512:
        ang = 2.0 * np.pi * ((k * np.arange(length)[None, :]) % length) / length
        return (jnp.asarray(np.cos(ang) * scale, BF16), jnp.asarray(np.sin(ang) * scale, BF16))
    r = 64
    assert length % r == 0
    alpha = 2.0 * np.pi * ((k * np.arange(length // r)[None, :] * r) % length) / length
    beta = 2.0 * np.pi * ((k * np.arange(r)[None, :]) % length) / length
    ca, sa = jnp.asarray(np.cos(alpha), F32)[:, :, None], jnp.asarray(np.sin(alpha), F32)[:, :, None]
    cb, sb = jnp.asarray(np.cos(beta) * scale, F32)[:, None, :], jnp.asarray(np.sin(beta) * scale, F32)[:, None, :]
    cmat = (ca * cb - sa * sb).reshape(k.shape[0], length).astype(BF16)
    smat = (sa * cb + ca * sb).reshape(k.shape[0], length).astype(BF16)
    return cmat, smat


def _outproj_kernel(*refs, in_heads, n_head):
    n_in = len(in_heads)
    ins, ws = refs[:2 * n_in], refs[2 * n_in:3 * n_in]
    (xh_ref, xt_ref, g1_ref, gate_ref, g2_ref, sh_ref, sc_ref, wh_ref, wl_ref, rb_ref,
     xo_ref, tok_ref, tokp_ref, e_ref, gt_ref, pos_ref, cnt_ref, carry) = refs[3 * n_in:]
    y = None
    for j, w_ref in enumerate(ws):
        part = _dot(_pick_rows(ins[2 * j], ins[2 * j + 1], in_heads[j]), w_ref[...])
        y = part if y is None else y + part
    x1 = _pick_rows(xh_ref, xt_ref, n_head) + gate_ref[...] * _rms(y, g1_ref[...])
    xo_ref[...] = x1
    tok = _normmod(x1, g2_ref[...], sh_ref[...], sc_ref[...])
    tok_ref[...] = tok.astype(BF16)
    _store_pieces(tokp_ref, _pack_halves(tok))
    _route(tok, wh_ref, wl_ref, rb_ref, e_ref, gt_ref, pos_ref, cnt_ref, carry)


def _outproj_call(ins, ws, x_head, x_tail, g1, gate, g2, shift, scale, router_w, router_b,
                  *, rows, tm, seq, n_batch):
    d = x_head.shape[1]
    n_head = min(x_head.shape[0], rows) // tm
    n_exp = router_w.shape[1]
    w_t = router_w.T
    w_hi = w_t.astype(BF16)
    w_lo = (w_t - w_hi.astype(F32)).astype(BF16)
    seg = _seg_map(tm, seq, n_batch)
    in_heads = tuple(min(head.shape[0], rows) // tm for head, _ in ins)
    pair_specs = [s for (head, _), nh in zip(ins, in_heads) for s in _two_source_specs(tm, head.shape[1], nh)]
    full_spec = lambda a: pl.BlockSpec(a.shape, lambda i: (0, 0))
    lane_spec = pl.BlockSpec((SUBLANES, tm), lambda i: (0, i))
    tok_rows = lambda dt: jax.ShapeDtypeStruct((SUBLANES, rows), dt)
    return pl.pallas_call(
        functools.partial(_outproj_kernel, in_heads=in_heads, n_head=n_head),
        grid=(rows // tm,),
        in_specs=pair_specs + [full_spec(w) for w in ws] + [
            *_two_source_specs(tm, d, n_head), full_spec(g1), pl.BlockSpec((None, 1, d), seg), full_spec(g2),
            pl.BlockSpec((None, 1, d), seg), pl.BlockSpec((None, 1, d), seg),
            full_spec(w_hi), full_spec(w_lo), pl.BlockSpec((n_exp, 1), lambda i: (0, 0))],
        out_specs=[pl.BlockSpec((tm, d), lambda i: (i, 0)), pl.BlockSpec((tm, d), lambda i: (i, 0)),
                   pl.BlockSpec((SC_SPLIT, tm, d // 2 // SC_SPLIT), lambda i: (0, i, 0)),
                   lane_spec, pl.BlockSpec((tm, SUBLANES), lambda i: (i, 0)), lane_spec,
                   pl.BlockSpec((n_exp, LANES), lambda i: (0, 0))],
        out_shape=[jax.ShapeDtypeStruct((rows, d), F32), jax.ShapeDtypeStruct((rows, d), BF16),
                   jax.ShapeDtypeStruct((SC_SPLIT, rows, d // 2 // SC_SPLIT), jnp.int32),
                   tok_rows(jnp.int32), jax.ShapeDtypeStruct((rows, SUBLANES), F32), tok_rows(jnp.int32),
                   jax.ShapeDtypeStruct((n_exp, LANES), F32)],
        scratch_shapes=[pltpu.VMEM((n_exp, 1), F32)],
        compiler_params=_cparams(1, 48),
        name="outproj_residual_route",
    )(*[a for pair in ins for a in pair], *ws, x_head, x_tail, g1, gate, g2, shift, scale, w_hi, w_lo,
      router_b.reshape(n_exp, 1))


def _pick_max(cur, idx):
    mx = jnp.max(cur, axis=0, keepdims=True)
    first = jnp.min(jnp.where(cur == mx, idx, float(cur.shape[0])), axis=0, keepdims=True)
    return first, idx == first


def _route(tok, wh_ref, wl_ref, rb_ref, e_ref, gt_ref, pos_ref, cnt_ref, carry):
    @pl.when(pl.program_id(0) == 0)
    def _():
        carry[...] = jnp.zeros_like(carry)

    t_hi, t_lo = _split_bf16(tok)
    wh, wl = wh_ref[...], wl_ref[...]
    logits = _dot_nt(wh, t_hi) + _dot_nt(wh, t_lo) + _dot_nt(wl, t_hi)
    n_exp, tm = logits.shape
    scores = _sigmoid(logits)
    biased = scores + rb_ref[...]
    gsz = n_exp // N_GROUPS
    neg = -jnp.inf

    b3 = biased.reshape(N_GROUPS, gsz, tm)
    im = lax.broadcasted_iota(jnp.int32, b3.shape, 1).astype(F32)
    m1 = jnp.max(b3, axis=1, keepdims=True)
    i1 = jnp.min(jnp.where(b3 == m1, im, float(gsz)), axis=1, keepdims=True)
    m2 = jnp.max(jnp.where(im == i1, neg, b3), axis=1, keepdims=True)
    gscore = (m1 + m2).reshape(N_GROUPS, tm)

    ig = lax.broadcasted_iota(jnp.int32, gscore.shape, 0).astype(F32)
    gsel = jnp.zeros_like(gscore)
    cur = gscore
    for _ in range(TOPK_GROUPS):
        _, hit = _pick_max(cur, ig)
        gsel = jnp.where(hit, 1.0, gsel)
        cur = jnp.where(hit, neg, cur)
    gsel3 = jnp.broadcast_to(gsel.reshape(N_GROUPS, 1, tm), b3.shape)
    cur = jnp.where(gsel3 > 0.0, b3, neg).reshape(n_exp, tm)

    ie = lax.broadcasted_iota(jnp.int32, (n_exp, tm), 0).astype(F32)
    sel = jnp.zeros((n_exp, tm), F32)
    picks, raw = [], []
    for _ in range(TOP_K):
        first, hit = _pick_max(cur, ie)
        picks.append(first)
        raw.append(jnp.sum(jnp.where(hit, scores, 0.0), axis=0, keepdims=True))
        sel = jnp.where(hit, 1.0, sel)
        cur = jnp.where(hit, neg, cur)
    total = raw[0]
    for r in raw[1:]:
        total = total + r

    ri = lax.broadcasted_iota(jnp.int32, (tm, tm), 0)
    ci = lax.broadcasted_iota(jnp.int32, (tm, tm), 1)
    upper = jnp.where(ri < ci, 1.0, 0.0).astype(BF16)
    rank = _dot(sel.astype(BF16), upper) + carry[...]
    ranks = [jnp.sum(jnp.where(ie == p, rank, 0.0), axis=0, keepdims=True) for p in picks]
    carry[...] = carry[...] + jnp.sum(sel, axis=1, keepdims=True)

    pad = jnp.zeros((SUBLANES - TOP_K, tm), F32)
    e_ref[...] = jnp.concatenate(picks + [pad], axis=0).astype(jnp.int32)
    gates = jnp.concatenate([r / total * ROUTED_SCALE for r in raw]
                            + [jnp.zeros((LANES - TOP_K, tm), F32)], axis=0)
    gt_ref[...] = gates.T[:, :SUBLANES]
    pos_ref[...] = jnp.concatenate(ranks + [pad], axis=0).astype(jnp.int32)
    cnt_ref[...] = jnp.broadcast_to(carry[...], cnt_ref.shape)


def _expert_kernel(be_ref, nv_ref, nu_ref, xs_hbm, wg_ref, wu_ref, wd_ref, ys_ref, wgu, wdn, xbuf, sem,
                   *, ff, n_sub):
    b = pl.program_id(0)
    n_used = nu_ref[0]
    depth, _, te, q = xbuf.shape

    def row_block_copy(blk):
        slot = blk % depth
        return pltpu.make_async_copy(xs_hbm.at[:, pl.ds(blk * te, te), :], xbuf.at[slot], sem.at[slot])

    @pl.when(b == 0)
    def _():
        for j in range(depth - 1):
            @pl.when(j < n_used)
            def _():
                row_block_copy(j).start()

    @pl.when(b + depth - 1 < n_used)
    def _():
        row_block_copy(b + depth - 1).start()

    changed = jnp.logical_or(b == 0, be_ref[b] != be_ref[jnp.maximum(b - 1, 0)])

    @pl.when(changed)
    def _():
        wgu[:, :ff] = wg_ref[...].astype(BF16)
        wgu[:, ff:] = wu_ref[...].astype(BF16)
        wdn[...] = wd_ref[...].astype(BF16)

    @pl.when(b < n_used)
    def _():
        row_block_copy(b).wait()
        slot = b % depth
        n_valid = nv_ref[b]
        ts = te // n_sub
        gus = []
        for a in range(n_sub):
            r0 = a * ts
            xw = jnp.concatenate([xbuf[slot, j, r0:r0 + ts, :] for j in range(SC_SPLIT)], axis=-1)
            row = lax.broadcasted_iota(jnp.int32, xw.shape, 0) + r0
            lo, hi = _unpack_halves(jnp.where(row < n_valid, xw, 0))
            half = lo.shape[1]
            gus.append(_dot(lo.astype(BF16), wgu[:half, :]) + _dot(hi.astype(BF16), wgu[half:, :]))
        outs = []
        for gu in gus:
            gate = gu[:, :ff]
            hmid = (gate * _sigmoid(gate) * gu[:, ff:]).astype(BF16)
            outs.append(_dot(hmid, wdn[...]))
        for a, y in enumerate(outs):
            words = _pack_halves(y)
            for j in range(SC_SPLIT):
                ys_ref[j, a * ts:(a + 1) * ts, :] = words[:, j * q:(j + 1) * q]


def _expert_call(block_e, n_valid, n_used, xs, layer, w_gate, w_up, w_down):
    _, n_rows, q = xs.shape
    d = 2 * SC_SPLIT * q
    ff = w_gate.shape[3]
    te = EXPERT_ROWS
    row_map = lambda b, be, nv, nu: (0, jnp.minimum(b, nu[0] - 1), 0)
    w_map = lambda b, be, nv, nu: (layer, be[b], 0, 0)
    grid_spec = pltpu.PrefetchScalarGridSpec(
        num_scalar_prefetch=3,
        grid=(n_rows // te,),
        in_specs=[pl.BlockSpec(memory_space=pl.ANY),
                  pl.BlockSpec((None, None, d, ff), w_map),
                  pl.BlockSpec((None, None, d, ff), w_map),
                  pl.BlockSpec((None, None, ff, d), w_map)],
        out_specs=pl.BlockSpec((SC_SPLIT, te, q), row_map),
        scratch_shapes=[pltpu.VMEM((d, 2 * ff), BF16), pltpu.VMEM((ff, d), BF16),
                        pltpu.VMEM((EXPERT_RING, SC_SPLIT, te, q), jnp.int32),
                        pltpu.SemaphoreType.DMA((EXPERT_RING,))])
    return pl.pallas_call(
        functools.partial(_expert_kernel, ff=ff, n_sub=EXPERT_ROWS // EXPERT_SUB_ROWS),
        grid_spec=grid_spec,
        out_shape=jax.ShapeDtypeStruct(xs.shape, jnp.int32),
        compiler_params=_cparams(1, 40),
        name="moe_experts",
    )(block_e, n_valid, n_used, xs, w_gate, w_up, w_down)


SC_WINDOW = 128


def _sc_mesh():
    return plsc.VectorSubcoreMesh(core_axis_name="core", subcore_axis_name="subcore")


def _sc_scatter_rows(rows, dests, n_out):
    n, width = rows.shape
    n_k = len(dests)
    assert n % SC_WINDOW == 0

    @functools.partial(pl.kernel, out_type=jax.ShapeDtypeStruct((n_out, width), rows.dtype),
                       mesh=_sc_mesh(), name="sc_dispatch_rows")
    def scatter_kernel(x_hbm, *refs):
        idx_hbm, o_hbm = refs[:n_k], refs[n_k]

        def body(x_vmem, *idx_vmem):
            for iv in idx_vmem:
                pltpu.sync_copy(x_vmem, o_hbm.at[iv.at[0]])

        pltpu.emit_pipeline(
            body,
            grid=(n // SC_WINDOW,),
            in_specs=[pl.BlockSpec((SC_WINDOW, width), lambda i: (i, 0))]
            + [pl.BlockSpec((1, SC_WINDOW), lambda i: (0, i))] * n_k,
            out_specs=[],
            core_axis_name=("core", "subcore"),
            dimension_semantics=(pltpu.PARALLEL,),
        )(x_hbm, *idx_hbm)

    return scatter_kernel(rows, *dests)


def _sc_gather_rows(table, idx):
    n = idx.shape[1]
    width = table.shape[1]
    assert n % SC_WINDOW == 0

    @functools.partial(pl.kernel, out_type=jax.ShapeDtypeStruct((n, width), table.dtype),
                       mesh=_sc_mesh(), name="sc_collect_rows")
    def gather_kernel(t_hbm, i_hbm, o_hbm):
        def body(i_vmem, o_vmem):
            pltpu.sync_copy(t_hbm.at[i_vmem.at[0]], o_vmem)

        pltpu.emit_pipeline(
            body,
            grid=(n // SC_WINDOW,),
            in_specs=[pl.BlockSpec((1, SC_WINDOW), lambda i: (0, i))],
            out_specs=[pl.BlockSpec((SC_WINDOW, width), lambda i: (i, 0))],
            core_axis_name=("core", "subcore"),
            dimension_semantics=(pltpu.PARALLEL,),
        )(i_hbm, o_hbm)

    return gather_kernel(table, idx)


def _shared_kernel(tok_ref, sgu_ref, sd_ref, o_ref, *, ff):
    gu = _dot(tok_ref[...], sgu_ref[...])
    gate = gu[:, :ff]
    hmid = (gate * _sigmoid(gate) * gu[:, ff:]).astype(BF16)
    o_ref[...] = _dot(hmid, sd_ref[...]).astype(BF16)


def _shared_call(tok, sgu, sd, *, rows, tm):
    d = tok.shape[1]
    ff = sd.shape[0]
    return pl.pallas_call(
        functools.partial(_shared_kernel, ff=ff),
        grid=(rows // tm,),
        in_specs=[pl.BlockSpec((tm, d), lambda i: (i, 0)),
                  pl.BlockSpec((d, 2 * ff), lambda i: (0, 0)),
                  pl.BlockSpec((ff, d), lambda i: (0, 0))],
        out_specs=pl.BlockSpec((tm, d), lambda i: (i, 0)),
        out_shape=jax.ShapeDtypeStruct((rows, d), BF16),
        compiler_params=_cparams(1, 40),
        name="moe_shared_expert",
    )(tok, sgu, sd)


def _moe_out_kernel(shared_ref, yg_ref, gk_ref, x_ref, g_ref, gate_ref, xo_ref):
    shared = shared_ref[...].astype(F32)
    gk = gk_ref[...]
    lo_acc, hi_acc = None, None
    for k in range(TOP_K):
        lo, hi = _unpack_halves(jnp.concatenate([yg_ref[j, k] for j in range(SC_SPLIT)], axis=-1))
        w = gk[:, k:k + 1]
        lo_acc = lo * w if lo_acc is None else lo_acc + lo * w
        hi_acc = hi * w if hi_acc is None else hi_acc + hi * w
    y = jnp.concatenate([lo_acc, hi_acc], axis=-1) + shared
    xo_ref[...] = x_ref[...] + gate_ref[...] * _rms(y, g_ref[...])


def _moe_out_call(shared, yg, gk, xs, g, gate, *, rows, tm, seq, n_batch):
    d = xs.shape[1]
    seg = _seg_map(tm, seq, n_batch)
    row = lambda: pl.BlockSpec((tm, d), lambda i: (i, 0))
    return pl.pallas_call(
        _moe_out_kernel,
        grid=(rows // tm,),
        in_specs=[row(),
                  pl.BlockSpec((SC_SPLIT, TOP_K, tm, d // 2 // SC_SPLIT), lambda i: (0, 0, i, 0)),
                  pl.BlockSpec((tm, SUBLANES), lambda i: (i, 0)),
                  row(),
                  pl.BlockSpec((1, d), lambda i: (0, 0)),
                  pl.BlockSpec((None, 1, d), seg)],
        out_specs=row(),
        out_shape=jax.ShapeDtypeStruct((rows, d), F32),
        compiler_params=_cparams(1, 48),
        name="moe_combine",
    )(shared, yg, gk, xs, g, gate)


def _dest_kernel(e_ref, pos_ref, ps_ref, o_ref, *, n_rows):
    e = e_ref[...].astype(F32)
    ps = ps_ref[...]
    n_exp, tm = ps.shape[0], e.shape[1]
    ids = lax.broadcasted_iota(jnp.int32, (n_exp, tm), 0).astype(F32)
    rows = [jnp.sum(jnp.where(ids == e[k:k + 1, :], ps, 0.0), axis=0, keepdims=True) for k in range(SUBLANES)]
    dest = jnp.concatenate(rows, axis=0).astype(jnp.int32) + pos_ref[...]
    for j in range(SC_SPLIT):
        o_ref[j] = dest + j * n_rows


def _dest_call(e_tk, pos_tk, pad_starts, *, n_rows, tm):
    rows = e_tk.shape[1]
    n_exp = pad_starts.shape[0]
    lane_spec = pl.BlockSpec((SUBLANES, tm), lambda i: (0, i))
    return pl.pallas_call(
        functools.partial(_dest_kernel, n_rows=n_rows),
        grid=(rows // tm,),
        in_specs=[lane_spec, lane_spec, pl.BlockSpec((n_exp, 1), lambda i: (0, 0))],
        out_specs=pl.BlockSpec((SC_SPLIT, SUBLANES, tm), lambda i: (0, 0, i)),
        out_shape=jax.ShapeDtypeStruct((SC_SPLIT, SUBLANES, rows), jnp.int32),
        compiler_params=_cparams(1, 32),
        name="moe_dest_rows",
    )(e_tk, pos_tk, pad_starts.astype(F32).reshape(n_exp, 1))


def _moe_layer(xs, tok, tokp, routing, g_out, gate, layer, w_gate, w_up, w_down, sh_gate, sh_up, sh_down,
               *, rows, tm, seq, n_batch):
    e_tk, g_tk, pos_tk, counts = routing
    n_exp = counts.shape[0]
    te = EXPERT_ROWS
    cnt = counts[:, 0].astype(jnp.int32)
    padded = (cnt + te - 1) // te * te
    pad_ends = jnp.cumsum(padded)
    pad_starts = pad_ends - padded
    expert_ids = jnp.arange(n_exp, dtype=jnp.int32)
    n_blocks = -(-(rows * TOP_K + n_exp * (te - 1)) // te)
    n_used = pad_ends[-1] // te
    blk = jnp.arange(n_blocks, dtype=jnp.int32)
    last = jnp.minimum(blk, n_used - 1) * te
    owner = (last[:, None] >= pad_ends[None, :]).astype(jnp.int32)
    block_e = jnp.minimum(jnp.sum(owner, axis=1), n_exp - 1)
    own_hot = block_e[:, None] == expert_ids[None, :]
    n_valid = jnp.sum(jnp.where(own_hot, (pad_starts + cnt)[None, :], 0), axis=1) - last
    n_valid = jnp.clip(n_valid, 0, te)
    q = tokp.shape[2]
    n_rows = n_blocks * te
    piece = _dest_call(e_tk, pos_tk, pad_starts, n_rows=n_rows, tm=tm)[:, :TOP_K, :]
    xg = _sc_scatter_rows(tokp.reshape(SC_SPLIT * rows, q),
                          [piece[:, k, :].reshape(1, SC_SPLIT * rows) for k in range(TOP_K)],
                          SC_SPLIT * n_rows).reshape(SC_SPLIT, n_rows, q)
    sgu = jnp.concatenate([sh_gate, sh_up], axis=1).astype(BF16)
    shared = _shared_call(tok, sgu, sh_down.astype(BF16), rows=rows, tm=tm)
    ys = _expert_call(block_e, n_valid, n_used.reshape(1).astype(jnp.int32), xg, layer, w_gate, w_up, w_down)
    yg = _sc_gather_rows(ys.reshape(SC_SPLIT * n_rows, q),
                         piece.reshape(1, -1)).reshape(SC_SPLIT, TOP_K, rows, q)
    return _moe_out_call(shared, yg, g_tk, xs, g_out, gate, rows=rows, tm=tm, seq=seq, n_batch=n_batch)


def _rope(t, cos, sin):
    half = DA_QK_DIM // 2
    up = pltpu.roll(t, shift=LANES - half, axis=1)
    dn = pltpu.roll(t, shift=half, axis=1)
    lane = lax.broadcasted_iota(jnp.int32, t.shape, 1) % DA_QK_DIM
    return t * cos + jnp.where(lane < half, -up, dn) * sin


def _qkv_kernel(x_ref, g_ref, sh_ref, sc_ref, w_ref, cos_ref, sin_ref, q_ref, k_ref, v_ref, *, qkw):
    h = _normmod(x_ref[...], g_ref[...], sh_ref[...], sc_ref[...]).astype(BF16)
    qkv = _dot(h, w_ref[...])
    cos, sin = cos_ref[...], sin_ref[...]
    q_scale = DA_QK_DIM ** -0.5 * math.log2(math.e)
    for hb in range(qkw // LANES):
        lo, hi = hb * LANES, (hb + 1) * LANES
        q_ref[:, lo:hi] = (_rope(qkv[:, lo:hi], cos, sin) * q_scale).astype(BF16)
        k_ref[:, lo:hi] = _rope(qkv[:, qkw + lo:qkw + hi], cos, sin).astype(BF16)
    v_ref[...] = qkv[:, 2 * qkw:].astype(BF16)


def _qkv_call(xs, g, shift, scale, w_qkv, cos_t, sin_t, *, qkw, tm, seq, ctx_len, n_batch):
    rows, d = xs.shape
    n = w_qkv.shape[1]
    vw = n - 2 * qkw
    lat_tiles, lat_tps, ctx_tps = n_batch * seq // tm, seq // tm, ctx_len // tm
    seg = _seg_map(tm, seq, n_batch)

    def kv_map(i):
        c = i - lat_tiles
        is_lat = i < lat_tiles
        return (jnp.where(is_lat, i // lat_tps, c // ctx_tps),
                jnp.where(is_lat, ctx_tps + i % lat_tps, c % ctx_tps), 0)

    rope_map = lambda i: (jnp.where(i < lat_tiles, i % lat_tps, lat_tps), 0)
    return pl.pallas_call(
        functools.partial(_qkv_kernel, qkw=qkw),
        grid=(rows // tm,),
        in_specs=[pl.BlockSpec((tm, d), lambda i: (i, 0)),
                  pl.BlockSpec((1, d), lambda i: (0, 0)),
                  pl.BlockSpec((None, 1, d), seg),
                  pl.BlockSpec((None, 1, d), seg),
                  pl.BlockSpec((d, n), lambda i: (0, 0)),
                  pl.BlockSpec((tm, LANES), rope_map),
                  pl.BlockSpec((tm, LANES), rope_map)],
        out_specs=[pl.BlockSpec((tm, qkw), lambda i: (i, 0)),
                   pl.BlockSpec((None, tm, qkw), kv_map),
                   pl.BlockSpec((None, tm, vw), kv_map)],
        out_shape=[jax.ShapeDtypeStruct((rows, qkw), BF16),
                   jax.ShapeDtypeStruct((n_batch, ctx_len + seq, qkw), BF16),
                   jax.ShapeDtypeStruct((n_batch, ctx_len + seq, vw), BF16)],
        compiler_params=_cparams(1, 48),
        name="qkv_rope",
    )(xs, g, shift, scale, w_qkv, cos_t, sin_t)


def _rope_tables(seq, tm):
    pos = np.arange(seq)
    n_freq = DA_QK_DIM // 4
    inv = np.power(ROPE_BASE, -np.arange(n_freq, dtype=np.float32) / n_freq).astype(np.float32)
    row = (pos // GRID_W).astype(np.float32)[:, None] * inv
    col = (pos % GRID_W).astype(np.float32)[:, None] * inv
    ang = np.concatenate([row, col], axis=-1).astype(np.float32)
    ang = np.tile(ang, (1, LANES // ang.shape[1]))
    cos = np.concatenate([np.cos(ang), np.ones((tm, LANES))], axis=0)
    sin = np.concatenate([np.sin(ang), np.zeros((tm, LANES))], axis=0)
    return jnp.asarray(cos, F32), jnp.asarray(sin, F32)


def _attn_kernel(lp_ref, q_ref, k_ref, v_ref, sg_ref, o_ref, vext, *, n_sub, lam_init):
    @pl.when(pl.program_id(2) == 0)
    def _():
        vext[:, :LANES] = v_ref[...]
        vext[:, LANES:] = jnp.ones((vext.shape[0], LANES), BF16)

    lp = lp_ref[...]
    lam = (jnp.exp(jnp.sum(lp[0:1] * lp[1:2], axis=1, keepdims=True))
           - jnp.exp(jnp.sum(lp[2:3] * lp[3:4], axis=1, keepdims=True)) + lam_init)
    ts = q_ref.shape[0] // n_sub
    lane = lax.broadcasted_iota(jnp.int32, (ts, LANES), 1)
    zero = jnp.zeros((ts, LANES), BF16)
    k = k_ref[...]
    scores = []
    for a in range(n_sub):
        q = q_ref[a * ts:(a + 1) * ts, :]
        qq = jnp.concatenate([jnp.where(lane < DA_QK_DIM, q, zero),
                              jnp.where(lane >= DA_QK_DIM, q, zero)], axis=0)
        scores.append(_dot_nt(qq, k))
    for a in range(n_sub):
        s = scores[a]
        p = jnp.exp2(s - jnp.max(s, axis=-1, keepdims=True)).astype(BF16)
        oe = _dot(p, vext[...])
        on = oe[:, :LANES] / oe[:, LANES:LANES + 1]
        o = on[:ts] - lam * on[ts:]
        o_ref[a * ts:(a + 1) * ts, :] = (_rms(o, sg_ref[...]) * (1.0 - lam_init)).astype(BF16)


def _attn_call(lam_p, q, k_all, v_all, subln_g, *, tq, seq, n_batch, lam_init):
    n_heads = q.shape[1] // LANES
    lk = k_all.shape[1]
    qt = seq // tq
    return pl.pallas_call(
        functools.partial(_attn_kernel, n_sub=tq // ATTN_SUB_ROWS, lam_init=lam_init),
        grid=(n_batch, n_heads, qt),
        in_specs=[pl.BlockSpec(lam_p.shape, lambda b, h, i: (0, 0)),
                  pl.BlockSpec((tq, LANES), lambda b, h, i: (b * qt + i, h)),
                  pl.BlockSpec((None, lk, LANES), lambda b, h, i: (b, 0, h)),
                  pl.BlockSpec((None, lk, LANES), lambda b, h, i: (b, 0, h)),
                  pl.BlockSpec((1, LANES), lambda b, h, i: (0, 0))],
        out_specs=pl.BlockSpec((tq, LANES), lambda b, h, i: (b * qt + i, h)),
        out_shape=jax.ShapeDtypeStruct((n_batch * seq, n_heads * LANES), BF16),
        scratch_shapes=[pltpu.VMEM((lk, 2 * LANES), BF16)],
        compiler_params=_cparams(3, 48),
        name="diff_attention",
    )(lam_p, q, k_all, v_all, subln_g)


def kernel(x, c, ctx, c_ctx, ada_w, ada_b, norm_g, cf_w_in, cf_conv_w, cf_conv_b, cf_ln_g, cf_ln_b,
           cf_w_out, da_w_qkv, da_lambda, da_subln_g, da_w_out, moe_router_w, moe_router_b,
           moe_w_gate, moe_w_up, moe_w_down, moe_sh_gate, moe_sh_up, moe_sh_down):
    n_batch, seq, d = x.shape
    ctx_len = ctx.shape[1]
    depth = ada_w.shape[0]
    assert depth == 2 and n_batch + 1 <= N_SEG
    assert DA_QK_DIM * 2 == LANES and da_subln_g.shape[1] == LANES
    rows_lat, rows_ctx = n_batch * seq, n_batch * ctx_len
    tm = ROW_TILE
    tq = SEQ_TILE
    assert seq % tm == 0 and rows_ctx % tm == 0 and seq % tq == 0 and ctx_len % tq == 0

    x_lat, x_ctx = x.reshape(rows_lat, d), ctx.reshape(rows_ctx, d)
    cond = jnp.concatenate([c, c_ctx[None, :], jnp.zeros((N_SEG - n_batch - 1, d), F32)], axis=0)
    mods = _mods_call(cond, ada_w, ada_b)
    mod = lambda layer, k: mods[layer, :, k * d:(k + 1) * d][:, None, :]
    gain = lambda layer, k: norm_g[layer, k][None, :]
    common = dict(seq=seq, n_batch=n_batch)

    cc = cf_conv_w.shape[-1]
    fc = cf_w_in.shape[2] - 2 * cc
    gw = fc // FOURIER_GROUPS
    ch_ang = 2.0 * np.pi * (np.outer(np.arange(gw), np.arange(gw)) % gw) / gw
    cs = jnp.asarray(np.concatenate([np.cos(ch_ang), np.sin(ch_ang)], axis=1) / math.sqrt(gw), BF16)
    a_glu, ab = _inproj_call(x_lat, x_ctx, gain(0, 0), mod(0, 0), mod(0, 1), cf_w_in[0].astype(BF16), cs,
                             cc=cc, gw=gw, tm=tm, **common)
    gsz = cc // CONV_GROUPS
    gid = np.arange(cc) // gsz
    gmean = jnp.asarray((gid[:, None] == gid[None, :]) / gsz, BF16)
    a_act = _conv_call(a_glu, cf_conv_w[0][:, 0, :], cf_conv_b[0][None, :], cf_ln_g[0][None, :],
                       cf_ln_b[0][None, :], gmean, tl=tq, seq=seq, ctx_len=ctx_len, n_batch=n_batch)
    fr_lat = _seqdft_call(ab, seq, row0=0, n_seq=n_batch, fc=fc, tm=min(seq // 2, 256))
    fr_ctx = _seqdft_call(ab, ctx_len, row0=rows_lat, n_seq=n_batch, fc=fc, tm=min(ctx_len // 2, 256))
    w_out = cf_w_out[0].astype(BF16)
    xs, tok, tokp, *routing = _outproj_call(
        [(a_act, a_act), (fr_lat, fr_ctx)], [w_out[:cc], w_out[cc:]], x_lat, x_ctx, gain(0, 1), mod(0, 2), gain(0, 2), mod(0, 3),
        mod(0, 4), moe_router_w[0], moe_router_b[0], rows=rows_lat + rows_ctx, tm=tm, **common)
    xs = _moe_layer(xs, tok, tokp, routing, gain(0, 3), mod(0, 5), 0, moe_w_gate, moe_w_up, moe_w_down,
                    moe_sh_gate[0], moe_sh_up[0], moe_sh_down[0], rows=rows_lat + rows_ctx, tm=tm, **common)

    qkw = DA_HEADS * 2 * DA_QK_DIM
    lam_init = 0.8 - 0.6 * math.exp(-0.3 * 1)
    cos_t, sin_t = _rope_tables(seq, tq)
    q, k_all, v_all = _qkv_call(xs, gain(1, 0), mod(1, 0), mod(1, 1), da_w_qkv[0].astype(BF16),
                                cos_t, sin_t, qkw=qkw, tm=tq, ctx_len=ctx_len, **common)
    o = _attn_call(da_lambda[0], q, k_all, v_all, da_subln_g[0][None, :], tq=math.gcd(seq, ATTN_ROWS), seq=seq,
                   n_batch=n_batch, lam_init=lam_init)
    xs, tok, tokp, *routing = _outproj_call(
        [(o, o)], [da_w_out[0].astype(BF16)], xs, xs, gain(1, 1), mod(1, 2), gain(1, 2), mod(1, 3), mod(1, 4),
        moe_router_w[1], moe_router_b[1], rows=rows_lat, tm=tm, **common)
    xs = _moe_layer(xs, tok, tokp, routing, gain(1, 3), mod(1, 5), 1, moe_w_gate, moe_w_up, moe_w_down,
                    moe_sh_gate[1], moe_sh_up[1], moe_sh_down[1], rows=rows_lat, tm=tm, **common)
    return xs.reshape(n_batch, seq, d)
```

```python
import functools
import math

import numpy as np
import jax
import jax.numpy as jnp
from jax import lax
from jax.experimental import pallas as pl
from jax.experimental.pallas import tpu as pltpu
from jax.experimental.pallas import tpu_sc as plsc

F32 = jnp.float32
BF16 = jnp.bfloat16

EPS = 1e-6
GRID_W = 64
CONV_GROUPS = 8
FOURIER_GROUPS = 4
DA_HEADS = 8
DA_QK_DIM = 64
ROPE_BASE = 10000.0
N_GROUPS = 8
TOPK_GROUPS = 4
TOP_K = 6
ROUTED_SCALE = 2.5

LANES = 128
SUBLANES = 8
ROW_TILE = 512
SEQ_TILE = 256
N_SEG = 16
HALO = 16
CONV_GROUP_ROWS = 128
EXPERT_ROWS = 1024
EXPERT_SUB_ROWS = 128
EXPERT_RING = 3
ATTN_ROWS = 1024
ATTN_SUB_ROWS = 128
MIB = 1024 * 1024


def _cparams(n_axes, vmem_mib):
    return pltpu.CompilerParams(dimension_semantics=("arbitrary",) * n_axes,
                                vmem_limit_bytes=vmem_mib * MIB)


def _sigmoid(v):
    return 1.0 / (1.0 + jnp.exp(-v))


def _rms(v, g):
    return v * lax.rsqrt(jnp.mean(v * v, axis=-1, keepdims=True) + EPS) * g


def _normmod(v, g, shift, scale):
    return _rms(v, g) * (1.0 + scale) + shift


def _split_bf16(v):
    hi = v.astype(BF16)
    lo = (v - hi.astype(F32)).astype(BF16)
    return hi, lo


def _pack_halves(v):
    half = v.shape[1] // 2
    word = pltpu.pack_elementwise([v[:, :half], v[:, half:]], packed_dtype=BF16)
    return lax.bitcast_convert_type(word, jnp.int32)


def _unpack_halves(w):
    u = lax.bitcast_convert_type(w, jnp.uint32)
    lo = lax.bitcast_convert_type(u << 16, F32)
    hi = lax.bitcast_convert_type(u & jnp.uint32(0xFFFF0000), F32)
    return lo, hi


SC_SPLIT = 2


def _store_pieces(ref, words):
    q = words.shape[1] // SC_SPLIT
    for j in range(SC_SPLIT):
        ref[j] = words[:, j * q:(j + 1) * q]


def _load_pieces(ref):
    return jnp.concatenate([ref[j] for j in range(SC_SPLIT)], axis=-1)


def _dot(a, b):
    return jnp.dot(a, b, preferred_element_type=F32)


def _dot_nt(a, b):
    return lax.dot_general(a, b, (((1,), (1,)), ((), ())), preferred_element_type=F32)


def _mods_kernel(c_ref, w_ref, b_ref, o_ref):
    cv = c_ref[...]
    o_ref[...] = _dot(cv * _sigmoid(cv), w_ref[...]) + b_ref[...]


def _mods_call(cond, ada_w, ada_b):
    depth, d, n = ada_w.shape
    tn = n // 4
    return pl.pallas_call(
        _mods_kernel,
        grid=(depth, n // tn),
        in_specs=[pl.BlockSpec((N_SEG, d), lambda l, j: (0, 0)),
                  pl.BlockSpec((None, d, tn), lambda l, j: (l, 0, j)),
                  pl.BlockSpec((None, 1, tn), lambda l, j: (l, 0, j))],
        out_specs=pl.BlockSpec((None, N_SEG, tn), lambda l, j: (l, 0, j)),
        out_shape=jax.ShapeDtypeStruct((depth, N_SEG, n), F32),
        compiler_params=_cparams(2, 40),
        name="adaln_mods",
    )(cond, ada_w, ada_b.reshape(depth, 1, n))


def _two_source_specs(tm, d, n_head_tiles):
    return (pl.BlockSpec((tm, d), lambda i: (jnp.minimum(i, n_head_tiles - 1), 0)),
            pl.BlockSpec((tm, d), lambda i: (jnp.maximum(i - n_head_tiles, 0), 0)))


def _pick_rows(head_ref, tail_ref, n_head_tiles):
    return jnp.where(pl.program_id(0) < n_head_tiles, head_ref[...], tail_ref[...])


def _inproj_kernel(xh_ref, xt_ref, g_ref, sh_ref, sc_ref, w_ref, cs_ref, a_ref, ab_ref, *, cc, gw, n_head):
    x = _pick_rows(xh_ref, xt_ref, n_head)
    h = _normmod(x, g_ref[...], sh_ref[...], sc_ref[...]).astype(BF16)
    u = _dot(h, w_ref[...])
    a_ref[...] = u[:, :cc] * _sigmoid(u[:, cc:2 * cc])
    f = u[:, 2 * cc:].astype(BF16)
    n_g = f.shape[1] // gw
    parts = [_dot(f[:, g * gw:(g + 1) * gw], cs_ref[...]) for g in range(n_g)]
    cos_part = [p[:, :gw] for p in parts]
    sin_part = [p[:, gw:] for p in parts]
    ab_ref[...] = jnp.concatenate(cos_part + sin_part, axis=-1).astype(BF16)


def _seg_map(tm, seq, n_batch):
    return lambda i: (jnp.minimum(i * tm // seq, n_batch), 0, 0)


def _inproj_call(x_head, x_tail, g, shift, scale, w_in, cs, *, cc, gw, tm, seq, n_batch):
    d = x_head.shape[1]
    rows = x_head.shape[0] + x_tail.shape[0]
    n_head = x_head.shape[0] // tm
    n = w_in.shape[1]
    fc = n - 2 * cc
    seg = _seg_map(tm, seq, n_batch)
    return pl.pallas_call(
        functools.partial(_inproj_kernel, cc=cc, gw=gw, n_head=n_head),
        grid=(rows // tm,),
        in_specs=[*_two_source_specs(tm, d, n_head),
                  pl.BlockSpec((1, d), lambda i: (0, 0)),
                  pl.BlockSpec((None, 1, d), seg),
                  pl.BlockSpec((None, 1, d), seg),
                  pl.BlockSpec((d, n), lambda i: (0, 0)),
                  pl.BlockSpec((gw, 2 * gw), lambda i: (0, 0))],
        out_specs=[pl.BlockSpec((tm, cc), lambda i: (i, 0)),
                   pl.BlockSpec((tm, 2 * fc), lambda i: (i, 0))],
        out_shape=[jax.ShapeDtypeStruct((rows, cc), F32),
                   jax.ShapeDtypeStruct((rows, 2 * fc), BF16)],
        compiler_params=_cparams(1, 40),
        name="inproj_glu_chdft",
    )(x_head, x_tail, g, shift, scale, w_in, cs)


def _conv_kernel(prev_ref, main_ref, next_ref, w_ref, cb_ref, lg_ref, lb_ref, gm_ref, o_ref,
                 buf, cv, shifted, *, tl, width, lat_tiles, lat_tps, ctx_tps, chunk):
    i = pl.program_id(0)
    is_lat = i < lat_tiles
    tps = jnp.where(is_lat, lat_tps, ctx_tps)
    j = jnp.where(is_lat, i, i - lat_tiles) % tps
    zero = jnp.zeros((HALO, buf.shape[1]), F32)
    buf[0:HALO, :] = jnp.where(j > 0, prev_ref[...], zero)
    buf[HALO:HALO + tl, :] = main_ref[...]
    buf[HALO + tl:HALO + tl + HALO, :] = jnp.where(j < tps - 1, next_ref[...], zero)
    base = HALO - width // 2
    span = shifted.shape[1]
    for s in range(1, SUBLANES):
        shifted[s] = buf[s:s + span, :]

    def window(off, r0):
        s, m = off % SUBLANES, off - off % SUBLANES
        rows = buf[m + r0:m + r0 + chunk, :] if s == 0 else shifted[s, m + r0:m + r0 + chunk, :]
        return rows.reshape(chunk // SUBLANES, SUBLANES, rows.shape[1])

    gm = gm_ref[...]
    for g0 in range(0, tl, CONV_GROUP_ROWS):
        for r0 in range(g0, g0 + CONV_GROUP_ROWS, chunk):
            acc = window(base, r0) * w_ref[0]
            for t in range(1, width):
                acc = acc + window(base + t, r0) * w_ref[t]
            cv[r0:r0 + chunk, :] = acc.reshape(chunk, acc.shape[2])
        a = cv[g0:g0 + CONV_GROUP_ROWS, :] + cb_ref[...]
        a_hi, a_lo = _split_bf16(a)
        mu = _dot(a_hi, gm) + _dot(a_lo, gm)
        dl = a - mu
        q_hi, q_lo = _split_bf16(dl * dl)
        var = _dot(q_hi, gm) + _dot(q_lo, gm)
        y = dl * lax.rsqrt(var + EPS) * lg_ref[...] + lb_ref[...]
        o_ref[g0:g0 + CONV_GROUP_ROWS, :] = (y * _sigmoid(y)).astype(BF16)


def _conv_call(a_glu, conv_w, conv_b, ln_g, ln_b, gmean, *, tl, seq, ctx_len, n_batch):
    rows, ch = a_glu.shape
    width = conv_w.shape[0]
    assert width // 2 <= HALO and tl % HALO == 0 and tl % CONV_GROUP_ROWS == 0
    hb = tl // HALO
    last_halo = rows // HALO - 1
    return pl.pallas_call(
        functools.partial(_conv_kernel, tl=tl, width=width, lat_tiles=n_batch * seq // tl,
                          lat_tps=seq // tl, ctx_tps=ctx_len // tl, chunk=32),
        grid=(rows // tl,),
        in_specs=[pl.BlockSpec((HALO, ch), lambda i: (jnp.maximum(i * hb - 1, 0), 0)),
                  pl.BlockSpec((tl, ch), lambda i: (i, 0)),
                  pl.BlockSpec((HALO, ch), lambda i: (jnp.minimum((i + 1) * hb, last_halo), 0)),
                  pl.BlockSpec((width, SUBLANES, ch), lambda i: (0, 0, 0)),
                  pl.BlockSpec((1, ch), lambda i: (0, 0)),
                  pl.BlockSpec((1, ch), lambda i: (0, 0)),
                  pl.BlockSpec((1, ch), lambda i: (0, 0)),
                  pl.BlockSpec((ch, ch), lambda i: (0, 0))],
        out_specs=pl.BlockSpec((tl, ch), lambda i: (i, 0)),
        out_shape=jax.ShapeDtypeStruct((rows, ch), BF16),
        scratch_shapes=[pltpu.VMEM((tl + 2 * HALO, ch), F32), pltpu.VMEM((tl, ch), F32),
                        pltpu.VMEM((SUBLANES, tl + 2 * HALO - SUBLANES, ch), F32)],
        compiler_params=_cparams(1, 40),
        name="dwconv_groupln_swish",
    )(a_glu, a_glu, a_glu, jnp.broadcast_to(conv_w[:, None, :], (width, SUBLANES, ch)), conv_b, ln_g, ln_b, gmean)


EDGE_ROWS = 16


def _seqdft_kernel(c_ref, s_ref, cx_ref, sx_ref, jm_ref, a_ref, b_ref, o_ref, *, tm, n_tiles):
    i = pl.program_id(1)
    a, b = a_ref[...], b_ref[...]
    p, qv = _dot(c_ref[...], a), _dot(s_ref[...], b)
    o_ref[pl.ds(pl.multiple_of(i * tm, tm), tm), :] = (p - qv).astype(BF16)
    edge = _dot(cx_ref[...], a) + _dot(sx_ref[...], b)
    pick = lax.broadcasted_iota(jnp.int32, edge.shape, 0) == i
    edge_row = jnp.sum(jnp.where(pick, edge, 0.0), axis=0, keepdims=True)
    mirrored = _dot(jm_ref[...], (p + qv).astype(BF16))
    srow = lax.broadcasted_iota(jnp.int32, mirrored.shape, 0)
    mirrored = jnp.where(srow == 0, edge_row, mirrored)
    o_ref[pl.ds(pl.multiple_of((n_tiles - 1 - i) * tm, tm), tm), :] = mirrored.astype(BF16)


def _seqdft_call(ab, length, *, row0, n_seq, fc, tm):
    n_tiles = length // tm
    n_half = n_tiles // 2
    assert n_tiles % 2 == 0 and n_half <= EDGE_ROWS
    scale = 1.0 / math.sqrt(length)
    cmat, smat = _dft_tables(length, scale, np.arange(length // 2))
    edge_k = np.zeros((EDGE_ROWS,), np.int64)
    edge_k[:n_half] = tm * (np.arange(n_half) + 1)
    cx, sx = _dft_tables(length, scale, edge_k)
    s_idx = np.arange(tm)
    reversal = jnp.asarray((s_idx[None, :] == (tm - s_idx)[:, None]) & (s_idx[:, None] > 0), BF16)
    assert row0 % length == 0
    seq0 = row0 // length
    whole = lambda arr: pl.BlockSpec(arr.shape, lambda b, i: (0, 0))
    return pl.pallas_call(
        functools.partial(_seqdft_kernel, tm=tm, n_tiles=n_tiles),
        grid=(n_seq, n_half),
        in_specs=[pl.BlockSpec((tm, length), lambda b, i: (i, 0)),
                  pl.BlockSpec((tm, length), lambda b, i: (i, 0)),
                  whole(cx), whole(sx), whole(reversal),
                  pl.BlockSpec((length, fc), lambda b, i: (seq0 + b, 0)),
                  pl.BlockSpec((length, fc), lambda b, i: (seq0 + b, 1))],
        out_specs=pl.BlockSpec((length, fc), lambda b, i: (b, 0)),
        out_shape=jax.ShapeDtypeStruct((n_seq * length, fc), BF16),
        compiler_params=_cparams(2, 48),
        name="seq_dft",
    )(cmat, smat, cx, sx, reversal, ab, ab)


def _dft_tables(length, scale, rows):
    k = np.asarray(rows, np.int64)[:, None]
    if length <= 512:
        ang = 2.0 * np.pi * ((k * np.arange(length)[None, :]) % length) / length
        return (jnp.asarray(np.cos(ang) * scale, BF16), jnp.asarray(np.sin(ang) * scale, BF16))
    r = 64
    assert length % r == 0
    alpha = 2.0 * np.pi * ((k * np.arange(length // r)[None, :] * r) % length) / length
    beta = 2.0 * np.pi * ((k * np.arange(r)[None, :]) % length) / length
    ca, sa = jnp.asarray(np.cos(alpha), F32)[:, :, None], jnp.asarray(np.sin(alpha), F32)[:, :, None]
    cb, sb = jnp.asarray(np.cos(beta) * scale, F32)[:, None, :], jnp.asarray(np.sin(beta) * scale, F32)[:, None, :]
    cmat = (ca * cb - sa * sb).reshape(k.shape[0], length).astype(BF16)
    smat = (sa * cb + ca * sb).reshape(k.shape[0], length).astype(BF16)
    return cmat, smat


def _outproj_kernel(*refs, in_heads, n_head):
    n_in = len(in_heads)
    ins, ws = refs[:2 * n_in], refs[2 * n_in:3 * n_in]
    (xh_ref, xt_ref, g1_ref, gate_ref, g2_ref, sh_ref, sc_ref, wh_ref, wl_ref, rb_ref, upper_ref,
     xo_ref, tok_ref, tokp_ref, e_ref, gt_ref, pos_ref, cnt_ref, carry) = refs[3 * n_in:]
    y = None
    for j, w_ref in enumerate(ws):
        part = _dot(_pick_rows(ins[2 * j], ins[2 * j + 1], in_heads[j]), w_ref[...])
        y = part if y is None else y + part
    x1 = _pick_rows(xh_ref, xt_ref, n_head) + gate_ref[...] * _rms(y, g1_ref[...])
    xo_ref[...] = x1
    tok = _normmod(x1, g2_ref[...], sh_ref[...], sc_ref[...])
    tok_ref[...] = tok.astype(BF16)
    _store_pieces(tokp_ref, _pack_halves(tok))
    _route(tok, wh_ref, wl_ref, rb_ref, upper_ref, e_ref, gt_ref, pos_ref, cnt_ref, carry)


def _outproj_call(ins, ws, x_head, x_tail, g1, gate, g2, shift, scale, router_w, router_b,
                  *, rows, tm, seq, n_batch):
    d = x_head.shape[1]
    n_head = min(x_head.shape[0], rows) // tm
    n_exp = router_w.shape[1]
    w_t = router_w.T
    w_hi = w_t.astype(BF16)
    w_lo = (w_t - w_hi.astype(F32)).astype(BF16)
    upper = jnp.asarray(np.triu(np.ones((tm, tm), np.float32), k=1), BF16)
    seg = _seg_map(tm, seq, n_batch)
    in_heads =tuple(min(head.shape[0], rows) // tm for head, _ in ins)
    pair_specs = [s for (head, _), nh in zip(ins, in_heads) for s in _two_source_specs(tm, head.shape[1], nh)]
    full_spec = lambda a: pl.BlockSpec(a.shape, lambda i: (0, 0))
    lane_spec = pl.BlockSpec((SUBLANES, tm), lambda i: (0, i))
    tok_rows = lambda dt: jax.ShapeDtypeStruct((SUBLANES, rows), dt)
    return pl.pallas_call(
        functools.partial(_outproj_kernel, in_heads=in_heads, n_head=n_head),
        grid=(rows // tm,),
        in_specs=pair_specs + [full_spec(w) for w in ws] + [
            *_two_source_specs(tm, d, n_head), full_spec(g1), pl.BlockSpec((None, 1, d), seg), full_spec(g2),
            pl.BlockSpec((None, 1, d), seg), pl.BlockSpec((None, 1, d), seg),
            full_spec(w_hi), full_spec(w_lo), pl.BlockSpec((n_exp, 1), lambda i: (0, 0)), full_spec(upper)],
        out_specs=[pl.BlockSpec((tm, d), lambda i: (i, 0)), pl.BlockSpec((tm, d), lambda i: (i, 0)),
                   pl.BlockSpec((SC_SPLIT, tm, d // 2 // SC_SPLIT), lambda i: (0, i, 0)),
                   lane_spec, pl.BlockSpec((tm, SUBLANES), lambda i: (i, 0)), lane_spec,
                   pl.BlockSpec((n_exp, LANES), lambda i: (0, 0))],
        out_shape=[jax.ShapeDtypeStruct((rows, d), F32), jax.ShapeDtypeStruct((rows, d), BF16),
                   jax.ShapeDtypeStruct((SC_SPLIT, rows, d // 2 // SC_SPLIT), jnp.int32),
                   tok_rows(jnp.int32), jax.ShapeDtypeStruct((rows, SUBLANES), F32), tok_rows(jnp.int32),
                   jax.ShapeDtypeStruct((n_exp, LANES), F32)],
        scratch_shapes=[pltpu.VMEM((n_exp, 1), F32)],
        compiler_params=_cparams(1, 48),
        name="outproj_residual_route",
    )(*[a for pair in ins for a in pair], *ws, x_head, x_tail, g1, gate, g2, shift, scale, w_hi, w_lo,
      router_b.reshape(n_exp, 1), upper)


def _pick_max(cur, idx):
    mx = jnp.max(cur, axis=0, keepdims=True)
    first = jnp.min(jnp.where(cur == mx, idx, float(cur.shape[0])), axis=0, keepdims=True)
    return first, idx == first


def _route(tok, wh_ref, wl_ref, rb_ref, upper_ref, e_ref, gt_ref, pos_ref, cnt_ref, carry):
    @pl.when(pl.program_id(0) == 0)
    def _():
        carry[...] = jnp.zeros_like(carry)

    t_hi, t_lo = _split_bf16(tok)
    wh, wl = wh_ref[...], wl_ref[...]
    logits = _dot_nt(wh, t_hi) + _dot_nt(wh, t_lo) + _dot_nt(wl, t_hi)
    n_exp, tm = logits.shape
    scores = _sigmoid(logits)
    biased = scores + rb_ref[...]
    gsz = n_exp // N_GROUPS
    neg = -jnp.inf

    b3 = biased.reshape(N_GROUPS, gsz, tm)
    im = lax.broadcasted_iota(jnp.int32, b3.shape, 1).astype(F32)
    m1 = jnp.max(b3, axis=1, keepdims=True)
    i1 = jnp.min(jnp.where(b3 == m1, im, float(gsz)), axis=1, keepdims=True)
    m2 = jnp.max(jnp.where(im == i1, neg, b3), axis=1, keepdims=True)
    gscore = (m1 + m2).reshape(N_GROUPS, tm)

    ig = lax.broadcasted_iota(jnp.int32, gscore.shape, 0).astype(F32)
    gsel = jnp.zeros_like(gscore)
    cur = gscore
    for _ in range(TOPK_GROUPS):
        _, hit = _pick_max(cur, ig)
        gsel = jnp.where(hit, 1.0, gsel)
        cur = jnp.where(hit, neg, cur)
    gsel3 = jnp.broadcast_to(gsel.reshape(N_GROUPS, 1, tm), b3.shape)
    cur = jnp.where(gsel3 > 0.0, b3, neg).reshape(n_exp, tm)

    ie = lax.broadcasted_iota(jnp.int32, (n_exp, tm), 0).astype(F32)
    sel = jnp.zeros((n_exp, tm), F32)
    picks, raw = [], []
    for _ in range(TOP_K):
        first, hit = _pick_max(cur, ie)
        picks.append(first)
        raw.append(jnp.sum(jnp.where(hit, scores, 0.0), axis=0, keepdims=True))
        sel = jnp.where(hit, 1.0, sel)
        cur = jnp.where(hit, neg, cur)
    total = raw[0]
    for r in raw[1:]:
        total = total + r

    rank = _dot(sel.astype(BF16), upper_ref[...]) + carry[...]
    ranks = [jnp.sum(jnp.where(ie == p, rank, 0.0), axis=0, keepdims=True) for p in picks]
    carry[...] = carry[...] + jnp.sum(sel, axis=1, keepdims=True)

    pad = jnp.zeros((SUBLANES - TOP_K, tm), F32)
    e_ref[...] = jnp.concatenate(picks + [pad], axis=0).astype(jnp.int32)
    gates = jnp.concatenate([r / total * ROUTED_SCALE for r in raw]
                            + [jnp.zeros((LANES - TOP_K, tm), F32)], axis=0)
    gt_ref[...] = gates.T[:, :SUBLANES]
    pos_ref[...] = jnp.concatenate(ranks + [pad], axis=0).astype(jnp.int32)
    cnt_ref[...] = jnp.broadcast_to(carry[...], cnt_ref.shape)


def _expert_kernel(be_ref, nv_ref, nu_ref, xs_hbm, wg_ref, wu_ref, wd_ref, ys_ref, wgu, wdn, xbuf, sem,
                   *, ff, n_sub):
    b = pl.program_id(0)
    n_used = nu_ref[0]
    depth, _, te, q = xbuf.shape

    def row_block_copy(blk):
        slot = blk % depth
        return pltpu.make_async_copy(xs_hbm.at[:, pl.ds(blk * te, te), :], xbuf.at[slot], sem.at[slot])

    @pl.when(b == 0)
    def _():
        for j in range(depth - 1):
            @pl.when(j < n_used)
            def _():
                row_block_copy(j).start()

    @pl.when(b + depth - 1 < n_used)
    def _():
        row_block_copy(b + depth - 1).start()

    changed = jnp.logical_or(b == 0, be_ref[b] != be_ref[jnp.maximum(b - 1, 0)])

    @pl.when(changed)
    def _():
        wgu[:, :ff] = wg_ref[...].astype(BF16)
        wgu[:, ff:] = wu_ref[...].astype(BF16)
        wdn[...] = wd_ref[...].astype(BF16)

    @pl.when(b < n_used)
    def _():
        row_block_copy(b).wait()
        slot = b % depth
        n_valid = nv_ref[b]
        ts = te // n_sub
        gus = []
        for a in range(n_sub):
            r0 = a * ts
            xw = jnp.concatenate([xbuf[slot, j, r0:r0 + ts, :] for j in range(SC_SPLIT)], axis=-1)
            row = lax.broadcasted_iota(jnp.int32, xw.shape, 0) + r0
            lo, hi = _unpack_halves(jnp.where(row < n_valid, xw, 0))
            half = lo.shape[1]
            gus.append(_dot(lo.astype(BF16), wgu[:half, :]) + _dot(hi.astype(BF16), wgu[half:, :]))
        outs = []
        for gu in gus:
            gate = gu[:, :ff]
            hmid = (gate * _sigmoid(gate) * gu[:, ff:]).astype(BF16)
            outs.append(_dot(hmid, wdn[...]))
        for a, y in enumerate(outs):
            words = _pack_halves(y)
            for j in range(SC_SPLIT):
                ys_ref[j, a * ts:(a + 1) * ts, :] = words[:, j * q:(j + 1) * q]


def _expert_call(block_e, n_valid, n_used, xs, layer, w_gate, w_up, w_down):
    _, n_rows, q = xs.shape
    d = 2 * SC_SPLIT * q
    ff = w_gate.shape[3]
    te = EXPERT_ROWS
    row_map = lambda b, be, nv, nu: (0, jnp.minimum(b, nu[0] - 1), 0)
    w_map = lambda b, be, nv, nu: (layer, be[b], 0, 0)
    grid_spec = pltpu.PrefetchScalarGridSpec(
        num_scalar_prefetch=3,
        grid=(n_rows // te,),
        in_specs=[pl.BlockSpec(memory_space=pl.ANY),
                  pl.BlockSpec((None, None, d, ff), w_map),
                  pl.BlockSpec((None, None, d, ff), w_map),
                  pl.BlockSpec((None, None, ff, d), w_map)],
        out_specs=pl.BlockSpec((SC_SPLIT, te, q), row_map),
        scratch_shapes=[pltpu.VMEM((d, 2 * ff), BF16), pltpu.VMEM((ff, d), BF16),
                        pltpu.VMEM((EXPERT_RING, SC_SPLIT, te, q), jnp.int32),
                        pltpu.SemaphoreType.DMA((EXPERT_RING,))])
    return pl.pallas_call(
        functools.partial(_expert_kernel, ff=ff, n_sub=EXPERT_ROWS // EXPERT_SUB_ROWS),
        grid_spec=grid_spec,
        out_shape=jax.ShapeDtypeStruct(xs.shape, jnp.int32),
        compiler_params=_cparams(1, 40),
        name="moe_experts",
    )(block_e, n_valid, n_used, xs, w_gate, w_up, w_down)


SC_WINDOW = 128


def _sc_mesh():
    return plsc.VectorSubcoreMesh(core_axis_name="core", subcore_axis_name="subcore")


def _sc_scatter_rows(rows, dests, n_out):
    n, width = rows.shape
    n_k = len(dests)
    assert n % SC_WINDOW == 0

    @functools.partial(pl.kernel, out_type=jax.ShapeDtypeStruct((n_out, width), rows.dtype),
                       mesh=_sc_mesh(), name="sc_dispatch_rows")
    def scatter_kernel(x_hbm, *refs):
        idx_hbm, o_hbm = refs[:n_k], refs[n_k]

        def body(x_vmem, *idx_vmem):
            for iv in idx_vmem:
                pltpu.sync_copy(x_vmem, o_hbm.at[iv.at[0]])

        pltpu.emit_pipeline(
            body,
            grid=(n // SC_WINDOW,),
            in_specs=[pl.BlockSpec((SC_WINDOW, width), lambda i: (i, 0))]
            + [pl.BlockSpec((1, SC_WINDOW), lambda i: (0, i))] * n_k,
            out_specs=[],
            core_axis_name=("core", "subcore"),
            dimension_semantics=(pltpu.PARALLEL,),
        )(x_hbm, *idx_hbm)

    return scatter_kernel(rows, *dests)


def _sc_gather_rows(table, idx):
    n = idx.shape[1]
    width = table.shape[1]
    assert n % SC_WINDOW == 0

    @functools.partial(pl.kernel, out_type=jax.ShapeDtypeStruct((n, width), table.dtype),
                       mesh=_sc_mesh(), name="sc_collect_rows")
    def gather_kernel(t_hbm, i_hbm, o_hbm):
        def body(i_vmem, o_vmem):
            pltpu.sync_copy(t_hbm.at[i_vmem.at[0]], o_vmem)

        pltpu.emit_pipeline(
            body,
            grid=(n // SC_WINDOW,),
            in_specs=[pl.BlockSpec((1, SC_WINDOW), lambda i: (0, i))],
            out_specs=[pl.BlockSpec((SC_WINDOW, width), lambda i: (i, 0))],
            core_axis_name=("core", "subcore"),
            dimension_semantics=(pltpu.PARALLEL,),
        )(i_hbm, o_hbm)

    return gather_kernel(table, idx)


def _shared_kernel(tok_ref, sgu_ref, sd_ref, o_ref, *, ff):
    gu = _dot(tok_ref[...], sgu_ref[...])
    gate = gu[:, :ff]
    hmid = (gate * _sigmoid(gate) * gu[:, ff:]).astype(BF16)
    o_ref[...] = _dot(hmid, sd_ref[...]).astype(BF16)


def _shared_call(tok, sgu, sd, *, rows, tm):
    d = tok.shape[1]
    ff = sd.shape[0]
    return pl.pallas_call(
        functools.partial(_shared_kernel, ff=ff),
        grid=(rows // tm,),
        in_specs=[pl.BlockSpec((tm, d), lambda i: (i, 0)),
                  pl.BlockSpec((d, 2 * ff), lambda i: (0, 0)),
                  pl.BlockSpec((ff, d), lambda i: (0, 0))],
        out_specs=pl.BlockSpec((tm, d), lambda i: (i, 0)),
        out_shape=jax.ShapeDtypeStruct((rows, d), BF16),
        compiler_params=_cparams(1, 40),
        name="moe_shared_expert",
    )(tok, sgu, sd)


def _moe_out_kernel(shared_ref, yg_ref, gk_ref, x_ref, g_ref, gate_ref, xo_ref):
    shared = shared_ref[...].astype(F32)
    gk = gk_ref[...]
    lo_acc, hi_acc = None, None
    for k in range(TOP_K):
        lo, hi = _unpack_halves(jnp.concatenate([yg_ref[j, k] for j in range(SC_SPLIT)], axis=-1))
        w = gk[:, k:k + 1]
        lo_acc = lo * w if lo_acc is None else lo_acc + lo * w
        hi_acc = hi * w if hi_acc is None else hi_acc + hi * w
    y = jnp.concatenate([lo_acc, hi_acc], axis=-1) + shared
    xo_ref[...] = x_ref[...] + gate_ref[...] * _rms(y, g_ref[...])


def _moe_out_call(shared, yg, gk, xs, g, gate, *, rows, tm, seq, n_batch):
    d = xs.shape[1]
    seg = _seg_map(tm, seq, n_batch)
    row = lambda: pl.BlockSpec((tm, d), lambda i: (i, 0))
    return pl.pallas_call(
        _moe_out_kernel,
        grid=(rows // tm,),
        in_specs=[row(),
                  pl.BlockSpec((SC_SPLIT, TOP_K, tm, d // 2 // SC_SPLIT), lambda i: (0, 0, i, 0)),
                  pl.BlockSpec((tm, SUBLANES), lambda i: (i, 0)),
                  row(),
                  pl.BlockSpec((1, d), lambda i: (0, 0)),
                  pl.BlockSpec((None, 1, d), seg)],
        out_specs=row(),
        out_shape=jax.ShapeDtypeStruct((rows, d), F32),
        compiler_params=_cparams(1, 48),
        name="moe_combine",
    )(shared, yg, gk, xs, g, gate)


def _dest_kernel(e_ref, pos_ref, ps_ref, o_ref, *, n_rows):
    e = e_ref[...].astype(F32)
    ps = ps_ref[...]
    n_exp, tm = ps.shape[0], e.shape[1]
    ids = lax.broadcasted_iota(jnp.int32, (n_exp, tm), 0).astype(F32)
    rows = [jnp.sum(jnp.where(ids == e[k:k + 1, :], ps, 0.0), axis=0, keepdims=True) for k in range(SUBLANES)]
    dest = jnp.concatenate(rows, axis=0).astype(jnp.int32) + pos_ref[...]
    for j in range(SC_SPLIT):
        o_ref[j] = dest + j * n_rows


def _dest_call(e_tk, pos_tk, pad_starts, *, n_rows, tm):
    rows = e_tk.shape[1]
    n_exp = pad_starts.shape[0]
    lane_spec = pl.BlockSpec((SUBLANES, tm), lambda i: (0, i))
    return pl.pallas_call(
        functools.partial(_dest_kernel, n_rows=n_rows),
        grid=(rows // tm,),
        in_specs=[lane_spec, lane_spec, pl.BlockSpec((n_exp, 1), lambda i: (0, 0))],
        out_specs=pl.BlockSpec((SC_SPLIT, SUBLANES, tm), lambda i: (0, 0, i)),
        out_shape=jax.ShapeDtypeStruct((SC_SPLIT, SUBLANES, rows), jnp.int32),
        compiler_params=_cparams(1, 32),
        name="moe_dest_rows",
    )(e_tk, pos_tk, pad_starts.astype(F32).reshape(n_exp, 1))


def _moe_layer(xs, tok, tokp, routing, g_out, gate, layer, w_gate, w_up, w_down, sh_gate, sh_up, sh_down,
               *, rows, tm, seq, n_batch):
    e_tk, g_tk, pos_tk, counts = routing
    n_exp = counts.shape[0]
    te = EXPERT_ROWS
    cnt = counts[:, 0].astype(jnp.int32)
    padded = (cnt + te - 1) // te * te
    pad_ends = jnp.cumsum(padded)
    pad_starts = pad_ends - padded
    expert_ids = jnp.arange(n_exp, dtype=jnp.int32)
    n_blocks = -(-(rows * TOP_K + n_exp * (te - 1)) // te)
    n_used = pad_ends[-1] // te
    blk = jnp.arange(n_blocks, dtype=jnp.int32)
    last = jnp.minimum(blk, n_used - 1) * te
    owner = (last[:, None] >= pad_ends[None, :]).astype(jnp.int32)
    block_e = jnp.minimum(jnp.sum(owner, axis=1), n_exp - 1)
    own_hot = block_e[:, None] == expert_ids[None, :]
    n_valid = jnp.sum(jnp.where(own_hot, (pad_starts + cnt)[None, :], 0), axis=1) - last
    n_valid = jnp.clip(n_valid, 0, te)
    q = tokp.shape[2]
    n_rows = n_blocks * te
    piece = _dest_call(e_tk, pos_tk, pad_starts, n_rows=n_rows, tm=tm)[:, :TOP_K, :]
    xg = _sc_scatter_rows(tokp.reshape(SC_SPLIT * rows, q),
                          [piece[:, k, :].reshape(1, SC_SPLIT * rows) for k in range(TOP_K)],
                          SC_SPLIT * n_rows).reshape(SC_SPLIT, n_rows, q)
    sgu = jnp.concatenate([sh_gate, sh_up], axis=1).astype(BF16)
    shared = _shared_call(tok, sgu, sh_down.astype(BF16), rows=rows, tm=tm)
    ys = _expert_call(block_e, n_valid, n_used.reshape(1).astype(jnp.int32), xg, layer, w_gate, w_up, w_down)
    yg = _sc_gather_rows(ys.reshape(SC_SPLIT * n_rows, q),
                         piece.reshape(1, -1)).reshape(SC_SPLIT, TOP_K, rows, q)
    return _moe_out_call(shared, yg, g_tk, xs, g_out, gate, rows=rows, tm=tm, seq=seq, n_batch=n_batch)


def _rope(t, cos, sin):
    half = DA_QK_DIM // 2
    up = pltpu.roll(t, shift=LANES - half, axis=1)
    dn = pltpu.roll(t, shift=half, axis=1)
    lane = lax.broadcasted_iota(jnp.int32, t.shape, 1) % DA_QK_DIM
    return t * cos + jnp.where(lane < half, -up, dn) * sin


def _qkv_kernel(x_ref, g_ref, sh_ref, sc_ref, w_ref, cos_ref, sin_ref, q_ref, k_ref, v_ref, *, qkw):
    h = _normmod(x_ref[...], g_ref[...], sh_ref[...], sc_ref[...]).astype(BF16)
    qkv = _dot(h, w_ref[...])
    cos, sin = cos_ref[...], sin_ref[...]
    q_scale = DA_QK_DIM ** -0.5 * math.log2(math.e)
    for hb in range(qkw // LANES):
        lo, hi = hb * LANES, (hb + 1) * LANES
        q_ref[:, lo:hi] = (_rope(qkv[:, lo:hi], cos, sin) * q_scale).astype(BF16)
        k_ref[:, lo:hi] = _rope(qkv[:, qkw + lo:qkw + hi], cos, sin).astype(BF16)
    v_ref[...] = qkv[:, 2 * qkw:].astype(BF16)


def _qkv_call(xs, g, shift, scale, w_qkv, cos_t, sin_t, *, qkw, tm, seq, ctx_len, n_batch):
    rows, d = xs.shape
    n = w_qkv.shape[1]
    vw = n - 2 * qkw
    lat_tiles, lat_tps, ctx_tps = n_batch * seq // tm, seq // tm, ctx_len // tm
    seg = _seg_map(tm, seq, n_batch)

    def kv_map(i):
        c = i - lat_tiles
        is_lat = i < lat_tiles
        return (jnp.where(is_lat, i // lat_tps, c // ctx_tps),
                jnp.where(is_lat, ctx_tps + i % lat_tps, c % ctx_tps), 0)

    rope_map = lambda i: (jnp.where(i < lat_tiles, i % lat_tps, lat_tps), 0)
    return pl.pallas_call(
        functools.partial(_qkv_kernel, qkw=qkw),
        grid=(rows // tm,),
        in_specs=[pl.BlockSpec((tm, d), lambda i: (i, 0)),
                  pl.BlockSpec((1, d), lambda i: (0, 0)),
                  pl.BlockSpec((None, 1, d), seg),
                  pl.BlockSpec((None, 1, d), seg),
                  pl.BlockSpec((d, n), lambda i: (0, 0)),
                  pl.BlockSpec((tm, LANES), rope_map),
                  pl.BlockSpec((tm, LANES), rope_map)],
        out_specs=[pl.BlockSpec((tm, qkw), lambda i: (i, 0)),
                   pl.BlockSpec((None, tm, qkw), kv_map),
                   pl.BlockSpec((None, tm, vw), kv_map)],
        out_shape=[jax.ShapeDtypeStruct((rows, qkw), BF16),
                   jax.ShapeDtypeStruct((n_batch, ctx_len + seq, qkw), BF16),
                   jax.ShapeDtypeStruct((n_batch, ctx_len + seq, vw), BF16)],
        compiler_params=_cparams(1, 48),
        name="qkv_rope",
    )(xs, g, shift, scale, w_qkv, cos_t, sin_t)


def _rope_tables(seq, tm):
    pos = np.arange(seq)
    n_freq = DA_QK_DIM // 4
    inv = np.power(ROPE_BASE, -np.arange(n_freq, dtype=np.float32) / n_freq).astype(np.float32)
    row = (pos // GRID_W).astype(np.float32)[:, None] * inv
    col = (pos % GRID_W).astype(np.float32)[:, None] * inv
    ang = np.concatenate([row, col], axis=-1).astype(np.float32)
    ang = np.tile(ang, (1, LANES // ang.shape[1]))
    cos = np.concatenate([np.cos(ang), np.ones((tm, LANES))], axis=0)
    sin = np.concatenate([np.sin(ang), np.zeros((tm, LANES))], axis=0)
    return jnp.asarray(cos, F32), jnp.asarray(sin, F32)


def _attn_kernel(lp_ref, q_ref, k_ref, v_ref, sg_ref, o_ref, vext, *, n_sub, lam_init):
    @pl.when(pl.program_id(2) == 0)
    def _():
        vext[:, :LANES] = v_ref[...]
        vext[:, LANES:] = jnp.ones((vext.shape[0], LANES), BF16)

    lp = lp_ref[...]
    lam = (jnp.exp(jnp.sum(lp[0:1] * lp[1:2], axis=1, keepdims=True))
           - jnp.exp(jnp.sum(lp[2:3] * lp[3:4], axis=1, keepdims=True)) + lam_init)
    ts = q_ref.shape[0] // n_sub
    lane = lax.broadcasted_iota(jnp.int32, (ts, LANES), 1)
    zero = jnp.zeros((ts, LANES), BF16)
    k = k_ref[...]
    scores = []
    for a in range(n_sub):
        q = q_ref[a * ts:(a + 1) * ts, :]
        qq = jnp.concatenate([jnp.where(lane < DA_QK_DIM, q, zero),
                              jnp.where(lane >= DA_QK_DIM, q, zero)], axis=0)
        scores.append(_dot_nt(qq, k))
    for a in range(n_sub):
        s = scores[a]
        p = jnp.exp2(s - jnp.max(s, axis=-1, keepdims=True)).astype(BF16)
        oe = _dot(p, vext[...])
        on = oe[:, :LANES] / oe[:, LANES:LANES + 1]
        o = on[:ts] - lam * on[ts:]
        o_ref[a * ts:(a + 1) * ts, :] = (_rms(o, sg_ref[...]) * (1.0 - lam_init)).astype(BF16)


def _attn_call(lam_p, q, k_all, v_all, subln_g, *, tq, seq, n_batch, lam_init):
    n_heads = q.shape[1] // LANES
    lk = k_all.shape[1]
    qt = seq // tq
    return pl.pallas_call(
        functools.partial(_attn_kernel, n_sub=tq // ATTN_SUB_ROWS, lam_init=lam_init),
        grid=(n_batch, n_heads, qt),
        in_specs=[pl.BlockSpec(lam_p.shape, lambda b, h, i: (0, 0)),
                  pl.BlockSpec((tq, LANES), lambda b, h, i: (b * qt + i, h)),
                  pl.BlockSpec((None, lk, LANES), lambda b, h, i: (b, 0, h)),
                  pl.BlockSpec((None, lk, LANES), lambda b, h, i: (b, 0, h)),
                  pl.BlockSpec((1, LANES), lambda b, h, i: (0, 0))],
        out_specs=pl.BlockSpec((tq, LANES), lambda b, h, i: (b * qt + i, h)),
        out_shape=jax.ShapeDtypeStruct((n_batch * seq, n_heads * LANES), BF16),
        scratch_shapes=[pltpu.VMEM((lk, 2 * LANES), BF16)],
        compiler_params=_cparams(3, 48),
        name="diff_attention",
    )(lam_p, q, k_all, v_all, subln_g)


def kernel(x, c, ctx, c_ctx, ada_w, ada_b, norm_g, cf_w_in, cf_conv_w, cf_conv_b, cf_ln_g, cf_ln_b,
           cf_w_out, da_w_qkv, da_lambda, da_subln_g, da_w_out, moe_router_w, moe_router_b,
           moe_w_gate, moe_w_up, moe_w_down, moe_sh_gate, moe_sh_up, moe_sh_down):
    n_batch, seq, d = x.shape
    ctx_len = ctx.shape[1]
    depth = ada_w.shape[0]
    assert depth == 2 and n_batch + 1 <= N_SEG
    assert DA_QK_DIM * 2 == LANES and da_subln_g.shape[1] == LANES
    rows_lat, rows_ctx = n_batch * seq, n_batch * ctx_len
    tm = ROW_TILE
    tq = SEQ_TILE
    assert seq % tm == 0 and rows_ctx % tm == 0 and seq % tq == 0 and ctx_len % tq == 0

    x_lat, x_ctx = x.reshape(rows_lat, d), ctx.reshape(rows_ctx, d)
    cond = jnp.concatenate([c, c_ctx[None, :], jnp.zeros((N_SEG - n_batch - 1, d), F32)], axis=0)
    mods = _mods_call(cond, ada_w, ada_b)
    mod = lambda layer, k: mods[layer, :, k * d:(k + 1) * d][:, None, :]
    gain = lambda layer, k: norm_g[layer, k][None, :]
    common = dict(seq=seq, n_batch=n_batch)

    cc = cf_conv_w.shape[-1]
    fc = cf_w_in.shape[2] - 2 * cc
    gw = fc // FOURIER_GROUPS
    ch_ang = 2.0 * np.pi * (np.outer(np.arange(gw), np.arange(gw)) % gw) / gw
    cs = jnp.asarray(np.concatenate([np.cos(ch_ang), np.sin(ch_ang)], axis=1) / math.sqrt(gw), BF16)
    a_glu, ab = _inproj_call(x_lat, x_ctx, gain(0, 0), mod(0, 0), mod(0, 1), cf_w_in[0].astype(BF16), cs,
                             cc=cc, gw=gw, tm=tm, **common)
    gsz = cc // CONV_GROUPS
    gid = np.arange(cc) // gsz
    gmean = jnp.asarray((gid[:, None] == gid[None, :]) / gsz, BF16)
    a_act = _conv_call(a_glu, cf_conv_w[0][:, 0, :], cf_conv_b[0][None, :], cf_ln_g[0][None, :],
                       cf_ln_b[0][None, :], gmean, tl=tq, seq=seq, ctx_len=ctx_len, n_batch=n_batch)
    fr_lat = _seqdft_call(ab, seq, row0=0, n_seq=n_batch, fc=fc, tm=min(seq // 2, 256))
    fr_ctx = _seqdft_call(ab, ctx_len, row0=rows_lat, n_seq=n_batch, fc=fc, tm=min(ctx_len // 2, 256))
    w_out = cf_w_out[0].astype(BF16)
    xs, tok, tokp, *routing = _outproj_call(
        [(a_act, a_act), (fr_lat, fr_ctx)], [w_out[:cc], w_out[cc:]], x_lat, x_ctx, gain(0, 1), mod(0, 2), gain(0, 2), mod(0, 3),
        mod(0, 4), moe_router_w[0], moe_router_b[0], rows=rows_lat + rows_ctx, tm=tm, **common)
    xs = _moe_layer(xs, tok, tokp, routing, gain(0, 3), mod(0, 5), 0, moe_w_gate, moe_w_up, moe_w_down,
                    moe_sh_gate[0], moe_sh_up[0], moe_sh_down[0], rows=rows_lat + rows_ctx, tm=tm, **common)

    qkw = DA_HEADS * 2 * DA_QK_DIM
    lam_init = 0.8 - 0.6 * math.exp(-0.3 * 1)
    cos_t, sin_t = _rope_tables(seq, tq)
    q, k_all, v_all = _qkv_call(xs, gain(1, 0), mod(1, 0), mod(1, 1), da_w_qkv[0].astype(BF16),
                                cos_t, sin_t, qkw=qkw, tm=tq, ctx_len=ctx_len, **common)
    o = _attn_call(da_lambda[0], q, k_all, v_all, da_subln_g[0][None, :], tq=math.gcd(seq, ATTN_ROWS), seq=seq,
                   n_batch=n_batch, lam_init=lam_init)
    xs, tok, tokp, *routing = _outproj_call(
        [(o, o)], [da_w_out[0].astype(BF16)], xs, xs, gain(1, 1), mod(1, 2), gain(1, 2), mod(1, 3), mod(1, 4),
        moe_router_w[1], moe_router_b[1], rows=rows_lat, tm=tm, **common)
    xs = _moe_layer(xs, tok, tokp, routing, gain(1, 3), mod(1, 5), 1, moe_w_gate, moe_w_up, moe_w_down,
                    moe_sh_gate[1], moe_sh_up[1], moe_sh_down[1], rows=rows_lat, tm=tm, **common)
    return xs.reshape(n_batch, seq, d)
```

```python
import functools
import math

import numpy as np
import jax
import jax.numpy as jnp
from jax import lax
from jax.experimental import pallas as pl
from jax.experimental.pallas import tpu as pltpu
from jax.experimental.pallas import tpu_sc as plsc

F32 = jnp.float32
BF16 = jnp.bfloat16

EPS = 1e-6
GRID_W = 64
CONV_GROUPS = 8
FOURIER_GROUPS = 4
DA_HEADS = 8
DA_QK_DIM = 64
ROPE_BASE = 10000.0
N_GROUPS = 8
TOPK_GROUPS = 4
TOP_K = 6
ROUTED_SCALE = 2.5

LANES = 128
SUBLANES = 8
ROW_TILE = 512
SEQ_TILE = 256
N_SEG = 16
HALO = 16
CONV_GROUP_ROWS = 128
EXPERT_ROWS = 1024
EXPERT_SUB_ROWS = 128
EXPERT_RING = 3
ATTN_ROWS = 1024
ATTN_SUB_ROWS = 128
MIB = 1024 * 1024


def _cparams(n_axes, vmem_mib):
    return pltpu.CompilerParams(dimension_semantics=("arbitrary",) * n_axes,
                                vmem_limit_bytes=vmem_mib * MIB)


def _sigmoid(v):
    return 1.0 / (1.0 + jnp.exp(-v))


def _rms(v, g):
    return v * lax.rsqrt(jnp.mean(v * v, axis=-1, keepdims=True) + EPS) * g


def _normmod(v, g, shift, scale):
    return _rms(v, g) * (1.0 + scale) + shift


def _split_bf16(v):
    hi = v.astype(BF16)
    lo = (v - hi.astype(F32)).astype(BF16)
    return hi, lo


def _pack_halves(v):
    half = v.shape[1] // 2
    word = pltpu.pack_elementwise([v[:, :half], v[:, half:]], packed_dtype=BF16)
    return lax.bitcast_convert_type(word, jnp.int32)


def _unpack_halves(w):
    u = lax.bitcast_convert_type(w, jnp.uint32)
    lo = lax.bitcast_convert_type(u << 16, F32)
    hi = lax.bitcast_convert_type(u & jnp.uint32(0xFFFF0000), F32)
    return lo, hi


SC_SPLIT = 2


def _store_pieces(ref, words):
    q = words.shape[1] // SC_SPLIT
    for j in range(SC_SPLIT):
        ref[j] = words[:, j * q:(j + 1) * q]


def _load_pieces(ref):
    return jnp.concatenate([ref[j] for j in range(SC_SPLIT)], axis=-1)


def _dot(a, b):
    return jnp.dot(a, b, preferred_element_type=F32)


def _dot_nt(a, b):
    return lax.dot_general(a, b, (((1,), (1,)), ((), ())), preferred_element_type=F32)


def _mods_kernel(c_ref, w_ref, b_ref, o_ref):
    cv = c_ref[...]
    o_ref[...] = _dot(cv * _sigmoid(cv), w_ref[...]) + b_ref[...]


def _mods_call(cond, ada_w, ada_b):
    depth, d, n = ada_w.shape
    tn = n // 4
    return pl.pallas_call(
        _mods_kernel,
        grid=(depth, n // tn),
        in_specs=[pl.BlockSpec((N_SEG, d), lambda l, j: (0, 0)),
                  pl.BlockSpec((None, d, tn), lambda l, j: (l, 0, j)),
                  pl.BlockSpec((None, 1, tn), lambda l, j: (l, 0, j))],
        out_specs=pl.BlockSpec((None, N_SEG, tn), lambda l, j: (l, 0, j)),
        out_shape=jax.ShapeDtypeStruct((depth, N_SEG, n), F32),
        compiler_params=_cparams(2, 40),
        name="adaln_mods",
    )(cond, ada_w, ada_b.reshape(depth, 1, n))


def _two_source_specs(tm, d, n_head_tiles):
    return (pl.BlockSpec((tm, d), lambda i: (jnp.minimum(i, n_head_tiles - 1), 0)),
            pl.BlockSpec((tm, d), lambda i: (jnp.maximum(i - n_head_tiles, 0), 0)))


def _pick_rows(head_ref, tail_ref, n_head_tiles):
    return jnp.where(pl.program_id(0) < n_head_tiles, head_ref[...], tail_ref[...])


def _inproj_kernel(xh_ref, xt_ref, g_ref, sh_ref, sc_ref, w_ref, cs_ref, a_ref, ab_ref, *, cc, gw, n_head):
    x = _pick_rows(xh_ref, xt_ref, n_head)
    h = _normmod(x, g_ref[...], sh_ref[...], sc_ref[...]).astype(BF16)
    u = _dot(h, w_ref[...])
    a_ref[...] = u[:, :cc] * _sigmoid(u[:, cc:2 * cc])
    f = u[:, 2 * cc:].astype(BF16)
    n_g = f.shape[1] // gw
    parts = [_dot(f[:, g * gw:(g + 1) * gw], cs_ref[...]) for g in range(n_g)]
    cos_part = [p[:, :gw] for p in parts]
    sin_part = [p[:, gw:] for p in parts]
    ab_ref[...] = jnp.concatenate(cos_part + sin_part, axis=-1).astype(BF16)


def _seg_map(tm, seq, n_batch):
    return lambda i: (jnp.minimum(i * tm // seq, n_batch), 0, 0)


def _inproj_call(x_head, x_tail, g, shift, scale, w_in, cs, *, cc, gw, tm, seq, n_batch):
    d = x_head.shape[1]
    rows = x_head.shape[0] + x_tail.shape[0]
    n_head = x_head.shape[0] // tm
    n = w_in.shape[1]
    fc = n - 2 * cc
    seg = _seg_map(tm, seq, n_batch)
    return pl.pallas_call(
        functools.partial(_inproj_kernel, cc=cc, gw=gw, n_head=n_head),
        grid=(rows // tm,),
        in_specs=[*_two_source_specs(tm, d, n_head),
                  pl.BlockSpec((1, d), lambda i: (0, 0)),
                  pl.BlockSpec((None, 1, d), seg),
                  pl.BlockSpec((None, 1, d), seg),
                  pl.BlockSpec((d, n), lambda i: (0, 0)),
                  pl.BlockSpec((gw, 2 * gw), lambda i: (0, 0))],
        out_specs=[pl.BlockSpec((tm, cc), lambda i: (i, 0)),
                   pl.BlockSpec((tm, 2 * fc), lambda i: (i, 0))],
        out_shape=[jax.ShapeDtypeStruct((rows, cc), F32),
                   jax.ShapeDtypeStruct((rows, 2 * fc), BF16)],
        compiler_params=_cparams(1, 40),
        name="inproj_glu_chdft",
    )(x_head, x_tail, g, shift, scale, w_in, cs)


def _conv_kernel(prev_ref, main_ref, next_ref, w_ref, cb_ref, lg_ref, lb_ref, gm_ref, o_ref,
                 buf, cv, shifted, *, tl, width, lat_tiles, lat_tps, ctx_tps, chunk):
    i = pl.program_id(0)
    is_lat = i < lat_tiles
    tps = jnp.where(is_lat, lat_tps, ctx_tps)
    j = jnp.where(is_lat, i, i - lat_tiles) % tps
    zero = jnp.zeros((HALO, buf.shape[1]), F32)
    buf[0:HALO, :] = jnp.where(j > 0, prev_ref[...], zero)
    buf[HALO:HALO + tl, :] = main_ref[...]
    buf[HALO + tl:HALO + tl + HALO, :] = jnp.where(j < tps - 1, next_ref[...], zero)
    base = HALO - width // 2
    span = shifted.shape[1]
    for s in range(1, SUBLANES):
        shifted[s] = buf[s:s + span, :]

    def window(off, r0):
        s, m = off % SUBLANES, off - off % SUBLANES
        rows = buf[m + r0:m + r0 + chunk, :] if s == 0 else shifted[s, m + r0:m + r0 + chunk, :]
        return rows.reshape(chunk // SUBLANES, SUBLANES, rows.shape[1])

    gm = gm_ref[...]
    for g0 in range(0, tl, CONV_GROUP_ROWS):
        for r0 in range(g0, g0 + CONV_GROUP_ROWS, chunk):
            acc = window(base, r0) * w_ref[0]
            for t in range(1, width):
                acc = acc + window(base + t, r0) * w_ref[t]
            cv[r0:r0 + chunk, :] = acc.reshape(chunk, acc.shape[2])
        a = cv[g0:g0 + CONV_GROUP_ROWS, :] + cb_ref[...]
        a_hi, a_lo = _split_bf16(a)
        mu = _dot(a_hi, gm) + _dot(a_lo, gm)
        dl = a - mu
        q_hi, q_lo = _split_bf16(dl * dl)
        var = _dot(q_hi, gm) + _dot(q_lo, gm)
        y = dl * lax.rsqrt(var + EPS) * lg_ref[...] + lb_ref[...]
        o_ref[g0:g0 + CONV_GROUP_ROWS, :] = (y * _sigmoid(y)).astype(BF16)


def _conv_call(a_glu, conv_w, conv_b, ln_g, ln_b, gmean, *, tl, seq, ctx_len, n_batch):
    rows, ch = a_glu.shape
    width = conv_w.shape[0]
    assert width // 2 <= HALO and tl % HALO == 0 and tl % CONV_GROUP_ROWS == 0
    hb = tl // HALO
    last_halo = rows // HALO - 1
    return pl.pallas_call(
        functools.partial(_conv_kernel, tl=tl, width=width, lat_tiles=n_batch * seq // tl,
                          lat_tps=seq // tl, ctx_tps=ctx_len // tl, chunk=32),
        grid=(rows // tl,),
        in_specs=[pl.BlockSpec((HALO, ch), lambda i: (jnp.maximum(i * hb - 1, 0), 0)),
                  pl.BlockSpec((tl, ch), lambda i: (i, 0)),
                  pl.BlockSpec((HALO, ch), lambda i: (jnp.minimum((i + 1) * hb, last_halo), 0)),
                  pl.BlockSpec((width, SUBLANES, ch), lambda i: (0, 0, 0)),
                  pl.BlockSpec((1, ch), lambda i: (0, 0)),
                  pl.BlockSpec((1, ch), lambda i: (0, 0)),
                  pl.BlockSpec((1, ch), lambda i: (0, 0)),
                  pl.BlockSpec((ch, ch), lambda i: (0, 0))],
        out_specs=pl.BlockSpec((tl, ch), lambda i: (i, 0)),
        out_shape=jax.ShapeDtypeStruct((rows, ch), BF16),
        scratch_shapes=[pltpu.VMEM((tl + 2 * HALO, ch), F32), pltpu.VMEM((tl, ch), F32),
                        pltpu.VMEM((SUBLANES, tl + 2 * HALO - SUBLANES, ch), F32)],
        compiler_params=_cparams(1, 40),
        name="dwconv_groupln_swish",
    )(a_glu, a_glu, a_glu, jnp.broadcast_to(conv_w[:, None, :], (width, SUBLANES, ch)), conv_b, ln_g, ln_b, gmean)


EDGE_ROWS = 16


def _seqdft_kernel(c_ref, s_ref, cx_ref, sx_ref, jm_ref, a_ref, b_ref, o_ref, *, tm, n_tiles):
    i = pl.program_id(1)
    a, b = a_ref[...], b_ref[...]
    p, qv = _dot(c_ref[...], a), _dot(s_ref[...], b)
    o_ref[pl.ds(pl.multiple_of(i * tm, tm), tm), :] = (p - qv).astype(BF16)
    edge = _dot(cx_ref[...], a) + _dot(sx_ref[...], b)
    pick = lax.broadcasted_iota(jnp.int32, edge.shape, 0) == i
    edge_row = jnp.sum(jnp.where(pick, edge, 0.0), axis=0, keepdims=True)
    mirrored = _dot(jm_ref[...], (p + qv).astype(BF16))
    srow = lax.broadcasted_iota(jnp.int32, mirrored.shape, 0)
    mirrored = jnp.where(srow == 0, edge_row, mirrored)
    o_ref[pl.ds(pl.multiple_of((n_tiles - 1 - i) * tm, tm), tm), :] = mirrored.astype(BF16)


def _seqdft_call(ab, length, *, row0, n_seq, fc, tm):
    n_tiles = length // tm
    n_half = n_tiles // 2
    assert n_tiles % 2 == 0 and n_half <= EDGE_ROWS
    scale = 1.0 / math.sqrt(length)
    cmat, smat = _dft_tables(length, scale, np.arange(length // 2))
    edge_k = np.zeros((EDGE_ROWS,), np.int64)
    edge_k[:n_half] = tm * (np.arange(n_half) + 1)
    cx, sx = _dft_tables(length, scale, edge_k)
    s_idx = np.arange(tm)
    reversal = jnp.asarray((s_idx[None, :] == (tm - s_idx)[:, None]) & (s_idx[:, None] > 0), BF16)
    assert row0 % length == 0
    seq0 = row0 // length
    whole = lambda arr: pl.BlockSpec(arr.shape, lambda b, i: (0, 0))
    return pl.pallas_call(
        functools.partial(_seqdft_kernel, tm=tm, n_tiles=n_tiles),
        grid=(n_seq, n_half),
        in_specs=[pl.BlockSpec((tm, length), lambda b, i: (i, 0)),
                  pl.BlockSpec((tm, length), lambda b, i: (i, 0)),
                  whole(cx), whole(sx), whole(reversal),
                  pl.BlockSpec((length, fc), lambda b, i: (seq0 + b, 0)),
                  pl.BlockSpec((length, fc), lambda b, i: (seq0 + b, 1))],
        out_specs=pl.BlockSpec((length, fc), lambda b, i: (b, 0)),
        out_shape=jax.ShapeDtypeStruct((n_seq * length, fc), BF16),
        compiler_params=_cparams(2, 48),
        name="seq_dft",
    )(cmat, smat, cx, sx, reversal, ab, ab)


def _dft_tables(length, scale, rows):
    k = np.asarray(rows, np.int64)[:, None]
    if length <= 512:
        ang = 2.0 * np.pi * ((k * np.arange(length)[None, :]) % length) / length
        return (jnp.asarray(np.cos(ang) * scale, BF16), jnp.asarray(np.sin(ang) * scale, BF16))
    r = 64
    assert length % r == 0
    alpha = 2.0 * np.pi * ((k * np.arange(length // r)[None, :] * r) % length) / length
    beta = 2.0 * np.pi * ((k * np.arange(r)[None, :]) % length) / length
    ca, sa = jnp.asarray(np.cos(alpha), F32)[:, :, None], jnp.asarray(np.sin(alpha), F32)[:, :, None]
    cb, sb = jnp.asarray(np.cos(beta) * scale, F32)[:, None, :], jnp.asarray(np.sin(beta) * scale, F32)[:, None, :]
    cmat = (ca * cb - sa * sb).reshape(k.shape[0], length).astype(BF16)
    smat = (sa * cb + ca * sb).reshape(k.shape[0], length).astype(BF16)
    return cmat, smat


def _outproj_kernel(*refs, in_heads, n_head):
    n_in = len(in_heads)
    ins, ws = refs[:2 * n_in], refs[2 * n_in:3 * n_in]
    (xh_ref, xt_ref, g1_ref, gate_ref, g2_ref, sh_ref, sc_ref, wh_ref, wl_ref, rb_ref, upper_ref,
     xo_ref, tok_ref, tokp_ref, e_ref, gt_ref, pos_ref, cnt_ref, carry) = refs[3 * n_in:]
    @pl.when(pl.program_id(0) == 0)
    def _():
        carry[...] = jnp.zeros_like(carry)

    half = xo_ref.shape[0] // 2
    spans = [slice(0, half), slice(half, 2 * half)]
    ys = []
    for rs in spans:
        y = None
        for j, w_ref in enumerate(ws):
            part = _dot(_pick_rows(ins[2 * j], ins[2 * j + 1], in_heads[j])[rs], w_ref[...])
            y = part if y is None else y + part
        ys.append(y)
    for rs, y in zip(spans, ys):
        x1 = _pick_rows(xh_ref, xt_ref, n_head)[rs] + gate_ref[...] * _rms(y, g1_ref[...])
        xo_ref[rs, :] = x1
        tok = _normmod(x1, g2_ref[...], sh_ref[...], sc_ref[...])
        tok_ref[rs, :] = tok.astype(BF16)
        words = _pack_halves(tok)
        q = words.shape[1] // SC_SPLIT
        for j in range(SC_SPLIT):
            tokp_ref[j, rs, :] = words[:, j * q:(j + 1) * q]
        _route(tok, rs, wh_ref, wl_ref, rb_ref, upper_ref, e_ref, gt_ref, pos_ref, cnt_ref, carry)


def _outproj_call(ins, ws, x_head, x_tail, g1, gate, g2, shift, scale, router_w, router_b,
                  *, rows, tm, seq, n_batch):
    d = x_head.shape[1]
    n_head = min(x_head.shape[0], rows) // tm
    n_exp = router_w.shape[1]
    w_t = router_w.T
    w_hi = w_t.astype(BF16)
    w_lo = (w_t - w_hi.astype(F32)).astype(BF16)
    upper = jnp.asarray(np.triu(np.ones((tm // 2, tm // 2), np.float32), k=1), BF16)
    seg = _seg_map(tm, seq, n_batch)
    in_heads =tuple(min(head.shape[0], rows) // tm for head, _ in ins)
    pair_specs = [s for (head, _), nh in zip(ins, in_heads) for s in _two_source_specs(tm, head.shape[1], nh)]
    full_spec = lambda a: pl.BlockSpec(a.shape, lambda i: (0, 0))
    lane_spec = pl.BlockSpec((SUBLANES, tm), lambda i: (0, i))
    tok_rows = lambda dt: jax.ShapeDtypeStruct((SUBLANES, rows), dt)
    return pl.pallas_call(
        functools.partial(_outproj_kernel, in_heads=in_heads, n_head=n_head),
        grid=(rows // tm,),
        in_specs=pair_specs + [full_spec(w) for w in ws] + [
            *_two_source_specs(tm, d, n_head), full_spec(g1), pl.BlockSpec((None, 1, d), seg), full_spec(g2),
            pl.BlockSpec((None, 1, d), seg), pl.BlockSpec((None, 1, d), seg),
            full_spec(w_hi), full_spec(w_lo), pl.BlockSpec((n_exp, 1), lambda i: (0, 0)), full_spec(upper)],
        out_specs=[pl.BlockSpec((tm, d), lambda i: (i, 0)), pl.BlockSpec((tm, d), lambda i: (i, 0)),
                   pl.BlockSpec((SC_SPLIT, tm, d // 2 // SC_SPLIT), lambda i: (0, i, 0)),
                   lane_spec, pl.BlockSpec((tm, SUBLANES), lambda i: (i, 0)), lane_spec,
                   pl.BlockSpec((n_exp, LANES), lambda i: (0, 0))],
        out_shape=[jax.ShapeDtypeStruct((rows, d), F32), jax.ShapeDtypeStruct((rows, d), BF16),
                   jax.ShapeDtypeStruct((SC_SPLIT, rows, d // 2 // SC_SPLIT), jnp.int32),
                   tok_rows(jnp.int32), jax.ShapeDtypeStruct((rows, SUBLANES), F32), tok_rows(jnp.int32),
                   jax.ShapeDtypeStruct((n_exp, LANES), F32)],
        scratch_shapes=[pltpu.VMEM((n_exp, 1), F32)],
        compiler_params=_cparams(1, 48),
        name="outproj_residual_route",
    )(*[a for pair in ins for a in pair], *ws, x_head, x_tail, g1, gate, g2, shift, scale, w_hi, w_lo,
      router_b.reshape(n_exp, 1), upper)


def _pick_max(cur, idx):
    mx = jnp.max(cur, axis=0, keepdims=True)
    first = jnp.min(jnp.where(cur == mx, idx, float(cur.shape[0])), axis=0, keepdims=True)
    return first, idx == first


def _route(tok, rs, wh_ref, wl_ref, rb_ref, upper_ref, e_ref, gt_ref, pos_ref, cnt_ref, carry):
    t_hi, t_lo = _split_bf16(tok)
    wh, wl = wh_ref[...], wl_ref[...]
    logits = _dot_nt(wh, t_hi) + _dot_nt(wh, t_lo) + _dot_nt(wl, t_hi)
    n_exp, tm = logits.shape
    scores = _sigmoid(logits)
    biased = scores + rb_ref[...]
    gsz = n_exp // N_GROUPS
    neg = -jnp.inf

    b3 = biased.reshape(N_GROUPS, gsz, tm)
    im = lax.broadcasted_iota(jnp.int32, b3.shape, 1).astype(F32)
    m1 = jnp.max(b3, axis=1, keepdims=True)
    i1 = jnp.min(jnp.where(b3 == m1, im, float(gsz)), axis=1, keepdims=True)
    m2 = jnp.max(jnp.where(im == i1, neg, b3), axis=1, keepdims=True)
    gscore = (m1 + m2).reshape(N_GROUPS, tm)

    ig = lax.broadcasted_iota(jnp.int32, gscore.shape, 0).astype(F32)
    gsel = jnp.zeros_like(gscore)
    cur = gscore
    for _ in range(TOPK_GROUPS):
        _, hit = _pick_max(cur, ig)
        gsel = jnp.where(hit, 1.0, gsel)
        cur = jnp.where(hit, neg, cur)
    gsel3 = jnp.broadcast_to(gsel.reshape(N_GROUPS, 1, tm), b3.shape)
    cur = jnp.where(gsel3 > 0.0, b3, neg).reshape(n_exp, tm)

    ie = lax.broadcasted_iota(jnp.int32, (n_exp, tm), 0).astype(F32)
    sel = jnp.zeros((n_exp, tm), F32)
    picks, raw = [], []
    for _ in range(TOP_K):
        first, hit = _pick_max(cur, ie)
        picks.append(first)
        raw.append(jnp.sum(jnp.where(hit, scores, 0.0), axis=0, keepdims=True))
        sel = jnp.where(hit, 1.0, sel)
        cur = jnp.where(hit, neg, cur)
    total = raw[0]
    for r in raw[1:]:
        total = total + r

    rank = _dot(sel.astype(BF16), upper_ref[...]) + carry[...]
    ranks = [jnp.sum(jnp.where(ie == p, rank, 0.0), axis=0, keepdims=True) for p in picks]
    carry[...] = carry[...] + jnp.sum(sel, axis=1, keepdims=True)

    pad = jnp.zeros((SUBLANES - TOP_K, tm), F32)
    e_ref[:, rs] = jnp.concatenate(picks + [pad], axis=0).astype(jnp.int32)
    gates = jnp.concatenate([r / total * ROUTED_SCALE for r in raw]
                            + [jnp.zeros((LANES - TOP_K, tm), F32)], axis=0)
    gt_ref[rs, :] = gates.T[:, :SUBLANES]
    pos_ref[:, rs] = jnp.concatenate(ranks + [pad], axis=0).astype(jnp.int32)
    cnt_ref[...] = jnp.broadcast_to(carry[...], cnt_ref.shape)


def _expert_kernel(be_ref, nv_ref, nu_ref, xs_hbm, wg_ref, wu_ref, wd_ref, ys_ref, wgu, wdn, xbuf, sem,
                   *, ff, n_sub):
    b = pl.program_id(0)
    n_used = nu_ref[0]
    depth, _, te, q = xbuf.shape

    def row_block_copy(blk):
        slot = blk % depth
        return pltpu.make_async_copy(xs_hbm.at[:, pl.ds(blk * te, te), :], xbuf.at[slot], sem.at[slot])

    @pl.when(b == 0)
    def _():
        for j in range(depth - 1):
            @pl.when(j < n_used)
            def _():
                row_block_copy(j).start()

    @pl.when(b + depth - 1 < n_used)
    def _():
        row_block_copy(b + depth - 1).start()

    changed = jnp.logical_or(b == 0, be_ref[b] != be_ref[jnp.maximum(b - 1, 0)])

    @pl.when(changed)
    def _():
        wgu[:, :ff] = wg_ref[...].astype(BF16)
        wgu[:, ff:] = wu_ref[...].astype(BF16)
        wdn[...] = wd_ref[...].astype(BF16)

    @pl.when(b < n_used)
    def _():
        row_block_copy(b).wait()
        slot = b % depth
        n_valid = nv_ref[b]
        ts = te // n_sub
        gus = []
        for a in range(n_sub):
            r0 = a * ts
            xw = jnp.concatenate([xbuf[slot, j, r0:r0 + ts, :] for j in range(SC_SPLIT)], axis=-1)
            row = lax.broadcasted_iota(jnp.int32, xw.shape, 0) + r0
            lo, hi = _unpack_halves(jnp.where(row < n_valid, xw, 0))
            half = lo.shape[1]
            gus.append(_dot(lo.astype(BF16), wgu[:half, :]) + _dot(hi.astype(BF16), wgu[half:, :]))
        outs = []
        for gu in gus:
            gate = gu[:, :ff]
            hmid = (gate * _sigmoid(gate) * gu[:, ff:]).astype(BF16)
            outs.append(_dot(hmid, wdn[...]))
        for a, y in enumerate(outs):
            words = _pack_halves(y)
            for j in range(SC_SPLIT):
                ys_ref[j, a * ts:(a + 1) * ts, :] = words[:, j * q:(j + 1) * q]


def _expert_call(block_e, n_valid, n_used, xs, layer, w_gate, w_up, w_down):
    _, n_rows, q = xs.shape
    d = 2 * SC_SPLIT * q
    ff = w_gate.shape[3]
    te = EXPERT_ROWS
    row_map = lambda b, be, nv, nu: (0, jnp.minimum(b, nu[0] - 1), 0)
    w_map = lambda b, be, nv, nu: (layer, be[b], 0, 0)
    grid_spec = pltpu.PrefetchScalarGridSpec(
        num_scalar_prefetch=3,
        grid=(n_rows // te,),
        in_specs=[pl.BlockSpec(memory_space=pl.ANY),
                  pl.BlockSpec((None, None, d, ff), w_map),
                  pl.BlockSpec((None, None, d, ff), w_map),
                  pl.BlockSpec((None, None, ff, d), w_map)],
        out_specs=pl.BlockSpec((SC_SPLIT, te, q), row_map),
        scratch_shapes=[pltpu.VMEM((d, 2 * ff), BF16), pltpu.VMEM((ff, d), BF16),
                        pltpu.VMEM((EXPERT_RING, SC_SPLIT, te, q), jnp.int32),
                        pltpu.SemaphoreType.DMA((EXPERT_RING,))])
    return pl.pallas_call(
        functools.partial(_expert_kernel, ff=ff, n_sub=EXPERT_ROWS // EXPERT_SUB_ROWS),
        grid_spec=grid_spec,
        out_shape=jax.ShapeDtypeStruct(xs.shape, jnp.int32),
        compiler_params=_cparams(1, 40),
        name="moe_experts",
    )(block_e, n_valid, n_used, xs, w_gate, w_up, w_down)


SC_WINDOW = 128


def _sc_mesh():
    return plsc.VectorSubcoreMesh(core_axis_name="core", subcore_axis_name="subcore")


def _sc_scatter_rows(rows, dests, n_out):
    n, width = rows.shape
    n_k = len(dests)
    assert n % SC_WINDOW == 0

    @functools.partial(pl.kernel, out_type=jax.ShapeDtypeStruct((n_out, width), rows.dtype),
                       mesh=_sc_mesh(), name="sc_dispatch_rows")
    def scatter_kernel(x_hbm, *refs):
        idx_hbm, o_hbm = refs[:n_k], refs[n_k]

        def body(x_vmem, *idx_vmem):
            for iv in idx_vmem:
                pltpu.sync_copy(x_vmem, o_hbm.at[iv.at[0]])

        pltpu.emit_pipeline(
            body,
            grid=(n // SC_WINDOW,),
            in_specs=[pl.BlockSpec((SC_WINDOW, width), lambda i: (i, 0))]
            + [pl.BlockSpec((1, SC_WINDOW), lambda i: (0, i))] * n_k,
            out_specs=[],
            core_axis_name=("core", "subcore"),
            dimension_semantics=(pltpu.PARALLEL,),
        )(x_hbm, *idx_hbm)

    return scatter_kernel(rows, *dests)


def _sc_gather_rows(table, idx):
    n = idx.shape[1]
    width = table.shape[1]
    assert n % SC_WINDOW == 0

    @functools.partial(pl.kernel, out_type=jax.ShapeDtypeStruct((n, width), table.dtype),
                       mesh=_sc_mesh(), name="sc_collect_rows")
    def gather_kernel(t_hbm, i_hbm, o_hbm):
        def body(i_vmem, o_vmem):
            pltpu.sync_copy(t_hbm.at[i_vmem.at[0]], o_vmem)

        pltpu.emit_pipeline(
            body,
            grid=(n // SC_WINDOW,),
            in_specs=[pl.BlockSpec((1, SC_WINDOW), lambda i: (0, i))],
            out_specs=[pl.BlockSpec((SC_WINDOW, width), lambda i: (i, 0))],
            core_axis_name=("core", "subcore"),
            dimension_semantics=(pltpu.PARALLEL,),
        )(i_hbm, o_hbm)

    return gather_kernel(table, idx)


def _shared_kernel(tok_ref, sgu_ref, sd_ref, o_ref, *, ff):
    gu = _dot(tok_ref[...], sgu_ref[...])
    gate = gu[:, :ff]
    hmid = (gate * _sigmoid(gate) * gu[:, ff:]).astype(BF16)
    o_ref[...] = _dot(hmid, sd_ref[...]).astype(BF16)


def _shared_call(tok, sgu, sd, *, rows, tm):
    d = tok.shape[1]
    ff = sd.shape[0]
    return pl.pallas_call(
        functools.partial(_shared_kernel, ff=ff),
        grid=(rows // tm,),
        in_specs=[pl.BlockSpec((tm, d), lambda i: (i, 0)),
                  pl.BlockSpec((d, 2 * ff), lambda i: (0, 0)),
                  pl.BlockSpec((ff, d), lambda i: (0, 0))],
        out_specs=pl.BlockSpec((tm, d), lambda i: (i, 0)),
        out_shape=jax.ShapeDtypeStruct((rows, d), BF16),
        compiler_params=_cparams(1, 40),
        name="moe_shared_expert",
    )(tok, sgu, sd)


def _moe_out_kernel(shared_ref, yg_ref, gk_ref, x_ref, g_ref, gate_ref, xo_ref):
    shared = shared_ref[...].astype(F32)
    gk = gk_ref[...]
    lo_acc, hi_acc = None, None
    for k in range(TOP_K):
        lo, hi = _unpack_halves(jnp.concatenate([yg_ref[j, k] for j in range(SC_SPLIT)], axis=-1))
        w = gk[:, k:k + 1]
        lo_acc = lo * w if lo_acc is None else lo_acc + lo * w
        hi_acc = hi * w if hi_acc is None else hi_acc + hi * w
    y = jnp.concatenate([lo_acc, hi_acc], axis=-1) + shared
    xo_ref[...] = x_ref[...] + gate_ref[...] * _rms(y, g_ref[...])


def _moe_out_call(shared, yg, gk, xs, g, gate, *, rows, tm, seq, n_batch):
    d = xs.shape[1]
    seg = _seg_map(tm, seq, n_batch)
    row = lambda: pl.BlockSpec((tm, d), lambda i: (i, 0))
    return pl.pallas_call(
        _moe_out_kernel,
        grid=(rows // tm,),
        in_specs=[row(),
                  pl.BlockSpec((SC_SPLIT, TOP_K, tm, d // 2 // SC_SPLIT), lambda i: (0, 0, i, 0)),
                  pl.BlockSpec((tm, SUBLANES), lambda i: (i, 0)),
                  row(),
                  pl.BlockSpec((1, d), lambda i: (0, 0)),
                  pl.BlockSpec((None, 1, d), seg)],
        out_specs=row(),
        out_shape=jax.ShapeDtypeStruct((rows, d), F32),
        compiler_params=_cparams(1, 48),
        name="moe_combine",
    )(shared, yg, gk, xs, g, gate)


def _dest_kernel(e_ref, pos_ref, ps_ref, o_ref, *, n_rows):
    e = e_ref[...].astype(F32)
    ps = ps_ref[...]
    n_exp, tm = ps.shape[0], e.shape[1]
    ids = lax.broadcasted_iota(jnp.int32, (n_exp, tm), 0).astype(F32)
    rows = [jnp.sum(jnp.where(ids == e[k:k + 1, :], ps, 0.0), axis=0, keepdims=True) for k in range(SUBLANES)]
    dest = jnp.concatenate(rows, axis=0).astype(jnp.int32) + pos_ref[...]
    for j in range(SC_SPLIT):
        o_ref[j] = dest + j * n_rows


def _dest_call(e_tk, pos_tk, pad_starts, *, n_rows, tm):
    rows = e_tk.shape[1]
    n_exp = pad_starts.shape[0]
    lane_spec = pl.BlockSpec((SUBLANES, tm), lambda i: (0, i))
    return pl.pallas_call(
        functools.partial(_dest_kernel, n_rows=n_rows),
        grid=(rows // tm,),
        in_specs=[lane_spec, lane_spec, pl.BlockSpec((n_exp, 1), lambda i: (0, 0))],
        out_specs=pl.BlockSpec((SC_SPLIT, SUBLANES, tm), lambda i: (0, 0, i)),
        out_shape=jax.ShapeDtypeStruct((SC_SPLIT, SUBLANES, rows), jnp.int32),
        compiler_params=_cparams(1, 32),
        name="moe_dest_rows",
    )(e_tk, pos_tk, pad_starts.astype(F32).reshape(n_exp, 1))


def _moe_layer(xs, tok, tokp, routing, g_out, gate, layer, w_gate, w_up, w_down, sh_gate, sh_up, sh_down,
               *, rows, tm, seq, n_batch):
    e_tk, g_tk, pos_tk, counts = routing
    n_exp = counts.shape[0]
    te = EXPERT_ROWS
    cnt = counts[:, 0].astype(jnp.int32)
    padded = (cnt + te - 1) // te * te
    pad_ends = jnp.cumsum(padded)
    pad_starts = pad_ends - padded
    expert_ids = jnp.arange(n_exp, dtype=jnp.int32)
    n_blocks = -(-(rows * TOP_K + n_exp * (te - 1)) // te)
    n_used = pad_ends[-1] // te
    blk = jnp.arange(n_blocks, dtype=jnp.int32)
    last = jnp.minimum(blk, n_used - 1) * te
    owner = (last[:, None] >= pad_ends[None, :]).astype(jnp.int32)
    block_e = jnp.minimum(jnp.sum(owner, axis=1), n_exp - 1)
    own_hot = block_e[:, None] == expert_ids[None, :]
    n_valid = jnp.sum(jnp.where(own_hot, (pad_starts + cnt)[None, :], 0), axis=1) - last
    n_valid = jnp.clip(n_valid, 0, te)
    q = tokp.shape[2]
    n_rows = n_blocks * te
    piece = _dest_call(e_tk, pos_tk, pad_starts, n_rows=n_rows, tm=tm)[:, :TOP_K, :]
    xg = _sc_scatter_rows(tokp.reshape(SC_SPLIT * rows, q),
                          [piece[:, k, :].reshape(1, SC_SPLIT * rows) for k in range(TOP_K)],
                          SC_SPLIT * n_rows).reshape(SC_SPLIT, n_rows, q)
    sgu = jnp.concatenate([sh_gate, sh_up], axis=1).astype(BF16)
    shared = _shared_call(tok, sgu, sh_down.astype(BF16), rows=rows, tm=tm)
    ys = _expert_call(block_e, n_valid, n_used.reshape(1).astype(jnp.int32), xg, layer, w_gate, w_up, w_down)
    yg = _sc_gather_rows(ys.reshape(SC_SPLIT * n_rows, q),
                         piece.reshape(1, -1)).reshape(SC_SPLIT, TOP_K, rows, q)
    return _moe_out_call(shared, yg, g_tk, xs, g_out, gate, rows=rows, tm=tm, seq=seq, n_batch=n_batch)


def _rope(t, cos, sin):
    half = DA_QK_DIM // 2
    up = pltpu.roll(t, shift=LANES - half, axis=1)
    dn = pltpu.roll(t, shift=half, axis=1)
    lane = lax.broadcasted_iota(jnp.int32, t.shape, 1) % DA_QK_DIM
    return t * cos + jnp.where(lane < half, -up, dn) * sin


def _qkv_kernel(x_ref, g_ref, sh_ref, sc_ref, w_ref, cos_ref, sin_ref, q_ref, k_ref, v_ref, *, qkw):
    h = _normmod(x_ref[...], g_ref[...], sh_ref[...], sc_ref[...]).astype(BF16)
    qkv = _dot(h, w_ref[...])
    cos, sin = cos_ref[...], sin_ref[...]
    q_scale = DA_QK_DIM ** -0.5 * math.log2(math.e)
    for hb in range(qkw // LANES):
        lo, hi = hb * LANES, (hb + 1) * LANES
        q_ref[:, lo:hi] = (_rope(qkv[:, lo:hi], cos, sin) * q_scale).astype(BF16)
        k_ref[:, lo:hi] = _rope(qkv[:, qkw + lo:qkw + hi], cos, sin).astype(BF16)
    v_ref[...] = qkv[:, 2 * qkw:].astype(BF16)


def _qkv_call(xs, g, shift, scale, w_qkv, cos_t, sin_t, *, qkw, tm, seq, ctx_len, n_batch):
    rows, d = xs.shape
    n = w_qkv.shape[1]
    vw = n - 2 * qkw
    lat_tiles, lat_tps, ctx_tps = n_batch * seq // tm, seq // tm, ctx_len // tm
    seg = _seg_map(tm, seq, n_batch)

    def kv_map(i):
        c = i - lat_tiles
        is_lat = i < lat_tiles
        return (jnp.where(is_lat, i // lat_tps, c // ctx_tps),
                jnp.where(is_lat, ctx_tps + i % lat_tps, c % ctx_tps), 0)

    rope_map = lambda i: (jnp.where(i < lat_tiles, i % lat_tps, lat_tps), 0)
    return pl.pallas_call(
        functools.partial(_qkv_kernel, qkw=qkw),
        grid=(rows // tm,),
        in_specs=[pl.BlockSpec((tm, d), lambda i: (i, 0)),
                  pl.BlockSpec((1, d), lambda i: (0, 0)),
                  pl.BlockSpec((None, 1, d), seg),
                  pl.BlockSpec((None, 1, d), seg),
                  pl.BlockSpec((d, n), lambda i: (0, 0)),
                  pl.BlockSpec((tm, LANES), rope_map),
                  pl.BlockSpec((tm, LANES), rope_map)],
        out_specs=[pl.BlockSpec((tm, qkw), lambda i: (i, 0)),
                   pl.BlockSpec((None, tm, qkw), kv_map),
                   pl.BlockSpec((None, tm, vw), kv_map)],
        out_shape=[jax.ShapeDtypeStruct((rows, qkw), BF16),
                   jax.ShapeDtypeStruct((n_batch, ctx_len + seq, qkw), BF16),
                   jax.ShapeDtypeStruct((n_batch, ctx_len + seq, vw), BF16)],
        compiler_params=_cparams(1, 48),
        name="qkv_rope",
    )(xs, g, shift, scale, w_qkv, cos_t, sin_t)


def _rope_tables(seq, tm):
    pos = np.arange(seq)
    n_freq = DA_QK_DIM // 4
    inv = np.power(ROPE_BASE, -np.arange(n_freq, dtype=np.float32) / n_freq).astype(np.float32)
    row = (pos // GRID_W).astype(np.float32)[:, None] * inv
    col = (pos % GRID_W).astype(np.float32)[:, None] * inv
    ang = np.concatenate([row, col], axis=-1).astype(np.float32)
    ang = np.tile(ang, (1, LANES // ang.shape[1]))
    cos = np.concatenate([np.cos(ang), np.ones((tm, LANES))], axis=0)
    sin = np.concatenate([np.sin(ang), np.zeros((tm, LANES))], axis=0)
    return jnp.asarray(cos, F32), jnp.asarray(sin, F32)


def _attn_kernel(lp_ref, q_ref, k_ref, v_ref, sg_ref, o_ref, vext, *, n_sub, lam_init):
    @pl.when(pl.program_id(2) == 0)
    def _():
        vext[:, :LANES] = v_ref[...]
        vext[:, LANES:] = jnp.ones((vext.shape[0], LANES), BF16)

    lp = lp_ref[...]
    lam = (jnp.exp(jnp.sum(lp[0:1] * lp[1:2], axis=1, keepdims=True))
           - jnp.exp(jnp.sum(lp[2:3] * lp[3:4], axis=1, keepdims=True)) + lam_init)
    ts = q_ref.shape[0] // n_sub
    lane = lax.broadcasted_iota(jnp.int32, (ts, LANES), 1)
    zero = jnp.zeros((ts, LANES), BF16)
    k = k_ref[...]
    scores = []
    for a in range(n_sub):
        q = q_ref[a * ts:(a + 1) * ts, :]
        qq = jnp.concatenate([jnp.where(lane < DA_QK_DIM, q, zero),
                              jnp.where(lane >= DA_QK_DIM, q, zero)], axis=0)
        scores.append(_dot_nt(qq, k))
    for a in range(n_sub):
        s = scores[a]
        p = jnp.exp2(s - jnp.max(s, axis=-1, keepdims=True)).astype(BF16)
        oe = _dot(p, vext[...])
        on = oe[:, :LANES] / oe[:, LANES:LANES + 1]
        o = on[:ts] - lam * on[ts:]
        o_ref[a * ts:(a + 1) * ts, :] = (_rms(o, sg_ref[...]) * (1.0 - lam_init)).astype(BF16)


def _attn_call(lam_p, q, k_all, v_all, subln_g, *, tq, seq, n_batch, lam_init):
    n_heads = q.shape[1] // LANES
    lk = k_all.shape[1]
    qt = seq // tq
    return pl.pallas_call(
        functools.partial(_attn_kernel, n_sub=tq // ATTN_SUB_ROWS, lam_init=lam_init),
        grid=(n_batch, n_heads, qt),
        in_specs=[pl.BlockSpec(lam_p.shape, lambda b, h, i: (0, 0)),
                  pl.BlockSpec((tq, LANES), lambda b, h, i: (b * qt + i, h)),
                  pl.BlockSpec((None, lk, LANES), lambda b, h, i: (b, 0, h)),
                  pl.BlockSpec((None, lk, LANES), lambda b, h, i: (b, 0, h)),
                  pl.BlockSpec((1, LANES), lambda b, h, i: (0, 0))],
        out_specs=pl.BlockSpec((tq, LANES), lambda b, h, i: (b * qt + i, h)),
        out_shape=jax.ShapeDtypeStruct((n_batch * seq, n_heads * LANES), BF16),
        scratch_shapes=[pltpu.VMEM((lk, 2 * LANES), BF16)],
        compiler_params=_cparams(3, 48),
        name="diff_attention",
    )(lam_p, q, k_all, v_all, subln_g)


def kernel(x, c, ctx, c_ctx, ada_w, ada_b, norm_g, cf_w_in, cf_conv_w, cf_conv_b, cf_ln_g, cf_ln_b,
           cf_w_out, da_w_qkv, da_lambda, da_subln_g, da_w_out, moe_router_w, moe_router_b,
           moe_w_gate, moe_w_up, moe_w_down, moe_sh_gate, moe_sh_up, moe_sh_down):
    n_batch, seq, d = x.shape
    ctx_len = ctx.shape[1]
    depth = ada_w.shape[0]
    assert depth == 2 and n_batch + 1 <= N_SEG
    assert DA_QK_DIM * 2 == LANES and da_subln_g.shape[1] == LANES
    rows_lat, rows_ctx = n_batch * seq, n_batch * ctx_len
    tm = ROW_TILE
    tq = SEQ_TILE
    assert seq % tm == 0 and rows_ctx % tm == 0 and seq % tq == 0 and ctx_len % tq == 0

    x_lat, x_ctx = x.reshape(rows_lat, d), ctx.reshape(rows_ctx, d)
    cond = jnp.concatenate([c, c_ctx[None, :], jnp.zeros((N_SEG - n_batch - 1, d), F32)], axis=0)
    mods = _mods_call(cond, ada_w, ada_b)
    mod = lambda layer, k: mods[layer, :, k * d:(k + 1) * d][:, None, :]
    gain = lambda layer, k: norm_g[layer, k][None, :]
    common = dict(seq=seq, n_batch=n_batch)

    cc = cf_conv_w.shape[-1]
    fc = cf_w_in.shape[2] - 2 * cc
    gw = fc // FOURIER_GROUPS
    ch_ang = 2.0 * np.pi * (np.outer(np.arange(gw), np.arange(gw)) % gw) / gw
    cs = jnp.asarray(np.concatenate([np.cos(ch_ang), np.sin(ch_ang)], axis=1) / math.sqrt(gw), BF16)
    a_glu, ab = _inproj_call(x_lat, x_ctx, gain(0, 0), mod(0, 0), mod(0, 1), cf_w_in[0].astype(BF16), cs,
                             cc=cc, gw=gw, tm=tm, **common)
    gsz = cc // CONV_GROUPS
    gid = np.arange(cc) // gsz
    gmean = jnp.asarray((gid[:, None] == gid[None, :]) / gsz, BF16)
    a_act = _conv_call(a_glu, cf_conv_w[0][:, 0, :], cf_conv_b[0][None, :], cf_ln_g[0][None, :],
                       cf_ln_b[0][None, :], gmean, tl=tq, seq=seq, ctx_len=ctx_len, n_batch=n_batch)
    fr_lat = _seqdft_call(ab, seq, row0=0, n_seq=n_batch, fc=fc, tm=min(seq // 2, 256))
    fr_ctx = _seqdft_call(ab, ctx_len, row0=rows_lat, n_seq=n_batch, fc=fc, tm=min(ctx_len // 2, 256))
    w_out = cf_w_out[0].astype(BF16)
    xs, tok, tokp, *routing = _outproj_call(
        [(a_act, a_act), (fr_lat, fr_ctx)], [w_out[:cc], w_out[cc:]], x_lat, x_ctx, gain(0, 1), mod(0, 2), gain(0, 2), mod(0, 3),
        mod(0, 4), moe_router_w[0], moe_router_b[0], rows=rows_lat + rows_ctx, tm=tm, **common)
    xs = _moe_layer(xs, tok, tokp, routing, gain(0, 3), mod(0, 5), 0, moe_w_gate, moe_w_up, moe_w_down,
                    moe_sh_gate[0], moe_sh_up[0], moe_sh_down[0], rows=rows_lat + rows_ctx, tm=tm, **common)

    qkw = DA_HEADS * 2 * DA_QK_DIM
    lam_init = 0.8 - 0.6 * math.exp(-0.3 * 1)
    cos_t, sin_t = _rope_tables(seq, tq)
    q, k_all, v_all = _qkv_call(xs, gain(1, 0), mod(1, 0), mod(1, 1), da_w_qkv[0].astype(BF16),
                                cos_t, sin_t, qkw=qkw, tm=tq, ctx_len=ctx_len, **common)
    o = _attn_call(da_lambda[0], q, k_all, v_all, da_subln_g[0][None, :], tq=math.gcd(seq, ATTN_ROWS), seq=seq,
                   n_batch=n_batch, lam_init=lam_init)
    xs, tok, tokp, *routing = _outproj_call(
        [(o, o)], [da_w_out[0].astype(BF16)], xs, xs, gain(1, 1), mod(1, 2), gain(1, 2), mod(1, 3), mod(1, 4),
        moe_router_w[1], moe_router_b[1], rows=rows_lat, tm=tm, **common)
    xs = _moe_layer(xs, tok, tokp, routing, gain(1, 3), mod(1, 5), 1, moe_w_gate, moe_w_up, moe_w_down,
                    moe_sh_gate[1], moe_sh_up[1], moe_sh_down[1], rows=rows_lat, tm=tm, **common)
    return xs.reshape(n_batch, seq, d)
```

```python
import functools
import math

import numpy as np
import jax
import jax.numpy as jnp
from jax import lax
from jax.experimental import pallas as pl
from jax.experimental.pallas import tpu as pltpu
from jax.experimental.pallas import tpu_sc as plsc

F32 = jnp.float32
BF16 = jnp.bfloat16

EPS = 1e-6
GRID_W = 64
CONV_GROUPS = 8
FOURIER_GROUPS = 4
DA_HEADS = 8
DA_QK_DIM = 64
ROPE_BASE = 10000.0
N_GROUPS = 8
TOPK_GROUPS = 4
TOP_K = 6
ROUTED_SCALE = 2.5

LANES = 128
SUBLANES = 8
ROW_TILE = 512
SEQ_TILE = 256
N_SEG = 16
HALO = 16
EXPERT_ROWS = 1024
EXPERT_SUB_ROWS = 128
EXPERT_RING = 3
ATTN_ROWS = 1024
ATTN_SUB_ROWS = 128
MIB = 1024 * 1024


def _cparams(n_axes, vmem_mib):
    return pltpu.CompilerParams(dimension_semantics=("arbitrary",) * n_axes,
                                vmem_limit_bytes=vmem_mib * MIB)


def _sigmoid(v):
    return 1.0 / (1.0 + jnp.exp(-v))


def _rms(v, g):
    return v * lax.rsqrt(jnp.mean(v * v, axis=-1, keepdims=True) + EPS) * g


def _normmod(v, g, shift, scale):
    return _rms(v, g) * (1.0 + scale) + shift


def _split_bf16(v):
    hi = v.astype(BF16)
    lo = (v - hi.astype(F32)).astype(BF16)
    return hi, lo


def _pack_halves(v):
    half = v.shape[1] // 2
    word = pltpu.pack_elementwise([v[:, :half], v[:, half:]], packed_dtype=BF16)
    return lax.bitcast_convert_type(word, jnp.int32)


def _unpack_halves(w):
    u = lax.bitcast_convert_type(w, jnp.uint32)
    lo = lax.bitcast_convert_type(u << 16, F32)
    hi = lax.bitcast_convert_type(u & jnp.uint32(0xFFFF0000), F32)
    return lo, hi


SC_SPLIT = 2


def _store_pieces(ref, words):
    q = words.shape[1] // SC_SPLIT
    for j in range(SC_SPLIT):
        ref[j] = words[:, j * q:(j + 1) * q]


def _load_pieces(ref):
    return jnp.concatenate([ref[j] for j in range(SC_SPLIT)], axis=-1)


def _dot(a, b):
    return jnp.dot(a, b, preferred_element_type=F32)


def _dot_nt(a, b):
    return lax.dot_general(a, b, (((1,), (1,)), ((), ())), preferred_element_type=F32)


def _mods_kernel(c_ref, w_ref, b_ref, o_ref):
    cv = c_ref[...]
    o_ref[...] = _dot(cv * _sigmoid(cv), w_ref[...]) + b_ref[...]


def _mods_call(cond, ada_w, ada_b):
    depth, d, n = ada_w.shape
    tn = n // 4
    return pl.pallas_call(
        _mods_kernel,
        grid=(depth, n // tn),
        in_specs=[pl.BlockSpec((N_SEG, d), lambda l, j: (0, 0)),
                  pl.BlockSpec((None, d, tn), lambda l, j: (l, 0, j)),
                  pl.BlockSpec((None, 1, tn), lambda l, j: (l, 0, j))],
        out_specs=pl.BlockSpec((None, N_SEG, tn), lambda l, j: (l, 0, j)),
        out_shape=jax.ShapeDtypeStruct((depth, N_SEG, n), F32),
        compiler_params=_cparams(2, 40),
        name="adaln_mods",
    )(cond, ada_w, ada_b.reshape(depth, 1, n))


def _two_source_specs(tm, d, n_head_tiles):
    return (pl.BlockSpec((tm, d), lambda i: (jnp.minimum(i, n_head_tiles - 1), 0)),
            pl.BlockSpec((tm, d), lambda i: (jnp.maximum(i - n_head_tiles, 0), 0)))


def _pick_rows(head_ref, tail_ref, n_head_tiles):
    return jnp.where(pl.program_id(0) < n_head_tiles, head_ref[...], tail_ref[...])


def _inproj_kernel(xh_ref, xt_ref, g_ref, sh_ref, sc_ref, w_ref, cs_ref, a_ref, ab_ref, *, cc, gw, n_head):
    x = _pick_rows(xh_ref, xt_ref, n_head)
    h = _normmod(x, g_ref[...], sh_ref[...], sc_ref[...]).astype(BF16)
    u = _dot(h, w_ref[...])
    a_ref[...] = u[:, :cc] * _sigmoid(u[:, cc:2 * cc])
    f = u[:, 2 * cc:].astype(BF16)
    n_g = f.shape[1] // gw
    parts = [_dot(f[:, g * gw:(g + 1) * gw], cs_ref[...]) for g in range(n_g)]
    cos_part = [p[:, :gw] for p in parts]
    sin_part = [p[:, gw:] for p in parts]
    ab_ref[...] = jnp.concatenate(cos_part + sin_part, axis=-1).astype(BF16)


def _seg_map(tm, seq, n_batch):
    return lambda i: (jnp.minimum(i * tm // seq, n_batch), 0, 0)


def _inproj_call(x_head, x_tail, g, shift, scale, w_in, cs, *, cc, gw, tm, seq, n_batch):
    d = x_head.shape[1]
    rows = x_head.shape[0] + x_tail.shape[0]
    n_head = x_head.shape[0] // tm
    n = w_in.shape[1]
    fc = n - 2 * cc
    seg = _seg_map(tm, seq, n_batch)
    return pl.pallas_call(
        functools.partial(_inproj_kernel, cc=cc, gw=gw, n_head=n_head),
        grid=(rows // tm,),
        in_specs=[*_two_source_specs(tm, d, n_head),
                  pl.BlockSpec((1, d), lambda i: (0, 0)),
                  pl.BlockSpec((None, 1, d), seg),
                  pl.BlockSpec((None, 1, d), seg),
                  pl.BlockSpec((d, n), lambda i: (0, 0)),
                  pl.BlockSpec((gw, 2 * gw), lambda i: (0, 0))],
        out_specs=[pl.BlockSpec((tm, cc), lambda i: (i, 0)),
                   pl.BlockSpec((tm, 2 * fc), lambda i: (i, 0))],
        out_shape=[jax.ShapeDtypeStruct((rows, cc), F32),
                   jax.ShapeDtypeStruct((rows, 2 * fc), BF16)],
        compiler_params=_cparams(1, 40),
        name="inproj_glu_chdft",
    )(x_head, x_tail, g, shift, scale, w_in, cs)


def _conv_kernel(prev_ref, main_ref, next_ref, w_ref, cb_ref, lg_ref, lb_ref, gm_ref, o_ref,
                 buf, cv, shifted, *, tl, width, lat_tiles, lat_tps, ctx_tps, chunk):
    i = pl.program_id(0)
    is_lat = i < lat_tiles
    tps = jnp.where(is_lat, lat_tps, ctx_tps)
    j = jnp.where(is_lat, i, i - lat_tiles) % tps
    zero = jnp.zeros((HALO, buf.shape[1]), F32)
    buf[0:HALO, :] = jnp.where(j > 0, prev_ref[...], zero)
    buf[HALO:HALO + tl, :] = main_ref[...]
    buf[HALO + tl:HALO + tl + HALO, :] = jnp.where(j < tps - 1, next_ref[...], zero)
    base = HALO - width // 2
    span = shifted.shape[1]
    for s in range(1, SUBLANES):
        shifted[s] = buf[s:s + span, :]

    def window(off, r0):
        s, m = off % SUBLANES, off - off % SUBLANES
        if s == 0:
            return buf[m + r0:m + r0 + chunk, :]
        return shifted[s, m + r0:m + r0 + chunk, :]

    for r0 in range(0, tl, chunk):
        acc = window(base, r0) * w_ref[0:1, :]
        for t in range(1, width):
            acc = acc + window(base + t, r0) * w_ref[t:t + 1, :]
        cv[r0:r0 + chunk, :] = acc
    a = cv[...] + cb_ref[...]
    gm = gm_ref[...]
    a_hi, a_lo = _split_bf16(a)
    mu = _dot(a_hi, gm) + _dot(a_lo, gm)
    dl = a - mu
    q_hi, q_lo = _split_bf16(dl * dl)
    var = _dot(q_hi, gm) + _dot(q_lo, gm)
    y = dl * lax.rsqrt(var + EPS) * lg_ref[...] + lb_ref[...]
    o_ref[...] = (y * _sigmoid(y)).astype(BF16)


def _conv_call(a_glu, conv_w, conv_b, ln_g, ln_b, gmean, *, tl, seq, ctx_len, n_batch):
    rows, ch = a_glu.shape
    width = conv_w.shape[0]
    assert width // 2 <= HALO and tl % HALO == 0
    hb = tl // HALO
    last_halo = rows // HALO - 1
    return pl.pallas_call(
        functools.partial(_conv_kernel, tl=tl, width=width, lat_tiles=n_batch * seq // tl,
                          lat_tps=seq // tl, ctx_tps=ctx_len // tl, chunk=32),
        grid=(rows // tl,),
        in_specs=[pl.BlockSpec((HALO, ch), lambda i: (jnp.maximum(i * hb - 1, 0), 0)),
                  pl.BlockSpec((tl, ch), lambda i: (i, 0)),
                  pl.BlockSpec((HALO, ch), lambda i: (jnp.minimum((i + 1) * hb, last_halo), 0)),
                  pl.BlockSpec((width, ch), lambda i: (0, 0)),
                  pl.BlockSpec((1, ch), lambda i: (0, 0)),
                  pl.BlockSpec((1, ch), lambda i: (0, 0)),
                  pl.BlockSpec((1, ch), lambda i: (0, 0)),
                  pl.BlockSpec((ch, ch), lambda i: (0, 0))],
        out_specs=pl.BlockSpec((tl, ch), lambda i: (i, 0)),
        out_shape=jax.ShapeDtypeStruct((rows, ch), BF16),
        scratch_shapes=[pltpu.VMEM((tl + 2 * HALO, ch), F32), pltpu.VMEM((tl, ch), F32),
                        pltpu.VMEM((SUBLANES, tl + 2 * HALO - SUBLANES, ch), F32)],
        compiler_params=_cparams(1, 40),
        name="dwconv_groupln_swish",
    )(a_glu, a_glu, a_glu, conv_w, conv_b, ln_g, ln_b, gmean)


EDGE_ROWS = 16


def _seqdft_kernel(c_ref, s_ref, cx_ref, sx_ref, jm_ref, a_ref, b_ref, o_ref, *, tm, n_tiles):
    i = pl.program_id(1)
    a, b = a_ref[...], b_ref[...]
    p, qv = _dot(c_ref[...], a), _dot(s_ref[...], b)
    o_ref[pl.ds(pl.multiple_of(i * tm, tm), tm), :] = (p - qv).astype(BF16)
    edge = _dot(cx_ref[...], a) + _dot(sx_ref[...], b)
    pick = lax.broadcasted_iota(jnp.int32, edge.shape, 0) == i
    edge_row = jnp.sum(jnp.where(pick, edge, 0.0), axis=0, keepdims=True)
    mirrored = _dot(jm_ref[...], (p + qv).astype(BF16))
    srow = lax.broadcasted_iota(jnp.int32, mirrored.shape, 0)
    mirrored = jnp.where(srow == 0, edge_row, mirrored)
    o_ref[pl.ds(pl.multiple_of((n_tiles - 1 - i) * tm, tm), tm), :] = mirrored.astype(BF16)


def _seqdft_call(ab, length, *, row0, n_seq, fc, tm):
    n_tiles = length // tm
    n_half = n_tiles // 2
    assert n_tiles % 2 == 0 and n_half <= EDGE_ROWS
    scale = 1.0 / math.sqrt(length)
    cmat, smat = _dft_tables(length, scale, np.arange(length // 2))
    edge_k = np.zeros((EDGE_ROWS,), np.int64)
    edge_k[:n_half] = tm * (np.arange(n_half) + 1)
    cx, sx = _dft_tables(length, scale, edge_k)
    s_idx = np.arange(tm)
    reversal = jnp.asarray((s_idx[None, :] == (tm - s_idx)[:, None]) & (s_idx[:, None] > 0), BF16)
    assert row0 % length == 0
    seq0 = row0 // length
    whole = lambda arr: pl.BlockSpec(arr.shape, lambda b, i: (0, 0))
    return pl.pallas_call(
        functools.partial(_seqdft_kernel, tm=tm, n_tiles=n_tiles),
        grid=(n_seq, n_half),
        in_specs=[pl.BlockSpec((tm, length), lambda b, i: (i, 0)),
                  pl.BlockSpec((tm, length), lambda b, i: (i, 0)),
                  whole(cx), whole(sx), whole(reversal),
                  pl.BlockSpec((length, fc), lambda b, i: (seq0 + b, 0)),
                  pl.BlockSpec((length, fc), lambda b, i: (seq0 + b, 1))],
        out_specs=pl.BlockSpec((length, fc), lambda b, i: (b, 0)),
        out_shape=jax.ShapeDtypeStruct((n_seq * length, fc), BF16),
        compiler_params=_cparams(2, 48),
        name="seq_dft",
    )(cmat, smat, cx, sx, reversal, ab, ab)


def _dft_tables(length, scale, rows):
    k = np.asarray(rows, np.int64)[:, None]
    if length <= 512:
        ang = 2.0 * np.pi * ((k * np.arange(length)[None, :]) % length) / length
        return (jnp.asarray(np.cos(ang) * scale, BF16), jnp.asarray(np.sin(ang) * scale, BF16))
    r = 64
    assert length % r == 0
    alpha = 2.0 * np.pi * ((k * np.arange(length // r)[None, :] * r) % length) / length
    beta = 2.0 * np.pi * ((k * np.arange(r)[None, :]) % length) / length
    ca, sa = jnp.asarray(np.cos(alpha), F32)[:, :, None], jnp.asarray(np.sin(alpha), F32)[:, :, None]
    cb, sb = jnp.asarray(np.cos(beta) * scale, F32)[:, None, :], jnp.asarray(np.sin(beta) * scale, F32)[:, None, :]
    cmat = (ca * cb - sa * sb).reshape(k.shape[0], length).astype(BF16)
    smat = (sa * cb + ca * sb).reshape(k.shape[0], length).astype(BF16)
    return cmat, smat


def _outproj_kernel(*refs, in_heads, n_head):
    n_in = len(in_heads)
    ins, ws = refs[:2 * n_in], refs[2 * n_in:3 * n_in]
    (xh_ref, xt_ref, g1_ref, gate_ref, g2_ref, sh_ref, sc_ref, wh_ref, wl_ref, rb_ref,
     xo_ref, tok_ref, tokp_ref, e_ref, gt_ref, pos_ref, cnt_ref, carry) = refs[3 * n_in:]
    y = None
    for j, w_ref in enumerate(ws):
        part = _dot(_pick_rows(ins[2 * j], ins[2 * j + 1], in_heads[j]), w_ref[...])
        y = part if y is None else y + part
    x1 = _pick_rows(xh_ref, xt_ref, n_head) + gate_ref[...] * _rms(y, g1_ref[...])
    xo_ref[...] = x1
    tok = _normmod(x1, g2_ref[...], sh_ref[...], sc_ref[...])
    tok_ref[...] = tok.astype(BF16)
    _store_pieces(tokp_ref, _pack_halves(tok))
    _route(tok, wh_ref, wl_ref, rb_ref, e_ref, gt_ref, pos_ref, cnt_ref, carry)


def _outproj_call(ins, ws, x_head, x_tail, g1, gate, g2, shift, scale, router_w, router_b,
                  *, rows, tm, seq, n_batch):
    d = x_head.shape[1]
    n_head = min(x_head.shape[0], rows) // tm
    n_exp = router_w.shape[1]
    w_t = router_w.T
    w_hi = w_t.astype(BF16)
    w_lo = (w_t - w_hi.astype(F32)).astype(BF16)
    seg = _seg_map(tm, seq, n_batch)
    in_heads = tuple(min(head.shape[0], rows) // tm for head, _ in ins)
    pair_specs = [s for (head, _), nh in zip(ins, in_heads) for s in _two_source_specs(tm, head.shape[1], nh)]
    full_spec = lambda a: pl.BlockSpec(a.shape, lambda i: (0, 0))
    lane_spec = pl.BlockSpec((SUBLANES, tm), lambda i: (0, i))
    tok_rows = lambda dt: jax.ShapeDtypeStruct((SUBLANES, rows), dt)
    return pl.pallas_call(
        functools.partial(_outproj_kernel, in_heads=in_heads, n_head=n_head),
        grid=(rows // tm,),
        in_specs=pair_specs + [full_spec(w) for w in ws] + [
            *_two_source_specs(tm, d, n_head), full_spec(g1), pl.BlockSpec((None, 1, d), seg), full_spec(g2),
            pl.BlockSpec((None, 1, d), seg), pl.BlockSpec((None, 1, d), seg),
            full_spec(w_hi), full_spec(w_lo), pl.BlockSpec((n_exp, 1), lambda i: (0, 0))],
        out_specs=[pl.BlockSpec((tm, d), lambda i: (i, 0)), pl.BlockSpec((tm, d), lambda i: (i, 0)),
                   pl.BlockSpec((SC_SPLIT, tm, d // 2 // SC_SPLIT), lambda i: (0, i, 0)),
                   lane_spec, pl.BlockSpec((tm, SUBLANES), lambda i: (i, 0)), lane_spec,
                   pl.BlockSpec((n_exp, LANES), lambda i: (0, 0))],
        out_shape=[jax.ShapeDtypeStruct((rows, d), F32), jax.ShapeDtypeStruct((rows, d), BF16),
                   jax.ShapeDtypeStruct((SC_SPLIT, rows, d // 2 // SC_SPLIT), jnp.int32),
                   tok_rows(jnp.int32), jax.ShapeDtypeStruct((rows, SUBLANES), F32), tok_rows(jnp.int32),
                   jax.ShapeDtypeStruct((n_exp, LANES), F32)],
        scratch_shapes=[pltpu.VMEM((n_exp, 1), F32)],
        compiler_params=_cparams(1, 48),
        name="outproj_residual_route",
    )(*[a for pair in ins for a in pair], *ws, x_head, x_tail, g1, gate, g2, shift, scale, w_hi, w_lo,
      router_b.reshape(n_exp, 1))


def _pick_max(cur, idx):
    mx = jnp.max(cur, axis=0, keepdims=True)
    first = jnp.min(jnp.where(cur == mx, idx, float(cur.shape[0])), axis=0, keepdims=True)
    return first, idx == first


def _route(tok, wh_ref, wl_ref, rb_ref, e_ref, gt_ref, pos_ref, cnt_ref, carry):
    @pl.when(pl.program_id(0) == 0)
    def _():
        carry[...] = jnp.zeros_like(carry)

    t_hi, t_lo = _split_bf16(tok)
    wh, wl = wh_ref[...], wl_ref[...]
    logits = _dot_nt(wh, t_hi) + _dot_nt(wh, t_lo) + _dot_nt(wl, t_hi)
    n_exp, tm = logits.shape
    scores = _sigmoid(logits)
    biased = scores + rb_ref[...]
    gsz = n_exp // N_GROUPS
    neg = -jnp.inf

    b3 = biased.reshape(N_GROUPS, gsz, tm)
    im = lax.broadcasted_iota(jnp.int32, b3.shape, 1).astype(F32)
    m1 = jnp.max(b3, axis=1, keepdims=True)
    i1 = jnp.min(jnp.where(b3 == m1, im, float(gsz)), axis=1, keepdims=True)
    m2 = jnp.max(jnp.where(im == i1, neg, b3), axis=1, keepdims=True)
    gscore = (m1 + m2).reshape(N_GROUPS, tm)

    ig = lax.broadcasted_iota(jnp.int32, gscore.shape, 0).astype(F32)
    gsel = jnp.zeros_like(gscore)
    cur = gscore
    for _ in range(TOPK_GROUPS):
        _, hit = _pick_max(cur, ig)
        gsel = jnp.where(hit, 1.0, gsel)
        cur = jnp.where(hit, neg, cur)
    gsel3 = jnp.broadcast_to(gsel.reshape(N_GROUPS, 1, tm), b3.shape)
    cur = jnp.where(gsel3 > 0.0, b3, neg).reshape(n_exp, tm)

    ie = lax.broadcasted_iota(jnp.int32, (n_exp, tm), 0).astype(F32)
    sel = jnp.zeros((n_exp, tm), F32)
    picks, raw = [], []
    for _ in range(TOP_K):
        first, hit = _pick_max(cur, ie)
        picks.append(first)
        raw.append(jnp.sum(jnp.where(hit, scores, 0.0), axis=0, keepdims=True))
        sel = jnp.where(hit, 1.0, sel)
        cur = jnp.where(hit, neg, cur)
    total = raw[0]
    for r in raw[1:]:
        total = total + r

    ri = lax.broadcasted_iota(jnp.int32, (tm, tm), 0)
    ci = lax.broadcasted_iota(jnp.int32, (tm, tm), 1)
    upper = jnp.where(ri < ci, 1.0, 0.0).astype(BF16)
    rank = _dot(sel.astype(BF16), upper) + carry[...]
    ranks = [jnp.sum(jnp.where(ie == p, rank, 0.0), axis=0, keepdims=True) for p in picks]
    carry[...] = carry[...] + jnp.sum(sel, axis=1, keepdims=True)

    pad = jnp.zeros((SUBLANES - TOP_K, tm), F32)
    e_ref[...] = jnp.concatenate(picks + [pad], axis=0).astype(jnp.int32)
    gates = jnp.concatenate([r / total * ROUTED_SCALE for r in raw]
                            + [jnp.zeros((LANES - TOP_K, tm), F32)], axis=0)
    gt_ref[...] = gates.T[:, :SUBLANES]
    pos_ref[...] = jnp.concatenate(ranks + [pad], axis=0).astype(jnp.int32)
    cnt_ref[...] = jnp.broadcast_to(carry[...], cnt_ref.shape)


def _expert_kernel(be_ref, nv_ref, nu_ref, xs_hbm, wg_ref, wu_ref, wd_ref, ys_ref, wgu, wdn, xbuf, sem,
                   *, ff, n_sub):
    b = pl.program_id(0)
    n_used = nu_ref[0]
    depth, _, te, q = xbuf.shape

    def row_block_copy(blk):
        slot = blk % depth
        return pltpu.make_async_copy(xs_hbm.at[:, pl.ds(blk * te, te), :], xbuf.at[slot], sem.at[slot])

    @pl.when(b == 0)
    def _():
        for j in range(depth - 1):
            @pl.when(j < n_used)
            def _():
                row_block_copy(j).start()

    @pl.when(b + depth - 1 < n_used)
    def _():
        row_block_copy(b + depth - 1).start()

    changed = jnp.logical_or(b == 0, be_ref[b] != be_ref[jnp.maximum(b - 1, 0)])

    @pl.when(changed)
    def _():
        wgu[:, :ff] = wg_ref[...].astype(BF16)
        wgu[:, ff:] = wu_ref[...].astype(BF16)
        wdn[...] = wd_ref[...].astype(BF16)

    @pl.when(b < n_used)
    def _():
        row_block_copy(b).wait()
        slot = b % depth
        n_valid = nv_ref[b]
        ts = te // n_sub
        gus = []
        for a in range(n_sub):
            r0 = a * ts
            xw = jnp.concatenate([xbuf[slot, j, r0:r0 + ts, :] for j in range(SC_SPLIT)], axis=-1)
            row = lax.broadcasted_iota(jnp.int32, xw.shape, 0) + r0
            lo, hi = _unpack_halves(jnp.where(row < n_valid, xw, 0))
            half = lo.shape[1]
            gus.append(_dot(lo.astype(BF16), wgu[:half, :]) + _dot(hi.astype(BF16), wgu[half:, :]))
        outs = []
        for gu in gus:
            gate = gu[:, :ff]
            hmid = (gate * _sigmoid(gate) * gu[:, ff:]).astype(BF16)
            outs.append(_dot(hmid, wdn[...]))
        for a, y in enumerate(outs):
            words = _pack_halves(y)
            for j in range(SC_SPLIT):
                ys_ref[j, a * ts:(a + 1) * ts, :] = words[:, j * q:(j + 1) * q]


def _expert_call(block_e, n_valid, n_used, xs, layer, w_gate, w_up, w_down):
    _, n_rows, q = xs.shape
    d = 2 * SC_SPLIT * q
    ff = w_gate.shape[3]
    te = EXPERT_ROWS
    row_map = lambda b, be, nv, nu: (0, jnp.minimum(b, nu[0] - 1), 0)
    w_map = lambda b, be, nv, nu: (layer, be[b], 0, 0)
    grid_spec = pltpu.PrefetchScalarGridSpec(
        num_scalar_prefetch=3,
        grid=(n_rows // te,),
        in_specs=[pl.BlockSpec(memory_space=pl.ANY),
                  pl.BlockSpec((None, None, d, ff), w_map),
                  pl.BlockSpec((None, None, d, ff), w_map),
                  pl.BlockSpec((None, None, ff, d), w_map)],
        out_specs=pl.BlockSpec((SC_SPLIT, te, q), row_map),
        scratch_shapes=[pltpu.VMEM((d, 2 * ff), BF16), pltpu.VMEM((ff, d), BF16),
                        pltpu.VMEM((EXPERT_RING, SC_SPLIT, te, q), jnp.int32),
                        pltpu.SemaphoreType.DMA((EXPERT_RING,))])
    return pl.pallas_call(
        functools.partial(_expert_kernel, ff=ff, n_sub=EXPERT_ROWS // EXPERT_SUB_ROWS),
        grid_spec=grid_spec,
        out_shape=jax.ShapeDtypeStruct(xs.shape, jnp.int32),
        compiler_params=_cparams(1, 40),
        name="moe_experts",
    )(block_e, n_valid, n_used, xs, w_gate, w_up, w_down)


SC_WINDOW = 128


def _sc_mesh():
    return plsc.VectorSubcoreMesh(core_axis_name="core", subcore_axis_name="subcore")


def _sc_scatter_rows(rows, dests, n_out):
    n, width = rows.shape
    n_k = len(dests)
    assert n % SC_WINDOW == 0

    @functools.partial(pl.kernel, out_type=jax.ShapeDtypeStruct((n_out, width), rows.dtype),
                       mesh=_sc_mesh(), name="sc_dispatch_rows")
    def scatter_kernel(x_hbm, *refs):
        idx_hbm, o_hbm = refs[:n_k], refs[n_k]

        def body(x_vmem, *idx_vmem):
            for iv in idx_vmem:
                pltpu.sync_copy(x_vmem, o_hbm.at[iv.at[0]])

        pltpu.emit_pipeline(
            body,
            grid=(n // SC_WINDOW,),
            in_specs=[pl.BlockSpec((SC_WINDOW, width), lambda i: (i, 0))]
            + [pl.BlockSpec((1, SC_WINDOW), lambda i: (0, i))] * n_k,
            out_specs=[],
            core_axis_name=("core", "subcore"),
            dimension_semantics=(pltpu.PARALLEL,),
        )(x_hbm, *idx_hbm)

    return scatter_kernel(rows, *dests)


def _sc_gather_rows(table, idx):
    n = idx.shape[1]
    width = table.shape[1]
    assert n % SC_WINDOW == 0

    @functools.partial(pl.kernel, out_type=jax.ShapeDtypeStruct((n, width), table.dtype),
                       mesh=_sc_mesh(), name="sc_collect_rows")
    def gather_kernel(t_hbm, i_hbm, o_hbm):
        def body(i_vmem, o_vmem):
            pltpu.sync_copy(t_hbm.at[i_vmem.at[0]], o_vmem)

        pltpu.emit_pipeline(
            body,
            grid=(n // SC_WINDOW,),
            in_specs=[pl.BlockSpec((1, SC_WINDOW), lambda i: (0, i))],
            out_specs=[pl.BlockSpec((SC_WINDOW, width), lambda i: (i, 0))],
            core_axis_name=("core", "subcore"),
            dimension_semantics=(pltpu.PARALLEL,),
        )(i_hbm, o_hbm)

    return gather_kernel(table, idx)


def _shared_kernel(tok_ref, sgu_ref, sd_ref, o_ref, *, ff):
    gu = _dot(tok_ref[...], sgu_ref[...])
    gate = gu[:, :ff]
    hmid = (gate * _sigmoid(gate) * gu[:, ff:]).astype(BF16)
    o_ref[...] = _dot(hmid, sd_ref[...]).astype(BF16)


def _shared_call(tok, sgu, sd, *, rows, tm):
    d = tok.shape[1]
    ff = sd.shape[0]
    return pl.pallas_call(
        functools.partial(_shared_kernel, ff=ff),
        grid=(rows // tm,),
        in_specs=[pl.BlockSpec((tm, d), lambda i: (i, 0)),
                  pl.BlockSpec((d, 2 * ff), lambda i: (0, 0)),
                  pl.BlockSpec((ff, d), lambda i: (0, 0))],
        out_specs=pl.BlockSpec((tm, d), lambda i: (i, 0)),
        out_shape=jax.ShapeDtypeStruct((rows, d), BF16),
        compiler_params=_cparams(1, 40),
        name="moe_shared_expert",
    )(tok, sgu, sd)


def _moe_out_kernel(shared_ref, yg_ref, gk_ref, x_ref, g_ref, gate_ref, xo_ref):
    shared = shared_ref[...].astype(F32)
    gk = gk_ref[...]
    lo_acc, hi_acc = None, None
    for k in range(TOP_K):
        lo, hi = _unpack_halves(jnp.concatenate([yg_ref[j, k] for j in range(SC_SPLIT)], axis=-1))
        w = gk[:, k:k + 1]
        lo_acc = lo * w if lo_acc is None else lo_acc + lo * w
        hi_acc = hi * w if hi_acc is None else hi_acc + hi * w
    y = jnp.concatenate([lo_acc, hi_acc], axis=-1) + shared
    xo_ref[...] = x_ref[...] + gate_ref[...] * _rms(y, g_ref[...])


def _moe_out_call(shared, yg, gk, xs, g, gate, *, rows, tm, seq, n_batch):
    d = xs.shape[1]
    seg = _seg_map(tm, seq, n_batch)
    row = lambda: pl.BlockSpec((tm, d), lambda i: (i, 0))
    return pl.pallas_call(
        _moe_out_kernel,
        grid=(rows // tm,),
        in_specs=[row(),
                  pl.BlockSpec((SC_SPLIT, TOP_K, tm, d // 2 // SC_SPLIT), lambda i: (0, 0, i, 0)),
                  pl.BlockSpec((tm, SUBLANES), lambda i: (i, 0)),
                  row(),
                  pl.BlockSpec((1, d), lambda i: (0, 0)),
                  pl.BlockSpec((None, 1, d), seg)],
        out_specs=row(),
        out_shape=jax.ShapeDtypeStruct((rows, d), F32),
        compiler_params=_cparams(1, 48),
        name="moe_combine",
    )(shared, yg, gk, xs, g, gate)


def _dest_kernel(e_ref, pos_ref, ps_ref, o_ref, *, n_rows):
    e = e_ref[...].astype(F32)
    ps = ps_ref[...]
    n_exp, tm = ps.shape[0], e.shape[1]
    ids = lax.broadcasted_iota(jnp.int32, (n_exp, tm), 0).astype(F32)
    rows = [jnp.sum(jnp.where(ids == e[k:k + 1, :], ps, 0.0), axis=0, keepdims=True) for k in range(SUBLANES)]
    dest = jnp.concatenate(rows, axis=0).astype(jnp.int32) + pos_ref[...]
    for j in range(SC_SPLIT):
        o_ref[j] = dest + j * n_rows


def _dest_call(e_tk, pos_tk, pad_starts, *, n_rows, tm):
    rows = e_tk.shape[1]
    n_exp = pad_starts.shape[0]
    lane_spec = pl.BlockSpec((SUBLANES, tm), lambda i: (0, i))
    return pl.pallas_call(
        functools.partial(_dest_kernel, n_rows=n_rows),
        grid=(rows // tm,),
        in_specs=[lane_spec, lane_spec, pl.BlockSpec((n_exp, 1), lambda i: (0, 0))],
        out_specs=pl.BlockSpec((SC_SPLIT, SUBLANES, tm), lambda i: (0, 0, i)),
        out_shape=jax.ShapeDtypeStruct((SC_SPLIT, SUBLANES, rows), jnp.int32),
        compiler_params=_cparams(1, 32),
        name="moe_dest_rows",
    )(e_tk, pos_tk, pad_starts.astype(F32).reshape(n_exp, 1))


def _moe_layer(xs, tok, tokp, routing, g_out, gate, layer, w_gate, w_up, w_down, sh_gate, sh_up, sh_down,
               *, rows, tm, seq, n_batch):
    e_tk, g_tk, pos_tk, counts = routing
    n_exp = counts.shape[0]
    te = EXPERT_ROWS
    cnt = counts[:, 0].astype(jnp.int32)
    padded = (cnt + te - 1) // te * te
    pad_ends = jnp.cumsum(padded)
    pad_starts = pad_ends - padded
    expert_ids = jnp.arange(n_exp, dtype=jnp.int32)
    n_blocks = -(-(rows * TOP_K + n_exp * (te - 1)) // te)
    n_used = pad_ends[-1] // te
    blk = jnp.arange(n_blocks, dtype=jnp.int32)
    last = jnp.minimum(blk, n_used - 1) * te
    owner = (last[:, None] >= pad_ends[None, :]).astype(jnp.int32)
    block_e = jnp.minimum(jnp.sum(owner, axis=1), n_exp - 1)
    own_hot = block_e[:, None] == expert_ids[None, :]
    n_valid = jnp.sum(jnp.where(own_hot, (pad_starts + cnt)[None, :], 0), axis=1) - last
    n_valid = jnp.clip(n_valid, 0, te)
    q = tokp.shape[2]
    n_rows = n_blocks * te
    piece = _dest_call(e_tk, pos_tk, pad_starts, n_rows=n_rows, tm=tm)[:, :TOP_K, :]
    xg = _sc_scatter_rows(tokp.reshape(SC_SPLIT * rows, q),
                          [piece[:, k, :].reshape(1, SC_SPLIT * rows) for k in range(TOP_K)],
                          SC_SPLIT * n_rows).reshape(SC_SPLIT, n_rows, q)
    sgu = jnp.concatenate([sh_gate, sh_up], axis=1).astype(BF16)
    shared = _shared_call(tok, sgu, sh_down.astype(BF16), rows=rows, tm=tm)
    ys = _expert_call(block_e, n_valid, n_used.reshape(1).astype(jnp.int32), xg, layer, w_gate, w_up, w_down)
    yg = _sc_gather_rows(ys.reshape(SC_SPLIT * n_rows, q),
                         piece.reshape(1, -1)).reshape(SC_SPLIT, TOP_K, rows, q)
    return _moe_out_call(shared, yg, g_tk, xs, g_out, gate, rows=rows, tm=tm, seq=seq, n_batch=n_batch)


def _rope(t, cos, sin):
    half = DA_QK_DIM // 2
    up = pltpu.roll(t, shift=LANES - half, axis=1)
    dn = pltpu.roll(t, shift=half, axis=1)
    lane = lax.broadcasted_iota(jnp.int32, t.shape, 1) % DA_QK_DIM
    return t * cos + jnp.where(lane < half, -up, dn) * sin


def _qkv_kernel(x_ref, g_ref, sh_ref, sc_ref, w_ref, cos_ref, sin_ref, q_ref, k_ref, v_ref, *, qkw):
    h = _normmod(x_ref[...], g_ref[...], sh_ref[...], sc_ref[...]).astype(BF16)
    qkv = _dot(h, w_ref[...])
    cos, sin = cos_ref[...], sin_ref[...]
    q_scale = DA_QK_DIM ** -0.5 * math.log2(math.e)
    for hb in range(qkw // LANES):
        lo, hi = hb * LANES, (hb + 1) * LANES
        q_ref[:, lo:hi] = (_rope(qkv[:, lo:hi], cos, sin) * q_scale).astype(BF16)
        k_ref[:, lo:hi] = _rope(qkv[:, qkw + lo:qkw + hi], cos, sin).astype(BF16)
    v_ref[...] = qkv[:, 2 * qkw:].astype(BF16)


def _qkv_call(xs, g, shift, scale, w_qkv, cos_t, sin_t, *, qkw, tm, seq, ctx_len, n_batch):
    rows, d = xs.shape
    n = w_qkv.shape[1]
    vw = n - 2 * qkw
    lat_tiles, lat_tps, ctx_tps = n_batch * seq // tm, seq // tm, ctx_len // tm
    seg = _seg_map(tm, seq, n_batch)

    def kv_map(i):
        c = i - lat_tiles
        is_lat = i < lat_tiles
        return (jnp.where(is_lat, i // lat_tps, c // ctx_tps),
                jnp.where(is_lat, ctx_tps + i % lat_tps, c % ctx_tps), 0)

    rope_map = lambda i: (jnp.where(i < lat_tiles, i % lat_tps, lat_tps), 0)
    return pl.pallas_call(
        functools.partial(_qkv_kernel, qkw=qkw),
        grid=(rows // tm,),
        in_specs=[pl.BlockSpec((tm, d), lambda i: (i, 0)),
                  pl.BlockSpec((1, d), lambda i: (0, 0)),
                  pl.BlockSpec((None, 1, d), seg),
                  pl.BlockSpec((None, 1, d), seg),
                  pl.BlockSpec((d, n), lambda i: (0, 0)),
                  pl.BlockSpec((tm, LANES), rope_map),
                  pl.BlockSpec((tm, LANES), rope_map)],
        out_specs=[pl.BlockSpec((tm, qkw), lambda i: (i, 0)),
                   pl.BlockSpec((None, tm, qkw), kv_map),
                   pl.BlockSpec((None, tm, vw), kv_map)],
        out_shape=[jax.ShapeDtypeStruct((rows, qkw), BF16),
                   jax.ShapeDtypeStruct((n_batch, ctx_len + seq, qkw), BF16),
                   jax.ShapeDtypeStruct((n_batch, ctx_len + seq, vw), BF16)],
        compiler_params=_cparams(1, 48),
        name="qkv_rope",
    )(xs, g, shift, scale, w_qkv, cos_t, sin_t)


def _rope_tables(seq, tm):
    pos = np.arange(seq)
    n_freq = DA_QK_DIM // 4
    inv = np.power(ROPE_BASE, -np.arange(n_freq, dtype=np.float32) / n_freq).astype(np.float32)
    row = (pos // GRID_W).astype(np.float32)[:, None] * inv
    col = (pos % GRID_W).astype(np.float32)[:, None] * inv
    ang = np.concatenate([row, col], axis=-1).astype(np.float32)
    ang = np.tile(ang, (1, LANES // ang.shape[1]))
    cos = np.concatenate([np.cos(ang), np.ones((tm, LANES))], axis=0)
    sin = np.concatenate([np.sin(ang), np.zeros((tm, LANES))], axis=0)
    return jnp.asarray(cos, F32), jnp.asarray(sin, F32)


def _attn_kernel(lp_ref, q_ref, k_ref, v_ref, sg_ref, o_ref, vext, *, n_sub, lam_init):
    @pl.when(pl.program_id(2) == 0)
    def _():
        vext[:, :LANES] = v_ref[...]
        vext[:, LANES:] = jnp.ones((vext.shape[0], LANES), BF16)

    lp = lp_ref[...]
    lam = (jnp.exp(jnp.sum(lp[0:1] * lp[1:2], axis=1, keepdims=True))
           - jnp.exp(jnp.sum(lp[2:3] * lp[3:4], axis=1, keepdims=True)) + lam_init)
    ts = q_ref.shape[0] // n_sub
    lane = lax.broadcasted_iota(jnp.int32, (ts, LANES), 1)
    zero = jnp.zeros((ts, LANES), BF16)
    k = k_ref[...]
    scores = []
    for a in range(n_sub):
        q = q_ref[a * ts:(a + 1) * ts, :]
        qq = jnp.concatenate([jnp.where(lane < DA_QK_DIM, q, zero),
                              jnp.where(lane >= DA_QK_DIM, q, zero)], axis=0)
        scores.append(_dot_nt(qq, k))
    for a in range(n_sub):
        s = scores[a]
        p = jnp.exp2(s - jnp.max(s, axis=-1, keepdims=True)).astype(BF16)
        oe = _dot(p, vext[...])
        on = oe[:, :LANES] / oe[:, LANES:LANES + 1]
        o = on[:ts] - lam * on[ts:]
        o_ref[a * ts:(a + 1) * ts, :] = (_rms(o, sg_ref[...]) * (1.0 - lam_init)).astype(BF16)


def _attn_call(lam_p, q, k_all, v_all, subln_g, *, tq, seq, n_batch, lam_init):
    n_heads = q.shape[1] // LANES
    lk = k_all.shape[1]
    qt = seq // tq
    return pl.pallas_call(
        functools.partial(_attn_kernel, n_sub=tq // ATTN_SUB_ROWS, lam_init=lam_init),
        grid=(n_batch, n_heads, qt),
        in_specs=[pl.BlockSpec(lam_p.shape, lambda b, h, i: (0, 0)),
                  pl.BlockSpec((tq, LANES), lambda b, h, i: (b * qt + i, h)),
                  pl.BlockSpec((None, lk, LANES), lambda b, h, i: (b, 0, h)),
                  pl.BlockSpec((None, lk, LANES), lambda b, h, i: (b, 0, h)),
                  pl.BlockSpec((1, LANES), lambda b, h, i: (0, 0))],
        out_specs=pl.BlockSpec((tq, LANES), lambda b, h, i: (b * qt + i, h)),
        out_shape=jax.ShapeDtypeStruct((n_batch * seq, n_heads * LANES), BF16),
        scratch_shapes=[pltpu.VMEM((lk, 2 * LANES), BF16)],
        compiler_params=_cparams(3, 48),
        name="diff_attention",
    )(lam_p, q, k_all, v_all, subln_g)


def kernel(x, c, ctx, c_ctx, ada_w, ada_b, norm_g, cf_w_in, cf_conv_w, cf_conv_b, cf_ln_g, cf_ln_b,
           cf_w_out, da_w_qkv, da_lambda, da_subln_g, da_w_out, moe_router_w, moe_router_b,
           moe_w_gate, moe_w_up, moe_w_down, moe_sh_gate, moe_sh_up, moe_sh_down):
    n_batch, seq, d = x.shape
    ctx_len = ctx.shape[1]
    depth = ada_w.shape[0]
    assert depth == 2 and n_batch + 1 <= N_SEG
    assert DA_QK_DIM * 2 == LANES and da_subln_g.shape[1] == LANES
    rows_lat, rows_ctx = n_batch * seq, n_batch * ctx_len
    tm = ROW_TILE
    tq = SEQ_TILE
    assert seq % tm == 0 and rows_ctx % tm == 0 and seq % tq == 0 and ctx_len % tq == 0

    x_lat, x_ctx = x.reshape(rows_lat, d), ctx.reshape(rows_ctx, d)
    cond = jnp.concatenate([c, c_ctx[None, :], jnp.zeros((N_SEG - n_batch - 1, d), F32)], axis=0)
    mods = _mods_call(cond, ada_w, ada_b)
    mod = lambda layer, k: mods[layer, :, k * d:(k + 1) * d][:, None, :]
    gain = lambda layer, k: norm_g[layer, k][None, :]
    common = dict(seq=seq, n_batch=n_batch)

    cc = cf_conv_w.shape[-1]
    fc = cf_w_in.shape[2] - 2 * cc
    gw = fc // FOURIER_GROUPS
    ch_ang = 2.0 * np.pi * (np.outer(np.arange(gw), np.arange(gw)) % gw) / gw
    cs = jnp.asarray(np.concatenate([np.cos(ch_ang), np.sin(ch_ang)], axis=1) / math.sqrt(gw), BF16)
    a_glu, ab = _inproj_call(x_lat, x_ctx, gain(0, 0), mod(0, 0), mod(0, 1), cf_w_in[0].astype(BF16), cs,
                             cc=cc, gw=gw, tm=tm, **common)
    gsz = cc // CONV_GROUPS
    gid = np.arange(cc) // gsz
    gmean = jnp.asarray((gid[:, None] == gid[None, :]) / gsz, BF16)
    a_act = _conv_call(a_glu, cf_conv_w[0][:, 0, :], cf_conv_b[0][None, :], cf_ln_g[0][None, :],
                       cf_ln_b[0][None, :], gmean, tl=tq, seq=seq, ctx_len=ctx_len, n_batch=n_batch)
    fr_lat = _seqdft_call(ab, seq, row0=0, n_seq=n_batch, fc=fc, tm=min(seq // 2, 512))
    fr_ctx = _seqdft_call(ab, ctx_len, row0=rows_lat, n_seq=n_batch, fc=fc, tm=min(ctx_len // 2, 256))
    w_out = cf_w_out[0].astype(BF16)
    xs, tok, tokp, *routing = _outproj_call(
        [(a_act, a_act), (fr_lat, fr_ctx)], [w_out[:cc], w_out[cc:]], x_lat, x_ctx, gain(0, 1), mod(0, 2), gain(0, 2), mod(0, 3),
        mod(0, 4), moe_router_w[0], moe_router_b[0], rows=rows_lat + rows_ctx, tm=tm, **common)
    xs = _moe_layer(xs, tok, tokp, routing, gain(0, 3), mod(0, 5), 0, moe_w_gate, moe_w_up, moe_w_down,
                    moe_sh_gate[0], moe_sh_up[0], moe_sh_down[0], rows=rows_lat + rows_ctx, tm=tm, **common)

    qkw = DA_HEADS * 2 * DA_QK_DIM
    lam_init = 0.8 - 0.6 * math.exp(-0.3 * 1)
    cos_t, sin_t = _rope_tables(seq, tq)
    q, k_all, v_all = _qkv_call(xs, gain(1, 0), mod(1, 0), mod(1, 1), da_w_qkv[0].astype(BF16),
                                cos_t, sin_t, qkw=qkw, tm=tq, ctx_len=ctx_len, **common)
    o = _attn_call(da_lambda[0], q, k_all, v_all, da_subln_g[0][None, :], tq=math.gcd(seq, ATTN_ROWS), seq=seq,
                   n_batch=n_batch, lam_init=lam_init)
    xs, tok, tokp, *routing = _outproj_call(
        [(o, o)], [da_w_out[0].astype(BF16)], xs, xs, gain(1, 1), mod(1, 2), gain(1, 2), mod(1, 3), mod(1, 4),
        moe_router_w[1], moe_router_b[1], rows=rows_lat, tm=tm, **common)
    xs = _moe_layer(xs, tok, tokp, routing, gain(1, 3), mod(1, 5), 1, moe_w_gate, moe_w_up, moe_w_down,
                    moe_sh_gate[1], moe_sh_up[1], moe_sh_down[1], rows=rows_lat, tm=tm, **common)
    return xs.reshape(n_batch, seq, d)
```

```python
import functools
import math

import numpy as np
import jax
import jax.numpy as jnp
from jax import lax
from jax.experimental import pallas as pl
from jax.experimental.pallas import tpu as pltpu
from jax.experimental.pallas import tpu_sc as plsc

F32 = jnp.float32
BF16 = jnp.bfloat16

EPS = 1e-6
GRID_W = 64
CONV_GROUPS = 8
FOURIER_GROUPS = 4
DA_HEADS = 8
DA_QK_DIM = 64
ROPE_BASE = 10000.0
N_GROUPS = 8
TOPK_GROUPS = 4
TOP_K = 6
ROUTED_SCALE = 2.5

LANES = 128
SUBLANES = 8
ROW_TILE = 512
SEQ_TILE = 256
N_SEG = 16
HALO = 16
EXPERT_ROWS = 1024
EXPERT_SUB_ROWS = 128
EXPERT_RING = 4
ATTN_ROWS = 1024
ATTN_SUB_ROWS = 128
MIB = 1024 * 1024


def _cparams(n_axes, vmem_mib):
    return pltpu.CompilerParams(dimension_semantics=("arbitrary",) * n_axes,
                                vmem_limit_bytes=vmem_mib * MIB)


def _sigmoid(v):
    return 1.0 / (1.0 + jnp.exp(-v))


def _rms(v, g):
    return v * lax.rsqrt(jnp.mean(v * v, axis=-1, keepdims=True) + EPS) * g


def _normmod(v, g, shift, scale):
    return _rms(v, g) * (1.0 + scale) + shift


def _split_bf16(v):
    hi = v.astype(BF16)
    lo = (v - hi.astype(F32)).astype(BF16)
    return hi, lo


def _pack_halves(v):
    half = v.shape[1] // 2
    word = pltpu.pack_elementwise([v[:, :half], v[:, half:]], packed_dtype=BF16)
    return lax.bitcast_convert_type(word, jnp.int32)


def _unpack_halves(w):
    u = lax.bitcast_convert_type(w, jnp.uint32)
    lo = lax.bitcast_convert_type(u << 16, F32)
    hi = lax.bitcast_convert_type(u & jnp.uint32(0xFFFF0000), F32)
    return lo, hi


SC_SPLIT = 2


def _store_pieces(ref, words):
    q = words.shape[1] // SC_SPLIT
    for j in range(SC_SPLIT):
        ref[j] = words[:, j * q:(j + 1) * q]


def _load_pieces(ref):
    return jnp.concatenate([ref[j] for j in range(SC_SPLIT)], axis=-1)


def _dot(a, b):
    return jnp.dot(a, b, preferred_element_type=F32)


def _dot_nt(a, b):
    return lax.dot_general(a, b, (((1,), (1,)), ((), ())), preferred_element_type=F32)


def _mods_kernel(c_ref, w_ref, b_ref, o_ref):
    cv = c_ref[...]
    o_ref[...] = _dot(cv * _sigmoid(cv), w_ref[...]) + b_ref[...]


def _mods_call(cond, ada_w, ada_b):
    depth, d, n = ada_w.shape
    tn = n // 4
    return pl.pallas_call(
        _mods_kernel,
        grid=(depth, n // tn),
        in_specs=[pl.BlockSpec((N_SEG, d), lambda l, j: (0, 0)),
                  pl.BlockSpec((None, d, tn), lambda l, j: (l, 0, j)),
                  pl.BlockSpec((None, 1, tn), lambda l, j: (l, 0, j))],
        out_specs=pl.BlockSpec((None, N_SEG, tn), lambda l, j: (l, 0, j)),
        out_shape=jax.ShapeDtypeStruct((depth, N_SEG, n), F32),
        compiler_params=_cparams(2, 40),
        name="adaln_mods",
    )(cond, ada_w, ada_b.reshape(depth, 1, n))


def _two_source_specs(tm, d, n_head_tiles):
    return (pl.BlockSpec((tm, d), lambda i: (jnp.minimum(i, n_head_tiles - 1), 0)),
            pl.BlockSpec((tm, d), lambda i: (jnp.maximum(i - n_head_tiles, 0), 0)))


def _pick_rows(head_ref, tail_ref, n_head_tiles):
    return jnp.where(pl.program_id(0) < n_head_tiles, head_ref[...], tail_ref[...])


def _inproj_kernel(xh_ref, xt_ref, g_ref, sh_ref, sc_ref, w_ref, cs_ref, a_ref, ab_ref, *, cc, gw, n_head):
    x = _pick_rows(xh_ref, xt_ref, n_head)
    h = _normmod(x, g_ref[...], sh_ref[...], sc_ref[...]).astype(BF16)
    u = _dot(h, w_ref[...])
    a_ref[...] = u[:, :cc] * _sigmoid(u[:, cc:2 * cc])
    f = u[:, 2 * cc:].astype(BF16)
    n_g = f.shape[1] // gw
    parts = [_dot(f[:, g * gw:(g + 1) * gw], cs_ref[...]) for g in range(n_g)]
    cos_part = [p[:, :gw] for p in parts]
    sin_part = [p[:, gw:] for p in parts]
    ab_ref[...] = jnp.concatenate(cos_part + sin_part, axis=-1).astype(BF16)


def _seg_map(tm, seq, n_batch):
    return lambda i: (jnp.minimum(i * tm // seq, n_batch), 0, 0)


def _inproj_call(x_head, x_tail, g, shift, scale, w_in, cs, *, cc, gw, tm, seq, n_batch):
    d = x_head.shape[1]
    rows = x_head.shape[0] + x_tail.shape[0]
    n_head = x_head.shape[0] // tm
    n = w_in.shape[1]
    fc = n - 2 * cc
    seg = _seg_map(tm, seq, n_batch)
    return pl.pallas_call(
        functools.partial(_inproj_kernel, cc=cc, gw=gw, n_head=n_head),
        grid=(rows // tm,),
        in_specs=[*_two_source_specs(tm, d, n_head),
                  pl.BlockSpec((1, d), lambda i: (0, 0)),
                  pl.BlockSpec((None, 1, d), seg),
                  pl.BlockSpec((None, 1, d), seg),
                  pl.BlockSpec((d, n), lambda i: (0, 0)),
                  pl.BlockSpec((gw, 2 * gw), lambda i: (0, 0))],
        out_specs=[pl.BlockSpec((tm, cc), lambda i: (i, 0)),
                   pl.BlockSpec((tm, 2 * fc), lambda i: (i, 0))],
        out_shape=[jax.ShapeDtypeStruct((rows, cc), F32),
                   jax.ShapeDtypeStruct((rows, 2 * fc), BF16)],
        compiler_params=_cparams(1, 40),
        name="inproj_glu_chdft",
    )(x_head, x_tail, g, shift, scale, w_in, cs)


def _conv_kernel(prev_ref, main_ref, next_ref, w_ref, cb_ref, lg_ref, lb_ref, gm_ref, o_ref,
                 buf, cv, shifted, *, tl, width, lat_tiles, lat_tps, ctx_tps, chunk):
    i = pl.program_id(0)
    is_lat = i < lat_tiles
    tps = jnp.where(is_lat, lat_tps, ctx_tps)
    j = jnp.where(is_lat, i, i - lat_tiles) % tps
    zero = jnp.zeros((HALO, buf.shape[1]), F32)
    buf[0:HALO, :] = jnp.where(j > 0, prev_ref[...], zero)
    buf[HALO:HALO + tl, :] = main_ref[...]
    buf[HALO + tl:HALO + tl + HALO, :] = jnp.where(j < tps - 1, next_ref[...], zero)
    base = HALO - width // 2
    span = shifted.shape[1]
    for s in range(1, SUBLANES):
        shifted[s] = buf[s:s + span, :]

    def window(off, r0):
        s, m = off % SUBLANES, off - off % SUBLANES
        if s == 0:
            return buf[m + r0:m + r0 + chunk, :]
        return shifted[s, m + r0:m + r0 + chunk, :]

    for r0 in range(0, tl, chunk):
        acc = window(base, r0) * w_ref[0:1, :]
        for t in range(1, width):
            acc = acc + window(base + t, r0) * w_ref[t:t + 1, :]
        cv[r0:r0 + chunk, :] = acc
    a = cv[...] + cb_ref[...]
    gm = gm_ref[...]
    a_hi, a_lo = _split_bf16(a)
    mu = _dot(a_hi, gm) + _dot(a_lo, gm)
    dl = a - mu
    q_hi, q_lo = _split_bf16(dl * dl)
    var = _dot(q_hi, gm) + _dot(q_lo, gm)
    y = dl * lax.rsqrt(var + EPS) * lg_ref[...] + lb_ref[...]
    o_ref[...] = (y * _sigmoid(y)).astype(BF16)


def _conv_call(a_glu, conv_w, conv_b, ln_g, ln_b, gmean, *, tl, seq, ctx_len, n_batch):
    rows, ch = a_glu.shape
    width = conv_w.shape[0]
    assert width // 2 <= HALO and tl % HALO == 0
    hb = tl // HALO
    last_halo = rows // HALO - 1
    return pl.pallas_call(
        functools.partial(_conv_kernel, tl=tl, width=width, lat_tiles=n_batch * seq // tl,
                          lat_tps=seq // tl, ctx_tps=ctx_len // tl, chunk=32),
        grid=(rows // tl,),
        in_specs=[pl.BlockSpec((HALO, ch), lambda i: (jnp.maximum(i * hb - 1, 0), 0)),
                  pl.BlockSpec((tl, ch), lambda i: (i, 0)),
                  pl.BlockSpec((HALO, ch), lambda i: (jnp.minimum((i + 1) * hb, last_halo), 0)),
                  pl.BlockSpec((width, ch), lambda i: (0, 0)),
                  pl.BlockSpec((1, ch), lambda i: (0, 0)),
                  pl.BlockSpec((1, ch), lambda i: (0, 0)),
                  pl.BlockSpec((1, ch), lambda i: (0, 0)),
                  pl.BlockSpec((ch, ch), lambda i: (0, 0))],
        out_specs=pl.BlockSpec((tl, ch), lambda i: (i, 0)),
        out_shape=jax.ShapeDtypeStruct((rows, ch), BF16),
        scratch_shapes=[pltpu.VMEM((tl + 2 * HALO, ch), F32), pltpu.VMEM((tl, ch), F32),
                        pltpu.VMEM((SUBLANES, tl + 2 * HALO - SUBLANES, ch), F32)],
        compiler_params=_cparams(1, 40),
        name="dwconv_groupln_swish",
    )(a_glu, a_glu, a_glu, conv_w, conv_b, ln_g, ln_b, gmean)


EDGE_ROWS = 16


def _seqdft_kernel(c_ref, s_ref, cx_ref, sx_ref, jm_ref, a_ref, b_ref, o_ref, *, tm, n_tiles):
    i = pl.program_id(1)
    a, b = a_ref[...], b_ref[...]
    p, qv = _dot(c_ref[...], a), _dot(s_ref[...], b)
    o_ref[pl.ds(pl.multiple_of(i * tm, tm), tm), :] = (p - qv).astype(BF16)
    edge = _dot(cx_ref[...], a) + _dot(sx_ref[...], b)
    pick = lax.broadcasted_iota(jnp.int32, edge.shape, 0) == i
    edge_row = jnp.sum(jnp.where(pick, edge, 0.0), axis=0, keepdims=True)
    mirrored = _dot(jm_ref[...], (p + qv).astype(BF16))
    srow = lax.broadcasted_iota(jnp.int32, mirrored.shape, 0)
    mirrored = jnp.where(srow == 0, edge_row, mirrored)
    o_ref[pl.ds(pl.multiple_of((n_tiles - 1 - i) * tm, tm), tm), :] = mirrored.astype(BF16)


def _seqdft_call(ab, length, *, row0, n_seq, fc, tm):
    n_tiles = length // tm
    n_half = n_tiles // 2
    assert n_tiles % 2 == 0 and n_half <= EDGE_ROWS
    scale = 1.0 / math.sqrt(length)
    cmat, smat = _dft_tables(length, scale, np.arange(length // 2))
    edge_k = np.zeros((EDGE_ROWS,), np.int64)
    edge_k[:n_half] = tm * (np.arange(n_half) + 1)
    cx, sx = _dft_tables(length, scale, edge_k)
    s_idx = np.arange(tm)
    reversal = jnp.asarray((s_idx[None, :] == (tm - s_idx)[:, None]) & (s_idx[:, None] > 0), BF16)
    assert row0 % length == 0
    seq0 = row0 // length
    whole = lambda arr: pl.BlockSpec(arr.shape, lambda b, i: (0, 0))
    return pl.pallas_call(
        functools.partial(_seqdft_kernel, tm=tm, n_tiles=n_tiles),
        grid=(n_seq, n_half),
        in_specs=[pl.BlockSpec((tm, length), lambda b, i: (i, 0)),
                  pl.BlockSpec((tm, length), lambda b, i: (i, 0)),
                  whole(cx), whole(sx), whole(reversal),
                  pl.BlockSpec((length, fc), lambda b, i: (seq0 + b, 0)),
                  pl.BlockSpec((length, fc), lambda b, i: (seq0 + b, 1))],
        out_specs=pl.BlockSpec((length, fc), lambda b, i: (b, 0)),
        out_shape=jax.ShapeDtypeStruct((n_seq * length, fc), BF16),
        compiler_params=_cparams(2, 48),
        name="seq_dft",
    )(cmat, smat, cx, sx, reversal, ab, ab)


def _dft_tables(length, scale, rows):
    k = np.asarray(rows, np.int64)[:, None]
    if length <= 512:
        ang = 2.0 * np.pi * ((k * np.arange(length)[None, :]) % length) / length
        return (jnp.asarray(np.cos(ang) * scale, BF16), jnp.asarray(np.sin(ang) * scale, BF16))
    r = 64
    assert length % r == 0
    alpha = 2.0 * np.pi * ((k * np.arange(length // r)[None, :] * r) % length) / length
    beta = 2.0 * np.pi * ((k * np.arange(r)[None, :]) % length) / length
    ca, sa = jnp.asarray(np.cos(alpha), F32)[:, :, None], jnp.asarray(np.sin(alpha), F32)[:, :, None]
    cb, sb = jnp.asarray(np.cos(beta) * scale, F32)[:, None, :], jnp.asarray(np.sin(beta) * scale, F32)[:, None, :]
    cmat = (ca * cb - sa * sb).reshape(k.shape[0], length).astype(BF16)
    smat = (sa * cb + ca * sb).reshape(k.shape[0], length).astype(BF16)
    return cmat, smat


def _outproj_kernel(*refs, in_heads, n_head):
    n_in = len(in_heads)
    ins, ws = refs[:2 * n_in], refs[2 * n_in:3 * n_in]
    (xh_ref, xt_ref, g1_ref, gate_ref, g2_ref, sh_ref, sc_ref, wh_ref, wl_ref, rb_ref,
     xo_ref, tok_ref, tokp_ref, e_ref, gt_ref, pos_ref, cnt_ref, carry) = refs[3 * n_in:]
    y = None
    for j, w_ref in enumerate(ws):
        part = _dot(_pick_rows(ins[2 * j], ins[2 * j + 1], in_heads[j]), w_ref[...])
        y = part if y is None else y + part
    x1 = _pick_rows(xh_ref, xt_ref, n_head) + gate_ref[...] * _rms(y, g1_ref[...])
    xo_ref[...] = x1
    tok = _normmod(x1, g2_ref[...], sh_ref[...], sc_ref[...])
    tok_ref[...] = tok.astype(BF16)
    _store_pieces(tokp_ref, _pack_halves(tok))
    _route(tok, wh_ref, wl_ref, rb_ref, e_ref, gt_ref, pos_ref, cnt_ref, carry)


def _outproj_call(ins, ws, x_head, x_tail, g1, gate, g2, shift, scale, router_w, router_b,
                  *, rows, tm, seq, n_batch):
    d = x_head.shape[1]
    n_head = min(x_head.shape[0], rows) // tm
    n_exp = router_w.shape[1]
    w_t = router_w.T
    w_hi = w_t.astype(BF16)
    w_lo = (w_t - w_hi.astype(F32)).astype(BF16)
    seg = _seg_map(tm, seq, n_batch)
    in_heads = tuple(min(head.shape[0], rows) // tm for head, _ in ins)
    pair_specs = [s for (head, _), nh in zip(ins, in_heads) for s in _two_source_specs(tm, head.shape[1], nh)]
    full_spec = lambda a: pl.BlockSpec(a.shape, lambda i: (0, 0))
    lane_spec = pl.BlockSpec((SUBLANES, tm), lambda i: (0, i))
    tok_rows = lambda dt: jax.ShapeDtypeStruct((SUBLANES, rows), dt)
    return pl.pallas_call(
        functools.partial(_outproj_kernel, in_heads=in_heads, n_head=n_head),
        grid=(rows // tm,),
        in_specs=pair_specs + [full_spec(w) for w in ws] + [
            *_two_source_specs(tm, d, n_head), full_spec(g1), pl.BlockSpec((None, 1, d), seg), full_spec(g2),
            pl.BlockSpec((None, 1, d), seg), pl.BlockSpec((None, 1, d), seg),
            full_spec(w_hi), full_spec(w_lo), pl.BlockSpec((n_exp, 1), lambda i: (0, 0))],
        out_specs=[pl.BlockSpec((tm, d), lambda i: (i, 0)), pl.BlockSpec((tm, d), lambda i: (i, 0)),
                   pl.BlockSpec((SC_SPLIT, tm, d // 2 // SC_SPLIT), lambda i: (0, i, 0)),
                   lane_spec, pl.BlockSpec((tm, SUBLANES), lambda i: (i, 0)), lane_spec,
                   pl.BlockSpec((n_exp, LANES), lambda i: (0, 0))],
        out_shape=[jax.ShapeDtypeStruct((rows, d), F32), jax.ShapeDtypeStruct((rows, d), BF16),
                   jax.ShapeDtypeStruct((SC_SPLIT, rows, d // 2 // SC_SPLIT), jnp.int32),
                   tok_rows(jnp.int32), jax.ShapeDtypeStruct((rows, SUBLANES), F32), tok_rows(jnp.int32),
                   jax.ShapeDtypeStruct((n_exp, LANES), F32)],
        scratch_shapes=[pltpu.VMEM((n_exp, 1), F32)],
        compiler_params=_cparams(1, 48),
        name="outproj_residual_route",
    )(*[a for pair in ins for a in pair], *ws, x_head, x_tail, g1, gate, g2, shift, scale, w_hi, w_lo,
      router_b.reshape(n_exp, 1))


def _pick_max(cur, idx):
    mx = jnp.max(cur, axis=0, keepdims=True)
    first = jnp.min(jnp.where(cur == mx, idx, float(cur.shape[0])), axis=0, keepdims=True)
    return first, idx == first


def _route(tok, wh_ref, wl_ref, rb_ref, e_ref, gt_ref, pos_ref, cnt_ref, carry):
    @pl.when(pl.program_id(0) == 0)
    def _():
        carry[...] = jnp.zeros_like(carry)

    t_hi, t_lo = _split_bf16(tok)
    wh, wl = wh_ref[...], wl_ref[...]
    logits = _dot_nt(wh, t_hi) + _dot_nt(wh, t_lo) + _dot_nt(wl, t_hi)
    n_exp, tm = logits.shape
    scores = _sigmoid(logits)
    biased = scores + rb_ref[...]
    gsz = n_exp // N_GROUPS
    neg = -jnp.inf

    b3 = biased.reshape(N_GROUPS, gsz, tm)
    im = lax.broadcasted_iota(jnp.int32, b3.shape, 1).astype(F32)
    m1 = jnp.max(b3, axis=1, keepdims=True)
    i1 = jnp.min(jnp.where(b3 == m1, im, float(gsz)), axis=1, keepdims=True)
    m2 = jnp.max(jnp.where(im == i1, neg, b3), axis=1, keepdims=True)
    gscore = (m1 + m2).reshape(N_GROUPS, tm)

    ig = lax.broadcasted_iota(jnp.int32, gscore.shape, 0).astype(F32)
    gsel = jnp.zeros_like(gscore)
    cur = gscore
    for _ in range(TOPK_GROUPS):
        _, hit = _pick_max(cur, ig)
        gsel = jnp.where(hit, 1.0, gsel)
        cur = jnp.where(hit, neg, cur)
    gsel3 = jnp.broadcast_to(gsel.reshape(N_GROUPS, 1, tm), b3.shape)
    cur = jnp.where(gsel3 > 0.0, b3, neg).reshape(n_exp, tm)

    ie = lax.broadcasted_iota(jnp.int32, (n_exp, tm), 0).astype(F32)
    sel = jnp.zeros((n_exp, tm), F32)
    picks, raw = [], []
    for _ in range(TOP_K):
        first, hit = _pick_max(cur, ie)
        picks.append(first)
        raw.append(jnp.sum(jnp.where(hit, scores, 0.0), axis=0, keepdims=True))
        sel = jnp.where(hit, 1.0, sel)
        cur = jnp.where(hit, neg, cur)
    total = raw[0]
    for r in raw[1:]:
        total = total + r

    ri = lax.broadcasted_iota(jnp.int32, (tm, tm), 0)
    ci = lax.broadcasted_iota(jnp.int32, (tm, tm), 1)
    upper = jnp.where(ri < ci, 1.0, 0.0).astype(BF16)
    rank = _dot(sel.astype(BF16), upper) + carry[...]
    ranks = [jnp.sum(jnp.where(ie == p, rank, 0.0), axis=0, keepdims=True) for p in picks]
    carry[...] = carry[...] + jnp.sum(sel, axis=1, keepdims=True)

    pad = jnp.zeros((SUBLANES - TOP_K, tm), F32)
    e_ref[...] = jnp.concatenate(picks + [pad], axis=0).astype(jnp.int32)
    gates = jnp.concatenate([r / total * ROUTED_SCALE for r in raw]
                            + [jnp.zeros((LANES - TOP_K, tm), F32)], axis=0)
    gt_ref[...] = gates.T[:, :SUBLANES]
    pos_ref[...] = jnp.concatenate(ranks + [pad], axis=0).astype(jnp.int32)
    cnt_ref[...] = jnp.broadcast_to(carry[...], cnt_ref.shape)


def _expert_kernel(be_ref, nv_ref, nu_ref, xs_hbm, wg_ref, wu_ref, wd_ref, ys_ref, wgu, wdn, xbuf, sem,
                   *, ff, n_sub):
    b = pl.program_id(0)
    n_used = nu_ref[0]
    depth, _, te, q = xbuf.shape

    def row_block_copy(blk):
        slot = blk % depth
        return pltpu.make_async_copy(xs_hbm.at[:, pl.ds(blk * te, te), :], xbuf.at[slot], sem.at[slot])

    @pl.when(b == 0)
    def _():
        for j in range(depth - 1):
            @pl.when(j < n_used)
            def _():
                row_block_copy(j).start()

    @pl.when(b + depth - 1 < n_used)
    def _():
        row_block_copy(b + depth - 1).start()

    changed = jnp.logical_or(b == 0, be_ref[b] != be_ref[jnp.maximum(b - 1, 0)])

    @pl.when(changed)
    def _():
        wgu[:, :ff] = wg_ref[...].astype(BF16)
        wgu[:, ff:] = wu_ref[...].astype(BF16)
        wdn[...] = wd_ref[...].astype(BF16)

    @pl.when(b < n_used)
    def _():
        row_block_copy(b).wait()
        slot = b % depth
        n_valid = nv_ref[b]
        ts = te // n_sub
        gus = []
        for a in range(n_sub):
            r0 = a * ts
            xw = jnp.concatenate([xbuf[slot, j, r0:r0 + ts, :] for j in range(SC_SPLIT)], axis=-1)
            row = lax.broadcasted_iota(jnp.int32, xw.shape, 0) + r0
            lo, hi = _unpack_halves(jnp.where(row < n_valid, xw, 0))
            half = lo.shape[1]
            gus.append(_dot(lo.astype(BF16), wgu[:half, :]) + _dot(hi.astype(BF16), wgu[half:, :]))
        outs = []
        for gu in gus:
            gate = gu[:, :ff]
            hmid = (gate * _sigmoid(gate) * gu[:, ff:]).astype(BF16)
            outs.append(_dot(hmid, wdn[...]))
        for a, y in enumerate(outs):
            words = _pack_halves(y)
            for j in range(SC_SPLIT):
                ys_ref[j, a * ts:(a + 1) * ts, :] = words[:, j * q:(j + 1) * q]


def _expert_call(block_e, n_valid, n_used, xs, layer, w_gate, w_up, w_down):
    _, n_rows, q = xs.shape
    d = 2 * SC_SPLIT * q
    ff = w_gate.shape[3]
    te = EXPERT_ROWS
    row_map = lambda b, be, nv, nu: (0, jnp.minimum(b, nu[0] - 1), 0)
    w_map = lambda b, be, nv, nu: (layer, be[b], 0, 0)
    grid_spec = pltpu.PrefetchScalarGridSpec(
        num_scalar_prefetch=3,
        grid=(n_rows // te,),
        in_specs=[pl.BlockSpec(memory_space=pl.ANY),
                  pl.BlockSpec((None, None, d, ff), w_map),
                  pl.BlockSpec((None, None, d, ff), w_map),
                  pl.BlockSpec((None, None, ff, d), w_map)],
        out_specs=pl.BlockSpec((SC_SPLIT, te, q), row_map),
        scratch_shapes=[pltpu.VMEM((d, 2 * ff), BF16), pltpu.VMEM((ff, d), BF16),
                        pltpu.VMEM((EXPERT_RING, SC_SPLIT, te, q), jnp.int32),
                        pltpu.SemaphoreType.DMA((EXPERT_RING,))])
    return pl.pallas_call(
        functools.partial(_expert_kernel, ff=ff, n_sub=EXPERT_ROWS // EXPERT_SUB_ROWS),
        grid_spec=grid_spec,
        out_shape=jax.ShapeDtypeStruct(xs.shape, jnp.int32),
        compiler_params=_cparams(1, 40),
        name="moe_experts",
    )(block_e, n_valid, n_used, xs, w_gate, w_up, w_down)


SC_WINDOW = 128


def _sc_mesh():
    return plsc.VectorSubcoreMesh(core_axis_name="core", subcore_axis_name="subcore")


def _sc_scatter_rows(rows, dests, n_out):
    n, width = rows.shape
    n_k = len(dests)
    assert n % SC_WINDOW == 0

    @functools.partial(pl.kernel, out_type=jax.ShapeDtypeStruct((n_out, width), rows.dtype),
                       mesh=_sc_mesh(), name="sc_dispatch_rows")
    def scatter_kernel(x_hbm, *refs):
        idx_hbm, o_hbm = refs[:n_k], refs[n_k]

        def body(x_vmem, *idx_vmem):
            for iv in idx_vmem:
                pltpu.sync_copy(x_vmem, o_hbm.at[iv.at[0]])

        pltpu.emit_pipeline(
            body,
            grid=(n // SC_WINDOW,),
            in_specs=[pl.BlockSpec((SC_WINDOW, width), lambda i: (i, 0))]
            + [pl.BlockSpec((1, SC_WINDOW), lambda i: (0, i))] * n_k,
            out_specs=[],
            core_axis_name=("core", "subcore"),
            dimension_semantics=(pltpu.PARALLEL,),
        )(x_hbm, *idx_hbm)

    return scatter_kernel(rows, *dests)


def _sc_gather_rows(table, idx):
    n = idx.shape[1]
    width = table.shape[1]
    assert n % SC_WINDOW == 0

    @functools.partial(pl.kernel, out_type=jax.ShapeDtypeStruct((n, width), table.dtype),
                       mesh=_sc_mesh(), name="sc_collect_rows")
    def gather_kernel(t_hbm, i_hbm, o_hbm):
        def body(i_vmem, o_vmem):
            pltpu.sync_copy(t_hbm.at[i_vmem.at[0]], o_vmem)

        pltpu.emit_pipeline(
            body,
            grid=(n // SC_WINDOW,),
            in_specs=[pl.BlockSpec((1, SC_WINDOW), lambda i: (0, i))],
            out_specs=[pl.BlockSpec((SC_WINDOW, width), lambda i: (i, 0))],
            core_axis_name=("core", "subcore"),
            dimension_semantics=(pltpu.PARALLEL,),
        )(i_hbm, o_hbm)

    return gather_kernel(table, idx)


def _shared_kernel(tok_ref, sgu_ref, sd_ref, o_ref, *, ff):
    gu = _dot(tok_ref[...], sgu_ref[...])
    gate = gu[:, :ff]
    hmid = (gate * _sigmoid(gate) * gu[:, ff:]).astype(BF16)
    o_ref[...] = _dot(hmid, sd_ref[...]).astype(BF16)


def _shared_call(tok, sgu, sd, *, rows, tm):
    d = tok.shape[1]
    ff = sd.shape[0]
    return pl.pallas_call(
        functools.partial(_shared_kernel, ff=ff),
        grid=(rows // tm,),
        in_specs=[pl.BlockSpec((tm, d), lambda i: (i, 0)),
                  pl.BlockSpec((d, 2 * ff), lambda i: (0, 0)),
                  pl.BlockSpec((ff, d), lambda i: (0, 0))],
        out_specs=pl.BlockSpec((tm, d), lambda i: (i, 0)),
        out_shape=jax.ShapeDtypeStruct((rows, d), BF16),
        compiler_params=_cparams(1, 40),
        name="moe_shared_expert",
    )(tok, sgu, sd)


def _moe_out_kernel(shared_ref, yg_ref, gk_ref, x_ref, g_ref, gate_ref, xo_ref):
    shared = shared_ref[...].astype(F32)
    gk = gk_ref[...]
    lo_acc, hi_acc = None, None
    for k in range(TOP_K):
        lo, hi = _unpack_halves(jnp.concatenate([yg_ref[j, k] for j in range(SC_SPLIT)], axis=-1))
        w = gk[:, k:k + 1]
        lo_acc = lo * w if lo_acc is None else lo_acc + lo * w
        hi_acc = hi * w if hi_acc is None else hi_acc + hi * w
    y = jnp.concatenate([lo_acc, hi_acc], axis=-1) + shared
    xo_ref[...] = x_ref[...] + gate_ref[...] * _rms(y, g_ref[...])


def _moe_out_call(shared, yg, gk, xs, g, gate, *, rows, tm, seq, n_batch):
    d = xs.shape[1]
    seg = _seg_map(tm, seq, n_batch)
    row = lambda: pl.BlockSpec((tm, d), lambda i: (i, 0))
    return pl.pallas_call(
        _moe_out_kernel,
        grid=(rows // tm,),
        in_specs=[row(),
                  pl.BlockSpec((SC_SPLIT, TOP_K, tm, d // 2 // SC_SPLIT), lambda i: (0, 0, i, 0)),
                  pl.BlockSpec((tm, SUBLANES), lambda i: (i, 0)),
                  row(),
                  pl.BlockSpec((1, d), lambda i: (0, 0)),
                  pl.BlockSpec((None, 1, d), seg)],
        out_specs=row(),
        out_shape=jax.ShapeDtypeStruct((rows, d), F32),
        compiler_params=_cparams(1, 48),
        name="moe_combine",
    )(shared, yg, gk, xs, g, gate)


def _dest_kernel(e_ref, pos_ref, ps_ref, o_ref, *, n_rows):
    e = e_ref[...].astype(F32)
    ps = ps_ref[...]
    n_exp, tm = ps.shape[0], e.shape[1]
    ids = lax.broadcasted_iota(jnp.int32, (n_exp, tm), 0).astype(F32)
    rows = [jnp.sum(jnp.where(ids == e[k:k + 1, :], ps, 0.0), axis=0, keepdims=True) for k in range(SUBLANES)]
    dest = jnp.concatenate(rows, axis=0).astype(jnp.int32) + pos_ref[...]
    for j in range(SC_SPLIT):
        o_ref[j] = dest + j * n_rows


def _dest_call(e_tk, pos_tk, pad_starts, *, n_rows, tm):
    rows = e_tk.shape[1]
    n_exp = pad_starts.shape[0]
    lane_spec = pl.BlockSpec((SUBLANES, tm), lambda i: (0, i))
    return pl.pallas_call(
        functools.partial(_dest_kernel, n_rows=n_rows),
        grid=(rows // tm,),
        in_specs=[lane_spec, lane_spec, pl.BlockSpec((n_exp, 1), lambda i: (0, 0))],
        out_specs=pl.BlockSpec((SC_SPLIT, SUBLANES, tm), lambda i: (0, 0, i)),
        out_shape=jax.ShapeDtypeStruct((SC_SPLIT, SUBLANES, rows), jnp.int32),
        compiler_params=_cparams(1, 32),
        name="moe_dest_rows",
    )(e_tk, pos_tk, pad_starts.astype(F32).reshape(n_exp, 1))


def _moe_layer(xs, tok, tokp, routing, g_out, gate, layer, w_gate, w_up, w_down, sh_gate, sh_up, sh_down,
               *, rows, tm, seq, n_batch):
    e_tk, g_tk, pos_tk, counts = routing
    n_exp = counts.shape[0]
    te = EXPERT_ROWS
    cnt = counts[:, 0].astype(jnp.int32)
    padded = (cnt + te - 1) // te * te
    pad_ends = jnp.cumsum(padded)
    pad_starts = pad_ends - padded
    expert_ids = jnp.arange(n_exp, dtype=jnp.int32)
    n_blocks = -(-(rows * TOP_K + n_exp * (te - 1)) // te)
    n_used = pad_ends[-1] // te
    blk = jnp.arange(n_blocks, dtype=jnp.int32)
    last = jnp.minimum(blk, n_used - 1) * te
    owner = (last[:, None] >= pad_ends[None, :]).astype(jnp.int32)
    block_e = jnp.minimum(jnp.sum(owner, axis=1), n_exp - 1)
    own_hot = block_e[:, None] == expert_ids[None, :]
    n_valid = jnp.sum(jnp.where(own_hot, (pad_starts + cnt)[None, :], 0), axis=1) - last
    n_valid = jnp.clip(n_valid, 0, te)
    q = tokp.shape[2]
    n_rows = n_blocks * te
    piece = _dest_call(e_tk, pos_tk, pad_starts, n_rows=n_rows, tm=tm)[:, :TOP_K, :]
    xg = _sc_scatter_rows(tokp.reshape(SC_SPLIT * rows, q),
                          [piece[:, k, :].reshape(1, SC_SPLIT * rows) for k in range(TOP_K)],
                          SC_SPLIT * n_rows).reshape(SC_SPLIT, n_rows, q)
    sgu = jnp.concatenate([sh_gate, sh_up], axis=1).astype(BF16)
    shared = _shared_call(tok, sgu, sh_down.astype(BF16), rows=rows, tm=tm)
    ys = _expert_call(block_e, n_valid, n_used.reshape(1).astype(jnp.int32), xg, layer, w_gate, w_up, w_down)
    yg = _sc_gather_rows(ys.reshape(SC_SPLIT * n_rows, q),
                         piece.reshape(1, -1)).reshape(SC_SPLIT, TOP_K, rows, q)
    return _moe_out_call(shared, yg, g_tk, xs, g_out, gate, rows=rows, tm=tm, seq=seq, n_batch=n_batch)


def _rope(t, cos, sin):
    half = DA_QK_DIM // 2
    up = pltpu.roll(t, shift=LANES - half, axis=1)
    dn = pltpu.roll(t, shift=half, axis=1)
    lane = lax.broadcasted_iota(jnp.int32, t.shape, 1) % DA_QK_DIM
    return t * cos + jnp.where(lane < half, -up, dn) * sin


def _qkv_kernel(x_ref, g_ref, sh_ref, sc_ref, w_ref, cos_ref, sin_ref, q_ref, k_ref, v_ref, *, qkw):
    h = _normmod(x_ref[...], g_ref[...], sh_ref[...], sc_ref[...]).astype(BF16)
    qkv = _dot(h, w_ref[...])
    cos, sin = cos_ref[...], sin_ref[...]
    q_scale = DA_QK_DIM ** -0.5 * math.log2(math.e)
    for hb in range(qkw // LANES):
        lo, hi = hb * LANES, (hb + 1) * LANES
        q_ref[:, lo:hi] = (_rope(qkv[:, lo:hi], cos, sin) * q_scale).astype(BF16)
        k_ref[:, lo:hi] = _rope(qkv[:, qkw + lo:qkw + hi], cos, sin).astype(BF16)
    v_ref[...] = qkv[:, 2 * qkw:].astype(BF16)


def _qkv_call(xs, g, shift, scale, w_qkv, cos_t, sin_t, *, qkw, tm, seq, ctx_len, n_batch):
    rows, d = xs.shape
    n = w_qkv.shape[1]
    vw = n - 2 * qkw
    lat_tiles, lat_tps, ctx_tps = n_batch * seq // tm, seq // tm, ctx_len // tm
    seg = _seg_map(tm, seq, n_batch)

    def kv_map(i):
        c = i - lat_tiles
        is_lat = i < lat_tiles
        return (jnp.where(is_lat, i // lat_tps, c // ctx_tps),
                jnp.where(is_lat, ctx_tps + i % lat_tps, c % ctx_tps), 0)

    rope_map = lambda i: (jnp.where(i < lat_tiles, i % lat_tps, lat_tps), 0)
    return pl.pallas_call(
        functools.partial(_qkv_kernel, qkw=qkw),
        grid=(rows // tm,),
        in_specs=[pl.BlockSpec((tm, d), lambda i: (i, 0)),
                  pl.BlockSpec((1, d), lambda i: (0, 0)),
                  pl.BlockSpec((None, 1, d), seg),
                  pl.BlockSpec((None, 1, d), seg),
                  pl.BlockSpec((d, n), lambda i: (0, 0)),
                  pl.BlockSpec((tm, LANES), rope_map),
                  pl.BlockSpec((tm, LANES), rope_map)],
        out_specs=[pl.BlockSpec((tm, qkw), lambda i: (i, 0)),
                   pl.BlockSpec((None, tm, qkw), kv_map),
                   pl.BlockSpec((None, tm, vw), kv_map)],
        out_shape=[jax.ShapeDtypeStruct((rows, qkw), BF16),
                   jax.ShapeDtypeStruct((n_batch, ctx_len + seq, qkw), BF16),
                   jax.ShapeDtypeStruct((n_batch, ctx_len + seq, vw), BF16)],
        compiler_params=_cparams(1, 48),
        name="qkv_rope",
    )(xs, g, shift, scale, w_qkv, cos_t, sin_t)


def _rope_tables(seq, tm):
    pos = np.arange(seq)
    n_freq = DA_QK_DIM // 4
    inv = np.power(ROPE_BASE, -np.arange(n_freq, dtype=np.float32) / n_freq).astype(np.float32)
    row = (pos // GRID_W).astype(np.float32)[:, None] * inv
    col = (pos % GRID_W).astype(np.float32)[:, None] * inv
    ang = np.concatenate([row, col], axis=-1).astype(np.float32)
    ang = np.tile(ang, (1, LANES // ang.shape[1]))
    cos = np.concatenate([np.cos(ang), np.ones((tm, LANES))], axis=0)
    sin = np.concatenate([np.sin(ang), np.zeros((tm, LANES))], axis=0)
    return jnp.asarray(cos, F32), jnp.asarray(sin, F32)


def _attn_kernel(lp_ref, q_ref, k_ref, v_ref, sg_ref, o_ref, vext, *, n_sub, lam_init):
    @pl.when(pl.program_id(2) == 0)
    def _():
        vext[:, :LANES] = v_ref[...]
        vext[:, LANES:] = jnp.ones((vext.shape[0], LANES), BF16)

    lp = lp_ref[...]
    lam = (jnp.exp(jnp.sum(lp[0:1] * lp[1:2], axis=1, keepdims=True))
           - jnp.exp(jnp.sum(lp[2:3] * lp[3:4], axis=1, keepdims=True)) + lam_init)
    ts = q_ref.shape[0] // n_sub
    lane = lax.broadcasted_iota(jnp.int32, (ts, LANES), 1)
    zero = jnp.zeros((ts, LANES), BF16)
    k = k_ref[...]
    scores = []
    for a in range(n_sub):
        q = q_ref[a * ts:(a + 1) * ts, :]
        qq = jnp.concatenate([jnp.where(lane < DA_QK_DIM, q, zero),
                              jnp.where(lane >= DA_QK_DIM, q, zero)], axis=0)
        scores.append(_dot_nt(qq, k))
    for a in range(n_sub):
        s = scores[a]
        p = jnp.exp2(s - jnp.max(s, axis=-1, keepdims=True)).astype(BF16)
        oe = _dot(p, vext[...])
        on = oe[:, :LANES] / oe[:, LANES:LANES + 1]
        o = on[:ts] - lam * on[ts:]
        o_ref[a * ts:(a + 1) * ts, :] = (_rms(o, sg_ref[...]) * (1.0 - lam_init)).astype(BF16)


def _attn_call(lam_p, q, k_all, v_all, subln_g, *, tq, seq, n_batch, lam_init):
    n_heads = q.shape[1] // LANES
    lk = k_all.shape[1]
    qt = seq // tq
    return pl.pallas_call(
        functools.partial(_attn_kernel, n_sub=tq // ATTN_SUB_ROWS, lam_init=lam_init),
        grid=(n_batch, n_heads, qt),
        in_specs=[pl.BlockSpec(lam_p.shape, lambda b, h, i: (0, 0)),
                  pl.BlockSpec((tq, LANES), lambda b, h, i: (b * qt + i, h)),
                  pl.BlockSpec((None, lk, LANES), lambda b, h, i: (b, 0, h)),
                  pl.BlockSpec((None, lk, LANES), lambda b, h, i: (b, 0, h)),
                  pl.BlockSpec((1, LANES), lambda b, h, i: (0, 0))],
        out_specs=pl.BlockSpec((tq, LANES), lambda b, h, i: (b * qt + i, h)),
        out_shape=jax.ShapeDtypeStruct((n_batch * seq, n_heads * LANES), BF16),
        scratch_shapes=[pltpu.VMEM((lk, 2 * LANES), BF16)],
        compiler_params=_cparams(3, 48),
        name="diff_attention",
    )(lam_p, q, k_all, v_all, subln_g)


def kernel(x, c, ctx, c_ctx, ada_w, ada_b, norm_g, cf_w_in, cf_conv_w, cf_conv_b, cf_ln_g, cf_ln_b,
           cf_w_out, da_w_qkv, da_lambda, da_subln_g, da_w_out, moe_router_w, moe_router_b,
           moe_w_gate, moe_w_up, moe_w_down, moe_sh_gate, moe_sh_up, moe_sh_down):
    n_batch, seq, d = x.shape
    ctx_len = ctx.shape[1]
    depth = ada_w.shape[0]
    assert depth == 2 and n_batch + 1 <= N_SEG
    assert DA_QK_DIM * 2 == LANES and da_subln_g.shape[1] == LANES
    rows_lat, rows_ctx = n_batch * seq, n_batch * ctx_len
    tm = ROW_TILE
    tq = SEQ_TILE
    assert seq % tm == 0 and rows_ctx % tm == 0 and seq % tq == 0 and ctx_len % tq == 0

    x_lat, x_ctx = x.reshape(rows_lat, d), ctx.reshape(rows_ctx, d)
    cond = jnp.concatenate([c, c_ctx[None, :], jnp.zeros((N_SEG - n_batch - 1, d), F32)], axis=0)
    mods = _mods_call(cond, ada_w, ada_b)
    mod = lambda layer, k: mods[layer, :, k * d:(k + 1) * d][:, None, :]
    gain = lambda layer, k: norm_g[layer, k][None, :]
    common = dict(seq=seq, n_batch=n_batch)

    cc = cf_conv_w.shape[-1]
    fc = cf_w_in.shape[2] - 2 * cc
    gw = fc // FOURIER_GROUPS
    ch_ang = 2.0 * np.pi * (np.outer(np.arange(gw), np.arange(gw)) % gw) / gw
    cs = jnp.asarray(np.concatenate([np.cos(ch_ang), np.sin(ch_ang)], axis=1) / math.sqrt(gw), BF16)
    a_glu, ab = _inproj_call(x_lat, x_ctx, gain(0, 0), mod(0, 0), mod(0, 1), cf_w_in[0].astype(BF16), cs,
                             cc=cc, gw=gw, tm=tm, **common)
    gsz = cc // CONV_GROUPS
    gid = np.arange(cc) // gsz
    gmean = jnp.asarray((gid[:, None] == gid[None, :]) / gsz, BF16)
    a_act = _conv_call(a_glu, cf_conv_w[0][:, 0, :], cf_conv_b[0][None, :], cf_ln_g[0][None, :],
                       cf_ln_b[0][None, :], gmean, tl=tq, seq=seq, ctx_len=ctx_len, n_batch=n_batch)
    fr_lat = _seqdft_call(ab, seq, row0=0, n_seq=n_batch, fc=fc, tm=min(seq // 2, 512))
    fr_ctx = _seqdft_call(ab, ctx_len, row0=rows_lat, n_seq=n_batch, fc=fc, tm=min(ctx_len // 2, 256))
    w_out = cf_w_out[0].astype(BF16)
    xs, tok, tokp, *routing = _outproj_call(
        [(a_act, a_act), (fr_lat, fr_ctx)], [w_out[:cc], w_out[cc:]], x_lat, x_ctx, gain(0, 1), mod(0, 2), gain(0, 2), mod(0, 3),
        mod(0, 4), moe_router_w[0], moe_router_b[0], rows=rows_lat + rows_ctx, tm=tm, **common)
    xs = _moe_layer(xs, tok, tokp, routing, gain(0, 3), mod(0, 5), 0, moe_w_gate, moe_w_up, moe_w_down,
                    moe_sh_gate[0], moe_sh_up[0], moe_sh_down[0], rows=rows_lat + rows_ctx, tm=tm, **common)

    qkw = DA_HEADS * 2 * DA_QK_DIM
    lam_init = 0.8 - 0.6 * math.exp(-0.3 * 1)
    cos_t, sin_t = _rope_tables(seq, tq)
    q, k_all, v_all = _qkv_call(xs, gain(1, 0), mod(1, 0), mod(1, 1), da_w_qkv[0].astype(BF16),
                                cos_t, sin_t, qkw=qkw, tm=tq, ctx_len=ctx_len, **common)
    o = _attn_call(da_lambda[0], q, k_all, v_all, da_subln_g[0][None, :], tq=math.gcd(seq, ATTN_ROWS), seq=seq,
                   n_batch=n_batch, lam_init=lam_init)
    xs, tok, tokp, *routing = _outproj_call(
        [(o, o)], [da_w_out[0].astype(BF16)], xs, xs, gain(1, 1), mod(1, 2), gain(1, 2), mod(1, 3), mod(1, 4),
        moe_router_w[1], moe_router_b[1], rows=rows_lat, tm=tm, **common)
    xs = _moe_layer(xs, tok, tokp, routing, gain(1, 3), mod(1, 5), 1, moe_w_gate, moe_w_up, moe_w_down,
                    moe_sh_gate[1], moe_sh_up[1], moe_sh_down[1], rows=rows_lat, tm=tm, **common)
    return xs.reshape(n_batch, seq, d)
```

```python
import functools
import math

import numpy as np
import jax
import jax.numpy as jnp
from jax import lax
from jax.experimental import pallas as pl
from jax.experimental.pallas import tpu as pltpu
from jax.experimental.pallas import tpu_sc as plsc

F32 = jnp.float32
BF16 = jnp.bfloat16

EPS = 1e-6
GRID_W = 64
CONV_GROUPS = 8
FOURIER_GROUPS = 4
DA_HEADS = 8
DA_QK_DIM = 64
ROPE_BASE = 10000.0
N_GROUPS = 8
TOPK_GROUPS = 4
TOP_K = 6
ROUTED_SCALE = 2.5

LANES = 128
SUBLANES = 8
ROW_TILE = 512
SEQ_TILE = 256
N_SEG = 16
HALO = 16
EXPERT_ROWS = 1024
EXPERT_SUB_ROWS = 128
EXPERT_RING = 4
ATTN_ROWS = 1024
ATTN_SUB_ROWS = 128
MIB = 1024 * 1024


def _cparams(n_axes, vmem_mib):
    return pltpu.CompilerParams(dimension_semantics=("arbitrary",) * n_axes,
                                vmem_limit_bytes=vmem_mib * MIB)


def _sigmoid(v):
    return 1.0 / (1.0 + jnp.exp(-v))


def _rms(v, g):
    return v * lax.rsqrt(jnp.mean(v * v, axis=-1, keepdims=True) + EPS) * g


def _normmod(v, g, shift, scale):
    return _rms(v, g) * (1.0 + scale) + shift


def _split_bf16(v):
    hi = v.astype(BF16)
    lo = (v - hi.astype(F32)).astype(BF16)
    return hi, lo


def _pack_halves(v):
    half = v.shape[1] // 2
    word = pltpu.pack_elementwise([v[:, :half], v[:, half:]], packed_dtype=BF16)
    return lax.bitcast_convert_type(word, jnp.int32)


def _unpack_halves(w):
    u = lax.bitcast_convert_type(w, jnp.uint32)
    lo = lax.bitcast_convert_type(u << 16, F32)
    hi = lax.bitcast_convert_type(u & jnp.uint32(0xFFFF0000), F32)
    return lo, hi


SC_SPLIT = 2


def _store_pieces(ref, words):
    q = words.shape[1] // SC_SPLIT
    for j in range(SC_SPLIT):
        ref[j] = words[:, j * q:(j + 1) * q]


def _load_pieces(ref):
    return jnp.concatenate([ref[j] for j in range(SC_SPLIT)], axis=-1)


def _dot(a, b):
    return jnp.dot(a, b, preferred_element_type=F32)


def _dot_nt(a, b):
    return lax.dot_general(a, b, (((1,), (1,)), ((), ())), preferred_element_type=F32)


def _mods_kernel(c_ref, w_ref, b_ref, o_ref):
    cv = c_ref[...]
    o_ref[...] = _dot(cv * _sigmoid(cv), w_ref[...]) + b_ref[...]


def _mods_call(cond, ada_w, ada_b):
    depth, d, n = ada_w.shape
    tn = n // 4
    return pl.pallas_call(
        _mods_kernel,
        grid=(depth, n // tn),
        in_specs=[pl.BlockSpec((N_SEG, d), lambda l, j: (0, 0)),
                  pl.BlockSpec((None, d, tn), lambda l, j: (l, 0, j)),
                  pl.BlockSpec((None, 1, tn), lambda l, j: (l, 0, j))],
        out_specs=pl.BlockSpec((None, N_SEG, tn), lambda l, j: (l, 0, j)),
        out_shape=jax.ShapeDtypeStruct((depth, N_SEG, n), F32),
        compiler_params=_cparams(2, 40),
        name="adaln_mods",
    )(cond, ada_w, ada_b.reshape(depth, 1, n))


def _two_source_specs(tm, d, n_head_tiles):
    return (pl.BlockSpec((tm, d), lambda i: (jnp.minimum(i, n_head_tiles - 1), 0)),
            pl.BlockSpec((tm, d), lambda i: (jnp.maximum(i - n_head_tiles, 0), 0)))


def _pick_rows(head_ref, tail_ref, n_head_tiles):
    return jnp.where(pl.program_id(0) < n_head_tiles, head_ref[...], tail_ref[...])


def _inproj_kernel(xh_ref, xt_ref, g_ref, sh_ref, sc_ref, w_ref, cs_ref, a_ref, ab_ref, *, cc, gw, n_head):
    x = _pick_rows(xh_ref, xt_ref, n_head)
    h = _normmod(x, g_ref[...], sh_ref[...], sc_ref[...]).astype(BF16)
    u = _dot(h, w_ref[...])
    a_ref[...] = u[:, :cc] * _sigmoid(u[:, cc:2 * cc])
    f = u[:, 2 * cc:].astype(BF16)
    n_g = f.shape[1] // gw
    parts = [_dot(f[:, g * gw:(g + 1) * gw], cs_ref[...]) for g in range(n_g)]
    cos_part = [p[:, :gw] for p in parts]
    sin_part = [p[:, gw:] for p in parts]
    ab_ref[...] = jnp.concatenate(cos_part + sin_part, axis=-1).astype(BF16)


def _seg_map(tm, seq, n_batch):
    return lambda i: (jnp.minimum(i * tm // seq, n_batch), 0, 0)


def _inproj_call(x_head, x_tail, g, shift, scale, w_in, cs, *, cc, gw, tm, seq, n_batch):
    d = x_head.shape[1]
    rows = x_head.shape[0] + x_tail.shape[0]
    n_head = x_head.shape[0] // tm
    n = w_in.shape[1]
    fc = n - 2 * cc
    seg = _seg_map(tm, seq, n_batch)
    return pl.pallas_call(
        functools.partial(_inproj_kernel, cc=cc, gw=gw, n_head=n_head),
        grid=(rows // tm,),
        in_specs=[*_two_source_specs(tm, d, n_head),
                  pl.BlockSpec((1, d), lambda i: (0, 0)),
                  pl.BlockSpec((None, 1, d), seg),
                  pl.BlockSpec((None, 1, d), seg),
                  pl.BlockSpec((d, n), lambda i: (0, 0)),
                  pl.BlockSpec((gw, 2 * gw), lambda i: (0, 0))],
        out_specs=[pl.BlockSpec((tm, cc), lambda i: (i, 0)),
                   pl.BlockSpec((tm, 2 * fc), lambda i: (i, 0))],
        out_shape=[jax.ShapeDtypeStruct((rows, cc), F32),
                   jax.ShapeDtypeStruct((rows, 2 * fc), BF16)],
        compiler_params=_cparams(1, 40),
        name="inproj_glu_chdft",
    )(x_head, x_tail, g, shift, scale, w_in, cs)


def _conv_kernel(prev_ref, main_ref, next_ref, w_ref, cb_ref, lg_ref, lb_ref, gm_ref, o_ref,
                 buf, cv, shifted, *, tl, width, lat_tiles, lat_tps, ctx_tps, chunk):
    i = pl.program_id(0)
    is_lat = i < lat_tiles
    tps = jnp.where(is_lat, lat_tps, ctx_tps)
    j = jnp.where(is_lat, i, i - lat_tiles) % tps
    zero = jnp.zeros((HALO, buf.shape[1]), F32)
    buf[0:HALO, :] = jnp.where(j > 0, prev_ref[...], zero)
    buf[HALO:HALO + tl, :] = main_ref[...]
    buf[HALO + tl:HALO + tl + HALO, :] = jnp.where(j < tps - 1, next_ref[...], zero)
    base = HALO - width // 2
    span = shifted.shape[1]
    for s in range(1, SUBLANES):
        shifted[s] = buf[s:s + span, :]

    def window(off, r0):
        s, m = off % SUBLANES, off - off % SUBLANES
        if s == 0:
            return buf[m + r0:m + r0 + chunk, :]
        return shifted[s, m + r0:m + r0 + chunk, :]

    for r0 in range(0, tl, chunk):
        acc = window(base, r0) * w_ref[0:1, :]
        for t in range(1, width):
            acc = acc + window(base + t, r0) * w_ref[t:t + 1, :]
        cv[r0:r0 + chunk, :] = acc
    a = cv[...] + cb_ref[...]
    gm = gm_ref[...]
    a_hi, a_lo = _split_bf16(a)
    mu = _dot(a_hi, gm) + _dot(a_lo, gm)
    dl = a - mu
    q_hi, q_lo = _split_bf16(dl * dl)
    var = _dot(q_hi, gm) + _dot(q_lo, gm)
    y = dl * lax.rsqrt(var + EPS) * lg_ref[...] + lb_ref[...]
    o_ref[...] = (y * _sigmoid(y)).astype(BF16)


def _conv_call(a_glu, conv_w, conv_b, ln_g, ln_b, gmean, *, tl, seq, ctx_len, n_batch):
    rows, ch = a_glu.shape
    width = conv_w.shape[0]
    assert width // 2 <= HALO and tl % HALO == 0
    hb = tl // HALO
    last_halo = rows // HALO - 1
    return pl.pallas_call(
        functools.partial(_conv_kernel, tl=tl, width=width, lat_tiles=n_batch * seq // tl,
                          lat_tps=seq // tl, ctx_tps=ctx_len // tl, chunk=32),
        grid=(rows // tl,),
        in_specs=[pl.BlockSpec((HALO, ch), lambda i: (jnp.maximum(i * hb - 1, 0), 0)),
                  pl.BlockSpec((tl, ch), lambda i: (i, 0)),
                  pl.BlockSpec((HALO, ch), lambda i: (jnp.minimum((i + 1) * hb, last_halo), 0)),
                  pl.BlockSpec((width, ch), lambda i: (0, 0)),
                  pl.BlockSpec((1, ch), lambda i: (0, 0)),
                  pl.BlockSpec((1, ch), lambda i: (0, 0)),
                  pl.BlockSpec((1, ch), lambda i: (0, 0)),
                  pl.BlockSpec((ch, ch), lambda i: (0, 0))],
        out_specs=pl.BlockSpec((tl, ch), lambda i: (i, 0)),
        out_shape=jax.ShapeDtypeStruct((rows, ch), BF16),
        scratch_shapes=[pltpu.VMEM((tl + 2 * HALO, ch), F32), pltpu.VMEM((tl, ch), F32),
                        pltpu.VMEM((SUBLANES, tl + 2 * HALO - SUBLANES, ch), F32)],
        compiler_params=_cparams(1, 40),
        name="dwconv_groupln_swish",
    )(a_glu, a_glu, a_glu, conv_w, conv_b, ln_g, ln_b, gmean)


EDGE_ROWS = 16


def _seqdft_kernel(c_ref, s_ref, cx_ref, sx_ref, jm_ref, a_ref, b_ref, o_ref, *, tm, n_tiles):
    i = pl.program_id(1)
    a, b = a_ref[...], b_ref[...]
    p, qv = _dot(c_ref[...], a), _dot(s_ref[...], b)
    o_ref[pl.ds(pl.multiple_of(i * tm, tm), tm), :] = (p - qv).astype(BF16)
    edge = _dot(cx_ref[...], a) + _dot(sx_ref[...], b)
    pick = lax.broadcasted_iota(jnp.int32, edge.shape, 0) == i
    edge_row = jnp.sum(jnp.where(pick, edge, 0.0), axis=0, keepdims=True)
    mirrored = _dot(jm_ref[...], (p + qv).astype(BF16))
    srow = lax.broadcasted_iota(jnp.int32, mirrored.shape, 0)
    mirrored = jnp.where(srow == 0, edge_row, mirrored)
    o_ref[pl.ds(pl.multiple_of((n_tiles - 1 - i) * tm, tm), tm), :] = mirrored.astype(BF16)


def _seqdft_call(ab, length, *, row0, n_seq, fc, tm):
    n_tiles = length // tm
    n_half = n_tiles // 2
    assert n_tiles % 2 == 0 and n_half <= EDGE_ROWS
    scale = 1.0 / math.sqrt(length)
    cmat, smat = _dft_tables(length, scale, np.arange(length // 2))
    edge_k = np.zeros((EDGE_ROWS,), np.int64)
    edge_k[:n_half] = tm * (np.arange(n_half) + 1)
    cx, sx = _dft_tables(length, scale, edge_k)
    s_idx = np.arange(tm)
    reversal = jnp.asarray((s_idx[None, :] == (tm - s_idx)[:, None]) & (s_idx[:, None] > 0), BF16)
    assert row0 % length == 0
    seq0 = row0 // length
    whole = lambda arr: pl.BlockSpec(arr.shape, lambda b, i: (0, 0))
    return pl.pallas_call(
        functools.partial(_seqdft_kernel, tm=tm, n_tiles=n_tiles),
        grid=(n_seq, n_half),
        in_specs=[pl.BlockSpec((tm, length), lambda b, i: (i, 0)),
                  pl.BlockSpec((tm, length), lambda b, i: (i, 0)),
                  whole(cx), whole(sx), whole(reversal),
                  pl.BlockSpec((length, fc), lambda b, i: (seq0 + b, 0)),
                  pl.BlockSpec((length, fc), lambda b, i: (seq0 + b, 1))],
        out_specs=pl.BlockSpec((length, fc), lambda b, i: (b, 0)),
        out_shape=jax.ShapeDtypeStruct((n_seq * length, fc), BF16),
        compiler_params=_cparams(2, 48),
        name="seq_dft",
    )(cmat, smat, cx, sx, reversal, ab, ab)


def _dft_tables(length, scale, rows):
    k = np.asarray(rows, np.int64)[:, None]
    if length <= 512:
        ang = 2.0 * np.pi * ((k * np.arange(length)[None, :]) % length) / length
        return (jnp.asarray(np.cos(ang) * scale, BF16), jnp.asarray(np.sin(ang) * scale, BF16))
    r = 64
    n_rows = k.shape[0]
    if n_rows % r == 0 and np.array_equal(k[:, 0], np.arange(n_rows)):
        n = np.arange(length)[None, :]
        alpha = 2.0 * np.pi * ((np.arange(n_rows // r)[:, None] * r * n) % length) / length
        beta = 2.0 * np.pi * ((np.arange(r)[:, None] * n) % length) / length
        ca, sa = jnp.asarray(np.cos(alpha), F32)[:, None, :], jnp.asarray(np.sin(alpha), F32)[:, None, :]
        cb, sb = jnp.asarray(np.cos(beta) * scale, F32)[None, :, :], jnp.asarray(np.sin(beta) * scale, F32)[None, :, :]
        return ((ca * cb - sa * sb).reshape(n_rows, length).astype(BF16),
                (sa * cb + ca * sb).reshape(n_rows, length).astype(BF16))
    assert length % r == 0
    alpha = 2.0 * np.pi * ((k * np.arange(length // r)[None, :] * r) % length) / length
    beta = 2.0 * np.pi * ((k * np.arange(r)[None, :]) % length) / length
    ca, sa = jnp.asarray(np.cos(alpha), F32)[:, :, None], jnp.asarray(np.sin(alpha), F32)[:, :, None]
    cb, sb = jnp.asarray(np.cos(beta) * scale, F32)[:, None, :], jnp.asarray(np.sin(beta) * scale, F32)[:, None, :]
    cmat = (ca * cb - sa * sb).reshape(k.shape[0], length).astype(BF16)
    smat = (sa * cb + ca * sb).reshape(k.shape[0], length).astype(BF16)
    return cmat, smat


def _outproj_kernel(*refs, in_heads, n_head):
    n_in = len(in_heads)
    ins, ws = refs[:2 * n_in], refs[2 * n_in:3 * n_in]
    (xh_ref, xt_ref, g1_ref, gate_ref, g2_ref, sh_ref, sc_ref, wh_ref, wl_ref, rb_ref,
     xo_ref, tok_ref, tokp_ref, e_ref, gt_ref, pos_ref, cnt_ref, carry) = refs[3 * n_in:]
    y = None
    for j, w_ref in enumerate(ws):
        part = _dot(_pick_rows(ins[2 * j], ins[2 * j + 1], in_heads[j]), w_ref[...])
        y = part if y is None else y + part
    x1 = _pick_rows(xh_ref, xt_ref, n_head) + gate_ref[...] * _rms(y, g1_ref[...])
    xo_ref[...] = x1
    tok = _normmod(x1, g2_ref[...], sh_ref[...], sc_ref[...])
    tok_ref[...] = tok.astype(BF16)
    _store_pieces(tokp_ref, _pack_halves(tok))
    _route(tok, wh_ref, wl_ref, rb_ref, e_ref, gt_ref, pos_ref, cnt_ref, carry)


def _outproj_call(ins, ws, x_head, x_tail, g1, gate, g2, shift, scale, router_w, router_b,
                  *, rows, tm, seq, n_batch):
    d = x_head.shape[1]
    n_head = min(x_head.shape[0], rows) // tm
    n_exp = router_w.shape[1]
    w_t = router_w.T
    w_hi = w_t.astype(BF16)
    w_lo = (w_t - w_hi.astype(F32)).astype(BF16)
    seg = _seg_map(tm, seq, n_batch)
    in_heads = tuple(min(head.shape[0], rows) // tm for head, _ in ins)
    pair_specs = [s for (head, _), nh in zip(ins, in_heads) for s in _two_source_specs(tm, head.shape[1], nh)]
    full_spec = lambda a: pl.BlockSpec(a.shape, lambda i: (0, 0))
    lane_spec = pl.BlockSpec((SUBLANES, tm), lambda i: (0, i))
    tok_rows = lambda dt: jax.ShapeDtypeStruct((SUBLANES, rows), dt)
    return pl.pallas_call(
        functools.partial(_outproj_kernel, in_heads=in_heads, n_head=n_head),
        grid=(rows // tm,),
        in_specs=pair_specs + [full_spec(w) for w in ws] + [
            *_two_source_specs(tm, d, n_head), full_spec(g1), pl.BlockSpec((None, 1, d), seg), full_spec(g2),
            pl.BlockSpec((None, 1, d), seg), pl.BlockSpec((None, 1, d), seg),
            full_spec(w_hi), full_spec(w_lo), pl.BlockSpec((n_exp, 1), lambda i: (0, 0))],
        out_specs=[pl.BlockSpec((tm, d), lambda i: (i, 0)), pl.BlockSpec((tm, d), lambda i: (i, 0)),
                   pl.BlockSpec((SC_SPLIT, tm, d // 2 // SC_SPLIT), lambda i: (0, i, 0)),
                   lane_spec, pl.BlockSpec((tm, SUBLANES), lambda i: (i, 0)), lane_spec,
                   pl.BlockSpec((n_exp, LANES), lambda i: (0, 0))],
        out_shape=[jax.ShapeDtypeStruct((rows, d), F32), jax.ShapeDtypeStruct((rows, d), BF16),
                   jax.ShapeDtypeStruct((SC_SPLIT, rows, d // 2 // SC_SPLIT), jnp.int32),
                   tok_rows(jnp.int32), jax.ShapeDtypeStruct((rows, SUBLANES), F32), tok_rows(jnp.int32),
                   jax.ShapeDtypeStruct((n_exp, LANES), F32)],
        scratch_shapes=[pltpu.VMEM((n_exp, 1), F32)],
        compiler_params=_cparams(1, 48),
        name="outproj_residual_route",
    )(*[a for pair in ins for a in pair], *ws, x_head, x_tail, g1, gate, g2, shift, scale, w_hi, w_lo,
      router_b.reshape(n_exp, 1))


def _pick_max(cur, idx):
    mx = jnp.max(cur, axis=0, keepdims=True)
    first = jnp.min(jnp.where(cur == mx, idx, float(cur.shape[0])), axis=0, keepdims=True)
    return first, idx == first


def _route(tok, wh_ref, wl_ref, rb_ref, e_ref, gt_ref, pos_ref, cnt_ref, carry):
    @pl.when(pl.program_id(0) == 0)
    def _():
        carry[...] = jnp.zeros_like(carry)

    t_hi, t_lo = _split_bf16(tok)
    wh, wl = wh_ref[...], wl_ref[...]
    logits = _dot_nt(wh, t_hi) + _dot_nt(wh, t_lo) + _dot_nt(wl, t_hi)
    n_exp, tm = logits.shape
    scores = _sigmoid(logits)
    biased = scores + rb_ref[...]
    gsz = n_exp // N_GROUPS
    neg = -jnp.inf

    b3 = biased.reshape(N_GROUPS, gsz, tm)
    im = lax.broadcasted_iota(jnp.int32, b3.shape, 1).astype(F32)
    m1 = jnp.max(b3, axis=1, keepdims=True)
    i1 = jnp.min(jnp.where(b3 == m1, im, float(gsz)), axis=1, keepdims=True)
    m2 = jnp.max(jnp.where(im == i1, neg, b3), axis=1, keepdims=True)
    gscore = (m1 + m2).reshape(N_GROUPS, tm)

    ig = lax.broadcasted_iota(jnp.int32, gscore.shape, 0).astype(F32)
    gsel = jnp.zeros_like(gscore)
    cur = gscore
    for _ in range(TOPK_GROUPS):
        _, hit = _pick_max(cur, ig)
        gsel = jnp.where(hit, 1.0, gsel)
        cur = jnp.where(hit, neg, cur)
    gsel3 = jnp.broadcast_to(gsel.reshape(N_GROUPS, 1, tm), b3.shape)
    cur = jnp.where(gsel3 > 0.0, b3, neg).reshape(n_exp, tm)

    ie = lax.broadcasted_iota(jnp.int32, (n_exp, tm), 0).astype(F32)
    sel = jnp.zeros((n_exp, tm), F32)
    picks, raw = [], []
    for _ in range(TOP_K):
        first, hit = _pick_max(cur, ie)
        picks.append(first)
        raw.append(jnp.sum(jnp.where(hit, scores, 0.0), axis=0, keepdims=True))
        sel = jnp.where(hit, 1.0, sel)
        cur = jnp.where(hit, neg, cur)
    total = raw[0]
    for r in raw[1:]:
        total = total + r

    ri = lax.broadcasted_iota(jnp.int32, (tm, tm), 0)
    ci = lax.broadcasted_iota(jnp.int32, (tm, tm), 1)
    upper = jnp.where(ri < ci, 1.0, 0.0).astype(BF16)
    rank = _dot(sel.astype(BF16), upper) + carry[...]
    ranks = [jnp.sum(jnp.where(ie == p, rank, 0.0), axis=0, keepdims=True) for p in picks]
    carry[...] = carry[...] + jnp.sum(sel, axis=1, keepdims=True)

    pad = jnp.zeros((SUBLANES - TOP_K, tm), F32)
    e_ref[...] = jnp.concatenate(picks + [pad], axis=0).astype(jnp.int32)
    gates = jnp.concatenate([r / total * ROUTED_SCALE for r in raw]
                            + [jnp.zeros((LANES - TOP_K, tm), F32)], axis=0)
    gt_ref[...] = gates.T[:, :SUBLANES]
    pos_ref[...] = jnp.concatenate(ranks + [pad], axis=0).astype(jnp.int32)
    cnt_ref[...] = jnp.broadcast_to(carry[...], cnt_ref.shape)


def _expert_kernel(be_ref, nv_ref, nu_ref, xs_hbm, wg_ref, wu_ref, wd_ref, ys_ref, wgu, wdn, xbuf, sem,
                   *, ff, n_sub):
    b = pl.program_id(0)
    n_used = nu_ref[0]
    depth, _, te, q = xbuf.shape

    def row_block_copy(blk):
        slot = blk % depth
        return pltpu.make_async_copy(xs_hbm.at[:, pl.ds(blk * te, te), :], xbuf.at[slot], sem.at[slot])

    @pl.when(b == 0)
    def _():
        for j in range(depth - 1):
            @pl.when(j < n_used)
            def _():
                row_block_copy(j).start()

    @pl.when(b + depth - 1 < n_used)
    def _():
        row_block_copy(b + depth - 1).start()

    changed = jnp.logical_or(b == 0, be_ref[b] != be_ref[jnp.maximum(b - 1, 0)])

    @pl.when(changed)
    def _():
        wgu[:, :ff] = wg_ref[...].astype(BF16)
        wgu[:, ff:] = wu_ref[...].astype(BF16)
        wdn[...] = wd_ref[...].astype(BF16)

    @pl.when(b < n_used)
    def _():
        row_block_copy(b).wait()
        slot = b % depth
        n_valid = nv_ref[b]
        ts = te // n_sub
        gus = []
        for a in range(n_sub):
            r0 = a * ts
            xw = jnp.concatenate([xbuf[slot, j, r0:r0 + ts, :] for j in range(SC_SPLIT)], axis=-1)
            row = lax.broadcasted_iota(jnp.int32, xw.shape, 0) + r0
            lo, hi = _unpack_halves(jnp.where(row < n_valid, xw, 0))
            half = lo.shape[1]
            gus.append(_dot(lo.astype(BF16), wgu[:half, :]) + _dot(hi.astype(BF16), wgu[half:, :]))
        outs = []
        for gu in gus:
            gate = gu[:, :ff]
            hmid = (gate * _sigmoid(gate) * gu[:, ff:]).astype(BF16)
            outs.append(_dot(hmid, wdn[...]))
        for a, y in enumerate(outs):
            words = _pack_halves(y)
            for j in range(SC_SPLIT):
                ys_ref[j, a * ts:(a + 1) * ts, :] = words[:, j * q:(j + 1) * q]


def _expert_call(block_e, n_valid, n_used, xs, layer, w_gate, w_up, w_down):
    _, n_rows, q = xs.shape
    d = 2 * SC_SPLIT * q
    ff = w_gate.shape[3]
    te = EXPERT_ROWS
    row_map = lambda b, be, nv, nu: (0, jnp.minimum(b, nu[0] - 1), 0)
    w_map = lambda b, be, nv, nu: (layer, be[b], 0, 0)
    grid_spec = pltpu.PrefetchScalarGridSpec(
        num_scalar_prefetch=3,
        grid=(n_rows // te,),
        in_specs=[pl.BlockSpec(memory_space=pl.ANY),
                  pl.BlockSpec((None, None, d, ff), w_map),
                  pl.BlockSpec((None, None, d, ff), w_map),
                  pl.BlockSpec((None, None, ff, d), w_map)],
        out_specs=pl.BlockSpec((SC_SPLIT, te, q), row_map),
        scratch_shapes=[pltpu.VMEM((d, 2 * ff), BF16), pltpu.VMEM((ff, d), BF16),
                        pltpu.VMEM((EXPERT_RING, SC_SPLIT, te, q), jnp.int32),
                        pltpu.SemaphoreType.DMA((EXPERT_RING,))])
    return pl.pallas_call(
        functools.partial(_expert_kernel, ff=ff, n_sub=EXPERT_ROWS // EXPERT_SUB_ROWS),
        grid_spec=grid_spec,
        out_shape=jax.ShapeDtypeStruct(xs.shape, jnp.int32),
        compiler_params=_cparams(1, 40),
        name="moe_experts",
    )(block_e, n_valid, n_used, xs, w_gate, w_up, w_down)


SC_WINDOW = 128


def _sc_mesh():
    return plsc.VectorSubcoreMesh(core_axis_name="core", subcore_axis_name="subcore")


def _sc_scatter_rows(rows, dests, n_out):
    n, width = rows.shape
    n_k = len(dests)
    assert n % SC_WINDOW == 0

    @functools.partial(pl.kernel, out_type=jax.ShapeDtypeStruct((n_out, width), rows.dtype),
                       mesh=_sc_mesh(), name="sc_dispatch_rows")
    def scatter_kernel(x_hbm, *refs):
        idx_hbm, o_hbm = refs[:n_k], refs[n_k]

        def body(x_vmem, *idx_vmem):
            for iv in idx_vmem:
                pltpu.sync_copy(x_vmem, o_hbm.at[iv.at[0]])

        pltpu.emit_pipeline(
            body,
            grid=(n // SC_WINDOW,),
            in_specs=[pl.BlockSpec((SC_WINDOW, width), lambda i: (i, 0))]
            + [pl.BlockSpec((1, SC_WINDOW), lambda i: (0, i))] * n_k,
            out_specs=[],
            core_axis_name=("core", "subcore"),
            dimension_semantics=(pltpu.PARALLEL,),
        )(x_hbm, *idx_hbm)

    return scatter_kernel(rows, *dests)


def _sc_gather_rows(table, idx):
    n = idx.shape[1]
    width = table.shape[1]
    assert n % SC_WINDOW == 0

    @functools.partial(pl.kernel, out_type=jax.ShapeDtypeStruct((n, width), table.dtype),
                       mesh=_sc_mesh(), name="sc_collect_rows")
    def gather_kernel(t_hbm, i_hbm, o_hbm):
        def body(i_vmem, o_vmem):
            pltpu.sync_copy(t_hbm.at[i_vmem.at[0]], o_vmem)

        pltpu.emit_pipeline(
            body,
            grid=(n // SC_WINDOW,),
            in_specs=[pl.BlockSpec((1, SC_WINDOW), lambda i: (0, i))],
            out_specs=[pl.BlockSpec((SC_WINDOW, width), lambda i: (i, 0))],
            core_axis_name=("core", "subcore"),
            dimension_semantics=(pltpu.PARALLEL,),
        )(i_hbm, o_hbm)

    return gather_kernel(table, idx)


def _shared_kernel(tok_ref, sgu_ref, sd_ref, o_ref, *, ff):
    gu = _dot(tok_ref[...], sgu_ref[...])
    gate = gu[:, :ff]
    hmid = (gate * _sigmoid(gate) * gu[:, ff:]).astype(BF16)
    o_ref[...] = _dot(hmid, sd_ref[...]).astype(BF16)


def _shared_call(tok, sgu, sd, *, rows, tm):
    d = tok.shape[1]
    ff = sd.shape[0]
    return pl.pallas_call(
        functools.partial(_shared_kernel, ff=ff),
        grid=(rows // tm,),
        in_specs=[pl.BlockSpec((tm, d), lambda i: (i, 0)),
                  pl.BlockSpec((d, 2 * ff), lambda i: (0, 0)),
                  pl.BlockSpec((ff, d), lambda i: (0, 0))],
        out_specs=pl.BlockSpec((tm, d), lambda i: (i, 0)),
        out_shape=jax.ShapeDtypeStruct((rows, d), BF16),
        compiler_params=_cparams(1, 40),
        name="moe_shared_expert",
    )(tok, sgu, sd)


def _moe_out_kernel(shared_ref, yg_ref, gk_ref, x_ref, g_ref, gate_ref, xo_ref):
    shared = shared_ref[...].astype(F32)
    gk = gk_ref[...]
    lo_acc, hi_acc = None, None
    for k in range(TOP_K):
        lo, hi = _unpack_halves(jnp.concatenate([yg_ref[j, k] for j in range(SC_SPLIT)], axis=-1))
        w = gk[:, k:k + 1]
        lo_acc = lo * w if lo_acc is None else lo_acc + lo * w
        hi_acc = hi * w if hi_acc is None else hi_acc + hi * w
    y = jnp.concatenate([lo_acc, hi_acc], axis=-1) + shared
    xo_ref[...] = x_ref[...] + gate_ref[...] * _rms(y, g_ref[...])


def _moe_out_call(shared, yg, gk, xs, g, gate, *, rows, tm, seq, n_batch):
    d = xs.shape[1]
    seg = _seg_map(tm, seq, n_batch)
    row = lambda: pl.BlockSpec((tm, d), lambda i: (i, 0))
    return pl.pallas_call(
        _moe_out_kernel,
        grid=(rows // tm,),
        in_specs=[row(),
                  pl.BlockSpec((SC_SPLIT, TOP_K, tm, d // 2 // SC_SPLIT), lambda i: (0, 0, i, 0)),
                  pl.BlockSpec((tm, SUBLANES), lambda i: (i, 0)),
                  row(),
                  pl.BlockSpec((1, d), lambda i: (0, 0)),
                  pl.BlockSpec((None, 1, d), seg)],
        out_specs=row(),
        out_shape=jax.ShapeDtypeStruct((rows, d), F32),
        compiler_params=_cparams(1, 48),
        name="moe_combine",
    )(shared, yg, gk, xs, g, gate)


def _dest_kernel(e_ref, pos_ref, ps_ref, o_ref, *, n_rows):
    e = e_ref[...].astype(F32)
    ps = ps_ref[...]
    n_exp, tm = ps.shape[0], e.shape[1]
    ids = lax.broadcasted_iota(jnp.int32, (n_exp, tm), 0).astype(F32)
    rows = [jnp.sum(jnp.where(ids == e[k:k + 1, :], ps, 0.0), axis=0, keepdims=True) for k in range(SUBLANES)]
    dest = jnp.concatenate(rows, axis=0).astype(jnp.int32) + pos_ref[...]
    for j in range(SC_SPLIT):
        o_ref[j] = dest + j * n_rows


def _dest_call(e_tk, pos_tk, pad_starts, *, n_rows, tm):
    rows = e_tk.shape[1]
    n_exp = pad_starts.shape[0]
    lane_spec = pl.BlockSpec((SUBLANES, tm), lambda i: (0, i))
    return pl.pallas_call(
        functools.partial(_dest_kernel, n_rows=n_rows),
        grid=(rows // tm,),
        in_specs=[lane_spec, lane_spec, pl.BlockSpec((n_exp, 1), lambda i: (0, 0))],
        out_specs=pl.BlockSpec((SC_SPLIT, SUBLANES, tm), lambda i: (0, 0, i)),
        out_shape=jax.ShapeDtypeStruct((SC_SPLIT, SUBLANES, rows), jnp.int32),
        compiler_params=_cparams(1, 32),
        name="moe_dest_rows",
    )(e_tk, pos_tk, pad_starts.astype(F32).reshape(n_exp, 1))


def _moe_layer(xs, tok, tokp, routing, g_out, gate, layer, w_gate, w_up, w_down, sh_gate, sh_up, sh_down,
               *, rows, tm, seq, n_batch):
    e_tk, g_tk, pos_tk, counts = routing
    n_exp = counts.shape[0]
    te = EXPERT_ROWS
    cnt = counts[:, 0].astype(jnp.int32)
    padded = (cnt + te - 1) // te * te
    pad_ends = jnp.cumsum(padded)
    pad_starts = pad_ends - padded
    expert_ids = jnp.arange(n_exp, dtype=jnp.int32)
    n_blocks = -(-(rows * TOP_K + n_exp * (te - 1)) // te)
    n_used = pad_ends[-1] // te
    blk = jnp.arange(n_blocks, dtype=jnp.int32)
    last = jnp.minimum(blk, n_used - 1) * te
    owner = (last[:, None] >= pad_ends[None, :]).astype(jnp.int32)
    block_e = jnp.minimum(jnp.sum(owner, axis=1), n_exp - 1)
    own_hot = block_e[:, None] == expert_ids[None, :]
    n_valid = jnp.sum(jnp.where(own_hot, (pad_starts + cnt)[None, :], 0), axis=1) - last
    n_valid = jnp.clip(n_valid, 0, te)
    q = tokp.shape[2]
    n_rows = n_blocks * te
    piece = _dest_call(e_tk, pos_tk, pad_starts, n_rows=n_rows, tm=tm)[:, :TOP_K, :]
    xg = _sc_scatter_rows(tokp.reshape(SC_SPLIT * rows, q),
                          [piece[:, k, :].reshape(1, SC_SPLIT * rows) for k in range(TOP_K)],
                          SC_SPLIT * n_rows).reshape(SC_SPLIT, n_rows, q)
    sgu = jnp.concatenate([sh_gate, sh_up], axis=1).astype(BF16)
    shared = _shared_call(tok, sgu, sh_down.astype(BF16), rows=rows, tm=tm)
    ys = _expert_call(block_e, n_valid, n_used.reshape(1).astype(jnp.int32), xg, layer, w_gate, w_up, w_down)
    yg = _sc_gather_rows(ys.reshape(SC_SPLIT * n_rows, q),
                         piece.reshape(1, -1)).reshape(SC_SPLIT, TOP_K, rows, q)
    return _moe_out_call(shared, yg, g_tk, xs, g_out, gate, rows=rows, tm=tm, seq=seq, n_batch=n_batch)


def _rope(t, cos, sin):
    half = DA_QK_DIM // 2
    up = pltpu.roll(t, shift=LANES - half, axis=1)
    dn = pltpu.roll(t, shift=half, axis=1)
    lane = lax.broadcasted_iota(jnp.int32, t.shape, 1) % DA_QK_DIM
    return t * cos + jnp.where(lane < half, -up, dn) * sin


def _qkv_kernel(x_ref, g_ref, sh_ref, sc_ref, w_ref, cos_ref, sin_ref, q_ref, k_ref, v_ref, *, qkw):
    h = _normmod(x_ref[...], g_ref[...], sh_ref[...], sc_ref[...]).astype(BF16)
    qkv = _dot(h, w_ref[...])
    cos, sin = cos_ref[...], sin_ref[...]
    q_scale = DA_QK_DIM ** -0.5 * math.log2(math.e)
    for hb in range(qkw // LANES):
        lo, hi = hb * LANES, (hb + 1) * LANES
        q_ref[:, lo:hi] = (_rope(qkv[:, lo:hi], cos, sin) * q_scale).astype(BF16)
        k_ref[:, lo:hi] = _rope(qkv[:, qkw + lo:qkw + hi], cos, sin).astype(BF16)
    v_ref[...] = qkv[:, 2 * qkw:].astype(BF16)


def _qkv_call(xs, g, shift, scale, w_qkv, cos_t, sin_t, *, qkw, tm, seq, ctx_len, n_batch):
    rows, d = xs.shape
    n = w_qkv.shape[1]
    vw = n - 2 * qkw
    lat_tiles, lat_tps, ctx_tps = n_batch * seq // tm, seq // tm, ctx_len // tm
    seg = _seg_map(tm, seq, n_batch)

    def kv_map(i):
        c = i - lat_tiles
        is_lat = i < lat_tiles
        return (jnp.where(is_lat, i // lat_tps, c // ctx_tps),
                jnp.where(is_lat, ctx_tps + i % lat_tps, c % ctx_tps), 0)

    rope_map = lambda i: (jnp.where(i < lat_tiles, i % lat_tps, lat_tps), 0)
    return pl.pallas_call(
        functools.partial(_qkv_kernel, qkw=qkw),
        grid=(rows // tm,),
        in_specs=[pl.BlockSpec((tm, d), lambda i: (i, 0)),
                  pl.BlockSpec((1, d), lambda i: (0, 0)),
                  pl.BlockSpec((None, 1, d), seg),
                  pl.BlockSpec((None, 1, d), seg),
                  pl.BlockSpec((d, n), lambda i: (0, 0)),
                  pl.BlockSpec((tm, LANES), rope_map),
                  pl.BlockSpec((tm, LANES), rope_map)],
        out_specs=[pl.BlockSpec((tm, qkw), lambda i: (i, 0)),
                   pl.BlockSpec((None, tm, qkw), kv_map),
                   pl.BlockSpec((None, tm, vw), kv_map)],
        out_shape=[jax.ShapeDtypeStruct((rows, qkw), BF16),
                   jax.ShapeDtypeStruct((n_batch, ctx_len + seq, qkw), BF16),
                   jax.ShapeDtypeStruct((n_batch, ctx_len + seq, vw), BF16)],
        compiler_params=_cparams(1, 48),
        name="qkv_rope",
    )(xs, g, shift, scale, w_qkv, cos_t, sin_t)


def _rope_tables(seq, tm):
    pos = np.arange(seq)
    n_freq = DA_QK_DIM // 4
    inv = np.power(ROPE_BASE, -np.arange(n_freq, dtype=np.float32) / n_freq).astype(np.float32)
    row = (pos // GRID_W).astype(np.float32)[:, None] * inv
    col = (pos % GRID_W).astype(np.float32)[:, None] * inv
    ang = np.concatenate([row, col], axis=-1).astype(np.float32)
    ang = np.tile(ang, (1, LANES // ang.shape[1]))
    cos = np.concatenate([np.cos(ang), np.ones((tm, LANES))], axis=0)
    sin = np.concatenate([np.sin(ang), np.zeros((tm, LANES))], axis=0)
    return jnp.asarray(cos, F32), jnp.asarray(sin, F32)


def _attn_kernel(lp_ref, q_ref, k_ref, v_ref, sg_ref, o_ref, vext, *, n_sub, lam_init):
    @pl.when(pl.program_id(2) == 0)
    def _():
        vext[:, :LANES] = v_ref[...]
        vext[:, LANES:] = jnp.ones((vext.shape[0], LANES), BF16)

    lp = lp_ref[...]
    lam = (jnp.exp(jnp.sum(lp[0:1] * lp[1:2], axis=1, keepdims=True))
           - jnp.exp(jnp.sum(lp[2:3] * lp[3:4], axis=1, keepdims=True)) + lam_init)
    ts = q_ref.shape[0] // n_sub
    lane = lax.broadcasted_iota(jnp.int32, (ts, LANES), 1)
    zero = jnp.zeros((ts, LANES), BF16)
    k = k_ref[...]
    scores = []
    for a in range(n_sub):
        q = q_ref[a * ts:(a + 1) * ts, :]
        qq = jnp.concatenate([jnp.where(lane < DA_QK_DIM, q, zero),
                              jnp.where(lane >= DA_QK_DIM, q, zero)], axis=0)
        scores.append(_dot_nt(qq, k))
    for a in range(n_sub):
        s = scores[a]
        p = jnp.exp2(s - jnp.max(s, axis=-1, keepdims=True)).astype(BF16)
        oe = _dot(p, vext[...])
        on = oe[:, :LANES] / oe[:, LANES:LANES + 1]
        o = on[:ts] - lam * on[ts:]
        o_ref[a * ts:(a + 1) * ts, :] = (_rms(o, sg_ref[...]) * (1.0 - lam_init)).astype(BF16)


def _attn_call(lam_p, q, k_all, v_all, subln_g, *, tq, seq, n_batch, lam_init):
    n_heads = q.shape[1] // LANES
    lk = k_all.shape[1]
    qt = seq // tq
    return pl.pallas_call(
        functools.partial(_attn_kernel, n_sub=tq // ATTN_SUB_ROWS, lam_init=lam_init),
        grid=(n_batch, n_heads, qt),
        in_specs=[pl.BlockSpec(lam_p.shape, lambda b, h, i: (0, 0)),
                  pl.BlockSpec((tq, LANES), lambda b, h, i: (b * qt + i, h)),
                  pl.BlockSpec((None, lk, LANES), lambda b, h, i: (b, 0, h)),
                  pl.BlockSpec((None, lk, LANES), lambda b, h, i: (b, 0, h)),
                  pl.BlockSpec((1, LANES), lambda b, h, i: (0, 0))],
        out_specs=pl.BlockSpec((tq, LANES), lambda b, h, i: (b * qt + i, h)),
        out_shape=jax.ShapeDtypeStruct((n_batch * seq, n_heads * LANES), BF16),
        scratch_shapes=[pltpu.VMEM((lk, 2 * LANES), BF16)],
        compiler_params=_cparams(3, 48),
        name="diff_attention",
    )(lam_p, q, k_all, v_all, subln_g)


def kernel(x, c, ctx, c_ctx, ada_w, ada_b, norm_g, cf_w_in, cf_conv_w, cf_conv_b, cf_ln_g, cf_ln_b,
           cf_w_out, da_w_qkv, da_lambda, da_subln_g, da_w_out, moe_router_w, moe_router_b,
           moe_w_gate, moe_w_up, moe_w_down, moe_sh_gate, moe_sh_up, moe_sh_down):
    n_batch, seq, d = x.shape
    ctx_len = ctx.shape[1]
    depth = ada_w.shape[0]
    assert depth == 2 and n_batch + 1 <= N_SEG
    assert DA_QK_DIM * 2 == LANES and da_subln_g.shape[1] == LANES
    rows_lat, rows_ctx = n_batch * seq, n_batch * ctx_len
    tm = ROW_TILE
    tq = SEQ_TILE
    assert seq % tm == 0 and rows_ctx % tm == 0 and seq % tq == 0 and ctx_len % tq == 0

    x_lat, x_ctx = x.reshape(rows_lat, d), ctx.reshape(rows_ctx, d)
    cond = jnp.concatenate([c, c_ctx[None, :], jnp.zeros((N_SEG - n_batch - 1, d), F32)], axis=0)
    mods = _mods_call(cond, ada_w, ada_b)
    mod = lambda layer, k: mods[layer, :, k * d:(k + 1) * d][:, None, :]
    gain = lambda layer, k: norm_g[layer, k][None, :]
    common = dict(seq=seq, n_batch=n_batch)

    cc = cf_conv_w.shape[-1]
    fc = cf_w_in.shape[2] - 2 * cc
    gw = fc // FOURIER_GROUPS
    ch_ang = 2.0 * np.pi * (np.outer(np.arange(gw), np.arange(gw)) % gw) / gw
    cs = jnp.asarray(np.concatenate([np.cos(ch_ang), np.sin(ch_ang)], axis=1) / math.sqrt(gw), BF16)
    a_glu, ab = _inproj_call(x_lat, x_ctx, gain(0, 0), mod(0, 0), mod(0, 1), cf_w_in[0].astype(BF16), cs,
                             cc=cc, gw=gw, tm=tm, **common)
    gsz = cc // CONV_GROUPS
    gid = np.arange(cc) // gsz
    gmean = jnp.asarray((gid[:, None] == gid[None, :]) / gsz, BF16)
    a_act = _conv_call(a_glu, cf_conv_w[0][:, 0, :], cf_conv_b[0][None, :], cf_ln_g[0][None, :],
                       cf_ln_b[0][None, :], gmean, tl=tq, seq=seq, ctx_len=ctx_len, n_batch=n_batch)
    fr_lat = _seqdft_call(ab, seq, row0=0, n_seq=n_batch, fc=fc, tm=min(seq // 2, 512))
    fr_ctx = _seqdft_call(ab, ctx_len, row0=rows_lat, n_seq=n_batch, fc=fc, tm=min(ctx_len // 2, 256))
    w_out = cf_w_out[0].astype(BF16)
    xs, tok, tokp, *routing = _outproj_call(
        [(a_act, a_act), (fr_lat, fr_ctx)], [w_out[:cc], w_out[cc:]], x_lat, x_ctx, gain(0, 1), mod(0, 2), gain(0, 2), mod(0, 3),
        mod(0, 4), moe_router_w[0], moe_router_b[0], rows=rows_lat + rows_ctx, tm=tm, **common)
    xs = _moe_layer(xs, tok, tokp, routing, gain(0, 3), mod(0, 5), 0, moe_w_gate, moe_w_up, moe_w_down,
                    moe_sh_gate[0], moe_sh_up[0], moe_sh_down[0], rows=rows_lat + rows_ctx, tm=tm, **common)

    qkw = DA_HEADS * 2 * DA_QK_DIM
    lam_init = 0.8 - 0.6 * math.exp(-0.3 * 1)
    cos_t, sin_t = _rope_tables(seq, tq)
    q, k_all, v_all = _qkv_call(xs, gain(1, 0), mod(1, 0), mod(1, 1), da_w_qkv[0].astype(BF16),
                                cos_t, sin_t, qkw=qkw, tm=tq, ctx_len=ctx_len, **common)
    o = _attn_call(da_lambda[0], q, k_all, v_all, da_subln_g[0][None, :], tq=math.gcd(seq, ATTN_ROWS), seq=seq,
                   n_batch=n_batch, lam_init=lam_init)
    xs, tok, tokp, *routing = _outproj_call(
        [(o, o)], [da_w_out[0].astype(BF16)], xs, xs, gain(1, 1), mod(1, 2), gain(1, 2), mod(1, 3), mod(1, 4),
        moe_router_w[1], moe_router_b[1], rows=rows_lat, tm=tm, **common)
    xs = _moe_layer(xs, tok, tokp, routing, gain(1, 3), mod(1, 5), 1, moe_w_gate, moe_w_up, moe_w_down,
                    moe_sh_gate[1], moe_sh_up[1], moe_sh_down[1], rows=rows_lat, tm=tm, **common)
    return xs.reshape(n_batch, seq, d)
```
